```python
import math
import jax, jax.numpy as jnp
from jax import lax
import numpy as np

D_MODEL = 2048
BATCH = 1
SEQ = 8192
DEPTH = 4

PLE_DIM = 256
N_MIXERS = 3
NORM_EPS = 1e-6
NEG_INF = -1e30
REL_BUCKETS = 32
REL_MAX_DIST = 128
BIAS_MAPS = 16
Q_BLOCK = 128
DIFF_HEADS = 8
DIFF_HEAD_DIM = D_MODEL // DIFF_HEADS // 2
MOBA_HEADS = 16
MOBA_HEAD_DIM = D_MODEL // MOBA_HEADS
MOBA_BLOCK = 256
MOBA_TOPK = 3
MOBA_Q_CHUNK = 32
MLA_HEADS = 16
MLA_Q_RANK = 512
MLA_KV_RANK = 512
MLA_NOPE = 128
MLA_ROPE = 64
MLA_V = 128
ROPE_THETA = 10000.0
D_FF = 5632
N_EXPERTS = 8
TOP_K = 2

N_DIFF = (DEPTH + 2) // 3
N_MOBA = (DEPTH + 1) // 3
N_MLA = DEPTH // 3
N_DENSE = (DEPTH + 1) // 2
N_MOE = DEPTH // 2

kernel_name = "hybrid_diff_moba_mla_moe_trunk"


def rms_norm(x, g):
    xf = x.astype(jnp.float32)
    y = xf * lax.rsqrt(jnp.mean(xf * xf, axis=-1, keepdims=True) + NORM_EPS)
    return (y * g.astype(jnp.float32)).astype(x.dtype)


def rel_bucket(dist):
    n = jnp.maximum(dist, 0)
    max_exact = REL_BUCKETS // 2
    nf = jnp.maximum(n, 1).astype(jnp.float32)
    large = max_exact + (jnp.log(nf / max_exact) / math.log(REL_MAX_DIST / max_exact)
                         * (REL_BUCKETS - max_exact)).astype(jnp.int32)
    large = jnp.minimum(large, REL_BUCKETS - 1)
    return jnp.where(n < max_exact, n, large)


def apply_rope(x, pos):
    half = x.shape[-1] // 2
    inv_freq = ROPE_THETA ** (-jnp.arange(half, dtype=jnp.float32) / half)
    ang = pos.astype(jnp.float32)[:, :, None] * inv_freq
    ang = ang.reshape(ang.shape[:2] + (1,) * (x.ndim - 3) + (half,))
    cos, sin = jnp.cos(ang), jnp.sin(ang)
    xf = x.astype(jnp.float32)
    x1, x2 = xf[..., :half], xf[..., half:]
    return jnp.concatenate([x1 * cos - x2 * sin, x2 * cos + x1 * sin], axis=-1).astype(x.dtype)


def to_blocks(a, size):
    B, S = a.shape[:2]
    a = a.reshape((B, S // size, size) + a.shape[2:])
    return jnp.moveaxis(a, 1, 0)


def from_blocks(o):
    o = jnp.moveaxis(o, 0, 1)
    return o.reshape((o.shape[0], -1) + o.shape[3:])


def diff_lambda_init(layer):
    return 0.8 - 0.6 * math.exp(-0.3 * layer)


def diff_attention(h, pos, w_qkv, lam, subln_g, w_o, rel_table, lambda_init):
    B, S, _ = h.shape
    H, dh = DIFF_HEADS, DIFF_HEAD_DIM
    q, k, v = jnp.split(h @ w_qkv, 3, axis=-1)
    q = q.reshape(B, S, H, 2, dh)
    k = k.reshape(B, S, H, 2, dh)
    v = v.reshape(B, S, H, 2 * dh)
    lamf = lam.astype(jnp.float32)
    lam_full = (jnp.exp(jnp.sum(lamf[0] * lamf[1])) - jnp.exp(jnp.sum(lamf[2] * lamf[3]))
                + lambda_init)
    scale = dh ** -0.5
    table = rel_table.astype(jnp.float32).T.reshape(H, 2, REL_BUCKETS)
    kidx = jnp.arange(S)
    nb = S // Q_BLOCK

    def block(args):
        qi, pi, c = args
        s = jnp.einsum('bqhmd,bkhmd->bhmqk', qi, k,
                       preferred_element_type=jnp.float32) * scale
        bucket = rel_bucket(pi[:, :, None] - pos[:, None, :])
        bias = jnp.transpose(table[:, :, bucket], (2, 0, 1, 3, 4))
        qidx = c * Q_BLOCK + jnp.arange(Q_BLOCK)
        mask = kidx[None, :] <= qidx[:, None]
        s = jnp.where(mask, s + bias, NEG_INF)
        pr = jax.nn.softmax(s, axis=-1)
        a = pr[:, :, 0] - lam_full * pr[:, :, 1]
        return jnp.einsum('bhqk,bkhd->bqhd', a.astype(v.dtype), v)

    o = lax.map(block, (to_blocks(q, Q_BLOCK), to_blocks(pos, Q_BLOCK),
                        jnp.arange(nb, dtype=jnp.int32)))
    o = from_blocks(o)
    o = rms_norm(o, subln_g) * (1.0 - lambda_init)
    return o.reshape(B, S, H * 2 * dh) @ w_o


def moba_attention(h, pos, w_qkv, w_o, rel_table):
    B, S, _ = h.shape
    H, dh, L, QC, K = MOBA_HEADS, MOBA_HEAD_DIM, MOBA_BLOCK, MOBA_Q_CHUNK, MOBA_TOPK
    q, k, v = jnp.split(h @ w_qkv, 3, axis=-1)
    nblk = -(-S // L)
    pad = nblk * L - S
    padw = ((0, 0), (0, pad), (0, 0))
    q = jnp.pad(q, padw).reshape(B, nblk * L, H, dh)
    k = jnp.pad(k, padw)
    v = jnp.pad(v, padw)
    posp = jnp.pad(pos, ((0, 0), (0, pad)), mode='edge')
    kb = k.reshape(B, nblk, L, H, dh).transpose(0, 3, 1, 2, 4)
    vb = v.reshape(B, nblk, L, H, dh).transpose(0, 3, 1, 2, 4)
    pb = posp.reshape(B, nblk, L)
    kmean = jnp.mean(kb.astype(jnp.float32), axis=3)
    table = rel_table.astype(jnp.float32).T
    scale = dh ** -0.5
    bi = jnp.arange(B)[:, None, None, None]
    hi = jnp.arange(H)[None, :, None, None]
    hi5 = jnp.arange(H)[None, :, None, None, None]
    nq = (nblk * L) // QC
    n_gate = max(nblk, K)

    def chunk(args):
        qi, pi, c = args
        own = (c * QC) // L
        gs = jnp.einsum('bhqd,bhnd->bhqn', qi.astype(jnp.float32), kmean)
        gs = jnp.pad(gs, ((0, 0), (0, 0), (0, 0), (0, n_gate - nblk)))
        gs = jnp.where(jnp.arange(n_gate) < own, gs, NEG_INF)
        _, sel = lax.top_k(gs, K)
        sel_valid = sel < own
        sel = jnp.minimum(sel, nblk - 1)
        kg = kb[bi, hi, sel]
        vg = vb[bi, hi, sel]
        pg = pb[bi, sel]
        s_sel = jnp.einsum('bhqd,bhqkld->bhqkl', qi, kg,
                           preferred_element_type=jnp.float32) * scale
        bias_sel = table[hi5, rel_bucket(pi[:, None, :, None, None] - pg)]
        s_sel = jnp.where(sel_valid[..., None], s_sel + bias_sel, NEG_INF)
        ko = lax.dynamic_index_in_dim(kb, own, axis=2, keepdims=False)
        vo = lax.dynamic_index_in_dim(vb, own, axis=2, keepdims=False)
        po = lax.dynamic_index_in_dim(pb, own, axis=1, keepdims=False)
        s_own = jnp.einsum('bhqd,bhld->bhql', qi, ko,
                           preferred_element_type=jnp.float32) * scale
        bias_own = jnp.transpose(table[:, rel_bucket(pi[:, :, None] - po[:, None, :])],
                                 (1, 0, 2, 3))
        causal = (own * L + jnp.arange(L))[None, :] <= (c * QC + jnp.arange(QC))[:, None]
        s_own = jnp.where(causal, s_own + bias_own, NEG_INF)
        s_all = jnp.concatenate([s_sel.reshape(B, H, QC, K * L), s_own], axis=-1)
        pr = jax.nn.softmax(s_all, axis=-1)
        p_sel = pr[..., :K * L].reshape(B, H, QC, K, L).astype(vg.dtype)
        p_own = pr[..., K * L:].astype(vo.dtype)
        return (jnp.einsum('bhqkl,bhqkld->bhqd', p_sel, vg)
                + jnp.einsum('bhql,bhld->bhqd', p_own, vo))

    qc = to_blocks(q, QC).transpose(0, 1, 3, 2, 4)
    o = lax.map(chunk, (qc, to_blocks(posp, QC), jnp.arange(nq, dtype=jnp.int32)))
    o = from_blocks(o.transpose(0, 1, 3, 2, 4))[:, :S]
    return o.reshape(B, S, H * dh) @ w_o


def mla_attention(h, pos, w_down, g_q, w_uq, g_kv, w_ukv, w_o):
    B, S, _ = h.shape
    H = MLA_HEADS
    down = h @ w_down
    cq = rms_norm(down[..., :MLA_Q_RANK], g_q)
    ckv = rms_norm(down[..., MLA_Q_RANK:MLA_Q_RANK + MLA_KV_RANK], g_kv)
    kr = apply_rope(down[..., MLA_Q_RANK + MLA_KV_RANK:], pos)
    qh = (cq @ w_uq).reshape(B, S, H, MLA_NOPE + MLA_ROPE)
    qn = qh[..., :MLA_NOPE]
    qr = apply_rope(qh[..., MLA_NOPE:], pos)
    kv = (ckv @ w_ukv).reshape(B, S, H, MLA_NOPE + MLA_V)
    kn, v = kv[..., :MLA_NOPE], kv[..., MLA_NOPE:]
    scale = (MLA_NOPE + MLA_ROPE) ** -0.5
    kidx = jnp.arange(S)
    nb = S // Q_BLOCK

    def block(args):
        qni, qri, c = args
        s = (jnp.einsum('bqhd,bkhd->bhqk', qni, kn, preferred_element_type=jnp.float32)
             + jnp.einsum('bqhd,bkd->bhqk', qri, kr, preferred_element_type=jnp.float32)) * scale
        qidx = c * Q_BLOCK + jnp.arange(Q_BLOCK)
        s = jnp.where(kidx[None, :] <= qidx[:, None], s, NEG_INF)
        pr = jax.nn.softmax(s, axis=-1).astype(v.dtype)
        return jnp.einsum('bhqk,bkhd->bqhd', pr, v)

    o = lax.map(block, (to_blocks(qn, Q_BLOCK), to_blocks(qr, Q_BLOCK),
                        jnp.arange(nb, dtype=jnp.int32)))
    return from_blocks(o).reshape(B, S, H * MLA_V) @ w_o


def swiglu(x, w_in, w_out):
    g, u = jnp.split(x @ w_in, 2, axis=-1)
    return (jax.nn.silu(g) * u) @ w_out


def moe_swiglu(x, w_router, w_in, w_out):
    logits = (x @ w_router).astype(jnp.float32)
    top_val, top_idx = lax.top_k(logits, TOP_K)
    gates = jax.nn.softmax(top_val, axis=-1)
    combine = jnp.sum(jax.nn.one_hot(top_idx, N_EXPERTS, dtype=jnp.float32)
                      * gates[..., None], axis=-2)
    y = jnp.zeros(x.shape, jnp.float32)
    for e in range(N_EXPERTS):
        y = y + combine[..., e:e + 1] * swiglu(x, w_in[e], w_out[e]).astype(jnp.float32)
    return y.astype(x.dtype)


def setup_inputs(seed: int = 0) -> dict:
    key = jax.random.key(seed)
    ks = iter(jax.random.split(key, 40))

    def w(shape, fan_in):
        return jax.random.normal(next(ks), shape, jnp.float32) * fan_in ** -0.5

    def gain(shape):
        return 1.0 + 0.01 * jax.random.normal(next(ks), shape, jnp.float32)

    D = D_MODEL
    return {
        "x": jax.random.normal(next(ks), (BATCH, SEQ, D), jnp.float32),
        "p": jax.random.normal(next(ks), (DEPTH, BATCH, SEQ, PLE_DIM), jnp.float32),
        "positions": jnp.broadcast_to(jnp.arange(SEQ, dtype=jnp.int32), (BATCH, SEQ)),
        "rel_bias": 0.5 * jax.random.normal(next(ks), (REL_BUCKETS, BIAS_MAPS), jnp.float32),
        "norm_mix": gain((DEPTH, D)),
        "norm_ffn": gain((DEPTH, D)),
        "norm_ple": gain((DEPTH, D)),
        "norm_final": gain((D,)),
        "diff_w_qkv": w((N_DIFF, D, 3 * D), D),
        "diff_lambda": 0.1 * jax.random.normal(next(ks), (N_DIFF, 4, DIFF_HEAD_DIM), jnp.float32),
        "diff_subln": gain((N_DIFF, 2 * DIFF_HEAD_DIM)),
        "diff_w_o": w((N_DIFF, D, D), D),
        "moba_w_qkv": w((N_MOBA, D, 3 * D), D),
        "moba_w_o": w((N_MOBA, D, D), D),
        "mla_w_down": w((N_MLA, D, MLA_Q_RANK + MLA_KV_RANK + MLA_ROPE), D),
        "mla_g_q": gain((N_MLA, MLA_Q_RANK)),
        "mla_w_uq": w((N_MLA, MLA_Q_RANK, MLA_HEADS * (MLA_NOPE + MLA_ROPE)), MLA_Q_RANK),
        "mla_g_kv": gain((N_MLA, MLA_KV_RANK)),
        "mla_w_ukv": w((N_MLA, MLA_KV_RANK, MLA_HEADS * (MLA_NOPE + MLA_V)), MLA_KV_RANK),
        "mla_w_o": w((N_MLA, MLA_HEADS * MLA_V, D), MLA_HEADS * MLA_V),
        "ffn_w_in": w((N_DENSE, D, 2 * D_FF), D),
        "ffn_w_out": w((N_DENSE, D_FF, D), D_FF),
        "moe_w_router": w((N_MOE, D, N_EXPERTS), D),
        "moe_w_in": w((N_MOE, N_EXPERTS, D, 2 * D_FF), D),
        "moe_w_out": w((N_MOE, N_EXPERTS, D_FF, D), D_FF),
        "ple_w_gate": w((DEPTH, D, D), D),
        "ple_w_proj": w((DEPTH, PLE_DIM, D), PLE_DIM),
    }


def reference(x, p, positions, rel_bias, norm_mix, norm_ffn, norm_ple, norm_final,
              diff_w_qkv, diff_lambda, diff_subln, diff_w_o, moba_w_qkv, moba_w_o,
              mla_w_down, mla_g_q, mla_w_uq, mla_g_kv, mla_w_ukv, mla_w_o,
              ffn_w_in, ffn_w_out, moe_w_router, moe_w_in, moe_w_out,
              ple_w_gate, ple_w_proj):
    for i in range(DEPTH):
        j = i // N_MIXERS
        hn = rms_norm(x, norm_mix[i])
        if i % N_MIXERS == 0:
            y = diff_attention(hn, positions, diff_w_qkv[j], diff_lambda[j], diff_subln[j],
                               diff_w_o[j], rel_bias, diff_lambda_init(i))
        elif i % N_MIXERS == 1:
            y = moba_attention(hn, positions, moba_w_qkv[j], moba_w_o[j], rel_bias)
        else:
            y = mla_attention(hn, positions, mla_w_down[j], mla_g_q[j], mla_w_uq[j],
                              mla_g_kv[j], mla_w_ukv[j], mla_w_o[j])
        x = x + y
        hn = rms_norm(x, norm_ffn[i])
        if i % 2 == 0:
            x = x + swiglu(hn, ffn_w_in[i // 2], ffn_w_out[i // 2])
        else:
            x = x + moe_swiglu(hn, moe_w_router[i // 2], moe_w_in[i // 2], moe_w_out[i // 2])
        gate = jax.nn.sigmoid(rms_norm(x, norm_ple[i]) @ ple_w_gate[i])
        x = x + gate * (p[i] @ ple_w_proj[i])
    return rms_norm(x, norm_final)
```

```python
import functools
import math

import numpy as np
import jax
import jax.numpy as jnp
from jax import lax
from jax.experimental import pallas as pl
from jax.experimental.pallas import tpu as pltpu

F32 = jnp.float32
BF16 = jnp.bfloat16

NORM_EPS = 1e-6
NEG_INF = -1e30
LOG2E = math.log2(math.e)
LANES = 128
VMEM_LIMIT_BYTES = 56 * 1024 * 1024

REL_BUCKETS = 32
REL_MAX_DIST = 128
LUT_SIZE = LANES
DIFF_HEADS = 8
DIFF_HEAD_DIM = 128
MOBA_HEADS = 16
MOBA_HEAD_DIM = 128
MOBA_BLOCK = 256
MOBA_TOPK = 3
MLA_HEADS = 16
MLA_Q_RANK = 512
MLA_KV_RANK = 512
MLA_NOPE = 128
MLA_ROPE = 64
MLA_V = 128
ROPE_THETA = 10000.0
N_EXPERTS = 8
N_MIXERS = 3


def _cparams(*sem):
    return pltpu.CompilerParams(dimension_semantics=sem, vmem_limit_bytes=VMEM_LIMIT_BYTES)


def _tile(n, pref):
    if n <= pref:
        return n
    t = pref
    while n % t:
        t //= 2
    return t


def _rms_rows(x, g):
    r = lax.rsqrt(jnp.mean(x * x, axis=-1, keepdims=True) + NORM_EPS)
    return x * r * g


def _norm_mm_kernel(x_ref, g_ref, w_ref, o_ref, xn_ref, *, scaled_tiles, scale):
    j = pl.program_id(1)

    @pl.when(j == 0)
    def _():
        xn_ref[...] = _rms_rows(x_ref[...], g_ref[...]).astype(BF16)

    acc = jnp.dot(xn_ref[...], w_ref[...], preferred_element_type=F32)
    if scaled_tiles:
        acc = acc * jnp.where(j < scaled_tiles, scale, 1.0)
    o_ref[...] = acc.astype(o_ref.dtype)


def norm_matmul(x, g, w, *, out_dtype, xcol=0, tm=512, tn=512, scaled_cols=0, scale=1.0):
    M = x.shape[0]
    K, N = w.shape
    tm, tn = _tile(M, tm), _tile(N, tn)
    assert scaled_cols % tn == 0
    kern = functools.partial(_norm_mm_kernel, scaled_tiles=scaled_cols // tn, scale=scale)
    return pl.pallas_call(
        kern,
        out_shape=jax.ShapeDtypeStruct((M, N), out_dtype),
        grid=(M // tm, N // tn),
        in_specs=[pl.BlockSpec((tm, K), lambda i, j: (i, xcol)),
                  pl.BlockSpec((1, K), lambda i, j: (0, 0)),
                  pl.BlockSpec((K, tn), lambda i, j: (0, j))],
        out_specs=pl.BlockSpec((tm, tn), lambda i, j: (i, j)),
        scratch_shapes=[pltpu.VMEM((tm, K), BF16)],
        compiler_params=_cparams("parallel", "arbitrary"),
        name="norm_matmul",
    )(x, g.reshape(1, K), w)


def _norm_swiglu_kernel(x_ref, g_ref, wg_ref, wu_ref, o_ref, xn_ref):
    @pl.when(pl.program_id(1) == 0)
    def _():
        xn_ref[...] = _rms_rows(x_ref[...], g_ref[...]).astype(BF16)

    xn = xn_ref[...]
    a = jnp.dot(xn, wg_ref[...], preferred_element_type=F32)
    u = jnp.dot(xn, wu_ref[...], preferred_element_type=F32)
    o_ref[...] = (a * jax.nn.sigmoid(a) * u).astype(o_ref.dtype)


def norm_swiglu_in(x, g, w_in, *, tm=512, tn=512):
    M, K = x.shape
    F = w_in.shape[1] // 2
    tm, tn = _tile(M, tm), _tile(F, tn)
    nj = F // tn
    return pl.pallas_call(
        _norm_swiglu_kernel,
        out_shape=jax.ShapeDtypeStruct((M, F), BF16),
        grid=(M // tm, nj),
        in_specs=[pl.BlockSpec((tm, K), lambda i, j: (i, 0)),
                  pl.BlockSpec((1, K), lambda i, j: (0, 0)),
                  pl.BlockSpec((K, tn), lambda i, j: (0, j)),
                  pl.BlockSpec((K, tn), lambda i, j: (0, j + nj))],
        out_specs=pl.BlockSpec((tm, tn), lambda i, j: (i, j)),
        scratch_shapes=[pltpu.VMEM((tm, K), BF16)],
        compiler_params=_cparams("parallel", "arbitrary"),
        name="norm_swiglu_in",
    )(x, g.reshape(1, K), w_in, w_in)


def _mm_res_kernel(a_ref, w_ref, r_ref, o_ref):
    o_ref[...] = r_ref[...] + jnp.dot(a_ref[...], w_ref[...], preferred_element_type=F32)


def matmul_residual(a, w, res, *, tm=512, tn=512):
    M, K = a.shape
    N = w.shape[1]
    tm, tn = _tile(M, tm), _tile(N, tn)
    return pl.pallas_call(
        _mm_res_kernel,
        out_shape=jax.ShapeDtypeStruct((M, N), F32),
        grid=(M // tm, N // tn),
        in_specs=[pl.BlockSpec((tm, K), lambda i, j: (i, 0)),
                  pl.BlockSpec((K, tn), lambda i, j: (0, j)),
                  pl.BlockSpec((tm, tn), lambda i, j: (i, j))],
        out_specs=pl.BlockSpec((tm, tn), lambda i, j: (i, j)),
        compiler_params=_cparams("parallel", "parallel"),
        name="matmul_residual",
    )(a, w, res)


def _ple_kernel(x_ref, g_ref, xr_ref, p_ref, wg_ref, wp_ref, o_ref, xn_ref):
    @pl.when(pl.program_id(1) == 0)
    def _():
        xn_ref[...] = _rms_rows(x_ref[...], g_ref[...]).astype(BF16)

    gate = jax.nn.sigmoid(jnp.dot(xn_ref[...], wg_ref[...], preferred_element_type=F32))
    proj = jnp.dot(p_ref[...].astype(BF16), wp_ref[...], preferred_element_type=F32)
    o_ref[...] = xr_ref[...] + gate * proj


def ple_update(x, g, p, w_gate, w_proj, *, tm=512, tn=512):
    M, D = x.shape
    P = p.shape[1]
    tm, tn = _tile(M, tm), _tile(D, tn)
    return pl.pallas_call(
        _ple_kernel,
        out_shape=jax.ShapeDtypeStruct((M, D), F32),
        grid=(M // tm, D // tn),
        in_specs=[pl.BlockSpec((tm, D), lambda i, j: (i, 0)),
                  pl.BlockSpec((1, D), lambda i, j: (0, 0)),
                  pl.BlockSpec((tm, tn), lambda i, j: (i, j)),
                  pl.BlockSpec((tm, P), lambda i, j: (i, 0)),
                  pl.BlockSpec((D, tn), lambda i, j: (0, j)),
                  pl.BlockSpec((P, tn), lambda i, j: (0, j))],
        out_specs=pl.BlockSpec((tm, tn), lambda i, j: (i, j)),
        scratch_shapes=[pltpu.VMEM((tm, D), BF16)],
        compiler_params=_cparams("parallel", "arbitrary"),
        name="ple_update",
    )(x, g.reshape(1, D), x, p, w_gate, w_proj)


def _final_norm_kernel(x_ref, g_ref, o_ref):
    o_ref[...] = _rms_rows(x_ref[...], g_ref[...])


def final_norm(x, g, *, tm=512):
    M, D = x.shape
    tm = _tile(M, tm)
    return pl.pallas_call(
        _final_norm_kernel,
        out_shape=jax.ShapeDtypeStruct((M, D), F32),
        grid=(M // tm,),
        in_specs=[pl.BlockSpec((tm, D), lambda i: (i, 0)),
                  pl.BlockSpec((1, D), lambda i: (0, 0))],
        out_specs=pl.BlockSpec((tm, D), lambda i: (i, 0)),
        compiler_params=_cparams("parallel"),
        name="final_norm",
    )(x, g.reshape(1, D))


def _pair_tables(n_tiles):
    qi, kj = [], []
    for i in range(n_tiles):
        for j in range(i + 1):
            qi.append(i)
            kj.append(j)
    return jnp.asarray(np.array(qi, np.int32)), jnp.asarray(np.array(kj, np.int32))


def _rel_bucket(dist):
    n = jnp.maximum(dist, 0)
    max_exact = REL_BUCKETS // 2
    nf = jnp.maximum(n, 1).astype(F32)
    large = max_exact + (jnp.log(nf / max_exact) / math.log(REL_MAX_DIST / max_exact)
                         * (REL_BUCKETS - max_exact)).astype(jnp.int32)
    large = jnp.minimum(large, REL_BUCKETS - 1)
    return jnp.where(n < max_exact, n, large)


def _bias_lut(rel_bias):
    buckets = _rel_bucket(jnp.arange(LUT_SIZE, dtype=jnp.int32))
    return rel_bias.astype(F32)[buckets].T * LOG2E


def _tile_pos_bounds(pos, tq, ck):
    S = pos.shape[0]
    return pos.reshape(S // tq, tq).min(axis=1), pos.reshape(S // ck, ck).max(axis=1)


def _lut_bias(lut_row, idx):
    R, C = idx.shape
    lut = jnp.broadcast_to(lut_row, (R, LANES))
    parts = [jnp.take_along_axis(lut, idx[:, c:c + LANES], axis=1) for c in range(0, C, LANES)]
    return parts[0] if len(parts) == 1 else jnp.concatenate(parts, axis=1)


def _online_softmax_step(s, v, m_ref, l_ref, acc_ref, rows):
    m_prev = m_ref[rows, :]
    m_new = jnp.maximum(m_prev, jnp.max(s, axis=-1, keepdims=True))
    alpha = jnp.exp2(m_prev - m_new)
    p = jnp.exp2(s - m_new)
    l_ref[rows, :] = alpha * l_ref[rows, :] + jnp.sum(p, axis=-1, keepdims=True)
    acc_ref[rows, :] = alpha * acc_ref[rows, :] + jnp.dot(p.astype(BF16), v, preferred_element_type=F32)
    m_ref[rows, :] = m_new


def _nt_dot(a, b):
    return lax.dot_general(a, b, (((1,), (1,)), ((), ())), preferred_element_type=F32)


def _causal(rows0, nrows, col0, ncols):
    r = rows0 + lax.broadcasted_iota(jnp.int32, (nrows, ncols), 0)
    c = col0 + lax.broadcasted_iota(jnp.int32, (nrows, ncols), 1)
    return c <= r


def _diff_attn_kernel(qi_ref, kj_ref, pminq_ref, pmaxk_ref,
                      q_ref, k_ref, v_ref, pq_ref, pk_ref, lut_ref, lam_ref, g_ref,
                      o_ref, m_ref, l_ref, acc_ref, *, tq, ck, lambda_init):
    p = pl.program_id(1)
    i, j = qi_ref[p], kj_ref[p]
    nc = tq // ck
    dh = DIFF_HEAD_DIM

    @pl.when(j == 0)
    def _():
        m_ref[...] = jnp.full(m_ref.shape, NEG_INF, F32)
        l_ref[...] = jnp.zeros(l_ref.shape, F32)
        acc_ref[...] = jnp.zeros(acc_ref.shape, F32)

    def chunk(c, r0, mode):
        rows = slice(r0, tq)
        cols = slice(c * ck, (c + 1) * ck)
        kc, vc = k_ref[cols, :], v_ref[cols, :]
        if mode != "far":
            idx = jnp.clip(pq_ref[rows, :] - pk_ref[:, cols], 0, LUT_SIZE - 1)
        for mp in range(2):
            s = _nt_dot(q_ref[rows, mp * dh:(mp + 1) * dh], kc[:, mp * dh:(mp + 1) * dh])
            if mode == "far":
                s = s + lut_ref[mp:mp + 1, LUT_SIZE - 1:LUT_SIZE]
            else:
                s = s + _lut_bias(lut_ref[mp:mp + 1, :], idx)
            if mode == "diag":
                s = jnp.where(_causal(r0, tq - r0, c * ck, ck), s, NEG_INF)
            _online_softmax_step(s, vc, m_ref.at[mp], l_ref.at[mp], acc_ref.at[mp], rows)

    @pl.when(j < i)
    def _():
        for c in range(nc):
            far = pminq_ref[i] - pmaxk_ref[j * nc + c] >= LUT_SIZE - 1
            pl.when(far)(functools.partial(chunk, c, 0, "far"))
            pl.when(jnp.logical_not(far))(functools.partial(chunk, c, 0, "near"))

    @pl.when(j == i)
    def _():
        for c in range(nc):
            chunk(c, c * ck, "diag")
        lam = lam_ref[...]
        lam_full = (jnp.exp(jnp.sum(lam[0:1] * lam[1:2], axis=-1, keepdims=True))
                    - jnp.exp(jnp.sum(lam[2:3] * lam[3:4], axis=-1, keepdims=True)) + lambda_init)
        a = acc_ref[0] / l_ref[0] - lam_full * (acc_ref[1] / l_ref[1])
        o_ref[...] = (_rms_rows(a, g_ref[...]) * (1.0 - lambda_init)).astype(o_ref.dtype)


def diff_attention(qkv, pos, lut, lam, subln_g, lambda_init, *, tile=1024, chunk=512):
    S = qkv.shape[0]
    H, dh = DIFF_HEADS, DIFF_HEAD_DIM
    tq = _tile(S, tile)
    ck = _tile(tq, chunk)
    qi, kj = _pair_tables(S // tq)
    pminq, pmaxk = _tile_pos_bounds(pos, tq, ck)
    kern = functools.partial(_diff_attn_kernel, tq=tq, ck=ck, lambda_init=lambda_init)
    grid_spec = pltpu.PrefetchScalarGridSpec(
        num_scalar_prefetch=4,
        grid=(H, qi.shape[0]),
        in_specs=[pl.BlockSpec((tq, 2 * dh), lambda h, p, qi, kj, a, b: (qi[p], h)),
                  pl.BlockSpec((tq, 2 * dh), lambda h, p, qi, kj, a, b: (kj[p], H + h)),
                  pl.BlockSpec((tq, 2 * dh), lambda h, p, qi, kj, a, b: (kj[p], 2 * H + h)),
                  pl.BlockSpec((tq, 1), lambda h, p, qi, kj, a, b: (qi[p], 0)),
                  pl.BlockSpec((1, tq), lambda h, p, qi, kj, a, b: (0, kj[p])),
                  pl.BlockSpec((None, 2, LUT_SIZE), lambda h, p, qi, kj, a, b: (h, 0, 0)),
                  pl.BlockSpec((4, dh), lambda h, p, qi, kj, a, b: (0, 0)),
                  pl.BlockSpec((1, 2 * dh), lambda h, p, qi, kj, a, b: (0, 0))],
        out_specs=pl.BlockSpec((tq, 2 * dh), lambda h, p, qi, kj, a, b: (qi[p], h)),
        scratch_shapes=[pltpu.VMEM((2, tq, 1), F32), pltpu.VMEM((2, tq, 1), F32),
                        pltpu.VMEM((2, tq, 2 * dh), F32)],
    )
    return pl.pallas_call(
        kern,
        out_shape=jax.ShapeDtypeStruct((S, H * 2 * dh), BF16),
        grid_spec=grid_spec,
        compiler_params=_cparams("parallel", "arbitrary"),
        name="diff_attention",
    )(qi, kj, pminq, pmaxk, qkv, qkv, qkv, pos.reshape(S, 1), pos.reshape(1, S),
      lut.reshape(H, 2, LUT_SIZE), lam, subln_g.reshape(1, 2 * dh))


def _kmean_kernel(k_ref, hi_ref, lo_ref):
    km = jnp.mean(k_ref[...].astype(F32), axis=0, keepdims=True)
    hi = km.astype(BF16)
    hi_ref[...] = hi
    lo_ref[...] = (km - hi.astype(F32)).astype(BF16)


def moba_block_means(qkv, D):
    S = qkv.shape[0]
    nblk = S // MOBA_BLOCK
    hi, lo = pl.pallas_call(
        _kmean_kernel,
        out_shape=[jax.ShapeDtypeStruct((nblk, 1, D), BF16)] * 2,
        grid=(nblk,),
        in_specs=[pl.BlockSpec((MOBA_BLOCK, D), lambda n: (n, 1))],
        out_specs=[pl.BlockSpec((None, 1, D), lambda n: (n, 0, 0))] * 2,
        compiler_params=_cparams("parallel"),
        name="moba_block_means",
    )(qkv)
    return hi.reshape(nblk, D), lo.reshape(nblk, D)


def _moba_attn_kernel(qi_ref, kj_ref, pminq_ref, pmaxk_ref,
                      q_ref, k_ref, v_ref, kmh_ref, kml_ref, pq_ref, pk_ref, lut_ref,
                      o_ref, m_ref, l_ref, acc_ref, sel_ref, *, tq, ck):
    p = pl.program_id(1)
    i, j = qi_ref[p], kj_ref[p]
    nc = tq // ck
    L = MOBA_BLOCK
    bpc = ck // L
    lane = lax.broadcasted_iota(jnp.int32, (tq, LANES), 1)
    lane_f = lane.astype(F32)
    log2_l = L.bit_length() - 1

    @pl.when(j == 0)
    def _():
        m_ref[...] = jnp.full(m_ref.shape, NEG_INF, F32)
        l_ref[...] = jnp.zeros(l_ref.shape, F32)
        acc_ref[...] = jnp.zeros(acc_ref.shape, F32)
        q = q_ref[...]
        gs = _nt_dot(q, kmh_ref[...]) + _nt_dot(q, kml_ref[...])
        own = lax.shift_right_logical(i * tq + lax.broadcasted_iota(jnp.int32, (tq, 1), 0), log2_l)
        g = jnp.where(lane < own, gs, NEG_INF)
        sel = jnp.zeros((tq, LANES), F32)
        for _ in range(MOBA_TOPK):
            best = jnp.max(g, axis=-1, keepdims=True)
            first = jnp.min(jnp.where(g == best, lane_f, float(LANES)), axis=-1, keepdims=True)
            pick = lane_f == first
            sel = jnp.where(pick, jnp.where(best > 0.5 * NEG_INF, 1.0, sel), sel)
            g = jnp.where(pick, -3e38, g)
        sel_ref[...] = sel

    def block_selected(n):
        return jnp.sum(jnp.where(lane == n, sel_ref[...], 0.0), axis=-1, keepdims=True)

    def chunk(c, r0, mode):
        rows = slice(r0, tq)
        cols = slice(c * ck, (c + 1) * ck)
        s = _nt_dot(q_ref[rows, :], k_ref[cols, :])
        if mode == "far":
            s = s + lut_ref[:, LUT_SIZE - 1:LUT_SIZE]
        else:
            idx = jnp.clip(pq_ref[rows, :] - pk_ref[:, cols], 0, LUT_SIZE - 1)
            s = s + _lut_bias(lut_ref[...], idx)
        allowed = []
        for b in range(bpc):
            n_local = c * bpc + b
            picked = jnp.broadcast_to(block_selected(j * (tq // L) + n_local)[rows, :], (tq - r0, L))
            if mode == "diag":
                own_local = lax.shift_right_logical(
                    r0 + lax.broadcasted_iota(jnp.int32, (tq - r0, L), 0), log2_l)
                visible = jnp.where(_causal(r0, tq - r0, n_local * L, L), 1.0, 0.0)
                picked = jnp.where(own_local == n_local, visible, picked)
            allowed.append(picked)
        allowed = allowed[0] if bpc == 1 else jnp.concatenate(allowed, axis=1)
        s = jnp.where(allowed > 0.0, s, NEG_INF)
        _online_softmax_step(s, v_ref[cols, :], m_ref, l_ref, acc_ref, rows)

    @pl.when(j < i)
    def _():
        for c in range(nc):
            far = pminq_ref[i] - pmaxk_ref[j * nc + c] >= LUT_SIZE - 1
            pl.when(far)(functools.partial(chunk, c, 0, "far"))
            pl.when(jnp.logical_not(far))(functools.partial(chunk, c, 0, "near"))

    @pl.when(j == i)
    def _():
        for c in range(nc):
            chunk(c, c * ck, "diag")
        o_ref[...] = (acc_ref[...] / l_ref[...]).astype(o_ref.dtype)


def moba_attention(qkv, pos, lut, *, tile=1024, chunk=512):
    S = qkv.shape[0]
    H, dh, L = MOBA_HEADS, MOBA_HEAD_DIM, MOBA_BLOCK
    D = H * dh
    nblk = S // L
    assert S % L == 0 and nblk <= LANES
    tq = _tile(S, tile)
    ck = _tile(tq, chunk)
    assert tq % L == 0 and ck % L == 0
    km_hi, km_lo = moba_block_means(qkv, D)
    km_hi = jnp.pad(km_hi, ((0, LANES - nblk), (0, 0)))
    km_lo = jnp.pad(km_lo, ((0, LANES - nblk), (0, 0)))
    qi, kj = _pair_tables(S // tq)
    pminq, pmaxk = _tile_pos_bounds(pos, tq, ck)
    kern = functools.partial(_moba_attn_kernel, tq=tq, ck=ck)
    grid_spec = pltpu.PrefetchScalarGridSpec(
        num_scalar_prefetch=4,
        grid=(H, qi.shape[0]),
        in_specs=[pl.BlockSpec((tq, dh), lambda h, p, qi, kj, a, b: (qi[p], h)),
                  pl.BlockSpec((tq, dh), lambda h, p, qi, kj, a, b: (kj[p], H + h)),
                  pl.BlockSpec((tq, dh), lambda h, p, qi, kj, a, b: (kj[p], 2 * H + h)),
                  pl.BlockSpec((LANES, dh), lambda h, p, qi, kj, a, b: (0, h)),
                  pl.BlockSpec((LANES, dh), lambda h, p, qi, kj, a, b: (0, h)),
                  pl.BlockSpec((tq, 1), lambda h, p, qi, kj, a, b: (qi[p], 0)),
                  pl.BlockSpec((1, tq), lambda h, p, qi, kj, a, b: (0, kj[p])),
                  pl.BlockSpec((None, 1, LUT_SIZE), lambda h, p, qi, kj, a, b: (h, 0, 0))],
        out_specs=pl.BlockSpec((tq, dh), lambda h, p, qi, kj, a, b: (qi[p], h)),
        scratch_shapes=[pltpu.VMEM((tq, 1), F32), pltpu.VMEM((tq, 1), F32),
                        pltpu.VMEM((tq, dh), F32), pltpu.VMEM((tq, LANES), F32)],
    )
    return pl.pallas_call(
        kern,
        out_shape=jax.ShapeDtypeStruct((S, D), BF16),
        grid_spec=grid_spec,
        compiler_params=_cparams("parallel", "arbitrary"),
        name="moba_attention",
    )(qi, kj, pminq, pmaxk, qkv, qkv, qkv, km_hi, km_lo, pos.reshape(S, 1), pos.reshape(1, S),
      lut.reshape(H, 1, LUT_SIZE))


def _rope_fold(y):
    lane = lax.broadcasted_iota(jnp.int32, y.shape, 1)
    return jnp.where(lane < MLA_ROPE, y + pltpu.roll(y, MLA_ROPE, 1), 0.0)


def _mla_q_kernel(cq_ref, g_ref, w_ref, kr_ref, t_ref, q_ref, kro_ref, cqn_ref, *, scale):
    t = t_ref[...]

    @pl.when(pl.program_id(1) == 0)
    def _():
        cqn_ref[...] = _rms_rows(cq_ref[...], g_ref[...]).astype(BF16)
        kro_ref[...] = _rope_fold(kr_ref[...] * t).astype(kro_ref.dtype)

    qh = jnp.dot(cqn_ref[...], w_ref[...], preferred_element_type=F32)
    q_ref[:, :MLA_NOPE] = (qh[:, :MLA_NOPE] * scale).astype(q_ref.dtype)
    q_ref[:, MLA_NOPE:] = (_rope_fold(qh[:, MLA_NOPE:] * t) * scale).astype(q_ref.dtype)


def mla_queries_and_rope_key(down, g_q, w_uq_ext, rope_tab, *, scale, tm=512):
    S = down.shape[0]
    H = MLA_HEADS
    tm = _tile(S, tm)
    W = 2 * LANES
    return pl.pallas_call(
        functools.partial(_mla_q_kernel, scale=scale),
        out_shape=[jax.ShapeDtypeStruct((S, H * W), BF16), jax.ShapeDtypeStruct((S, LANES), BF16)],
        grid=(S // tm, H),
        in_specs=[pl.BlockSpec((tm, MLA_Q_RANK), lambda i, h: (i, 0)),
                  pl.BlockSpec((1, MLA_Q_RANK), lambda i, h: (0, 0)),
                  pl.BlockSpec((MLA_Q_RANK, W), lambda i, h: (0, h)),
                  pl.BlockSpec((tm, LANES), lambda i, h: (i, (MLA_Q_RANK + MLA_KV_RANK) // LANES)),
                  pl.BlockSpec((tm, LANES), lambda i, h: (i, 0))],
        out_specs=[pl.BlockSpec((tm, W), lambda i, h: (i, h)),
                   pl.BlockSpec((tm, LANES), lambda i, h: (i, 0))],
        scratch_shapes=[pltpu.VMEM((tm, MLA_Q_RANK), BF16)],
        compiler_params=_cparams("parallel", "arbitrary"),
        name="mla_queries",
    )(down, g_q.reshape(1, MLA_Q_RANK), w_uq_ext, down, rope_tab)


def _mla_attn_kernel(qi_ref, kj_ref, q_ref, kn_ref, kr_ref, v_ref, o_ref, m_ref, l_ref, acc_ref, *, tq, ck):
    p = pl.program_id(1)
    i, j = qi_ref[p], kj_ref[p]
    nc = tq // ck

    @pl.when(j == 0)
    def _():
        m_ref[...] = jnp.full(m_ref.shape, NEG_INF, F32)
        l_ref[...] = jnp.zeros(l_ref.shape, F32)
        acc_ref[...] = jnp.zeros(acc_ref.shape, F32)

    def chunk(c, r0, diag):
        rows = slice(r0, tq)
        cols = slice(c * ck, (c + 1) * ck)
        s = (_nt_dot(q_ref[rows, :MLA_NOPE], kn_ref[cols, :])
             + _nt_dot(q_ref[rows, MLA_NOPE:], kr_ref[cols, :]))
        if diag:
            s = jnp.where(_causal(r0, tq - r0, c * ck, ck), s, NEG_INF)
        _online_softmax_step(s, v_ref[cols, :], m_ref, l_ref, acc_ref, rows)

    @pl.when(j < i)
    def _():
        for c in range(nc):
            chunk(c, 0, False)

    @pl.when(j == i)
    def _():
        for c in range(nc):
            chunk(c, c * ck, True)
        o_ref[...] = (acc_ref[...] / l_ref[...]).astype(o_ref.dtype)


def mla_attention(q, kv, kr, *, tile=1024, chunk=512):
    S = q.shape[0]
    H = MLA_HEADS
    tq = _tile(S, tile)
    ck = _tile(tq, chunk)
    qi, kj = _pair_tables(S // tq)
    grid_spec = pltpu.PrefetchScalarGridSpec(
        num_scalar_prefetch=2,
        grid=(H, qi.shape[0]),
        in_specs=[pl.BlockSpec((tq, 2 * LANES), lambda h, p, qi, kj: (qi[p], h)),
                  pl.BlockSpec((tq, MLA_NOPE), lambda h, p, qi, kj: (kj[p], 2 * h)),
                  pl.BlockSpec((tq, LANES), lambda h, p, qi, kj: (kj[p], 0)),
                  pl.BlockSpec((tq, MLA_V), lambda h, p, qi, kj: (kj[p], 2 * h + 1))],
        out_specs=pl.BlockSpec((tq, MLA_V), lambda h, p, qi, kj: (qi[p], h)),
        scratch_shapes=[pltpu.VMEM((tq, 1), F32), pltpu.VMEM((tq, 1), F32), pltpu.VMEM((tq, MLA_V), F32)],
    )
    return pl.pallas_call(
        functools.partial(_mla_attn_kernel, tq=tq, ck=ck),
        out_shape=jax.ShapeDtypeStruct((S, H * MLA_V), BF16),
        grid_spec=grid_spec,
        compiler_params=_cparams("parallel", "arbitrary"),
        name="mla_attention",
    )(qi, kj, q, kv, kr, kv)


def _router_kernel(x_ref, g_ref, wh_ref, wl_ref, xn_ref, r_ref):
    xn = _rms_rows(x_ref[...], g_ref[...])
    hi = xn.astype(BF16)
    lo = (xn - hi.astype(F32)).astype(BF16)
    xn_ref[...] = hi
    logits = (jnp.dot(hi, wh_ref[...], preferred_element_type=F32)
              + jnp.dot(hi, wl_ref[...], preferred_element_type=F32)
              + jnp.dot(lo, wh_ref[...], preferred_element_type=F32))
    lane = lax.broadcasted_iota(jnp.int32, logits.shape, 1)
    lane_f = lane.astype(F32)
    g = jnp.where(lane < N_EXPERTS, logits, NEG_INF)
    v1 = jnp.max(g, axis=-1, keepdims=True)
    i1 = jnp.min(jnp.where(g == v1, lane_f, float(LANES)), axis=-1, keepdims=True)
    g = jnp.where(lane_f == i1, NEG_INF, g)
    v2 = jnp.max(g, axis=-1, keepdims=True)
    i2 = jnp.min(jnp.where(g == v2, lane_f, float(LANES)), axis=-1, keepdims=True)
    g2 = 1.0 / (1.0 + jnp.exp(v1 - v2))
    g1 = 1.0 - g2
    r_ref[...] = jnp.where(lane == 0, i1,
                           jnp.where(lane == 1, i2,
                                     jnp.where(lane == 2, g1, jnp.where(lane == 3, g2, 0.0))))


def moe_route(x, g, w_router, *, tm=512):
    M, D = x.shape
    tm = _tile(M, tm)
    wr = jnp.pad(w_router.astype(F32), ((0, 0), (0, LANES - N_EXPERTS)))
    wh = wr.astype(BF16)
    wl = (wr - wh.astype(F32)).astype(BF16)
    return pl.pallas_call(
        _router_kernel,
        out_shape=[jax.ShapeDtypeStruct((M, D), BF16), jax.ShapeDtypeStruct((M, LANES), F32)],
        grid=(M // tm,),
        in_specs=[pl.BlockSpec((tm, D), lambda i: (i, 0)),
                  pl.BlockSpec((1, D), lambda i: (0, 0)),
                  pl.BlockSpec((D, LANES), lambda i: (0, 0)),
                  pl.BlockSpec((D, LANES), lambda i: (0, 0))],
        out_specs=[pl.BlockSpec((tm, D), lambda i: (i, 0)),
                   pl.BlockSpec((tm, LANES), lambda i: (i, 0))],
        compiler_params=_cparams("parallel"),
        name="moe_route",
    )(x, g.reshape(1, D), wh, wl)


def _gmm_swiglu_kernel(te_ref, nu_ref, x_ref, wg_ref, wu_ref, o_ref):
    used = pl.program_id(0) < nu_ref[0]

    @pl.when(used)
    def _():
        x = x_ref[...]
        a = jnp.dot(x, wg_ref[...], preferred_element_type=F32)
        u = jnp.dot(x, wu_ref[...], preferred_element_type=F32)
        o_ref[...] = (a * jax.nn.sigmoid(a) * u).astype(o_ref.dtype)

    @pl.when(jnp.logical_not(used))
    def _():
        o_ref[...] = jnp.zeros(o_ref.shape, o_ref.dtype)


def _gmm_out_kernel(te_ref, nu_ref, h_ref, w_ref, o_ref):
    used = pl.program_id(0) < nu_ref[0]

    @pl.when(used)
    def _():
        o_ref[...] = jnp.dot(h_ref[...], w_ref[...], preferred_element_type=F32)

    @pl.when(jnp.logical_not(used))
    def _():
        o_ref[...] = jnp.zeros(o_ref.shape, o_ref.dtype)


def moe_experts(xs, tile_expert, n_used, w_in, w_out, *, tm, tn=512):
    P, D = xs.shape
    F = w_in.shape[2] // 2
    nt = P // tm
    tn1, tn2 = _tile(F, tn), _tile(D, tn)
    nj1, nj2 = F // tn1, D // tn2

    def wcol(nj, base):
        return lambda t, j, te, nu: (te[t], 0, base + jnp.where(t < nu[0], j, nj - 1))

    h = pl.pallas_call(
        _gmm_swiglu_kernel,
        out_shape=jax.ShapeDtypeStruct((P, F), BF16),
        grid_spec=pltpu.PrefetchScalarGridSpec(
            num_scalar_prefetch=2,
            grid=(nt, nj1),
            in_specs=[pl.BlockSpec((tm, D), lambda t, j, te, nu: (t, 0)),
                      pl.BlockSpec((None, D, tn1), wcol(nj1, 0)),
                      pl.BlockSpec((None, D, tn1), wcol(nj1, nj1))],
            out_specs=pl.BlockSpec((tm, tn1), lambda t, j, te, nu: (t, j)),
        ),
        compiler_params=_cparams("parallel", "parallel"),
        name="moe_swiglu_in",
    )(tile_expert, n_used, xs, w_in, w_in)
    return pl.pallas_call(
        _gmm_out_kernel,
        out_shape=jax.ShapeDtypeStruct((P, D), F32),
        grid_spec=pltpu.PrefetchScalarGridSpec(
            num_scalar_prefetch=2,
            grid=(nt, nj2),
            in_specs=[pl.BlockSpec((tm, F), lambda t, j, te, nu: (t, 0)),
                      pl.BlockSpec((None, F, tn2), wcol(nj2, 0))],
            out_specs=pl.BlockSpec((tm, tn2), lambda t, j, te, nu: (t, j)),
        ),
        compiler_params=_cparams("parallel", "parallel"),
        name="moe_out",
    )(tile_expert, n_used, h, w_out)


def moe_swiglu(x, g, w_router, w_in, w_out, *, tm=512):
    S, D = x.shape
    E = N_EXPERTS
    tm = _tile(S, tm)
    xn, route = moe_route(x, g, w_router)
    experts = route[:, :2].astype(jnp.int32)
    gates = route[:, 2:4]
    flat_e = experts.reshape(-1)
    flat_tok = jnp.repeat(jnp.arange(S, dtype=jnp.int32), 2)
    order = jnp.argsort(flat_e, stable=True)
    counts = jnp.bincount(flat_e, length=E).astype(jnp.int32)
    padded = ((counts + tm - 1) // tm) * tm
    start = jnp.cumsum(counts) - counts
    pstart = jnp.cumsum(padded) - padded
    sorted_e = flat_e[order]
    dest = pstart[sorted_e] + (jnp.arange(2 * S, dtype=jnp.int32) - start[sorted_e])
    P = 2 * S + E * tm
    nt = P // tm
    row_token = jnp.zeros((P,), jnp.int32).at[dest].set(flat_tok[order])
    slot = jnp.zeros((2 * S,), jnp.int32).at[order].set(dest).reshape(S, 2)
    pend = jnp.cumsum(padded)
    tile_expert = jnp.minimum(
        jnp.searchsorted(pend, jnp.arange(nt, dtype=jnp.int32) * tm, side="right"), E - 1).astype(jnp.int32)
    n_used = (pend[-1:] // tm).astype(jnp.int32)
    xs = jnp.take(xn, row_token, axis=0)
    ys = moe_experts(xs, tile_expert, n_used, w_in, w_out, tm=tm)
    return x + gates[:, 0:1] * jnp.take(ys, slot[:, 0], axis=0) + gates[:, 1:2] * jnp.take(ys, slot[:, 1], axis=0)


def _diff_lambda_init(layer):
    return 0.8 - 0.6 * math.exp(-0.3 * layer)


def _rope_table(pos):
    half = MLA_ROPE // 2
    inv_freq = ROPE_THETA ** (-jnp.arange(half, dtype=F32) / half)
    ang = pos.astype(F32)[:, None] * inv_freq
    cos, sin = jnp.cos(ang), jnp.sin(ang)
    return jnp.concatenate([cos, cos, -sin, sin], axis=1)


def _rot_half_cols(w):
    half = w.shape[-1] // 2
    return jnp.concatenate([w[..., half:], w[..., :half]], axis=-1)


def kernel(x, p, positions, rel_bias, norm_mix, norm_ffn, norm_ple, norm_final, diff_w_qkv, diff_lambda, diff_subln, diff_w_o, moba_w_qkv, moba_w_o, mla_w_down, mla_g_q, mla_w_uq, mla_g_kv, mla_w_ukv, mla_w_o, ffn_w_in, ffn_w_out, moe_w_router, moe_w_in, moe_w_out, ple_w_gate, ple_w_proj):
    B, S, D = x.shape
    depth = p.shape[0]
    lut = _bias_lut(rel_bias)
    outs = []
    for b in range(B):
        xb = x[b]
        pos = positions[b].astype(jnp.int32)
        for i in range(depth):
            jm = i // N_MIXERS
            if i % N_MIXERS == 0:
                qkv = norm_matmul(xb, norm_mix[i], diff_w_qkv[jm].astype(BF16), out_dtype=BF16,
                                  scaled_cols=D, scale=DIFF_HEAD_DIM ** -0.5 * LOG2E)
                o = diff_attention(qkv, pos, lut, diff_lambda[jm].astype(F32), diff_subln[jm],
                                   _diff_lambda_init(i))
                xb = matmul_residual(o, diff_w_o[jm].astype(BF16), xb)
            elif i % N_MIXERS == 1:
                qkv = norm_matmul(xb, norm_mix[i], moba_w_qkv[jm].astype(BF16), out_dtype=BF16,
                                  scaled_cols=D, scale=MOBA_HEAD_DIM ** -0.5 * LOG2E)
                o = moba_attention(qkv, pos, lut)
                xb = matmul_residual(o, moba_w_o[jm].astype(BF16), xb)
            else:
                wd = mla_w_down[jm]
                kr0 = MLA_Q_RANK + MLA_KV_RANK
                wd_ext = jnp.concatenate([wd, _rot_half_cols(wd[:, kr0:])], axis=1).astype(BF16)
                wq = mla_w_uq[jm].reshape(MLA_Q_RANK, MLA_HEADS, MLA_NOPE + MLA_ROPE)
                wq_ext = jnp.concatenate([wq, _rot_half_cols(wq[..., MLA_NOPE:])], axis=-1)
                wq_ext = wq_ext.reshape(MLA_Q_RANK, MLA_HEADS * 2 * LANES).astype(BF16)
                down = norm_matmul(xb, norm_mix[i], wd_ext, out_dtype=F32, tn=wd_ext.shape[1])
                q, kr = mla_queries_and_rope_key(down, mla_g_q[jm], wq_ext, _rope_table(pos),
                                                 scale=(MLA_NOPE + MLA_ROPE) ** -0.5 * LOG2E)
                kv = norm_matmul(down, mla_g_kv[jm], mla_w_ukv[jm].astype(BF16), out_dtype=BF16, xcol=1)
                o = mla_attention(q, kv, kr)
                xb = matmul_residual(o, mla_w_o[jm].astype(BF16), xb)
            if i % 2 == 0:
                h = norm_swiglu_in(xb, norm_ffn[i], ffn_w_in[i // 2].astype(BF16))
                xb = matmul_residual(h, ffn_w_out[i // 2].astype(BF16), xb)
            else:
                xb = moe_swiglu(xb, norm_ffn[i], moe_w_router[i // 2],
                                moe_w_in[i // 2].astype(BF16), moe_w_out[i // 2].astype(BF16))
            xb = ple_update(xb, norm_ple[i], p[i, b], ple_w_gate[i].astype(BF16), ple_w_proj[i].astype(BF16))
        outs.append(final_norm(xb, norm_final))
    return jnp.stack(outs, axis=0)
```

```python
import functools
import math

import numpy as np
import jax
import jax.numpy as jnp
from jax import lax
from jax.experimental import pallas as pl
from jax.experimental.pallas import tpu as pltpu

F32 = jnp.float32
BF16 = jnp.bfloat16

NORM_EPS = 1e-6
NEG_INF = -1e30
LOG2E = math.log2(math.e)
LANES = 128
VMEM_LIMIT_BYTES = 56 * 1024 * 1024

REL_BUCKETS = 32
REL_MAX_DIST = 128
LUT_SIZE = LANES
DIFF_HEADS = 8
DIFF_HEAD_DIM = 128
MOBA_HEADS = 16
MOBA_HEAD_DIM = 128
MOBA_BLOCK = 256
MOBA_TOPK = 3
MLA_HEADS = 16
MLA_Q_RANK = 512
MLA_KV_RANK = 512
MLA_NOPE = 128
MLA_ROPE = 64
MLA_V = 128
ROPE_THETA = 10000.0
N_EXPERTS = 8
N_MIXERS = 3


def _cparams(*sem):
    return pltpu.CompilerParams(dimension_semantics=sem, vmem_limit_bytes=VMEM_LIMIT_BYTES)


def _tile(n, pref):
    if n <= pref:
        return n
    t = pref
    while n % t:
        t //= 2
    return t


def _rms_rows(x, g):
    r = lax.rsqrt(jnp.mean(x * x, axis=-1, keepdims=True) + NORM_EPS)
    return x * r * g


def _norm_mm_kernel(x_ref, g_ref, w_ref, o_ref, xn_ref, *, scaled_tiles, scale):
    j = pl.program_id(1)

    @pl.when(j == 0)
    def _():
        xn_ref[...] = _rms_rows(x_ref[...], g_ref[...]).astype(BF16)

    acc = jnp.dot(xn_ref[...], w_ref[...], preferred_element_type=F32)
    if scaled_tiles:
        acc = acc * jnp.where(j < scaled_tiles, scale, 1.0)
    o_ref[...] = acc.astype(o_ref.dtype)


def norm_matmul(x, g, w, *, out_dtype, xcol=0, tm=512, tn=512, scaled_cols=0, scale=1.0):
    M = x.shape[0]
    K, N = w.shape
    tm, tn = _tile(M, tm), _tile(N, tn)
    assert scaled_cols % tn == 0
    kern = functools.partial(_norm_mm_kernel, scaled_tiles=scaled_cols // tn, scale=scale)
    return pl.pallas_call(
        kern,
        out_shape=jax.ShapeDtypeStruct((M, N), out_dtype),
        grid=(M // tm, N // tn),
        in_specs=[pl.BlockSpec((tm, K), lambda i, j: (i, xcol)),
                  pl.BlockSpec((1, K), lambda i, j: (0, 0)),
                  pl.BlockSpec((K, tn), lambda i, j: (0, j))],
        out_specs=pl.BlockSpec((tm, tn), lambda i, j: (i, j)),
        scratch_shapes=[pltpu.VMEM((tm, K), BF16)],
        compiler_params=_cparams("parallel", "arbitrary"),
        name="norm_matmul",
    )(x, g.reshape(1, K), w)


def _norm_swiglu_kernel(x_ref, g_ref, wg_ref, wu_ref, o_ref, xn_ref):
    @pl.when(pl.program_id(1) == 0)
    def _():
        xn_ref[...] = _rms_rows(x_ref[...], g_ref[...]).astype(BF16)

    xn = xn_ref[...]
    a = jnp.dot(xn, wg_ref[...], preferred_element_type=F32)
    u = jnp.dot(xn, wu_ref[...], preferred_element_type=F32)
    o_ref[...] = (a * jax.nn.sigmoid(a) * u).astype(o_ref.dtype)


def norm_swiglu_in(x, g, w_in, *, tm=512, tn=512):
    M, K = x.shape
    F = w_in.shape[1] // 2
    tm, tn = _tile(M, tm), _tile(F, tn)
    nj = F // tn
    return pl.pallas_call(
        _norm_swiglu_kernel,
        out_shape=jax.ShapeDtypeStruct((M, F), BF16),
        grid=(M // tm, nj),
        in_specs=[pl.BlockSpec((tm, K), lambda i, j: (i, 0)),
                  pl.BlockSpec((1, K), lambda i, j: (0, 0)),
                  pl.BlockSpec((K, tn), lambda i, j: (0, j)),
                  pl.BlockSpec((K, tn), lambda i, j: (0, j + nj))],
        out_specs=pl.BlockSpec((tm, tn), lambda i, j: (i, j)),
        scratch_shapes=[pltpu.VMEM((tm, K), BF16)],
        compiler_params=_cparams("parallel", "arbitrary"),
        name="norm_swiglu_in",
    )(x, g.reshape(1, K), w_in, w_in)


def _mm_res_kernel(a_ref, w_ref, r_ref, o_ref):
    o_ref[...] = r_ref[...] + jnp.dot(a_ref[...], w_ref[...], preferred_element_type=F32)


def matmul_residual(a, w, res, *, tm=512, tn=512):
    M, K = a.shape
    N = w.shape[1]
    tm, tn = _tile(M, tm), _tile(N, tn)
    return pl.pallas_call(
        _mm_res_kernel,
        out_shape=jax.ShapeDtypeStruct((M, N), F32),
        grid=(M // tm, N // tn),
        in_specs=[pl.BlockSpec((tm, K), lambda i, j: (i, 0)),
                  pl.BlockSpec((K, tn), lambda i, j: (0, j)),
                  pl.BlockSpec((tm, tn), lambda i, j: (i, j))],
        out_specs=pl.BlockSpec((tm, tn), lambda i, j: (i, j)),
        compiler_params=_cparams("parallel", "parallel"),
        name="matmul_residual",
    )(a, w, res)


def _ple_kernel(x_ref, g_ref, xr_ref, p_ref, wg_ref, wp_ref, o_ref, xn_ref):
    @pl.when(pl.program_id(1) == 0)
    def _():
        xn_ref[...] = _rms_rows(x_ref[...], g_ref[...]).astype(BF16)

    gate = jax.nn.sigmoid(jnp.dot(xn_ref[...], wg_ref[...], preferred_element_type=F32))
    proj = jnp.dot(p_ref[...].astype(BF16), wp_ref[...], preferred_element_type=F32)
    o_ref[...] = xr_ref[...] + gate * proj


def ple_update(x, g, p, w_gate, w_proj, *, tm=512, tn=512):
    M, D = x.shape
    P = p.shape[1]
    tm, tn = _tile(M, tm), _tile(D, tn)
    return pl.pallas_call(
        _ple_kernel,
        out_shape=jax.ShapeDtypeStruct((M, D), F32),
        grid=(M // tm, D // tn),
        in_specs=[pl.BlockSpec((tm, D), lambda i, j: (i, 0)),
                  pl.BlockSpec((1, D), lambda i, j: (0, 0)),
                  pl.BlockSpec((tm, tn), lambda i, j: (i, j)),
                  pl.BlockSpec((tm, P), lambda i, j: (i, 0)),
                  pl.BlockSpec((D, tn), lambda i, j: (0, j)),
                  pl.BlockSpec((P, tn), lambda i, j: (0, j))],
        out_specs=pl.BlockSpec((tm, tn), lambda i, j: (i, j)),
        scratch_shapes=[pltpu.VMEM((tm, D), BF16)],
        compiler_params=_cparams("parallel", "arbitrary"),
        name="ple_update",
    )(x, g.reshape(1, D), x, p, w_gate, w_proj)


def _final_norm_kernel(x_ref, g_ref, o_ref):
    o_ref[...] = _rms_rows(x_ref[...], g_ref[...])


def final_norm(x, g, *, tm=512):
    M, D = x.shape
    tm = _tile(M, tm)
    return pl.pallas_call(
        _final_norm_kernel,
        out_shape=jax.ShapeDtypeStruct((M, D), F32),
        grid=(M // tm,),
        in_specs=[pl.BlockSpec((tm, D), lambda i: (i, 0)),
                  pl.BlockSpec((1, D), lambda i: (0, 0))],
        out_specs=pl.BlockSpec((tm, D), lambda i: (i, 0)),
        compiler_params=_cparams("parallel"),
        name="final_norm",
    )(x, g.reshape(1, D))


def _pair_tables(n_tiles):
    qi, kj = [], []
    for i in range(n_tiles):
        for j in range(i + 1):
            qi.append(i)
            kj.append(j)
    return jnp.asarray(np.array(qi, np.int32)), jnp.asarray(np.array(kj, np.int32))


def _rel_bucket(dist):
    n = jnp.maximum(dist, 0)
    max_exact = REL_BUCKETS // 2
    nf = jnp.maximum(n, 1).astype(F32)
    large = max_exact + (jnp.log(nf / max_exact) / math.log(REL_MAX_DIST / max_exact)
                         * (REL_BUCKETS - max_exact)).astype(jnp.int32)
    large = jnp.minimum(large, REL_BUCKETS - 1)
    return jnp.where(n < max_exact, n, large)


def _bias_lut(rel_bias):
    buckets = _rel_bucket(jnp.arange(LUT_SIZE, dtype=jnp.int32))
    return rel_bias.astype(F32)[buckets].T * LOG2E


def _tile_pos_bounds(pos, tq, ck):
    S = pos.shape[0]
    return pos.reshape(S // tq, tq).min(axis=1), pos.reshape(S // ck, ck).max(axis=1)


def _lut_bias(lut_row, idx):
    R, C = idx.shape
    lut = jnp.broadcast_to(lut_row, (R, LANES))
    parts = [jnp.take_along_axis(lut, idx[:, c:c + LANES], axis=1) for c in range(0, C, LANES)]
    return parts[0] if len(parts) == 1 else jnp.concatenate(parts, axis=1)


def _pipelined(units, scores, consume):
    st = scores(*units[0])
    for k, u in enumerate(units):
        nxt = scores(*units[k + 1]) if k + 1 < len(units) else None
        consume(*u, st)
        st = nxt


def _online_softmax_step_t(st, vt, m_ref, l_ref, acc_ref, qcols):
    m_prev = m_ref[:, qcols]
    m_new = jnp.maximum(m_prev, jnp.max(st, axis=0, keepdims=True))
    alpha = jnp.exp2(m_prev - m_new)
    p = jnp.exp2(st - m_new)
    l_ref[:, qcols] = alpha * l_ref[:, qcols] + jnp.sum(p, axis=0, keepdims=True)
    acc_ref[:, qcols] = alpha * acc_ref[:, qcols] + jnp.dot(vt, p.astype(BF16), preferred_element_type=F32)
    m_ref[:, qcols] = m_new


def _causal_t(k0, nk, q0, nq):
    r = k0 + lax.broadcasted_iota(jnp.int32, (nk, nq), 0)
    c = q0 + lax.broadcasted_iota(jnp.int32, (nk, nq), 1)
    return r <= c


def _nt_dot(a, b):
    return lax.dot_general(a, b, (((1,), (1,)), ((), ())), preferred_element_type=F32)


def _diff_attn_kernel(qi_ref, kj_ref, pminq_ref, pmaxk_ref,
                      q_ref, k_ref, vt_ref, pq_ref, pk_ref, lut_ref, lam_ref, g_ref,
                      o_ref, m_ref, l_ref, acc_ref, *, tq, ck, lambda_init):
    p = pl.program_id(1)
    i, j = qi_ref[p], kj_ref[p]
    nc = tq // ck
    dh = DIFF_HEAD_DIM

    @pl.when(j == 0)
    def _():
        m_ref[...] = jnp.full(m_ref.shape, NEG_INF, F32)
        l_ref[...] = jnp.zeros(l_ref.shape, F32)
        acc_ref[...] = jnp.zeros(acc_ref.shape, F32)

    def scores(c, mp, q0, mode):
        keys = slice(c * ck, (c + 1) * ck)
        st = _nt_dot(k_ref[keys, mp * dh:(mp + 1) * dh], q_ref[q0:tq, mp * dh:(mp + 1) * dh])
        if mode == "far":
            st = st + lut_ref[mp:mp + 1, LUT_SIZE - 1:LUT_SIZE]
        else:
            idx = jnp.clip(pq_ref[:, q0:tq] - pk_ref[keys, :], 0, LUT_SIZE - 1)
            st = st + _lut_bias(lut_ref[mp:mp + 1, :], idx)
        if mode == "diag":
            st = jnp.where(_causal_t(c * ck, ck, q0, tq - q0), st, NEG_INF)
        return st

    def consume(c, mp, q0, mode, st):
        _online_softmax_step_t(st, vt_ref[:, c * ck:(c + 1) * ck], m_ref.at[mp], l_ref.at[mp], acc_ref.at[mp],
                               slice(q0, tq))

    def sweep(chunks, mode):
        units = [(c, mp, c * ck if mode == "diag" else 0, mode) for c in chunks for mp in range(2)]
        _pipelined(units, scores, consume)

    @pl.when(j < i)
    def _():
        fars = [pminq_ref[i] - pmaxk_ref[j * nc + c] >= LUT_SIZE - 1 for c in range(nc)]
        all_far = functools.reduce(jnp.logical_and, fars)
        pl.when(all_far)(functools.partial(sweep, range(nc), "far"))

        @pl.when(jnp.logical_not(all_far))
        def _():
            for c in range(nc):
                pl.when(fars[c])(functools.partial(sweep, [c], "far"))
                pl.when(jnp.logical_not(fars[c]))(functools.partial(sweep, [c], "near"))

    @pl.when(j == i)
    def _():
        sweep(range(nc), "diag")
        lam = lam_ref[...]
        lam_full = (jnp.exp(jnp.sum(lam[0:1] * lam[1:2], axis=-1, keepdims=True))
                    - jnp.exp(jnp.sum(lam[2:3] * lam[3:4], axis=-1, keepdims=True)) + lambda_init)
        a = acc_ref[0] / l_ref[0] - lam_full * (acc_ref[1] / l_ref[1])
        r = lax.rsqrt(jnp.mean(a * a, axis=0, keepdims=True) + NORM_EPS)
        y = a * r * (g_ref[...] * (1.0 - lambda_init))
        o_ref[...] = y.T.astype(o_ref.dtype)


def diff_attention(qkv, vt, pos, lut, lam, subln_g, lambda_init, *, tile=1024, chunk=512):
    S = qkv.shape[0]
    H, dh = DIFF_HEADS, DIFF_HEAD_DIM
    tq = _tile(S, tile)
    ck = _tile(tq, chunk)
    qi, kj = _pair_tables(S // tq)
    pminq, pmaxk = _tile_pos_bounds(pos, tq, ck)
    kern = functools.partial(_diff_attn_kernel, tq=tq, ck=ck, lambda_init=lambda_init)
    grid_spec = pltpu.PrefetchScalarGridSpec(
        num_scalar_prefetch=4,
        grid=(H, qi.shape[0]),
        in_specs=[pl.BlockSpec((tq, 2 * dh), lambda h, p, qi, kj, a, b: (qi[p], h)),
                  pl.BlockSpec((tq, 2 * dh), lambda h, p, qi, kj, a, b: (kj[p], H + h)),
                  pl.BlockSpec((2 * dh, tq), lambda h, p, qi, kj, a, b: (h, kj[p])),
                  pl.BlockSpec((1, tq), lambda h, p, qi, kj, a, b: (0, qi[p])),
                  pl.BlockSpec((tq, 1), lambda h, p, qi, kj, a, b: (kj[p], 0)),
                  pl.BlockSpec((None, 2, LUT_SIZE), lambda h, p, qi, kj, a, b: (h, 0, 0)),
                  pl.BlockSpec((4, dh), lambda h, p, qi, kj, a, b: (0, 0)),
                  pl.BlockSpec((2 * dh, 1), lambda h, p, qi, kj, a, b: (0, 0))],
        out_specs=pl.BlockSpec((tq, 2 * dh), lambda h, p, qi, kj, a, b: (qi[p], h)),
        scratch_shapes=[pltpu.VMEM((2, 1, tq), F32), pltpu.VMEM((2, 1, tq), F32),
                        pltpu.VMEM((2, 2 * dh, tq), F32)],
    )
    return pl.pallas_call(
        kern,
        out_shape=jax.ShapeDtypeStruct((S, H * 2 * dh), BF16),
        grid_spec=grid_spec,
        compiler_params=_cparams("parallel", "arbitrary"),
        name="diff_attention",
    )(qi, kj, pminq, pmaxk, qkv, qkv, vt, pos.reshape(1, S), pos.reshape(S, 1),
      lut.reshape(H, 2, LUT_SIZE), lam, subln_g.reshape(2 * dh, 1))


def _kmean_kernel(k_ref, hi_ref, lo_ref):
    km = jnp.mean(k_ref[...].astype(F32), axis=0, keepdims=True)
    hi = km.astype(BF16)
    hi_ref[...] = hi
    lo_ref[...] = (km - hi.astype(F32)).astype(BF16)


def moba_block_means(qkv, D):
    S = qkv.shape[0]
    nblk = S // MOBA_BLOCK
    hi, lo = pl.pallas_call(
        _kmean_kernel,
        out_shape=[jax.ShapeDtypeStruct((nblk, 1, D), BF16)] * 2,
        grid=(nblk,),
        in_specs=[pl.BlockSpec((MOBA_BLOCK, D), lambda n: (n, 1))],
        out_specs=[pl.BlockSpec((None, 1, D), lambda n: (n, 0, 0))] * 2,
        compiler_params=_cparams("parallel"),
        name="moba_block_means",
    )(qkv)
    return hi.reshape(nblk, D), lo.reshape(nblk, D)


def _moba_attn_kernel(qi_ref, kj_ref, pminq_ref, pmaxk_ref,
                      q_ref, k_ref, vt_ref, kmh_ref, kml_ref, pq_ref, pk_ref, lut_ref,
                      o_ref, m_ref, l_ref, acc_ref, sel_ref, *, tq, ck):
    p = pl.program_id(1)
    i, j = qi_ref[p], kj_ref[p]
    nc = tq // ck
    L = MOBA_BLOCK
    bpc = ck // L
    log2_l = L.bit_length() - 1

    @pl.when(j == 0)
    def _():
        m_ref[...] = jnp.full(m_ref.shape, NEG_INF, F32)
        l_ref[...] = jnp.zeros(l_ref.shape, F32)
        acc_ref[...] = jnp.zeros(acc_ref.shape, F32)
        q = q_ref[...]
        gs = _nt_dot(kmh_ref[...], q) + _nt_dot(kml_ref[...], q)
        blk = lax.broadcasted_iota(jnp.int32, (LANES, tq), 0)
        blk_f = blk.astype(F32)
        own = lax.shift_right_logical(i * tq + lax.broadcasted_iota(jnp.int32, (1, tq), 1), log2_l)
        g = jnp.where(blk < own, gs, NEG_INF)
        sel = jnp.zeros((LANES, tq), F32)
        for _ in range(MOBA_TOPK):
            best = jnp.max(g, axis=0, keepdims=True)
            first = jnp.min(jnp.where(g == best, blk_f, float(LANES)), axis=0, keepdims=True)
            pick = blk_f == first
            sel = jnp.where(pick, jnp.where(best > 0.5 * NEG_INF, 1.0, sel), sel)
            g = jnp.where(pick, -3e38, g)
        sel_ref[...] = sel

    def scores(c, q0, mode):
        keys = slice(c * ck, (c + 1) * ck)
        nq = tq - q0
        st = _nt_dot(k_ref[keys, :], q_ref[q0:tq, :])
        if mode == "far":
            st = st + lut_ref[:, LUT_SIZE - 1:LUT_SIZE]
        else:
            idx = jnp.clip(pq_ref[:, q0:tq] - pk_ref[keys, :], 0, LUT_SIZE - 1)
            st = st + _lut_bias(lut_ref[...], idx)
        allowed = []
        for b in range(bpc):
            n_local = c * bpc + b
            picked = jnp.broadcast_to(sel_ref[pl.ds(j * (tq // L) + n_local, 1), q0:tq], (L, nq))
            if mode == "diag":
                own_local = lax.shift_right_logical(q0 + lax.broadcasted_iota(jnp.int32, (L, nq), 1), log2_l)
                visible = jnp.where(_causal_t(n_local * L, L, q0, nq), 1.0, 0.0)
                picked = jnp.where(own_local == n_local, visible, picked)
            allowed.append(picked)
        allowed = allowed[0] if bpc == 1 else jnp.concatenate(allowed, axis=0)
        return jnp.where(allowed > 0.0, st, NEG_INF)

    def consume(c, q0, mode, st):
        _online_softmax_step_t(st, vt_ref[:, c * ck:(c + 1) * ck], m_ref, l_ref, acc_ref, slice(q0, tq))

    def sweep(chunks, mode):
        _pipelined([(c, c * ck if mode == "diag" else 0, mode) for c in chunks], scores, consume)

    @pl.when(j < i)
    def _():
        fars = [pminq_ref[i] - pmaxk_ref[j * nc + c] >= LUT_SIZE - 1 for c in range(nc)]
        all_far = functools.reduce(jnp.logical_and, fars)
        pl.when(all_far)(functools.partial(sweep, range(nc), "far"))

        @pl.when(jnp.logical_not(all_far))
        def _():
            for c in range(nc):
                pl.when(fars[c])(functools.partial(sweep, [c], "far"))
                pl.when(jnp.logical_not(fars[c]))(functools.partial(sweep, [c], "near"))

    @pl.when(j == i)
    def _():
        sweep(range(nc), "diag")
        o_ref[...] = (acc_ref[...] / l_ref[...]).T.astype(o_ref.dtype)


def moba_attention(qkv, vt, pos, lut, *, tile=1024, chunk=512):
    S = qkv.shape[0]
    H, dh, L = MOBA_HEADS, MOBA_HEAD_DIM, MOBA_BLOCK
    D = H * dh
    nblk = S // L
    assert S % L == 0 and nblk <= LANES
    tq = _tile(S, tile)
    ck = _tile(tq, chunk)
    assert tq % L == 0 and ck % L == 0
    km_hi, km_lo = moba_block_means(qkv, D)
    km_hi = jnp.pad(km_hi, ((0, LANES - nblk), (0, 0)))
    km_lo = jnp.pad(km_lo, ((0, LANES - nblk), (0, 0)))
    qi, kj = _pair_tables(S // tq)
    pminq, pmaxk = _tile_pos_bounds(pos, tq, ck)
    kern = functools.partial(_moba_attn_kernel, tq=tq, ck=ck)
    grid_spec = pltpu.PrefetchScalarGridSpec(
        num_scalar_prefetch=4,
        grid=(H, qi.shape[0]),
        in_specs=[pl.BlockSpec((tq, dh), lambda h, p, qi, kj, a, b: (qi[p], h)),
                  pl.BlockSpec((tq, dh), lambda h, p, qi, kj, a, b: (kj[p], H + h)),
                  pl.BlockSpec((dh, tq), lambda h, p, qi, kj, a, b: (h, kj[p])),
                  pl.BlockSpec((LANES, dh), lambda h, p, qi, kj, a, b: (0, h)),
                  pl.BlockSpec((LANES, dh), lambda h, p, qi, kj, a, b: (0, h)),
                  pl.BlockSpec((1, tq), lambda h, p, qi, kj, a, b: (0, qi[p])),
                  pl.BlockSpec((tq, 1), lambda h, p, qi, kj, a, b: (kj[p], 0)),
                  pl.BlockSpec((None, 1, LUT_SIZE), lambda h, p, qi, kj, a, b: (h, 0, 0))],
        out_specs=pl.BlockSpec((tq, dh), lambda h, p, qi, kj, a, b: (qi[p], h)),
        scratch_shapes=[pltpu.VMEM((1, tq), F32), pltpu.VMEM((1, tq), F32),
                        pltpu.VMEM((dh, tq), F32), pltpu.VMEM((LANES, tq), F32)],
    )
    return pl.pallas_call(
        kern,
        out_shape=jax.ShapeDtypeStruct((S, D), BF16),
        grid_spec=grid_spec,
        compiler_params=_cparams("parallel", "arbitrary"),
        name="moba_attention",
    )(qi, kj, pminq, pmaxk, qkv, qkv, vt, km_hi, km_lo, pos.reshape(1, S), pos.reshape(S, 1),
      lut.reshape(H, 1, LUT_SIZE))


def _rope_fold(y):
    lane = lax.broadcasted_iota(jnp.int32, y.shape, 1)
    return jnp.where(lane < MLA_ROPE, y + pltpu.roll(y, MLA_ROPE, 1), 0.0)


def _mla_q_kernel(cq_ref, g_ref, w_ref, kr_ref, t_ref, q_ref, kro_ref, cqn_ref, *, scale):
    t = t_ref[...]

    @pl.when(pl.program_id(1) == 0)
    def _():
        cqn_ref[...] = _rms_rows(cq_ref[...], g_ref[...]).astype(BF16)
        kro_ref[...] = _rope_fold(kr_ref[...] * t).astype(kro_ref.dtype)

    qh = jnp.dot(cqn_ref[...], w_ref[...], preferred_element_type=F32)
    q_ref[:, :MLA_NOPE] = (qh[:, :MLA_NOPE] * scale).astype(q_ref.dtype)
    q_ref[:, MLA_NOPE:] = (_rope_fold(qh[:, MLA_NOPE:] * t) * scale).astype(q_ref.dtype)


def mla_queries_and_rope_key(down, g_q, w_uq_ext, rope_tab, *, scale, tm=512):
    S = down.shape[0]
    H = MLA_HEADS
    tm = _tile(S, tm)
    W = 2 * LANES
    return pl.pallas_call(
        functools.partial(_mla_q_kernel, scale=scale),
        out_shape=[jax.ShapeDtypeStruct((S, H * W), BF16), jax.ShapeDtypeStruct((S, LANES), BF16)],
        grid=(S // tm, H),
        in_specs=[pl.BlockSpec((tm, MLA_Q_RANK), lambda i, h: (i, 0)),
                  pl.BlockSpec((1, MLA_Q_RANK), lambda i, h: (0, 0)),
                  pl.BlockSpec((MLA_Q_RANK, W), lambda i, h: (0, h)),
                  pl.BlockSpec((tm, LANES), lambda i, h: (i, (MLA_Q_RANK + MLA_KV_RANK) // LANES)),
                  pl.BlockSpec((tm, LANES), lambda i, h: (i, 0))],
        out_specs=[pl.BlockSpec((tm, W), lambda i, h: (i, h)),
                   pl.BlockSpec((tm, LANES), lambda i, h: (i, 0))],
        scratch_shapes=[pltpu.VMEM((tm, MLA_Q_RANK), BF16)],
        compiler_params=_cparams("parallel", "arbitrary"),
        name="mla_queries",
    )(down, g_q.reshape(1, MLA_Q_RANK), w_uq_ext, down, rope_tab)


def _mla_attn_kernel(qi_ref, kj_ref, q_ref, kn_ref, kr_ref, vt_ref, o_ref, m_ref, l_ref, acc_ref, *, tq, ck):
    p = pl.program_id(1)
    i, j = qi_ref[p], kj_ref[p]
    nc = tq // ck

    @pl.when(j == 0)
    def _():
        m_ref[...] = jnp.full(m_ref.shape, NEG_INF, F32)
        l_ref[...] = jnp.zeros(l_ref.shape, F32)
        acc_ref[...] = jnp.zeros(acc_ref.shape, F32)

    def scores(c, q0, diag):
        keys = slice(c * ck, (c + 1) * ck)
        kc = jnp.concatenate([kn_ref[keys, :], kr_ref[keys, :]], axis=1)
        st = _nt_dot(kc, q_ref[q0:tq, :])
        if diag:
            st = jnp.where(_causal_t(c * ck, ck, q0, tq - q0), st, NEG_INF)
        return st

    def sweep(diag):
        q0 = (lambda c: c * ck) if diag else (lambda c: 0)
        st = scores(0, q0(0), diag)
        for c in range(nc):
            nxt = scores(c + 1, q0(c + 1), diag) if c + 1 < nc else None
            _online_softmax_step_t(st, vt_ref[:, c * ck:(c + 1) * ck], m_ref, l_ref, acc_ref, slice(q0(c), tq))
            st = nxt

    @pl.when(j < i)
    def _():
        sweep(False)

    @pl.when(j == i)
    def _():
        sweep(True)
        o_ref[...] = (acc_ref[...] / l_ref[...]).T.astype(o_ref.dtype)


def mla_attention(q, kv, kr, vt, *, tile=1024, chunk=512):
    S = q.shape[0]
    H = MLA_HEADS
    tq = _tile(S, tile)
    ck = _tile(tq, chunk)
    qi, kj = _pair_tables(S // tq)
    grid_spec = pltpu.PrefetchScalarGridSpec(
        num_scalar_prefetch=2,
        grid=(H, qi.shape[0]),
        in_specs=[pl.BlockSpec((tq, 2 * LANES), lambda h, p, qi, kj: (qi[p], h)),
                  pl.BlockSpec((tq, MLA_NOPE), lambda h, p, qi, kj: (kj[p], 2 * h)),
                  pl.BlockSpec((tq, LANES), lambda h, p, qi, kj: (kj[p], 0)),
                  pl.BlockSpec((MLA_V, tq), lambda h, p, qi, kj: (h, kj[p]))],
        out_specs=pl.BlockSpec((tq, MLA_V), lambda h, p, qi, kj: (qi[p], h)),
        scratch_shapes=[pltpu.VMEM((1, tq), F32), pltpu.VMEM((1, tq), F32), pltpu.VMEM((MLA_V, tq), F32)],
    )
    return pl.pallas_call(
        functools.partial(_mla_attn_kernel, tq=tq, ck=ck),
        out_shape=jax.ShapeDtypeStruct((S, H * MLA_V), BF16),
        grid_spec=grid_spec,
        compiler_params=_cparams("parallel", "arbitrary"),
        name="mla_attention",
    )(qi, kj, q, kv, kr, vt)


def _router_kernel(x_ref, g_ref, wh_ref, wl_ref, xn_ref, r_ref):
    xn = _rms_rows(x_ref[...], g_ref[...])
    hi = xn.astype(BF16)
    lo = (xn - hi.astype(F32)).astype(BF16)
    xn_ref[...] = hi
    logits = (jnp.dot(hi, wh_ref[...], preferred_element_type=F32)
              + jnp.dot(hi, wl_ref[...], preferred_element_type=F32)
              + jnp.dot(lo, wh_ref[...], preferred_element_type=F32))
    lane = lax.broadcasted_iota(jnp.int32, logits.shape, 1)
    lane_f = lane.astype(F32)
    g = jnp.where(lane < N_EXPERTS, logits, NEG_INF)
    v1 = jnp.max(g, axis=-1, keepdims=True)
    i1 = jnp.min(jnp.where(g == v1, lane_f, float(LANES)), axis=-1, keepdims=True)
    g = jnp.where(lane_f == i1, NEG_INF, g)
    v2 = jnp.max(g, axis=-1, keepdims=True)
    i2 = jnp.min(jnp.where(g == v2, lane_f, float(LANES)), axis=-1, keepdims=True)
    g2 = 1.0 / (1.0 + jnp.exp(v1 - v2))
    g1 = 1.0 - g2
    r_ref[...] = jnp.where(lane == 0, i1,
                           jnp.where(lane == 1, i2,
                                     jnp.where(lane == 2, g1, jnp.where(lane == 3, g2, 0.0))))


def moe_route(x, g, w_router, *, tm=512):
    M, D = x.shape
    tm = _tile(M, tm)
    wr = jnp.pad(w_router.astype(F32), ((0, 0), (0, LANES - N_EXPERTS)))
    wh = wr.astype(BF16)
    wl = (wr - wh.astype(F32)).astype(BF16)
    return pl.pallas_call(
        _router_kernel,
        out_shape=[jax.ShapeDtypeStruct((M, D), BF16), jax.ShapeDtypeStruct((M, LANES), F32)],
        grid=(M // tm,),
        in_specs=[pl.BlockSpec((tm, D), lambda i: (i, 0)),
                  pl.BlockSpec((1, D), lambda i: (0, 0)),
                  pl.BlockSpec((D, LANES), lambda i: (0, 0)),
                  pl.BlockSpec((D, LANES), lambda i: (0, 0))],
        out_specs=[pl.BlockSpec((tm, D), lambda i: (i, 0)),
                   pl.BlockSpec((tm, LANES), lambda i: (i, 0))],
        compiler_params=_cparams("parallel"),
        name="moe_route",
    )(x, g.reshape(1, D), wh, wl)


def _gmm_swiglu_kernel(te_ref, nu_ref, x_ref, wg_ref, wu_ref, o_ref):
    used = pl.program_id(0) < nu_ref[0]

    @pl.when(used)
    def _():
        x = x_ref[...]
        a = jnp.dot(x, wg_ref[...], preferred_element_type=F32)
        u = jnp.dot(x, wu_ref[...], preferred_element_type=F32)
        o_ref[...] = (a * jax.nn.sigmoid(a) * u).astype(o_ref.dtype)

    @pl.when(jnp.logical_not(used))
    def _():
        o_ref[...] = jnp.zeros(o_ref.shape, o_ref.dtype)


def _gmm_out_kernel(te_ref, nu_ref, h_ref, w_ref, o_ref):
    used = pl.program_id(0) < nu_ref[0]

    @pl.when(used)
    def _():
        o_ref[...] = jnp.dot(h_ref[...], w_ref[...], preferred_element_type=F32)

    @pl.when(jnp.logical_not(used))
    def _():
        o_ref[...] = jnp.zeros(o_ref.shape, o_ref.dtype)


def moe_experts(xs, tile_expert, n_used, w_in, w_out, *, tm, tn=512):
    P, D = xs.shape
    F = w_in.shape[2] // 2
    nt = P // tm
    tn1, tn2 = _tile(F, tn), _tile(D, tn)
    nj1, nj2 = F // tn1, D // tn2

    def wcol(nj, base):
        return lambda t, j, te, nu: (te[t], 0, base + jnp.where(t < nu[0], j, nj - 1))

    h = pl.pallas_call(
        _gmm_swiglu_kernel,
        out_shape=jax.ShapeDtypeStruct((P, F), BF16),
        grid_spec=pltpu.PrefetchScalarGridSpec(
            num_scalar_prefetch=2,
            grid=(nt, nj1),
            in_specs=[pl.BlockSpec((tm, D), lambda t, j, te, nu: (t, 0)),
                      pl.BlockSpec((None, D, tn1), wcol(nj1, 0)),
                      pl.BlockSpec((None, D, tn1), wcol(nj1, nj1))],
            out_specs=pl.BlockSpec((tm, tn1), lambda t, j, te, nu: (t, j)),
        ),
        compiler_params=_cparams("parallel", "parallel"),
        name="moe_swiglu_in",
    )(tile_expert, n_used, xs, w_in, w_in)
    return pl.pallas_call(
        _gmm_out_kernel,
        out_shape=jax.ShapeDtypeStruct((P, D), F32),
        grid_spec=pltpu.PrefetchScalarGridSpec(
            num_scalar_prefetch=2,
            grid=(nt, nj2),
            in_specs=[pl.BlockSpec((tm, F), lambda t, j, te, nu: (t, 0)),
                      pl.BlockSpec((None, F, tn2), wcol(nj2, 0))],
            out_specs=pl.BlockSpec((tm, tn2), lambda t, j, te, nu: (t, j)),
        ),
        compiler_params=_cparams("parallel", "parallel"),
        name="moe_out",
    )(tile_expert, n_used, h, w_out)


def moe_swiglu(x, g, w_router, w_in, w_out, *, tm=512):
    S, D = x.shape
    E = N_EXPERTS
    tm = _tile(S, tm)
    xn, route = moe_route(x, g, w_router)
    experts = route[:, :2].astype(jnp.int32)
    gates = route[:, 2:4]
    flat_e = experts.reshape(-1)
    flat_tok = jnp.repeat(jnp.arange(S, dtype=jnp.int32), 2)
    order = jnp.argsort(flat_e, stable=True)
    counts = jnp.bincount(flat_e, length=E).astype(jnp.int32)
    padded = ((counts + tm - 1) // tm) * tm
    start = jnp.cumsum(counts) - counts
    pstart = jnp.cumsum(padded) - padded
    sorted_e = flat_e[order]
    dest = pstart[sorted_e] + (jnp.arange(2 * S, dtype=jnp.int32) - start[sorted_e])
    P = 2 * S + E * tm
    nt = P // tm
    row_token = jnp.zeros((P,), jnp.int32).at[dest].set(flat_tok[order])
    slot = jnp.zeros((2 * S,), jnp.int32).at[order].set(dest).reshape(S, 2)
    pend = jnp.cumsum(padded)
    tile_expert = jnp.minimum(
        jnp.searchsorted(pend, jnp.arange(nt, dtype=jnp.int32) * tm, side="right"), E - 1).astype(jnp.int32)
    n_used = (pend[-1:] // tm).astype(jnp.int32)
    xs = jnp.take(xn, row_token, axis=0)
    ys = moe_experts(xs, tile_expert, n_used, w_in, w_out, tm=tm)
    return x + gates[:, 0:1] * jnp.take(ys, slot[:, 0], axis=0) + gates[:, 1:2] * jnp.take(ys, slot[:, 1], axis=0)


def _diff_lambda_init(layer):
    return 0.8 - 0.6 * math.exp(-0.3 * layer)


def _rope_table(pos):
    half = MLA_ROPE // 2
    inv_freq = ROPE_THETA ** (-jnp.arange(half, dtype=F32) / half)
    ang = pos.astype(F32)[:, None] * inv_freq
    cos, sin = jnp.cos(ang), jnp.sin(ang)
    return jnp.concatenate([cos, cos, -sin, sin], axis=1)


def _rot_half_cols(w):
    half = w.shape[-1] // 2
    return jnp.concatenate([w[..., half:], w[..., :half]], axis=-1)


def kernel(x, p, positions, rel_bias, norm_mix, norm_ffn, norm_ple, norm_final, diff_w_qkv, diff_lambda, diff_subln, diff_w_o, moba_w_qkv, moba_w_o, mla_w_down, mla_g_q, mla_w_uq, mla_g_kv, mla_w_ukv, mla_w_o, ffn_w_in, ffn_w_out, moe_w_router, moe_w_in, moe_w_out, ple_w_gate, ple_w_proj):
    B, S, D = x.shape
    depth = p.shape[0]
    lut = _bias_lut(rel_bias)
    outs = []
    for b in range(B):
        xb = x[b]
        pos = positions[b].astype(jnp.int32)
        for i in range(depth):
            jm = i // N_MIXERS
            if i % N_MIXERS == 0:
                qkv = norm_matmul(xb, norm_mix[i], diff_w_qkv[jm].astype(BF16), out_dtype=BF16,
                                  scaled_cols=D, scale=DIFF_HEAD_DIM ** -0.5 * LOG2E)
                o = diff_attention(qkv, qkv[:, 2 * D:].T, pos, lut, diff_lambda[jm].astype(F32), diff_subln[jm],
                                   _diff_lambda_init(i))
                xb = matmul_residual(o, diff_w_o[jm].astype(BF16), xb)
            elif i % N_MIXERS == 1:
                qkv = norm_matmul(xb, norm_mix[i], moba_w_qkv[jm].astype(BF16), out_dtype=BF16,
                                  scaled_cols=D, scale=MOBA_HEAD_DIM ** -0.5 * LOG2E)
                o = moba_attention(qkv, qkv[:, 2 * D:].T, pos, lut)
                xb = matmul_residual(o, moba_w_o[jm].astype(BF16), xb)
            else:
                wd = mla_w_down[jm]
                kr0 = MLA_Q_RANK + MLA_KV_RANK
                wd_ext = jnp.concatenate([wd, _rot_half_cols(wd[:, kr0:])], axis=1).astype(BF16)
                wq = mla_w_uq[jm].reshape(MLA_Q_RANK, MLA_HEADS, MLA_NOPE + MLA_ROPE)
                wq_ext = jnp.concatenate([wq, _rot_half_cols(wq[..., MLA_NOPE:])], axis=-1)
                wq_ext = wq_ext.reshape(MLA_Q_RANK, MLA_HEADS * 2 * LANES).astype(BF16)
                down = norm_matmul(xb, norm_mix[i], wd_ext, out_dtype=F32, tn=wd_ext.shape[1])
                q, kr = mla_queries_and_rope_key(down, mla_g_q[jm], wq_ext, _rope_table(pos),
                                                 scale=(MLA_NOPE + MLA_ROPE) ** -0.5 * LOG2E)
                kv = norm_matmul(down, mla_g_kv[jm], mla_w_ukv[jm].astype(BF16), out_dtype=BF16, xcol=1)
                vt = kv.reshape(S, MLA_HEADS, 2, MLA_V)[:, :, 1, :].reshape(S, MLA_HEADS * MLA_V).T
                o = mla_attention(q, kv, kr, vt)
                xb = matmul_residual(o, mla_w_o[jm].astype(BF16), xb)
            if i % 2 == 0:
                h = norm_swiglu_in(xb, norm_ffn[i], ffn_w_in[i // 2].astype(BF16))
                xb = matmul_residual(h, ffn_w_out[i // 2].astype(BF16), xb)
            else:
                xb = moe_swiglu(xb, norm_ffn[i], moe_w_router[i // 2],
                                moe_w_in[i // 2].astype(BF16), moe_w_out[i // 2].astype(BF16))
            xb = ple_update(xb, norm_ple[i], p[i, b], ple_w_gate[i].astype(BF16), ple_w_proj[i].astype(BF16))
        outs.append(final_norm(xb, norm_final))
    return jnp.stack(outs, axis=0)
```

```python
import functools
import math

import numpy as np
import jax
import jax.numpy as jnp
from jax import lax
from jax.experimental import pallas as pl
from jax.experimental.pallas import tpu as pltpu

F32 = jnp.float32
BF16 = jnp.bfloat16

NORM_EPS = 1e-6
NEG_INF = -1e30
LOG2E = math.log2(math.e)
LANES = 128
VMEM_LIMIT_BYTES = 56 * 1024 * 1024

REL_BUCKETS = 32
REL_MAX_DIST = 128
LUT_SIZE = LANES
DIFF_HEADS = 8
DIFF_HEAD_DIM = 128
MOBA_HEADS = 16
MOBA_HEAD_DIM = 128
MOBA_BLOCK = 256
MOBA_TOPK = 3
MLA_HEADS = 16
MLA_Q_RANK = 512
MLA_KV_RANK = 512
MLA_NOPE = 128
MLA_ROPE = 64
MLA_V = 128
ROPE_THETA = 10000.0
N_EXPERTS = 8
N_MIXERS = 3


def _cparams(*sem):
    return pltpu.CompilerParams(dimension_semantics=sem, vmem_limit_bytes=VMEM_LIMIT_BYTES)


def _tile(n, pref):
    if n <= pref:
        return n
    t = pref
    while n % t:
        t //= 2
    return t


def _rms_rows(x, g):
    r = lax.rsqrt(jnp.mean(x * x, axis=-1, keepdims=True) + NORM_EPS)
    return x * r * g


def _norm_mm_kernel(x_ref, g_ref, w_ref, o_ref, xn_ref, *, scaled_tiles, scale):
    j = pl.program_id(1)

    @pl.when(j == 0)
    def _():
        xn_ref[...] = _rms_rows(x_ref[...], g_ref[...]).astype(BF16)

    acc = jnp.dot(xn_ref[...], w_ref[...], preferred_element_type=F32)
    if scaled_tiles:
        acc = acc * jnp.where(j < scaled_tiles, scale, 1.0)
    o_ref[...] = acc.astype(o_ref.dtype)


def norm_matmul(x, g, w, *, out_dtype, xcol=0, tm=512, tn=512, scaled_cols=0, scale=1.0):
    M = x.shape[0]
    K, N = w.shape
    tm, tn = _tile(M, tm), _tile(N, tn)
    assert scaled_cols % tn == 0
    kern = functools.partial(_norm_mm_kernel, scaled_tiles=scaled_cols // tn, scale=scale)
    return pl.pallas_call(
        kern,
        out_shape=jax.ShapeDtypeStruct((M, N), out_dtype),
        grid=(M // tm, N // tn),
        in_specs=[pl.BlockSpec((tm, K), lambda i, j: (i, xcol)),
                  pl.BlockSpec((1, K), lambda i, j: (0, 0)),
                  pl.BlockSpec((K, tn), lambda i, j: (0, j))],
        out_specs=pl.BlockSpec((tm, tn), lambda i, j: (i, j)),
        scratch_shapes=[pltpu.VMEM((tm, K), BF16)],
        compiler_params=_cparams("parallel", "arbitrary"),
        name="norm_matmul",
    )(x, g.reshape(1, K), w)


def _norm_swiglu_kernel(x_ref, g_ref, wg_ref, wu_ref, o_ref, xn_ref):
    @pl.when(pl.program_id(1) == 0)
    def _():
        xn_ref[...] = _rms_rows(x_ref[...], g_ref[...]).astype(BF16)

    xn = xn_ref[...]
    a = jnp.dot(xn, wg_ref[...], preferred_element_type=F32)
    u = jnp.dot(xn, wu_ref[...], preferred_element_type=F32)
    o_ref[...] = (a * jax.nn.sigmoid(a) * u).astype(o_ref.dtype)


def norm_swiglu_in(x, g, w_in, *, tm=512, tn=512):
    M, K = x.shape
    F = w_in.shape[1] // 2
    tm, tn = _tile(M, tm), _tile(F, tn)
    nj = F // tn
    return pl.pallas_call(
        _norm_swiglu_kernel,
        out_shape=jax.ShapeDtypeStruct((M, F), BF16),
        grid=(M // tm, nj),
        in_specs=[pl.BlockSpec((tm, K), lambda i, j: (i, 0)),
                  pl.BlockSpec((1, K), lambda i, j: (0, 0)),
                  pl.BlockSpec((K, tn), lambda i, j: (0, j)),
                  pl.BlockSpec((K, tn), lambda i, j: (0, j + nj))],
        out_specs=pl.BlockSpec((tm, tn), lambda i, j: (i, j)),
        scratch_shapes=[pltpu.VMEM((tm, K), BF16)],
        compiler_params=_cparams("parallel", "arbitrary"),
        name="norm_swiglu_in",
    )(x, g.reshape(1, K), w_in, w_in)


def _mm_res_kernel(a_ref, w_ref, r_ref, o_ref):
    o_ref[...] = r_ref[...] + jnp.dot(a_ref[...], w_ref[...], preferred_element_type=F32)


def matmul_residual(a, w, res, *, tm=512, tn=512):
    M, K = a.shape
    N = w.shape[1]
    tm, tn = _tile(M, tm), _tile(N, tn)
    return pl.pallas_call(
        _mm_res_kernel,
        out_shape=jax.ShapeDtypeStruct((M, N), F32),
        grid=(M // tm, N // tn),
        in_specs=[pl.BlockSpec((tm, K), lambda i, j: (i, 0)),
                  pl.BlockSpec((K, tn), lambda i, j: (0, j)),
                  pl.BlockSpec((tm, tn), lambda i, j: (i, j))],
        out_specs=pl.BlockSpec((tm, tn), lambda i, j: (i, j)),
        compiler_params=_cparams("parallel", "parallel"),
        name="matmul_residual",
    )(a, w, res)


def _ple_kernel(x_ref, g_ref, xr_ref, p_ref, wg_ref, wp_ref, o_ref, xn_ref):
    @pl.when(pl.program_id(1) == 0)
    def _():
        xn_ref[...] = _rms_rows(x_ref[...], g_ref[...]).astype(BF16)

    gate = jax.nn.sigmoid(jnp.dot(xn_ref[...], wg_ref[...], preferred_element_type=F32))
    proj = jnp.dot(p_ref[...].astype(BF16), wp_ref[...], preferred_element_type=F32)
    o_ref[...] = xr_ref[...] + gate * proj


def ple_update(x, g, p, w_gate, w_proj, *, tm=512, tn=512):
    M, D = x.shape
    P = p.shape[1]
    tm, tn = _tile(M, tm), _tile(D, tn)
    return pl.pallas_call(
        _ple_kernel,
        out_shape=jax.ShapeDtypeStruct((M, D), F32),
        grid=(M // tm, D // tn),
        in_specs=[pl.BlockSpec((tm, D), lambda i, j: (i, 0)),
                  pl.BlockSpec((1, D), lambda i, j: (0, 0)),
                  pl.BlockSpec((tm, tn), lambda i, j: (i, j)),
                  pl.BlockSpec((tm, P), lambda i, j: (i, 0)),
                  pl.BlockSpec((D, tn), lambda i, j: (0, j)),
                  pl.BlockSpec((P, tn), lambda i, j: (0, j))],
        out_specs=pl.BlockSpec((tm, tn), lambda i, j: (i, j)),
        scratch_shapes=[pltpu.VMEM((tm, D), BF16)],
        compiler_params=_cparams("parallel", "arbitrary"),
        name="ple_update",
    )(x, g.reshape(1, D), x, p, w_gate, w_proj)


def _ple_moe_kernel(x_ref, y1_ref, y2_ref, gt_ref, g_ref, p_ref, wg_ref, wp_ref, o_ref, xs_ref, xn_ref, *, tn):
    j = pl.program_id(1)

    @pl.when(j == 0)
    def _():
        gt = gt_ref[...]
        xnew = x_ref[...] + gt[:, 0:1] * y1_ref[...] + gt[:, 1:2] * y2_ref[...]
        xn_ref[...] = _rms_rows(xnew, g_ref[...]).astype(BF16)
        for jj in range(xs_ref.shape[0]):
            xs_ref[jj] = xnew[:, jj * tn:(jj + 1) * tn]

    gate = jax.nn.sigmoid(jnp.dot(xn_ref[...], wg_ref[...], preferred_element_type=F32))
    proj = jnp.dot(p_ref[...].astype(BF16), wp_ref[...], preferred_element_type=F32)
    o_ref[...] = xs_ref[j] + gate * proj


def ple_update_moe(x, y1, y2, gates, g, p, w_gate, w_proj, *, tm=512, tn=512):
    M, D = x.shape
    P = p.shape[1]
    tm, tn = _tile(M, tm), _tile(D, tn)
    row = pl.BlockSpec((tm, D), lambda i, j: (i, 0))
    return pl.pallas_call(
        functools.partial(_ple_moe_kernel, tn=tn),
        out_shape=jax.ShapeDtypeStruct((M, D), F32),
        grid=(M // tm, D // tn),
        in_specs=[row, row, row,
                  pl.BlockSpec((tm, 2), lambda i, j: (i, 0)),
                  pl.BlockSpec((1, D), lambda i, j: (0, 0)),
                  pl.BlockSpec((tm, P), lambda i, j: (i, 0)),
                  pl.BlockSpec((D, tn), lambda i, j: (0, j)),
                  pl.BlockSpec((P, tn), lambda i, j: (0, j))],
        out_specs=pl.BlockSpec((tm, tn), lambda i, j: (i, j)),
        scratch_shapes=[pltpu.VMEM((D // tn, tm, tn), F32), pltpu.VMEM((tm, D), BF16)],
        compiler_params=_cparams("parallel", "arbitrary"),
        name="ple_update_moe",
    )(x, y1, y2, gates, g.reshape(1, D), p, w_gate, w_proj)


def _final_norm_kernel(x_ref, g_ref, o_ref):
    o_ref[...] = _rms_rows(x_ref[...], g_ref[...])


def final_norm(x, g, *, tm=512):
    M, D = x.shape
    tm = _tile(M, tm)
    return pl.pallas_call(
        _final_norm_kernel,
        out_shape=jax.ShapeDtypeStruct((M, D), F32),
        grid=(M // tm,),
        in_specs=[pl.BlockSpec((tm, D), lambda i: (i, 0)),
                  pl.BlockSpec((1, D), lambda i: (0, 0))],
        out_specs=pl.BlockSpec((tm, D), lambda i: (i, 0)),
        compiler_params=_cparams("parallel"),
        name="final_norm",
    )(x, g.reshape(1, D))


def _pair_tables(n_tiles):
    qi, kj = [], []
    for i in range(n_tiles):
        for j in range(i + 1):
            qi.append(i)
            kj.append(j)
    return jnp.asarray(np.array(qi, np.int32)), jnp.asarray(np.array(kj, np.int32))


def _rel_bucket(dist):
    n = jnp.maximum(dist, 0)
    max_exact = REL_BUCKETS // 2
    nf = jnp.maximum(n, 1).astype(F32)
    large = max_exact + (jnp.log(nf / max_exact) / math.log(REL_MAX_DIST / max_exact)
                         * (REL_BUCKETS - max_exact)).astype(jnp.int32)
    large = jnp.minimum(large, REL_BUCKETS - 1)
    return jnp.where(n < max_exact, n, large)


def _bias_lut(rel_bias):
    buckets = _rel_bucket(jnp.arange(LUT_SIZE, dtype=jnp.int32))
    return rel_bias.astype(F32)[buckets].T * LOG2E


def _tile_pos_bounds(pos, tq, ck):
    S = pos.shape[0]
    return pos.reshape(S // tq, tq).min(axis=1), pos.reshape(S // ck, ck).max(axis=1)


def _lut_bias(lut_row, idx):
    R, C = idx.shape
    lut = jnp.broadcast_to(lut_row, (R, LANES))
    parts = [jnp.take_along_axis(lut, idx[:, c:c + LANES], axis=1) for c in range(0, C, LANES)]
    return parts[0] if len(parts) == 1 else jnp.concatenate(parts, axis=1)


def _pipelined(units, scores, consume):
    st = scores(*units[0])
    for k, u in enumerate(units):
        nxt = scores(*units[k + 1]) if k + 1 < len(units) else None
        consume(*u, st)
        st = nxt


def _online_softmax_step_t(st, vt, m_ref, l_ref, acc_ref, qcols, m_cur=None, offsets=None):
    m_prev = m_ref[:, qcols]
    if m_cur is None:
        m_cur = jnp.max(st, axis=0, keepdims=True)
    m_new = jnp.maximum(m_prev, m_cur)
    alpha = jnp.exp2(m_prev - m_new)
    if offsets is None:
        p = jnp.exp2(st - m_new)
    else:
        parts = [jnp.exp2(st[rows, :] - fn(m_new)) for rows, fn in offsets]
        p = parts[0] if len(parts) == 1 else jnp.concatenate(parts, axis=0)
    l_ref[:, qcols] = alpha * l_ref[:, qcols] + jnp.sum(p, axis=0, keepdims=True)
    acc_ref[:, qcols] = alpha * acc_ref[:, qcols] + jnp.dot(vt, p.astype(BF16), preferred_element_type=F32)
    m_ref[:, qcols] = m_new


def _causal_t(k0, nk, q0, nq):
    r = k0 + lax.broadcasted_iota(jnp.int32, (nk, nq), 0)
    c = q0 + lax.broadcasted_iota(jnp.int32, (nk, nq), 1)
    return r <= c


def _nt_dot(a, b):
    return lax.dot_general(a, b, (((1,), (1,)), ((), ())), preferred_element_type=F32)


def _diff_attn_kernel(qi_ref, kj_ref, pminq_ref, pmaxk_ref,
                      q_ref, k_ref, vt_ref, pq_ref, pk_ref, lut_ref, lam_ref, g_ref,
                      o_ref, m_ref, l_ref, acc_ref, *, tq, ck, lambda_init):
    p = pl.program_id(1)
    i, j = qi_ref[p], kj_ref[p]
    nc = tq // ck
    dh = DIFF_HEAD_DIM

    @pl.when(j == 0)
    def _():
        m_ref[...] = jnp.full(m_ref.shape, NEG_INF, F32)
        l_ref[...] = jnp.zeros(l_ref.shape, F32)
        acc_ref[...] = jnp.zeros(acc_ref.shape, F32)

    def scores(c, mp, q0, mode):
        keys = slice(c * ck, (c + 1) * ck)
        st = _nt_dot(k_ref[keys, mp * dh:(mp + 1) * dh], q_ref[q0:tq, mp * dh:(mp + 1) * dh])
        if mode != "far":
            idx = jnp.clip(pq_ref[:, q0:tq] - pk_ref[keys, :], 0, LUT_SIZE - 1)
            st = st + _lut_bias(lut_ref[mp:mp + 1, :], idx)
        if mode == "diag":
            st = jnp.where(_causal_t(c * ck, ck, q0, tq - q0), st, NEG_INF)
        return st

    def consume(c, mp, q0, mode, st):
        m_cur = offsets = None
        if mode == "far":
            bias = lut_ref[mp:mp + 1, LUT_SIZE - 1:LUT_SIZE]
            m_cur = jnp.max(st, axis=0, keepdims=True) + bias
            offsets = [(slice(None), lambda m_new: m_new - bias)]
        _online_softmax_step_t(st, vt_ref[:, c * ck:(c + 1) * ck], m_ref.at[mp], l_ref.at[mp], acc_ref.at[mp],
                               slice(q0, tq), m_cur, offsets)

    def sweep(chunks, mode):
        units = [(c, mp, c * ck if mode == "diag" else 0, mode) for c in chunks for mp in range(2)]
        _pipelined(units, scores, consume)

    @pl.when(j < i)
    def _():
        fars = [pminq_ref[i] - pmaxk_ref[j * nc + c] >= LUT_SIZE - 1 for c in range(nc)]
        all_far = functools.reduce(jnp.logical_and, fars)
        pl.when(all_far)(functools.partial(sweep, range(nc), "far"))

        @pl.when(jnp.logical_not(all_far))
        def _():
            for c in range(nc):
                pl.when(fars[c])(functools.partial(sweep, [c], "far"))
                pl.when(jnp.logical_not(fars[c]))(functools.partial(sweep, [c], "near"))

    @pl.when(j == i)
    def _():
        sweep(range(nc), "diag")
        lam = lam_ref[...]
        lam_full = (jnp.exp(jnp.sum(lam[0:1] * lam[1:2], axis=-1, keepdims=True))
                    - jnp.exp(jnp.sum(lam[2:3] * lam[3:4], axis=-1, keepdims=True)) + lambda_init)
        a = acc_ref[0] / l_ref[0] - lam_full * (acc_ref[1] / l_ref[1])
        r = lax.rsqrt(jnp.mean(a * a, axis=0, keepdims=True) + NORM_EPS)
        y = a * r * (g_ref[...] * (1.0 - lambda_init))
        o_ref[...] = y.T.astype(o_ref.dtype)


def diff_attention(qkv, vt, pos, lut, lam, subln_g, lambda_init, *, tile=1024, chunk=512):
    S = qkv.shape[0]
    H, dh = DIFF_HEADS, DIFF_HEAD_DIM
    tq = _tile(S, tile)
    ck = _tile(tq, chunk)
    qi, kj = _pair_tables(S // tq)
    pminq, pmaxk = _tile_pos_bounds(pos, tq, ck)
    kern = functools.partial(_diff_attn_kernel, tq=tq, ck=ck, lambda_init=lambda_init)
    grid_spec = pltpu.PrefetchScalarGridSpec(
        num_scalar_prefetch=4,
        grid=(H, qi.shape[0]),
        in_specs=[pl.BlockSpec((tq, 2 * dh), lambda h, p, qi, kj, a, b: (qi[p], h)),
                  pl.BlockSpec((tq, 2 * dh), lambda h, p, qi, kj, a, b: (kj[p], H + h)),
                  pl.BlockSpec((2 * dh, tq), lambda h, p, qi, kj, a, b: (h, kj[p])),
                  pl.BlockSpec((1, tq), lambda h, p, qi, kj, a, b: (0, qi[p])),
                  pl.BlockSpec((tq, 1), lambda h, p, qi, kj, a, b: (kj[p], 0)),
                  pl.BlockSpec((None, 2, LUT_SIZE), lambda h, p, qi, kj, a, b: (h, 0, 0)),
                  pl.BlockSpec((4, dh), lambda h, p, qi, kj, a, b: (0, 0)),
                  pl.BlockSpec((2 * dh, 1), lambda h, p, qi, kj, a, b: (0, 0))],
        out_specs=pl.BlockSpec((tq, 2 * dh), lambda h, p, qi, kj, a, b: (qi[p], h)),
        scratch_shapes=[pltpu.VMEM((2, 1, tq), F32), pltpu.VMEM((2, 1, tq), F32),
                        pltpu.VMEM((2, 2 * dh, tq), F32)],
    )
    return pl.pallas_call(
        kern,
        out_shape=jax.ShapeDtypeStruct((S, H * 2 * dh), BF16),
        grid_spec=grid_spec,
        compiler_params=_cparams("parallel", "arbitrary"),
        name="diff_attention",
    )(qi, kj, pminq, pmaxk, qkv, qkv, vt, pos.reshape(1, S), pos.reshape(S, 1),
      lut.reshape(H, 2, LUT_SIZE), lam, subln_g.reshape(2 * dh, 1))


def _kmean_kernel(k_ref, hi_ref, lo_ref):
    km = jnp.mean(k_ref[...].astype(F32), axis=0, keepdims=True)
    hi = km.astype(BF16)
    hi_ref[...] = hi
    lo_ref[...] = (km - hi.astype(F32)).astype(BF16)


def moba_block_means(qkv, D):
    S = qkv.shape[0]
    nblk = S // MOBA_BLOCK
    hi, lo = pl.pallas_call(
        _kmean_kernel,
        out_shape=[jax.ShapeDtypeStruct((nblk, 1, D), BF16)] * 2,
        grid=(nblk,),
        in_specs=[pl.BlockSpec((MOBA_BLOCK, D), lambda n: (n, 1))],
        out_specs=[pl.BlockSpec((None, 1, D), lambda n: (n, 0, 0))] * 2,
        compiler_params=_cparams("parallel"),
        name="moba_block_means",
    )(qkv)
    return hi.reshape(nblk, D), lo.reshape(nblk, D)


def _moba_attn_kernel(qi_ref, kj_ref, pminq_ref, pmaxk_ref,
                      q_ref, k_ref, vt_ref, kmh_ref, kml_ref, pq_ref, pk_ref, lut_ref,
                      o_ref, m_ref, l_ref, acc_ref, sel_ref, *, tq, ck, hp):
    p = pl.program_id(1)
    i, j = qi_ref[p], kj_ref[p]
    nc = tq // ck
    L, dh = MOBA_BLOCK, MOBA_HEAD_DIM
    bpc = ck // L
    log2_l = L.bit_length() - 1

    @pl.when(j == 0)
    def _():
        m_ref[...] = jnp.full(m_ref.shape, NEG_INF, F32)
        l_ref[...] = jnp.zeros(l_ref.shape, F32)
        acc_ref[...] = jnp.zeros(acc_ref.shape, F32)
        blk = lax.broadcasted_iota(jnp.int32, (LANES, tq), 0)
        blk_f = blk.astype(F32)
        own = lax.shift_right_logical(i * tq + lax.broadcasted_iota(jnp.int32, (1, tq), 1), log2_l)
        for hd in range(hp):
            q = q_ref[:, hd * dh:(hd + 1) * dh]
            gs = (_nt_dot(kmh_ref[:, hd * dh:(hd + 1) * dh], q)
                  + _nt_dot(kml_ref[:, hd * dh:(hd + 1) * dh], q))
            g = jnp.where(blk < own, gs, NEG_INF)
            sel = jnp.zeros((LANES, tq), F32)
            for _ in range(MOBA_TOPK):
                best = jnp.max(g, axis=0, keepdims=True)
                first = jnp.min(jnp.where(g == best, blk_f, float(LANES)), axis=0, keepdims=True)
                pick = blk_f == first
                sel = jnp.where(pick, jnp.where(best > 0.5 * NEG_INF, 1.0, sel), sel)
                g = jnp.where(pick, -3e38, g)
            sel_ref[hd] = sel

    def scores(c, hd, q0, mode):
        keys = slice(c * ck, (c + 1) * ck)
        nq = tq - q0
        st = _nt_dot(k_ref[keys, hd * dh:(hd + 1) * dh], q_ref[q0:tq, hd * dh:(hd + 1) * dh])
        if mode != "far":
            idx = jnp.clip(pq_ref[:, q0:tq] - pk_ref[keys, :], 0, LUT_SIZE - 1)
            st = st + _lut_bias(lut_ref[hd:hd + 1, :], idx)
        if mode != "diag":
            return st
        allowed = []
        for b in range(bpc):
            n_local = c * bpc + b
            picked = jnp.broadcast_to(picked_row(hd, n_local, q0), (L, nq))
            own_local = lax.shift_right_logical(q0 + lax.broadcasted_iota(jnp.int32, (L, nq), 1), log2_l)
            visible = jnp.where(_causal_t(n_local * L, L, q0, nq), 1.0, 0.0)
            allowed.append(jnp.where(own_local == n_local, visible, picked))
        allowed = allowed[0] if bpc == 1 else jnp.concatenate(allowed, axis=0)
        return jnp.where(allowed > 0.0, st, NEG_INF)

    def picked_row(hd, n_local, q0):
        return sel_ref[hd, pl.ds(j * (tq // L) + n_local, 1), q0:tq]

    def consume(c, hd, q0, mode, st):
        m_cur = offsets = None
        if mode != "diag":
            bias = lut_ref[hd:hd + 1, LUT_SIZE - 1:LUT_SIZE] if mode == "far" else 0.0
            picked = [picked_row(hd, c * bpc + b, q0) > 0.0 for b in range(bpc)]
            m_cur = functools.reduce(jnp.maximum, [
                jnp.where(picked[b], jnp.max(st[b * L:(b + 1) * L, :], axis=0, keepdims=True) + bias, NEG_INF)
                for b in range(bpc)])
            offsets = [(slice(b * L, (b + 1) * L),
                        functools.partial(lambda m_new, pk: jnp.where(pk, m_new - bias, -NEG_INF), pk=picked[b]))
                       for b in range(bpc)]
        _online_softmax_step_t(st, vt_ref[hd * dh:(hd + 1) * dh, c * ck:(c + 1) * ck],
                               m_ref.at[hd], l_ref.at[hd], acc_ref.at[hd], slice(q0, tq), m_cur, offsets)

    def sweep(chunks, mode):
        _pipelined([(c, hd, c * ck if mode == "diag" else 0, mode) for c in chunks for hd in range(hp)],
                   scores, consume)

    @pl.when(j < i)
    def _():
        fars = [pminq_ref[i] - pmaxk_ref[j * nc + c] >= LUT_SIZE - 1 for c in range(nc)]
        all_far = functools.reduce(jnp.logical_and, fars)
        pl.when(all_far)(functools.partial(sweep, range(nc), "far"))

        @pl.when(jnp.logical_not(all_far))
        def _():
            for c in range(nc):
                pl.when(fars[c])(functools.partial(sweep, [c], "far"))
                pl.when(jnp.logical_not(fars[c]))(functools.partial(sweep, [c], "near"))

    @pl.when(j == i)
    def _():
        sweep(range(nc), "diag")
        for hd in range(hp):
            o_ref[:, hd * dh:(hd + 1) * dh] = (acc_ref[hd] / l_ref[hd]).T.astype(o_ref.dtype)


def moba_attention(qkv, vt, pos, lut, *, tile=1024, chunk=512, heads_per_step=2):
    S = qkv.shape[0]
    H, dh, L = MOBA_HEADS, MOBA_HEAD_DIM, MOBA_BLOCK
    D = H * dh
    nblk = S // L
    assert S % L == 0 and nblk <= LANES
    tq = _tile(S, tile)
    ck = _tile(tq, chunk)
    assert tq % L == 0 and ck % L == 0
    km_hi, km_lo = moba_block_means(qkv, D)
    km_hi = jnp.pad(km_hi, ((0, LANES - nblk), (0, 0)))
    km_lo = jnp.pad(km_lo, ((0, LANES - nblk), (0, 0)))
    qi, kj = _pair_tables(S // tq)
    pminq, pmaxk = _tile_pos_bounds(pos, tq, ck)
    hp = heads_per_step
    G = H // hp
    kern = functools.partial(_moba_attn_kernel, tq=tq, ck=ck, hp=hp)
    grid_spec = pltpu.PrefetchScalarGridSpec(
        num_scalar_prefetch=4,
        grid=(G, qi.shape[0]),
        in_specs=[pl.BlockSpec((tq, hp * dh), lambda h, p, qi, kj, a, b: (qi[p], h)),
                  pl.BlockSpec((tq, hp * dh), lambda h, p, qi, kj, a, b: (kj[p], G + h)),
                  pl.BlockSpec((hp * dh, tq), lambda h, p, qi, kj, a, b: (h, kj[p])),
                  pl.BlockSpec((LANES, hp * dh), lambda h, p, qi, kj, a, b: (0, h)),
                  pl.BlockSpec((LANES, hp * dh), lambda h, p, qi, kj, a, b: (0, h)),
                  pl.BlockSpec((1, tq), lambda h, p, qi, kj, a, b: (0, qi[p])),
                  pl.BlockSpec((tq, 1), lambda h, p, qi, kj, a, b: (kj[p], 0)),
                  pl.BlockSpec((None, hp, LUT_SIZE), lambda h, p, qi, kj, a, b: (h, 0, 0))],
        out_specs=pl.BlockSpec((tq, hp * dh), lambda h, p, qi, kj, a, b: (qi[p], h)),
        scratch_shapes=[pltpu.VMEM((hp, 1, tq), F32), pltpu.VMEM((hp, 1, tq), F32),
                        pltpu.VMEM((hp, dh, tq), F32), pltpu.VMEM((hp, LANES, tq), F32)],
    )
    return pl.pallas_call(
        kern,
        out_shape=jax.ShapeDtypeStruct((S, D), BF16),
        grid_spec=grid_spec,
        compiler_params=_cparams("parallel", "arbitrary"),
        name="moba_attention",
    )(qi, kj, pminq, pmaxk, qkv, qkv, vt, km_hi, km_lo, pos.reshape(1, S), pos.reshape(S, 1),
      lut.reshape(G, hp, LUT_SIZE))


def _rope_fold(y):
    lane = lax.broadcasted_iota(jnp.int32, y.shape, 1)
    return jnp.where(lane < MLA_ROPE, y + pltpu.roll(y, MLA_ROPE, 1), 0.0)


def _mla_q_kernel(cq_ref, g_ref, w_ref, kr_ref, t_ref, q_ref, kro_ref, cqn_ref, *, scale):
    t = t_ref[...]

    @pl.when(pl.program_id(1) == 0)
    def _():
        cqn_ref[...] = _rms_rows(cq_ref[...], g_ref[...]).astype(BF16)
        kro_ref[...] = _rope_fold(kr_ref[...] * t).astype(kro_ref.dtype)

    qh = jnp.dot(cqn_ref[...], w_ref[...], preferred_element_type=F32)
    q_ref[:, :MLA_NOPE] = (qh[:, :MLA_NOPE] * scale).astype(q_ref.dtype)
    q_ref[:, MLA_NOPE:] = (_rope_fold(qh[:, MLA_NOPE:] * t) * scale).astype(q_ref.dtype)


def mla_queries_and_rope_key(down, g_q, w_uq_ext, rope_tab, *, scale, tm=512):
    S = down.shape[0]
    H = MLA_HEADS
    tm = _tile(S, tm)
    W = 2 * LANES
    return pl.pallas_call(
        functools.partial(_mla_q_kernel, scale=scale),
        out_shape=[jax.ShapeDtypeStruct((S, H * W), BF16), jax.ShapeDtypeStruct((S, LANES), BF16)],
        grid=(S // tm, H),
        in_specs=[pl.BlockSpec((tm, MLA_Q_RANK), lambda i, h: (i, 0)),
                  pl.BlockSpec((1, MLA_Q_RANK), lambda i, h: (0, 0)),
                  pl.BlockSpec((MLA_Q_RANK, W), lambda i, h: (0, h)),
                  pl.BlockSpec((tm, LANES), lambda i, h: (i, (MLA_Q_RANK + MLA_KV_RANK) // LANES)),
                  pl.BlockSpec((tm, LANES), lambda i, h: (i, 0))],
        out_specs=[pl.BlockSpec((tm, W), lambda i, h: (i, h)),
                   pl.BlockSpec((tm, LANES), lambda i, h: (i, 0))],
        scratch_shapes=[pltpu.VMEM((tm, MLA_Q_RANK), BF16)],
        compiler_params=_cparams("parallel", "arbitrary"),
        name="mla_queries",
    )(down, g_q.reshape(1, MLA_Q_RANK), w_uq_ext, down, rope_tab)


def _mla_attn_kernel(qi_ref, kj_ref, q_ref, kv_ref, kr_ref, vt_ref, o_ref, m_ref, l_ref, acc_ref, *, tq, ck, hp):
    p = pl.program_id(1)
    i, j = qi_ref[p], kj_ref[p]
    nc = tq // ck
    W = 2 * LANES

    @pl.when(j == 0)
    def _():
        m_ref[...] = jnp.full(m_ref.shape, NEG_INF, F32)
        l_ref[...] = jnp.zeros(l_ref.shape, F32)
        acc_ref[...] = jnp.zeros(acc_ref.shape, F32)

    def scores(c, g, q0, diag):
        keys = slice(c * ck, (c + 1) * ck)
        kc = jnp.concatenate([kv_ref[keys, g * W:g * W + MLA_NOPE], kr_ref[keys, :]], axis=1)
        st = _nt_dot(kc, q_ref[q0:tq, g * W:(g + 1) * W])
        if diag:
            st = jnp.where(_causal_t(c * ck, ck, q0, tq - q0), st, NEG_INF)
        return st

    def consume(c, g, q0, diag, st):
        _online_softmax_step_t(st, vt_ref[g * MLA_V:(g + 1) * MLA_V, c * ck:(c + 1) * ck],
                               m_ref.at[g], l_ref.at[g], acc_ref.at[g], slice(q0, tq))

    def sweep(diag):
        _pipelined([(c, g, c * ck if diag else 0, diag) for c in range(nc) for g in range(hp)], scores, consume)

    @pl.when(j < i)
    def _():
        sweep(False)

    @pl.when(j == i)
    def _():
        sweep(True)
        for g in range(hp):
            o_ref[:, g * MLA_V:(g + 1) * MLA_V] = (acc_ref[g] / l_ref[g]).T.astype(o_ref.dtype)


def mla_attention(q, kv, kr, vt, *, tile=1024, chunk=512, heads_per_step=2):
    S = q.shape[0]
    H, hp = MLA_HEADS, heads_per_step
    tq = _tile(S, tile)
    ck = _tile(tq, chunk)
    qi, kj = _pair_tables(S // tq)
    W = 2 * LANES
    grid_spec = pltpu.PrefetchScalarGridSpec(
        num_scalar_prefetch=2,
        grid=(H // hp, qi.shape[0]),
        in_specs=[pl.BlockSpec((tq, hp * W), lambda h, p, qi, kj: (qi[p], h)),
                  pl.BlockSpec((tq, hp * W), lambda h, p, qi, kj: (kj[p], h)),
                  pl.BlockSpec((tq, LANES), lambda h, p, qi, kj: (kj[p], 0)),
                  pl.BlockSpec((hp * MLA_V, tq), lambda h, p, qi, kj: (h, kj[p]))],
        out_specs=pl.BlockSpec((tq, hp * MLA_V), lambda h, p, qi, kj: (qi[p], h)),
        scratch_shapes=[pltpu.VMEM((hp, 1, tq), F32), pltpu.VMEM((hp, 1, tq), F32),
                        pltpu.VMEM((hp, MLA_V, tq), F32)],
    )
    return pl.pallas_call(
        functools.partial(_mla_attn_kernel, tq=tq, ck=ck, hp=hp),
        out_shape=jax.ShapeDtypeStruct((S, H * MLA_V), BF16),
        grid_spec=grid_spec,
        compiler_params=_cparams("parallel", "arbitrary"),
        name="mla_attention",
    )(qi, kj, q, kv, kr, vt)


def _router_kernel(x_ref, g_ref, wh_ref, wl_ref, xn_ref, r_ref):
    xn = _rms_rows(x_ref[...], g_ref[...])
    hi = xn.astype(BF16)
    lo = (xn - hi.astype(F32)).astype(BF16)
    xn_ref[...] = hi
    logits = (jnp.dot(hi, wh_ref[...], preferred_element_type=F32)
              + jnp.dot(hi, wl_ref[...], preferred_element_type=F32)
              + jnp.dot(lo, wh_ref[...], preferred_element_type=F32))
    lane = lax.broadcasted_iota(jnp.int32, logits.shape, 1)
    lane_f = lane.astype(F32)
    g = jnp.where(lane < N_EXPERTS, logits, NEG_INF)
    v1 = jnp.max(g, axis=-1, keepdims=True)
    i1 = jnp.min(jnp.where(g == v1, lane_f, float(LANES)), axis=-1, keepdims=True)
    g = jnp.where(lane_f == i1, NEG_INF, g)
    v2 = jnp.max(g, axis=-1, keepdims=True)
    i2 = jnp.min(jnp.where(g == v2, lane_f, float(LANES)), axis=-1, keepdims=True)
    g2 = 1.0 / (1.0 + jnp.exp(v1 - v2))
    g1 = 1.0 - g2
    r_ref[...] = jnp.where(lane == 0, i1,
                           jnp.where(lane == 1, i2,
                                     jnp.where(lane == 2, g1, jnp.where(lane == 3, g2, 0.0))))


def moe_route(x, g, w_router, *, tm=512):
    M, D = x.shape
    tm = _tile(M, tm)
    wr = jnp.pad(w_router.astype(F32), ((0, 0), (0, LANES - N_EXPERTS)))
    wh = wr.astype(BF16)
    wl = (wr - wh.astype(F32)).astype(BF16)
    return pl.pallas_call(
        _router_kernel,
        out_shape=[jax.ShapeDtypeStruct((M, D), BF16), jax.ShapeDtypeStruct((M, LANES), F32)],
        grid=(M // tm,),
        in_specs=[pl.BlockSpec((tm, D), lambda i: (i, 0)),
                  pl.BlockSpec((1, D), lambda i: (0, 0)),
                  pl.BlockSpec((D, LANES), lambda i: (0, 0)),
                  pl.BlockSpec((D, LANES), lambda i: (0, 0))],
        out_specs=[pl.BlockSpec((tm, D), lambda i: (i, 0)),
                   pl.BlockSpec((tm, LANES), lambda i: (i, 0))],
        compiler_params=_cparams("parallel"),
        name="moe_route",
    )(x, g.reshape(1, D), wh, wl)


def _new_expert(te_ref, t):
    return jnp.logical_or(t == 0, te_ref[t] != te_ref[jnp.maximum(t - 1, 0)])


def _gmm_swiglu_kernel(te_ref, nu_ref, x_ref, wg_ref, wu_ref, o_ref, wgb_ref, wub_ref):
    t = pl.program_id(1)
    used = t < nu_ref[0]

    @pl.when(jnp.logical_and(used, _new_expert(te_ref, t)))
    def _():
        wgb_ref[...] = wg_ref[...].astype(BF16)
        wub_ref[...] = wu_ref[...].astype(BF16)

    @pl.when(used)
    def _():
        x = x_ref[...]
        a = jnp.dot(x, wgb_ref[...], preferred_element_type=F32)
        u = jnp.dot(x, wub_ref[...], preferred_element_type=F32)
        o_ref[...] = (a * jax.nn.sigmoid(a) * u).astype(o_ref.dtype)

    @pl.when(jnp.logical_not(used))
    def _():
        o_ref[...] = jnp.zeros(o_ref.shape, o_ref.dtype)


def _gmm_out_kernel(te_ref, nu_ref, h_ref, w_ref, o_ref, wb_ref):
    t = pl.program_id(1)
    used = t < nu_ref[0]

    @pl.when(jnp.logical_and(used, _new_expert(te_ref, t)))
    def _():
        wb_ref[...] = w_ref[...].astype(BF16)

    @pl.when(used)
    def _():
        o_ref[...] = jnp.dot(h_ref[...], wb_ref[...], preferred_element_type=F32)

    @pl.when(jnp.logical_not(used))
    def _():
        o_ref[...] = jnp.zeros(o_ref.shape, o_ref.dtype)


def moe_experts(xs, tile_expert, n_used, w_in, w_out, layer, *, tm, tn=512):
    P, D = xs.shape
    F = w_in.shape[3] // 2
    nt = P // tm
    tn1, tn2 = _tile(F, tn), _tile(D, tn)
    nj1, nj2 = F // tn1, D // tn2
    h = pl.pallas_call(
        _gmm_swiglu_kernel,
        out_shape=jax.ShapeDtypeStruct((P, F), BF16),
        grid_spec=pltpu.PrefetchScalarGridSpec(
            num_scalar_prefetch=2,
            grid=(nj1, nt),
            in_specs=[pl.BlockSpec((tm, D), lambda j, t, te, nu: (t, 0)),
                      pl.BlockSpec((None, None, D, tn1), lambda j, t, te, nu: (layer, te[t], 0, j)),
                      pl.BlockSpec((None, None, D, tn1), lambda j, t, te, nu: (layer, te[t], 0, nj1 + j))],
            out_specs=pl.BlockSpec((tm, tn1), lambda j, t, te, nu: (t, j)),
            scratch_shapes=[pltpu.VMEM((D, tn1), BF16), pltpu.VMEM((D, tn1), BF16)],
        ),
        compiler_params=_cparams("arbitrary", "arbitrary"),
        name="moe_swiglu_in",
    )(tile_expert, n_used, xs, w_in, w_in)
    return pl.pallas_call(
        _gmm_out_kernel,
        out_shape=jax.ShapeDtypeStruct((P, D), F32),
        grid_spec=pltpu.PrefetchScalarGridSpec(
            num_scalar_prefetch=2,
            grid=(nj2, nt),
            in_specs=[pl.BlockSpec((tm, F), lambda j, t, te, nu: (t, 0)),
                      pl.BlockSpec((None, None, F, tn2), lambda j, t, te, nu: (layer, te[t], 0, j))],
            out_specs=pl.BlockSpec((tm, tn2), lambda j, t, te, nu: (t, j)),
            scratch_shapes=[pltpu.VMEM((F, tn2), BF16)],
        ),
        compiler_params=_cparams("arbitrary", "arbitrary"),
        name="moe_out",
    )(tile_expert, n_used, h, w_out)


def moe_swiglu(x, g, w_router, w_in, w_out, layer, *, tm=512):
    S, D = x.shape
    E = N_EXPERTS
    tm = _tile(S, tm)
    xn, route = moe_route(x, g, w_router)
    experts = route[:, :2].astype(jnp.int32)
    gates = route[:, 2:4]
    flat_e = experts.reshape(-1)
    flat_tok = jnp.repeat(jnp.arange(S, dtype=jnp.int32), 2)
    order = jnp.argsort(flat_e, stable=True)
    counts = jnp.bincount(flat_e, length=E).astype(jnp.int32)
    padded = ((counts + tm - 1) // tm) * tm
    start = jnp.cumsum(counts) - counts
    pstart = jnp.cumsum(padded) - padded
    sorted_e = flat_e[order]
    dest = pstart[sorted_e] + (jnp.arange(2 * S, dtype=jnp.int32) - start[sorted_e])
    P = 2 * S + E * tm
    nt = P // tm
    row_token = jnp.zeros((P,), jnp.int32).at[dest].set(flat_tok[order])
    slot = jnp.zeros((2 * S,), jnp.int32).at[order].set(dest).reshape(S, 2)
    pend = jnp.cumsum(padded)
    tile_expert = jnp.minimum(
        jnp.searchsorted(pend, jnp.arange(nt, dtype=jnp.int32) * tm, side="right"), E - 1).astype(jnp.int32)
    n_used = (pend[-1:] // tm).astype(jnp.int32)
    tile_expert = jnp.where(jnp.arange(nt) < n_used[0], tile_expert, tile_expert[jnp.maximum(n_used[0] - 1, 0)])
    xs = jnp.take(xn, row_token, axis=0)
    ys = moe_experts(xs, tile_expert, n_used, w_in, w_out, layer, tm=tm)
    return jnp.take(ys, slot[:, 0], axis=0), jnp.take(ys, slot[:, 1], axis=0), gates


def _diff_lambda_init(layer):
    return 0.8 - 0.6 * math.exp(-0.3 * layer)


def _rope_table(pos):
    half = MLA_ROPE // 2
    inv_freq = ROPE_THETA ** (-jnp.arange(half, dtype=F32) / half)
    ang = pos.astype(F32)[:, None] * inv_freq
    cos, sin = jnp.cos(ang), jnp.sin(ang)
    return jnp.concatenate([cos, cos, -sin, sin], axis=1)


def _rot_half_cols(w):
    half = w.shape[-1] // 2
    return jnp.concatenate([w[..., half:], w[..., :half]], axis=-1)


def kernel(x, p, positions, rel_bias, norm_mix, norm_ffn, norm_ple, norm_final, diff_w_qkv, diff_lambda, diff_subln, diff_w_o, moba_w_qkv, moba_w_o, mla_w_down, mla_g_q, mla_w_uq, mla_g_kv, mla_w_ukv, mla_w_o, ffn_w_in, ffn_w_out, moe_w_router, moe_w_in, moe_w_out, ple_w_gate, ple_w_proj):
    B, S, D = x.shape
    depth = p.shape[0]
    lut = _bias_lut(rel_bias)
    outs = []
    for b in range(B):
        xb = x[b]
        pos = positions[b].astype(jnp.int32)
        for i in range(depth):
            jm = i // N_MIXERS
            if i % N_MIXERS == 0:
                qkv = norm_matmul(xb, norm_mix[i], diff_w_qkv[jm].astype(BF16), out_dtype=BF16,
                                  scaled_cols=D, scale=DIFF_HEAD_DIM ** -0.5 * LOG2E)
                o = diff_attention(qkv, qkv[:, 2 * D:].T, pos, lut, diff_lambda[jm].astype(F32), diff_subln[jm],
                                   _diff_lambda_init(i))
                xb = matmul_residual(o, diff_w_o[jm].astype(BF16), xb)
            elif i % N_MIXERS == 1:
                qkv = norm_matmul(xb, norm_mix[i], moba_w_qkv[jm].astype(BF16), out_dtype=BF16,
                                  scaled_cols=D, scale=MOBA_HEAD_DIM ** -0.5 * LOG2E)
                o = moba_attention(qkv, qkv[:, 2 * D:].T, pos, lut)
                xb = matmul_residual(o, moba_w_o[jm].astype(BF16), xb)
            else:
                wd = mla_w_down[jm]
                kr0 = MLA_Q_RANK + MLA_KV_RANK
                wd_ext = jnp.concatenate([wd, _rot_half_cols(wd[:, kr0:])], axis=1).astype(BF16)
                wq = mla_w_uq[jm].reshape(MLA_Q_RANK, MLA_HEADS, MLA_NOPE + MLA_ROPE)
                wq_ext = jnp.concatenate([wq, _rot_half_cols(wq[..., MLA_NOPE:])], axis=-1)
                wq_ext = wq_ext.reshape(MLA_Q_RANK, MLA_HEADS * 2 * LANES).astype(BF16)
                down = norm_matmul(xb, norm_mix[i], wd_ext, out_dtype=F32, tn=wd_ext.shape[1])
                q, kr = mla_queries_and_rope_key(down, mla_g_q[jm], wq_ext, _rope_table(pos),
                                                 scale=(MLA_NOPE + MLA_ROPE) ** -0.5 * LOG2E)
                kv = norm_matmul(down, mla_g_kv[jm], mla_w_ukv[jm].astype(BF16), out_dtype=BF16, xcol=1)
                vt = kv.reshape(S, MLA_HEADS, 2, MLA_V)[:, :, 1, :].reshape(S, MLA_HEADS * MLA_V).T
                o = mla_attention(q, kv, kr, vt)
                xb = matmul_residual(o, mla_w_o[jm].astype(BF16), xb)
            w_gate, w_proj = ple_w_gate[i].astype(BF16), ple_w_proj[i].astype(BF16)
            if i % 2 == 0:
                h = norm_swiglu_in(xb, norm_ffn[i], ffn_w_in[i // 2].astype(BF16))
                xb = matmul_residual(h, ffn_w_out[i // 2].astype(BF16), xb)
                xb = ple_update(xb, norm_ple[i], p[i, b], w_gate, w_proj)
            else:
                y1, y2, gates = moe_swiglu(xb, norm_ffn[i], moe_w_router[i // 2], moe_w_in, moe_w_out, i // 2)
                xb = ple_update_moe(xb, y1, y2, gates, norm_ple[i], p[i, b], w_gate, w_proj)
        outs.append(final_norm(xb, norm_final))
    return jnp.stack(outs, axis=0)
```

```python
import functools
import math

import numpy as np
import jax
import jax.numpy as jnp
from jax import lax
from jax.experimental import pallas as pl
from jax.experimental.pallas import tpu as pltpu

F32 = jnp.float32
BF16 = jnp.bfloat16

NORM_EPS = 1e-6
NEG_INF = -1e30
LOG2E = math.log2(math.e)
LANES = 128
VMEM_LIMIT_BYTES = 56 * 1024 * 1024

REL_BUCKETS = 32
REL_MAX_DIST = 128
LUT_SIZE = LANES
DIFF_HEADS = 8
DIFF_HEAD_DIM = 128
MOBA_HEADS = 16
MOBA_HEAD_DIM = 128
MOBA_BLOCK = 256
MOBA_TOPK = 3
MLA_HEADS = 16
MLA_Q_RANK = 512
MLA_KV_RANK = 512
MLA_NOPE = 128
MLA_ROPE = 64
MLA_V = 128
ROPE_THETA = 10000.0
N_EXPERTS = 8
N_MIXERS = 3


def _cparams(*sem):
    return pltpu.CompilerParams(dimension_semantics=sem, vmem_limit_bytes=VMEM_LIMIT_BYTES)


def _tile(n, pref):
    if n <= pref:
        return n
    t = pref
    while n % t:
        t //= 2
    return t


def _rms_rows(x, g):
    r = lax.rsqrt(jnp.mean(x * x, axis=-1, keepdims=True) + NORM_EPS)
    return x * r * g


def _norm_mm_kernel(x_ref, g_ref, w_ref, o_ref, xn_ref, *, scaled_tiles, scale):
    j = pl.program_id(1)

    @pl.when(j == 0)
    def _():
        xn_ref[...] = _rms_rows(x_ref[...], g_ref[...]).astype(BF16)

    acc = jnp.dot(xn_ref[...], w_ref[...], preferred_element_type=F32)
    if scaled_tiles:
        acc = acc * jnp.where(j < scaled_tiles, scale, 1.0)
    o_ref[...] = acc.astype(o_ref.dtype)


def norm_matmul(x, g, w, *, out_dtype, xcol=0, tm=1024, tn=512, scaled_cols=0, scale=1.0):
    M = x.shape[0]
    K, N = w.shape
    tm, tn = _tile(M, tm), _tile(N, tn)
    assert scaled_cols % tn == 0
    kern = functools.partial(_norm_mm_kernel, scaled_tiles=scaled_cols // tn, scale=scale)
    return pl.pallas_call(
        kern,
        out_shape=jax.ShapeDtypeStruct((M, N), out_dtype),
        grid=(M // tm, N // tn),
        in_specs=[pl.BlockSpec((tm, K), lambda i, j: (i, xcol)),
                  pl.BlockSpec((1, K), lambda i, j: (0, 0)),
                  pl.BlockSpec((K, tn), lambda i, j: (0, j))],
        out_specs=pl.BlockSpec((tm, tn), lambda i, j: (i, j)),
        scratch_shapes=[pltpu.VMEM((tm, K), BF16)],
        compiler_params=_cparams("parallel", "arbitrary"),
        name="norm_matmul",
    )(x, g.reshape(1, K), w)


def _norm_swiglu_kernel(x_ref, g_ref, wg_ref, wu_ref, o_ref, xn_ref):
    @pl.when(pl.program_id(1) == 0)
    def _():
        xn_ref[...] = _rms_rows(x_ref[...], g_ref[...]).astype(BF16)

    xn = xn_ref[...]
    a = jnp.dot(xn, wg_ref[...], preferred_element_type=F32)
    u = jnp.dot(xn, wu_ref[...], preferred_element_type=F32)
    o_ref[...] = (a * jax.nn.sigmoid(a) * u).astype(o_ref.dtype)


def norm_swiglu_in(x, g, w_in, *, tm=512, tn=512):
    M, K = x.shape
    F = w_in.shape[1] // 2
    tm, tn = _tile(M, tm), _tile(F, tn)
    nj = F // tn
    return pl.pallas_call(
        _norm_swiglu_kernel,
        out_shape=jax.ShapeDtypeStruct((M, F), BF16),
        grid=(M // tm, nj),
        in_specs=[pl.BlockSpec((tm, K), lambda i, j: (i, 0)),
                  pl.BlockSpec((1, K), lambda i, j: (0, 0)),
                  pl.BlockSpec((K, tn), lambda i, j: (0, j)),
                  pl.BlockSpec((K, tn), lambda i, j: (0, j + nj))],
        out_specs=pl.BlockSpec((tm, tn), lambda i, j: (i, j)),
        scratch_shapes=[pltpu.VMEM((tm, K), BF16)],
        compiler_params=_cparams("parallel", "arbitrary"),
        name="norm_swiglu_in",
    )(x, g.reshape(1, K), w_in, w_in)


def _mm_res_kernel(a_ref, w_ref, r_ref, o_ref):
    o_ref[...] = r_ref[...] + jnp.dot(a_ref[...], w_ref[...], preferred_element_type=F32)


def matmul_residual(a, w, res, *, tm=1024, tn=512):
    M, K = a.shape
    N = w.shape[1]
    tm, tn = _tile(M, tm), _tile(N, tn)
    return pl.pallas_call(
        _mm_res_kernel,
        out_shape=jax.ShapeDtypeStruct((M, N), F32),
        grid=(M // tm, N // tn),
        in_specs=[pl.BlockSpec((tm, K), lambda i, j: (i, 0)),
                  pl.BlockSpec((K, tn), lambda i, j: (0, j)),
                  pl.BlockSpec((tm, tn), lambda i, j: (i, j))],
        out_specs=pl.BlockSpec((tm, tn), lambda i, j: (i, j)),
        compiler_params=_cparams("parallel", "parallel"),
        name="matmul_residual",
    )(a, w, res)


def _ple_kernel(x_ref, g_ref, xr_ref, p_ref, wg_ref, wp_ref, o_ref, xn_ref):
    @pl.when(pl.program_id(1) == 0)
    def _():
        xn_ref[...] = _rms_rows(x_ref[...], g_ref[...]).astype(BF16)

    gate = jax.nn.sigmoid(jnp.dot(xn_ref[...], wg_ref[...], preferred_element_type=F32))
    proj = jnp.dot(p_ref[...].astype(BF16), wp_ref[...], preferred_element_type=F32)
    o_ref[...] = xr_ref[...] + gate * proj


def ple_update(x, g, p, w_gate, w_proj, *, tm=1024, tn=512):
    M, D = x.shape
    P = p.shape[1]
    tm, tn = _tile(M, tm), _tile(D, tn)
    return pl.pallas_call(
        _ple_kernel,
        out_shape=jax.ShapeDtypeStruct((M, D), F32),
        grid=(M // tm, D // tn),
        in_specs=[pl.BlockSpec((tm, D), lambda i, j: (i, 0)),
                  pl.BlockSpec((1, D), lambda i, j: (0, 0)),
                  pl.BlockSpec((tm, tn), lambda i, j: (i, j)),
                  pl.BlockSpec((tm, P), lambda i, j: (i, 0)),
                  pl.BlockSpec((D, tn), lambda i, j: (0, j)),
                  pl.BlockSpec((P, tn), lambda i, j: (0, j))],
        out_specs=pl.BlockSpec((tm, tn), lambda i, j: (i, j)),
        scratch_shapes=[pltpu.VMEM((tm, D), BF16)],
        compiler_params=_cparams("parallel", "arbitrary"),
        name="ple_update",
    )(x, g.reshape(1, D), x, p, w_gate, w_proj)


def _ple_moe_kernel(x_ref, y_ref, gt_ref, g_ref, p_ref, wg_ref, wp_ref, o_ref, xs_ref, xn_ref, *, tn):
    j = pl.program_id(1)
    D = x_ref.shape[1]

    @pl.when(j == 0)
    def _():
        gt = gt_ref[...]
        xnew = x_ref[...] + gt[:, 0:1] * y_ref[:, :D] + gt[:, 1:2] * y_ref[:, D:]
        xn_ref[...] = _rms_rows(xnew, g_ref[...]).astype(BF16)
        for jj in range(xs_ref.shape[0]):
            xs_ref[jj] = xnew[:, jj * tn:(jj + 1) * tn]

    gate = jax.nn.sigmoid(jnp.dot(xn_ref[...], wg_ref[...], preferred_element_type=F32))
    proj = jnp.dot(p_ref[...].astype(BF16), wp_ref[...], preferred_element_type=F32)
    o_ref[...] = xs_ref[j] + gate * proj


def ple_update_moe(x, y12, gates, g, p, w_gate, w_proj, *, tm=512, tn=512):
    M, D = x.shape
    P = p.shape[1]
    tm, tn = _tile(M, tm), _tile(D, tn)
    return pl.pallas_call(
        functools.partial(_ple_moe_kernel, tn=tn),
        out_shape=jax.ShapeDtypeStruct((M, D), F32),
        grid=(M // tm, D // tn),
        in_specs=[pl.BlockSpec((tm, D), lambda i, j: (i, 0)),
                  pl.BlockSpec((tm, 2 * D), lambda i, j: (i, 0)),
                  pl.BlockSpec((tm, 2), lambda i, j: (i, 0)),
                  pl.BlockSpec((1, D), lambda i, j: (0, 0)),
                  pl.BlockSpec((tm, P), lambda i, j: (i, 0)),
                  pl.BlockSpec((D, tn), lambda i, j: (0, j)),
                  pl.BlockSpec((P, tn), lambda i, j: (0, j))],
        out_specs=pl.BlockSpec((tm, tn), lambda i, j: (i, j)),
        scratch_shapes=[pltpu.VMEM((D // tn, tm, tn), F32), pltpu.VMEM((tm, D), BF16)],
        compiler_params=_cparams("parallel", "arbitrary"),
        name="ple_update_moe",
    )(x, y12, gates, g.reshape(1, D), p, w_gate, w_proj)


def _final_norm_kernel(x_ref, g_ref, o_ref):
    o_ref[...] = _rms_rows(x_ref[...], g_ref[...])


def final_norm(x, g, *, tm=512):
    M, D = x.shape
    tm = _tile(M, tm)
    return pl.pallas_call(
        _final_norm_kernel,
        out_shape=jax.ShapeDtypeStruct((M, D), F32),
        grid=(M // tm,),
        in_specs=[pl.BlockSpec((tm, D), lambda i: (i, 0)),
                  pl.BlockSpec((1, D), lambda i: (0, 0))],
        out_specs=pl.BlockSpec((tm, D), lambda i: (i, 0)),
        compiler_params=_cparams("parallel"),
        name="final_norm",
    )(x, g.reshape(1, D))


def _pair_tables(n_tiles):
    qi, kj = [], []
    for i in range(n_tiles):
        for j in range(i + 1):
            qi.append(i)
            kj.append(j)
    return jnp.asarray(np.array(qi, np.int32)), jnp.asarray(np.array(kj, np.int32))


def _rel_bucket(dist):
    n = jnp.maximum(dist, 0)
    max_exact = REL_BUCKETS // 2
    nf = jnp.maximum(n, 1).astype(F32)
    large = max_exact + (jnp.log(nf / max_exact) / math.log(REL_MAX_DIST / max_exact)
                         * (REL_BUCKETS - max_exact)).astype(jnp.int32)
    large = jnp.minimum(large, REL_BUCKETS - 1)
    return jnp.where(n < max_exact, n, large)


def _bias_lut(rel_bias):
    buckets = _rel_bucket(jnp.arange(LUT_SIZE, dtype=jnp.int32))
    return rel_bias.astype(F32)[buckets].T * LOG2E


def _tile_pos_bounds(pos, tq, ck):
    S = pos.shape[0]
    return pos.reshape(S // tq, tq).min(axis=1), pos.reshape(S // ck, ck).max(axis=1)


def _lut_bias(lut_row, idx):
    R, C = idx.shape
    lut = jnp.broadcast_to(lut_row, (R, LANES))
    parts = [jnp.take_along_axis(lut, idx[:, c:c + LANES], axis=1) for c in range(0, C, LANES)]
    return parts[0] if len(parts) == 1 else jnp.concatenate(parts, axis=1)


def _pipelined(units, scores, consume):
    st = scores(*units[0])
    for k, u in enumerate(units):
        nxt = scores(*units[k + 1]) if k + 1 < len(units) else None
        consume(*u, st)
        st = nxt


def _online_softmax_step_t(st, vt, m_ref, l_ref, acc_ref, qcols, m_cur=None, offsets=None):
    m_prev = m_ref[:, qcols]
    if m_cur is None:
        m_cur = jnp.max(st, axis=0, keepdims=True)
    m_new = jnp.maximum(m_prev, m_cur)
    alpha = jnp.exp2(m_prev - m_new)
    if offsets is None:
        p = jnp.exp2(st - m_new)
    else:
        parts = [jnp.exp2(st[rows, :] - fn(m_new)) for rows, fn in offsets]
        p = parts[0] if len(parts) == 1 else jnp.concatenate(parts, axis=0)
    l_ref[:, qcols] = alpha * l_ref[:, qcols] + jnp.sum(p, axis=0, keepdims=True)
    acc_ref[:, qcols] = alpha * acc_ref[:, qcols] + jnp.dot(vt, p.astype(BF16), preferred_element_type=F32)
    m_ref[:, qcols] = m_new


def _causal_t(k0, nk, q0, nq):
    r = k0 + lax.broadcasted_iota(jnp.int32, (nk, nq), 0)
    c = q0 + lax.broadcasted_iota(jnp.int32, (nk, nq), 1)
    return r <= c


def _nt_dot(a, b):
    return lax.dot_general(a, b, (((1,), (1,)), ((), ())), preferred_element_type=F32)


def _diff_attn_kernel(qi_ref, kj_ref, pminq_ref, pmaxk_ref,
                      q_ref, k_ref, vt_ref, pq_ref, pk_ref, lut_ref, lam_ref, g_ref,
                      o_ref, m_ref, l_ref, acc_ref, *, tq, ck, lambda_init):
    p = pl.program_id(1)
    i, j = qi_ref[p], kj_ref[p]
    nc = tq // ck
    dh = DIFF_HEAD_DIM

    @pl.when(j == 0)
    def _():
        m_ref[...] = jnp.full(m_ref.shape, NEG_INF, F32)
        l_ref[...] = jnp.zeros(l_ref.shape, F32)
        acc_ref[...] = jnp.zeros(acc_ref.shape, F32)

    def scores(c, mp, q0, mode):
        keys = slice(c * ck, (c + 1) * ck)
        st = _nt_dot(k_ref[keys, mp * dh:(mp + 1) * dh], q_ref[q0:tq, mp * dh:(mp + 1) * dh])
        if mode != "far":
            idx = jnp.clip(pq_ref[:, q0:tq] - pk_ref[keys, :], 0, LUT_SIZE - 1)
            st = st + _lut_bias(lut_ref[mp:mp + 1, :], idx)
        if mode == "diag":
            st = jnp.where(_causal_t(c * ck, ck, q0, tq - q0), st, NEG_INF)
        return st

    def consume(c, mp, q0, mode, st):
        m_cur = offsets = None
        if mode == "far":
            bias = lut_ref[mp:mp + 1, LUT_SIZE - 1:LUT_SIZE]
            m_cur = jnp.max(st, axis=0, keepdims=True) + bias
            offsets = [(slice(None), lambda m_new: m_new - bias)]
        _online_softmax_step_t(st, vt_ref[:, c * ck:(c + 1) * ck], m_ref.at[mp], l_ref.at[mp], acc_ref.at[mp],
                               slice(q0, tq), m_cur, offsets)

    def sweep(chunks, mode):
        units = [(c, mp, c * ck if mode == "diag" else 0, mode) for c in chunks for mp in range(2)]
        _pipelined(units, scores, consume)

    @pl.when(j < i)
    def _():
        fars = [pminq_ref[i] - pmaxk_ref[j * nc + c] >= LUT_SIZE - 1 for c in range(nc)]
        all_far = functools.reduce(jnp.logical_and, fars)
        pl.when(all_far)(functools.partial(sweep, range(nc), "far"))

        @pl.when(jnp.logical_not(all_far))
        def _():
            for c in range(nc):
                pl.when(fars[c])(functools.partial(sweep, [c], "far"))
                pl.when(jnp.logical_not(fars[c]))(functools.partial(sweep, [c], "near"))

    @pl.when(j == i)
    def _():
        sweep(range(nc), "diag")
        lam = lam_ref[...]
        lam_full = (jnp.exp(jnp.sum(lam[0:1] * lam[1:2], axis=-1, keepdims=True))
                    - jnp.exp(jnp.sum(lam[2:3] * lam[3:4], axis=-1, keepdims=True)) + lambda_init)
        a = acc_ref[0] / l_ref[0] - lam_full * (acc_ref[1] / l_ref[1])
        r = lax.rsqrt(jnp.mean(a * a, axis=0, keepdims=True) + NORM_EPS)
        y = a * r * (g_ref[...] * (1.0 - lambda_init))
        o_ref[...] = y.T.astype(o_ref.dtype)


def diff_attention(qkv, vt, pos, lut, lam, subln_g, lambda_init, *, tile=1024, chunk=512):
    S = qkv.shape[0]
    H, dh = DIFF_HEADS, DIFF_HEAD_DIM
    tq = _tile(S, tile)
    ck = _tile(tq, chunk)
    qi, kj = _pair_tables(S // tq)
    pminq, pmaxk = _tile_pos_bounds(pos, tq, ck)
    kern = functools.partial(_diff_attn_kernel, tq=tq, ck=ck, lambda_init=lambda_init)
    grid_spec = pltpu.PrefetchScalarGridSpec(
        num_scalar_prefetch=4,
        grid=(H, qi.shape[0]),
        in_specs=[pl.BlockSpec((tq, 2 * dh), lambda h, p, qi, kj, a, b: (qi[p], h)),
                  pl.BlockSpec((tq, 2 * dh), lambda h, p, qi, kj, a, b: (kj[p], H + h)),
                  pl.BlockSpec((2 * dh, tq), lambda h, p, qi, kj, a, b: (h, kj[p])),
                  pl.BlockSpec((1, tq), lambda h, p, qi, kj, a, b: (0, qi[p])),
                  pl.BlockSpec((tq, 1), lambda h, p, qi, kj, a, b: (kj[p], 0)),
                  pl.BlockSpec((None, 2, LUT_SIZE), lambda h, p, qi, kj, a, b: (h, 0, 0)),
                  pl.BlockSpec((4, dh), lambda h, p, qi, kj, a, b: (0, 0)),
                  pl.BlockSpec((2 * dh, 1), lambda h, p, qi, kj, a, b: (0, 0))],
        out_specs=pl.BlockSpec((tq, 2 * dh), lambda h, p, qi, kj, a, b: (qi[p], h)),
        scratch_shapes=[pltpu.VMEM((2, 1, tq), F32), pltpu.VMEM((2, 1, tq), F32),
                        pltpu.VMEM((2, 2 * dh, tq), F32)],
    )
    return pl.pallas_call(
        kern,
        out_shape=jax.ShapeDtypeStruct((S, H * 2 * dh), BF16),
        grid_spec=grid_spec,
        compiler_params=_cparams("parallel", "arbitrary"),
        name="diff_attention",
    )(qi, kj, pminq, pmaxk, qkv, qkv, vt, pos.reshape(1, S), pos.reshape(S, 1),
      lut.reshape(H, 2, LUT_SIZE), lam, subln_g.reshape(2 * dh, 1))


def _kmean_kernel(k_ref, hi_ref, lo_ref):
    km = jnp.mean(k_ref[...].astype(F32), axis=0, keepdims=True)
    hi = km.astype(BF16)
    hi_ref[...] = hi
    lo_ref[...] = (km - hi.astype(F32)).astype(BF16)


def moba_block_means(qkv, D):
    S = qkv.shape[0]
    nblk = S // MOBA_BLOCK
    hi, lo = pl.pallas_call(
        _kmean_kernel,
        out_shape=[jax.ShapeDtypeStruct((nblk, 1, D), BF16)] * 2,
        grid=(nblk,),
        in_specs=[pl.BlockSpec((MOBA_BLOCK, D), lambda n: (n, 1))],
        out_specs=[pl.BlockSpec((None, 1, D), lambda n: (n, 0, 0))] * 2,
        compiler_params=_cparams("parallel"),
        name="moba_block_means",
    )(qkv)
    return hi.reshape(nblk, D), lo.reshape(nblk, D)


def _moba_attn_kernel(qi_ref, kj_ref, pminq_ref, pmaxk_ref,
                      q_ref, k_ref, vt_ref, kmh_ref, kml_ref, pq_ref, pk_ref, lut_ref,
                      o_ref, m_ref, l_ref, acc_ref, sel_ref, *, tq, ck, hp):
    p = pl.program_id(1)
    i, j = qi_ref[p], kj_ref[p]
    nc = tq // ck
    L, dh = MOBA_BLOCK, MOBA_HEAD_DIM
    bpc = ck // L
    log2_l = L.bit_length() - 1

    @pl.when(j == 0)
    def _():
        m_ref[...] = jnp.full(m_ref.shape, NEG_INF, F32)
        l_ref[...] = jnp.zeros(l_ref.shape, F32)
        acc_ref[...] = jnp.zeros(acc_ref.shape, F32)
        blk = lax.broadcasted_iota(jnp.int32, (LANES, tq), 0)
        blk_f = blk.astype(F32)
        own = lax.shift_right_logical(i * tq + lax.broadcasted_iota(jnp.int32, (1, tq), 1), log2_l)
        for hd in range(hp):
            q = q_ref[:, hd * dh:(hd + 1) * dh]
            gs = (_nt_dot(kmh_ref[:, hd * dh:(hd + 1) * dh], q)
                  + _nt_dot(kml_ref[:, hd * dh:(hd + 1) * dh], q))
            g = jnp.where(blk < own, gs, NEG_INF)
            sel = jnp.zeros((LANES, tq), F32)
            for _ in range(MOBA_TOPK):
                best = jnp.max(g, axis=0, keepdims=True)
                first = jnp.min(jnp.where(g == best, blk_f, float(LANES)), axis=0, keepdims=True)
                pick = blk_f == first
                sel = jnp.where(pick, jnp.where(best > 0.5 * NEG_INF, 1.0, sel), sel)
                g = jnp.where(pick, -3e38, g)
            sel_ref[hd] = sel

    def scores(c, hd, q0, mode):
        keys = slice(c * ck, (c + 1) * ck)
        nq = tq - q0
        st = _nt_dot(k_ref[keys, hd * dh:(hd + 1) * dh], q_ref[q0:tq, hd * dh:(hd + 1) * dh])
        if mode != "far":
            idx = jnp.clip(pq_ref[:, q0:tq] - pk_ref[keys, :], 0, LUT_SIZE - 1)
            st = st + _lut_bias(lut_ref[hd:hd + 1, :], idx)
        if mode != "diag":
            return st
        allowed = []
        for b in range(bpc):
            n_local = c * bpc + b
            picked = jnp.broadcast_to(picked_row(hd, n_local, q0), (L, nq))
            own_local = lax.shift_right_logical(q0 + lax.broadcasted_iota(jnp.int32, (L, nq), 1), log2_l)
            visible = jnp.where(_causal_t(n_local * L, L, q0, nq), 1.0, 0.0)
            allowed.append(jnp.where(own_local == n_local, visible, picked))
        allowed = allowed[0] if bpc == 1 else jnp.concatenate(allowed, axis=0)
        return jnp.where(allowed > 0.0, st, NEG_INF)

    def picked_row(hd, n_local, q0):
        return sel_ref[hd, pl.ds(j * (tq // L) + n_local, 1), q0:tq]

    def consume(c, hd, q0, mode, st):
        m_cur = offsets = None
        if mode != "diag":
            bias = lut_ref[hd:hd + 1, LUT_SIZE - 1:LUT_SIZE] if mode == "far" else 0.0
            picked = [picked_row(hd, c * bpc + b, q0) > 0.0 for b in range(bpc)]
            m_cur = functools.reduce(jnp.maximum, [
                jnp.where(picked[b], jnp.max(st[b * L:(b + 1) * L, :], axis=0, keepdims=True) + bias, NEG_INF)
                for b in range(bpc)])
            offsets = [(slice(b * L, (b + 1) * L),
                        functools.partial(lambda m_new, pk: jnp.where(pk, m_new - bias, -NEG_INF), pk=picked[b]))
                       for b in range(bpc)]
        _online_softmax_step_t(st, vt_ref[hd * dh:(hd + 1) * dh, c * ck:(c + 1) * ck],
                               m_ref.at[hd], l_ref.at[hd], acc_ref.at[hd], slice(q0, tq), m_cur, offsets)

    def sweep(chunks, mode):
        _pipelined([(c, hd, c * ck if mode == "diag" else 0, mode) for c in chunks for hd in range(hp)],
                   scores, consume)

    @pl.when(j < i)
    def _():
        fars = [pminq_ref[i] - pmaxk_ref[j * nc + c] >= LUT_SIZE - 1 for c in range(nc)]
        all_far = functools.reduce(jnp.logical_and, fars)
        pl.when(all_far)(functools.partial(sweep, range(nc), "far"))

        @pl.when(jnp.logical_not(all_far))
        def _():
            for c in range(nc):
                pl.when(fars[c])(functools.partial(sweep, [c], "far"))
                pl.when(jnp.logical_not(fars[c]))(functools.partial(sweep, [c], "near"))

    @pl.when(j == i)
    def _():
        sweep(range(nc), "diag")
        for hd in range(hp):
            o_ref[:, hd * dh:(hd + 1) * dh] = (acc_ref[hd] / l_ref[hd]).T.astype(o_ref.dtype)


def moba_attention(qkv, vt, pos, lut, *, tile=1024, chunk=512, heads_per_step=2):
    S = qkv.shape[0]
    H, dh, L = MOBA_HEADS, MOBA_HEAD_DIM, MOBA_BLOCK
    D = H * dh
    nblk = S // L
    assert S % L == 0 and nblk <= LANES
    tq = _tile(S, tile)
    ck = _tile(tq, chunk)
    assert tq % L == 0 and ck % L == 0
    km_hi, km_lo = moba_block_means(qkv, D)
    km_hi = jnp.pad(km_hi, ((0, LANES - nblk), (0, 0)))
    km_lo = jnp.pad(km_lo, ((0, LANES - nblk), (0, 0)))
    qi, kj = _pair_tables(S // tq)
    pminq, pmaxk = _tile_pos_bounds(pos, tq, ck)
    hp = heads_per_step
    G = H // hp
    kern = functools.partial(_moba_attn_kernel, tq=tq, ck=ck, hp=hp)
    grid_spec = pltpu.PrefetchScalarGridSpec(
        num_scalar_prefetch=4,
        grid=(G, qi.shape[0]),
        in_specs=[pl.BlockSpec((tq, hp * dh), lambda h, p, qi, kj, a, b: (qi[p], h)),
                  pl.BlockSpec((tq, hp * dh), lambda h, p, qi, kj, a, b: (kj[p], G + h)),
                  pl.BlockSpec((hp * dh, tq), lambda h, p, qi, kj, a, b: (h, kj[p])),
                  pl.BlockSpec((LANES, hp * dh), lambda h, p, qi, kj, a, b: (0, h)),
                  pl.BlockSpec((LANES, hp * dh), lambda h, p, qi, kj, a, b: (0, h)),
                  pl.BlockSpec((1, tq), lambda h, p, qi, kj, a, b: (0, qi[p])),
                  pl.BlockSpec((tq, 1), lambda h, p, qi, kj, a, b: (kj[p], 0)),
                  pl.BlockSpec((None, hp, LUT_SIZE), lambda h, p, qi, kj, a, b: (h, 0, 0))],
        out_specs=pl.BlockSpec((tq, hp * dh), lambda h, p, qi, kj, a, b: (qi[p], h)),
        scratch_shapes=[pltpu.VMEM((hp, 1, tq), F32), pltpu.VMEM((hp, 1, tq), F32),
                        pltpu.VMEM((hp, dh, tq), F32), pltpu.VMEM((hp, LANES, tq), F32)],
    )
    return pl.pallas_call(
        kern,
        out_shape=jax.ShapeDtypeStruct((S, D), BF16),
        grid_spec=grid_spec,
        compiler_params=_cparams("parallel", "arbitrary"),
        name="moba_attention",
    )(qi, kj, pminq, pmaxk, qkv, qkv, vt, km_hi, km_lo, pos.reshape(1, S), pos.reshape(S, 1),
      lut.reshape(G, hp, LUT_SIZE))


def _rope_fold(y):
    lane = lax.broadcasted_iota(jnp.int32, y.shape, 1)
    return jnp.where(lane < MLA_ROPE, y + pltpu.roll(y, MLA_ROPE, 1), 0.0)


def _mla_q_kernel(cq_ref, g_ref, w_ref, kr_ref, t_ref, q_ref, kro_ref, cqn_ref, *, scale):
    t = t_ref[...]

    @pl.when(pl.program_id(1) == 0)
    def _():
        cqn_ref[...] = _rms_rows(cq_ref[...], g_ref[...]).astype(BF16)
        kro_ref[...] = _rope_fold(kr_ref[...] * t).astype(kro_ref.dtype)

    qh = jnp.dot(cqn_ref[...], w_ref[...], preferred_element_type=F32)
    q_ref[:, :MLA_NOPE] = (qh[:, :MLA_NOPE] * scale).astype(q_ref.dtype)
    q_ref[:, MLA_NOPE:] = (_rope_fold(qh[:, MLA_NOPE:] * t) * scale).astype(q_ref.dtype)


def mla_queries_and_rope_key(down, g_q, w_uq_ext, rope_tab, *, scale, tm=512):
    S = down.shape[0]
    H = MLA_HEADS
    tm = _tile(S, tm)
    W = 2 * LANES
    return pl.pallas_call(
        functools.partial(_mla_q_kernel, scale=scale),
        out_shape=[jax.ShapeDtypeStruct((S, H * W), BF16), jax.ShapeDtypeStruct((S, LANES), BF16)],
        grid=(S // tm, H),
        in_specs=[pl.BlockSpec((tm, MLA_Q_RANK), lambda i, h: (i, 0)),
                  pl.BlockSpec((1, MLA_Q_RANK), lambda i, h: (0, 0)),
                  pl.BlockSpec((MLA_Q_RANK, W), lambda i, h: (0, h)),
                  pl.BlockSpec((tm, LANES), lambda i, h: (i, (MLA_Q_RANK + MLA_KV_RANK) // LANES)),
                  pl.BlockSpec((tm, LANES), lambda i, h: (i, 0))],
        out_specs=[pl.BlockSpec((tm, W), lambda i, h: (i, h)),
                   pl.BlockSpec((tm, LANES), lambda i, h: (i, 0))],
        scratch_shapes=[pltpu.VMEM((tm, MLA_Q_RANK), BF16)],
        compiler_params=_cparams("parallel", "arbitrary"),
        name="mla_queries",
    )(down, g_q.reshape(1, MLA_Q_RANK), w_uq_ext, down, rope_tab)


def _mla_attn_kernel(qi_ref, kj_ref, q_ref, kv_ref, kr_ref, vt_ref, o_ref, m_ref, l_ref, acc_ref, *, tq, ck, hp):
    p = pl.program_id(1)
    i, j = qi_ref[p], kj_ref[p]
    nc = tq // ck
    W = 2 * LANES

    @pl.when(j == 0)
    def _():
        m_ref[...] = jnp.full(m_ref.shape, NEG_INF, F32)
        l_ref[...] = jnp.zeros(l_ref.shape, F32)
        acc_ref[...] = jnp.zeros(acc_ref.shape, F32)

    def scores(c, g, q0, diag):
        keys = slice(c * ck, (c + 1) * ck)
        kc = jnp.concatenate([kv_ref[keys, g * W:g * W + MLA_NOPE], kr_ref[keys, :]], axis=1)
        st = _nt_dot(kc, q_ref[q0:tq, g * W:(g + 1) * W])
        if diag:
            st = jnp.where(_causal_t(c * ck, ck, q0, tq - q0), st, NEG_INF)
        return st

    def consume(c, g, q0, diag, st):
        _online_softmax_step_t(st, vt_ref[g * MLA_V:(g + 1) * MLA_V, c * ck:(c + 1) * ck],
                               m_ref.at[g], l_ref.at[g], acc_ref.at[g], slice(q0, tq))

    def sweep(diag):
        _pipelined([(c, g, c * ck if diag else 0, diag) for c in range(nc) for g in range(hp)], scores, consume)

    @pl.when(j < i)
    def _():
        sweep(False)

    @pl.when(j == i)
    def _():
        sweep(True)
        for g in range(hp):
            o_ref[:, g * MLA_V:(g + 1) * MLA_V] = (acc_ref[g] / l_ref[g]).T.astype(o_ref.dtype)


def mla_attention(q, kv, kr, vt, *, tile=1024, chunk=512, heads_per_step=2):
    S = q.shape[0]
    H, hp = MLA_HEADS, heads_per_step
    tq = _tile(S, tile)
    ck = _tile(tq, chunk)
    qi, kj = _pair_tables(S // tq)
    W = 2 * LANES
    grid_spec = pltpu.PrefetchScalarGridSpec(
        num_scalar_prefetch=2,
        grid=(H // hp, qi.shape[0]),
        in_specs=[pl.BlockSpec((tq, hp * W), lambda h, p, qi, kj: (qi[p], h)),
                  pl.BlockSpec((tq, hp * W), lambda h, p, qi, kj: (kj[p], h)),
                  pl.BlockSpec((tq, LANES), lambda h, p, qi, kj: (kj[p], 0)),
                  pl.BlockSpec((hp * MLA_V, tq), lambda h, p, qi, kj: (h, kj[p]))],
        out_specs=pl.BlockSpec((tq, hp * MLA_V), lambda h, p, qi, kj: (qi[p], h)),
        scratch_shapes=[pltpu.VMEM((hp, 1, tq), F32), pltpu.VMEM((hp, 1, tq), F32),
                        pltpu.VMEM((hp, MLA_V, tq), F32)],
    )
    return pl.pallas_call(
        functools.partial(_mla_attn_kernel, tq=tq, ck=ck, hp=hp),
        out_shape=jax.ShapeDtypeStruct((S, H * MLA_V), BF16),
        grid_spec=grid_spec,
        compiler_params=_cparams("parallel", "arbitrary"),
        name="mla_attention",
    )(qi, kj, q, kv, kr, vt)


def _router_kernel(x_ref, g_ref, wh_ref, wl_ref, xn_ref, r_ref):
    xn = _rms_rows(x_ref[...], g_ref[...])
    hi = xn.astype(BF16)
    lo = (xn - hi.astype(F32)).astype(BF16)
    xn_ref[...] = hi
    logits = (jnp.dot(hi, wh_ref[...], preferred_element_type=F32)
              + jnp.dot(hi, wl_ref[...], preferred_element_type=F32)
              + jnp.dot(lo, wh_ref[...], preferred_element_type=F32))
    lane = lax.broadcasted_iota(jnp.int32, logits.shape, 1)
    lane_f = lane.astype(F32)
    g = jnp.where(lane < N_EXPERTS, logits, NEG_INF)
    v1 = jnp.max(g, axis=-1, keepdims=True)
    i1 = jnp.min(jnp.where(g == v1, lane_f, float(LANES)), axis=-1, keepdims=True)
    g = jnp.where(lane_f == i1, NEG_INF, g)
    v2 = jnp.max(g, axis=-1, keepdims=True)
    i2 = jnp.min(jnp.where(g == v2, lane_f, float(LANES)), axis=-1, keepdims=True)
    g2 = 1.0 / (1.0 + jnp.exp(v1 - v2))
    g1 = 1.0 - g2
    r_ref[...] = jnp.where(lane == 0, i1,
                           jnp.where(lane == 1, i2,
                                     jnp.where(lane == 2, g1, jnp.where(lane == 3, g2, 0.0))))


def moe_route(x, g, w_router, *, tm=512):
    M, D = x.shape
    tm = _tile(M, tm)
    wr = jnp.pad(w_router.astype(F32), ((0, 0), (0, LANES - N_EXPERTS)))
    wh = wr.astype(BF16)
    wl = (wr - wh.astype(F32)).astype(BF16)
    return pl.pallas_call(
        _router_kernel,
        out_shape=[jax.ShapeDtypeStruct((M, D), BF16), jax.ShapeDtypeStruct((M, LANES), F32)],
        grid=(M // tm,),
        in_specs=[pl.BlockSpec((tm, D), lambda i: (i, 0)),
                  pl.BlockSpec((1, D), lambda i: (0, 0)),
                  pl.BlockSpec((D, LANES), lambda i: (0, 0)),
                  pl.BlockSpec((D, LANES), lambda i: (0, 0))],
        out_specs=[pl.BlockSpec((tm, D), lambda i: (i, 0)),
                   pl.BlockSpec((tm, LANES), lambda i: (i, 0))],
        compiler_params=_cparams("parallel"),
        name="moe_route",
    )(x, g.reshape(1, D), wh, wl)


def _new_expert(te_ref, t):
    return jnp.logical_or(t == 0, te_ref[t] != te_ref[jnp.maximum(t - 1, 0)])


def _gmm_swiglu_kernel(te_ref, nu_ref, x_ref, wg_ref, wu_ref, o_ref, wgb_ref, wub_ref):
    t = pl.program_id(1)
    used = t < nu_ref[0]

    @pl.when(jnp.logical_and(used, _new_expert(te_ref, t)))
    def _():
        wgb_ref[...] = wg_ref[...].astype(BF16)
        wub_ref[...] = wu_ref[...].astype(BF16)

    @pl.when(used)
    def _():
        x = x_ref[...]
        a = jnp.dot(x, wgb_ref[...], preferred_element_type=F32)
        u = jnp.dot(x, wub_ref[...], preferred_element_type=F32)
        o_ref[...] = (a * jax.nn.sigmoid(a) * u).astype(o_ref.dtype)

    @pl.when(jnp.logical_not(used))
    def _():
        o_ref[...] = jnp.zeros(o_ref.shape, o_ref.dtype)


def _gmm_out_kernel(te_ref, nu_ref, h_ref, w_ref, o_ref, wb_ref):
    t = pl.program_id(1)
    used = t < nu_ref[0]

    @pl.when(jnp.logical_and(used, _new_expert(te_ref, t)))
    def _():
        wb_ref[...] = w_ref[...].astype(BF16)

    @pl.when(used)
    def _():
        o_ref[...] = jnp.dot(h_ref[...], wb_ref[...], preferred_element_type=F32)

    @pl.when(jnp.logical_not(used))
    def _():
        o_ref[...] = jnp.zeros(o_ref.shape, o_ref.dtype)


def moe_experts(xs, tile_expert, n_used, w_in, w_out, layer, *, tm, tn=512):
    P, D = xs.shape
    F = w_in.shape[3] // 2
    nt = P // tm
    tn1, tn2 = _tile(F, tn), _tile(D, tn)
    nj1, nj2 = F // tn1, D // tn2
    h = pl.pallas_call(
        _gmm_swiglu_kernel,
        out_shape=jax.ShapeDtypeStruct((P, F), BF16),
        grid_spec=pltpu.PrefetchScalarGridSpec(
            num_scalar_prefetch=2,
            grid=(nj1, nt),
            in_specs=[pl.BlockSpec((tm, D), lambda j, t, te, nu: (t, 0)),
                      pl.BlockSpec((None, None, D, tn1), lambda j, t, te, nu: (layer, te[t], 0, j)),
                      pl.BlockSpec((None, None, D, tn1), lambda j, t, te, nu: (layer, te[t], 0, nj1 + j))],
            out_specs=pl.BlockSpec((tm, tn1), lambda j, t, te, nu: (t, j)),
            scratch_shapes=[pltpu.VMEM((D, tn1), BF16), pltpu.VMEM((D, tn1), BF16)],
        ),
        compiler_params=_cparams("arbitrary", "arbitrary"),
        name="moe_swiglu_in",
    )(tile_expert, n_used, xs, w_in, w_in)
    return pl.pallas_call(
        _gmm_out_kernel,
        out_shape=jax.ShapeDtypeStruct((P, D), F32),
        grid_spec=pltpu.PrefetchScalarGridSpec(
            num_scalar_prefetch=2,
            grid=(nj2, nt),
            in_specs=[pl.BlockSpec((tm, F), lambda j, t, te, nu: (t, 0)),
                      pl.BlockSpec((None, None, F, tn2), lambda j, t, te, nu: (layer, te[t], 0, j))],
            out_specs=pl.BlockSpec((tm, tn2), lambda j, t, te, nu: (t, j)),
            scratch_shapes=[pltpu.VMEM((F, tn2), BF16)],
        ),
        compiler_params=_cparams("arbitrary", "arbitrary"),
        name="moe_out",
    )(tile_expert, n_used, h, w_out)


def moe_swiglu(x, g, w_router, w_in, w_out, layer, *, tm=512):
    S, D = x.shape
    E = N_EXPERTS
    tm = _tile(S, tm)
    xn, route = moe_route(x, g, w_router)
    experts = route[:, :2].astype(jnp.int32)
    gates = route[:, 2:4]
    flat_e = experts.reshape(-1)
    order = jnp.argsort(flat_e, stable=True).astype(jnp.int32)
    rank = jnp.argsort(order).astype(jnp.int32)
    counts = jnp.bincount(flat_e, length=E).astype(jnp.int32)
    padded = ((counts + tm - 1) // tm) * tm
    start = jnp.cumsum(counts) - counts
    pstart = jnp.cumsum(padded) - padded
    pend = jnp.cumsum(padded)
    P = 2 * S + E * tm
    nt = P // tm
    tile_expert = jnp.minimum(
        jnp.searchsorted(pend, jnp.arange(nt, dtype=jnp.int32) * tm, side="right"), E - 1).astype(jnp.int32)
    n_used = (pend[-1:] // tm).astype(jnp.int32)
    row_e = jnp.repeat(tile_expert, tm)
    row_rank = jnp.arange(P, dtype=jnp.int32) - pstart[row_e]
    row_valid = jnp.logical_and(row_rank < counts[row_e], jnp.arange(P) < pend[-1])
    row_token = jnp.where(row_valid, order[jnp.clip(start[row_e] + row_rank, 0, 2 * S - 1)] // 2, 0)
    slot = pstart[flat_e] + rank - start[flat_e]
    tile_expert = jnp.where(jnp.arange(nt) < n_used[0], tile_expert, tile_expert[jnp.maximum(n_used[0] - 1, 0)])
    xs = jnp.take(xn, row_token, axis=0)
    ys = moe_experts(xs, tile_expert, n_used, w_in, w_out, layer, tm=tm)
    return jnp.take(ys, slot, axis=0).reshape(S, 2 * D), gates


def _diff_lambda_init(layer):
    return 0.8 - 0.6 * math.exp(-0.3 * layer)


def _rope_table(pos):
    half = MLA_ROPE // 2
    inv_freq = ROPE_THETA ** (-jnp.arange(half, dtype=F32) / half)
    ang = pos.astype(F32)[:, None] * inv_freq
    cos, sin = jnp.cos(ang), jnp.sin(ang)
    return jnp.concatenate([cos, cos, -sin, sin], axis=1)


def _rot_half_cols(w):
    half = w.shape[-1] // 2
    return jnp.concatenate([w[..., half:], w[..., :half]], axis=-1)


def kernel(x, p, positions, rel_bias, norm_mix, norm_ffn, norm_ple, norm_final, diff_w_qkv, diff_lambda, diff_subln, diff_w_o, moba_w_qkv, moba_w_o, mla_w_down, mla_g_q, mla_w_uq, mla_g_kv, mla_w_ukv, mla_w_o, ffn_w_in, ffn_w_out, moe_w_router, moe_w_in, moe_w_out, ple_w_gate, ple_w_proj):
    B, S, D = x.shape
    depth = p.shape[0]
    lut = _bias_lut(rel_bias)
    outs = []
    for b in range(B):
        xb = x[b]
        pos = positions[b].astype(jnp.int32)
        for i in range(depth):
            jm = i // N_MIXERS
            if i % N_MIXERS == 0:
                qkv = norm_matmul(xb, norm_mix[i], diff_w_qkv[jm].astype(BF16), out_dtype=BF16,
                                  scaled_cols=D, scale=DIFF_HEAD_DIM ** -0.5 * LOG2E)
                o = diff_attention(qkv, qkv[:, 2 * D:].T, pos, lut, diff_lambda[jm].astype(F32), diff_subln[jm],
                                   _diff_lambda_init(i))
                xb = matmul_residual(o, diff_w_o[jm].astype(BF16), xb)
            elif i % N_MIXERS == 1:
                qkv = norm_matmul(xb, norm_mix[i], moba_w_qkv[jm].astype(BF16), out_dtype=BF16,
                                  scaled_cols=D, scale=MOBA_HEAD_DIM ** -0.5 * LOG2E)
                o = moba_attention(qkv, qkv[:, 2 * D:].T, pos, lut)
                xb = matmul_residual(o, moba_w_o[jm].astype(BF16), xb)
            else:
                wd = mla_w_down[jm]
                kr0 = MLA_Q_RANK + MLA_KV_RANK
                wd_ext = jnp.concatenate([wd, _rot_half_cols(wd[:, kr0:])], axis=1).astype(BF16)
                wq = mla_w_uq[jm].reshape(MLA_Q_RANK, MLA_HEADS, MLA_NOPE + MLA_ROPE)
                wq_ext = jnp.concatenate([wq, _rot_half_cols(wq[..., MLA_NOPE:])], axis=-1)
                wq_ext = wq_ext.reshape(MLA_Q_RANK, MLA_HEADS * 2 * LANES).astype(BF16)
                down = norm_matmul(xb, norm_mix[i], wd_ext, out_dtype=F32, tn=wd_ext.shape[1])
                q, kr = mla_queries_and_rope_key(down, mla_g_q[jm], wq_ext, _rope_table(pos),
                                                 scale=(MLA_NOPE + MLA_ROPE) ** -0.5 * LOG2E)
                kv = norm_matmul(down, mla_g_kv[jm], mla_w_ukv[jm].astype(BF16), out_dtype=BF16, xcol=1)
                vt = kv.reshape(S, MLA_HEADS, 2, MLA_V)[:, :, 1, :].reshape(S, MLA_HEADS * MLA_V).T
                o = mla_attention(q, kv, kr, vt)
                xb = matmul_residual(o, mla_w_o[jm].astype(BF16), xb)
            w_gate, w_proj = ple_w_gate[i].astype(BF16), ple_w_proj[i].astype(BF16)
            if i % 2 == 0:
                h = norm_swiglu_in(xb, norm_ffn[i], ffn_w_in[i // 2].astype(BF16))
                xb = matmul_residual(h, ffn_w_out[i // 2].astype(BF16), xb)
                xb = ple_update(xb, norm_ple[i], p[i, b], w_gate, w_proj)
            else:
                y12, gates = moe_swiglu(xb, norm_ffn[i], moe_w_router[i // 2], moe_w_in, moe_w_out, i // 2)
                xb = ple_update_moe(xb, y12, gates, norm_ple[i], p[i, b], w_gate, w_proj)
        outs.append(final_norm(xb, norm_final))
    return jnp.stack(outs, axis=0)
```

```python
import functools
import math

import numpy as np
import jax
import jax.numpy as jnp
from jax import lax
from jax.experimental import pallas as pl
from jax.experimental.pallas import tpu as pltpu

F32 = jnp.float32
BF16 = jnp.bfloat16

NORM_EPS = 1e-6
NEG_INF = -1e30
LOG2E = math.log2(math.e)
LANES = 128
VMEM_LIMIT_BYTES = 56 * 1024 * 1024

REL_BUCKETS = 32
REL_MAX_DIST = 128
LUT_SIZE = LANES
DIFF_HEADS = 8
DIFF_HEAD_DIM = 128
MOBA_HEADS = 16
MOBA_HEAD_DIM = 128
MOBA_BLOCK = 256
MOBA_TOPK = 3
MLA_HEADS = 16
MLA_Q_RANK = 512
MLA_KV_RANK = 512
MLA_NOPE = 128
MLA_ROPE = 64
MLA_V = 128
ROPE_THETA = 10000.0
N_EXPERTS = 8
N_MIXERS = 3


def _cparams(*sem):
    return pltpu.CompilerParams(dimension_semantics=sem, vmem_limit_bytes=VMEM_LIMIT_BYTES)


def _tile(n, pref):
    if n <= pref:
        return n
    t = pref
    while n % t:
        t //= 2
    return t


def _rms_rows(x, g):
    r = lax.rsqrt(jnp.mean(x * x, axis=-1, keepdims=True) + NORM_EPS)
    return x * r * g


def _norm_mm_kernel(x_ref, g_ref, w_ref, o_ref, xn_ref, *, scaled_tiles, scale):
    j = pl.program_id(1)

    @pl.when(j == 0)
    def _():
        xn_ref[...] = _rms_rows(x_ref[...], g_ref[...]).astype(BF16)

    acc = jnp.dot(xn_ref[...], w_ref[...], preferred_element_type=F32)
    if scaled_tiles:
        acc = acc * jnp.where(j < scaled_tiles, scale, 1.0)
    o_ref[...] = acc.astype(o_ref.dtype)


def norm_matmul(x, g, w, *, out_dtype, xcol=0, tm=1024, tn=512, scaled_cols=0, scale=1.0):
    M = x.shape[0]
    K, N = w.shape
    tm, tn = _tile(M, tm), _tile(N, tn)
    assert scaled_cols % tn == 0
    kern = functools.partial(_norm_mm_kernel, scaled_tiles=scaled_cols // tn, scale=scale)
    return pl.pallas_call(
        kern,
        out_shape=jax.ShapeDtypeStruct((M, N), out_dtype),
        grid=(M // tm, N // tn),
        in_specs=[pl.BlockSpec((tm, K), lambda i, j: (i, xcol)),
                  pl.BlockSpec((1, K), lambda i, j: (0, 0)),
                  pl.BlockSpec((K, tn), lambda i, j: (0, j))],
        out_specs=pl.BlockSpec((tm, tn), lambda i, j: (i, j)),
        scratch_shapes=[pltpu.VMEM((tm, K), BF16)],
        compiler_params=_cparams("parallel", "arbitrary"),
        name="norm_matmul",
    )(x, g.reshape(1, K), w)


def _norm_swiglu_kernel(x_ref, g_ref, wg_ref, wu_ref, o_ref, xn_ref):
    @pl.when(pl.program_id(1) == 0)
    def _():
        xn_ref[...] = _rms_rows(x_ref[...], g_ref[...]).astype(BF16)

    xn = xn_ref[...]
    a = jnp.dot(xn, wg_ref[...], preferred_element_type=F32)
    u = jnp.dot(xn, wu_ref[...], preferred_element_type=F32)
    o_ref[...] = (a * jax.nn.sigmoid(a) * u).astype(o_ref.dtype)


def norm_swiglu_in(x, g, w_in, *, tm=512, tn=512):
    M, K = x.shape
    F = w_in.shape[1] // 2
    tm, tn = _tile(M, tm), _tile(F, tn)
    nj = F // tn
    return pl.pallas_call(
        _norm_swiglu_kernel,
        out_shape=jax.ShapeDtypeStruct((M, F), BF16),
        grid=(M // tm, nj),
        in_specs=[pl.BlockSpec((tm, K), lambda i, j: (i, 0)),
                  pl.BlockSpec((1, K), lambda i, j: (0, 0)),
                  pl.BlockSpec((K, tn), lambda i, j: (0, j)),
                  pl.BlockSpec((K, tn), lambda i, j: (0, j + nj))],
        out_specs=pl.BlockSpec((tm, tn), lambda i, j: (i, j)),
        scratch_shapes=[pltpu.VMEM((tm, K), BF16)],
        compiler_params=_cparams("parallel", "arbitrary"),
        name="norm_swiglu_in",
    )(x, g.reshape(1, K), w_in, w_in)


def _mm_res_kernel(a_ref, w_ref, r_ref, o_ref):
    o_ref[...] = r_ref[...] + jnp.dot(a_ref[...], w_ref[...], preferred_element_type=F32)


def matmul_residual(a, w, res, *, tm=1024, tn=512):
    M, K = a.shape
    N = w.shape[1]
    tm, tn = _tile(M, tm), _tile(N, tn)
    return pl.pallas_call(
        _mm_res_kernel,
        out_shape=jax.ShapeDtypeStruct((M, N), F32),
        grid=(M // tm, N // tn),
        in_specs=[pl.BlockSpec((tm, K), lambda i, j: (i, 0)),
                  pl.BlockSpec((K, tn), lambda i, j: (0, j)),
                  pl.BlockSpec((tm, tn), lambda i, j: (i, j))],
        out_specs=pl.BlockSpec((tm, tn), lambda i, j: (i, j)),
        compiler_params=_cparams("parallel", "parallel"),
        name="matmul_residual",
    )(a, w, res)


def _ple_kernel(x_ref, g_ref, xr_ref, p_ref, wg_ref, wp_ref, o_ref, xn_ref):
    @pl.when(pl.program_id(1) == 0)
    def _():
        xn_ref[...] = _rms_rows(x_ref[...], g_ref[...]).astype(BF16)

    gate = jax.nn.sigmoid(jnp.dot(xn_ref[...], wg_ref[...], preferred_element_type=F32))
    proj = jnp.dot(p_ref[...].astype(BF16), wp_ref[...], preferred_element_type=F32)
    o_ref[...] = xr_ref[...] + gate * proj


def ple_update(x, g, p, w_gate, w_proj, *, tm=1024, tn=512):
    M, D = x.shape
    P = p.shape[1]
    tm, tn = _tile(M, tm), _tile(D, tn)
    return pl.pallas_call(
        _ple_kernel,
        out_shape=jax.ShapeDtypeStruct((M, D), F32),
        grid=(M // tm, D // tn),
        in_specs=[pl.BlockSpec((tm, D), lambda i, j: (i, 0)),
                  pl.BlockSpec((1, D), lambda i, j: (0, 0)),
                  pl.BlockSpec((tm, tn), lambda i, j: (i, j)),
                  pl.BlockSpec((tm, P), lambda i, j: (i, 0)),
                  pl.BlockSpec((D, tn), lambda i, j: (0, j)),
                  pl.BlockSpec((P, tn), lambda i, j: (0, j))],
        out_specs=pl.BlockSpec((tm, tn), lambda i, j: (i, j)),
        scratch_shapes=[pltpu.VMEM((tm, D), BF16)],
        compiler_params=_cparams("parallel", "arbitrary"),
        name="ple_update",
    )(x, g.reshape(1, D), x, p, w_gate, w_proj)


def _ple_moe_kernel(x_ref, y1_ref, y2_ref, gt_ref, g_ref, p_ref, wg_ref, wp_ref, o_ref, xs_ref, xn_ref, *, tn):
    j = pl.program_id(1)

    @pl.when(j == 0)
    def _():
        gt = gt_ref[...]
        xnew = x_ref[...] + gt[:, 0:1] * y1_ref[...] + gt[:, 1:2] * y2_ref[...]
        xn_ref[...] = _rms_rows(xnew, g_ref[...]).astype(BF16)
        for jj in range(xs_ref.shape[0]):
            xs_ref[jj] = xnew[:, jj * tn:(jj + 1) * tn]

    gate = jax.nn.sigmoid(jnp.dot(xn_ref[...], wg_ref[...], preferred_element_type=F32))
    proj = jnp.dot(p_ref[...].astype(BF16), wp_ref[...], preferred_element_type=F32)
    o_ref[...] = xs_ref[j] + gate * proj


def ple_update_moe(x, y1, y2, gates, g, p, w_gate, w_proj, *, tm=512, tn=512):
    M, D = x.shape
    P = p.shape[1]
    tm, tn = _tile(M, tm), _tile(D, tn)
    row = pl.BlockSpec((tm, D), lambda i, j: (i, 0))
    return pl.pallas_call(
        functools.partial(_ple_moe_kernel, tn=tn),
        out_shape=jax.ShapeDtypeStruct((M, D), F32),
        grid=(M // tm, D // tn),
        in_specs=[row, row, row,
                  pl.BlockSpec((tm, 2), lambda i, j: (i, 0)),
                  pl.BlockSpec((1, D), lambda i, j: (0, 0)),
                  pl.BlockSpec((tm, P), lambda i, j: (i, 0)),
                  pl.BlockSpec((D, tn), lambda i, j: (0, j)),
                  pl.BlockSpec((P, tn), lambda i, j: (0, j))],
        out_specs=pl.BlockSpec((tm, tn), lambda i, j: (i, j)),
        scratch_shapes=[pltpu.VMEM((D // tn, tm, tn), F32), pltpu.VMEM((tm, D), BF16)],
        compiler_params=_cparams("parallel", "arbitrary"),
        name="ple_update_moe",
    )(x, y1, y2, gates, g.reshape(1, D), p, w_gate, w_proj)


def _final_norm_kernel(x_ref, g_ref, o_ref):
    o_ref[...] = _rms_rows(x_ref[...], g_ref[...])


def final_norm(x, g, *, tm=512):
    M, D = x.shape
    tm = _tile(M, tm)
    return pl.pallas_call(
        _final_norm_kernel,
        out_shape=jax.ShapeDtypeStruct((M, D), F32),
        grid=(M // tm,),
        in_specs=[pl.BlockSpec((tm, D), lambda i: (i, 0)),
                  pl.BlockSpec((1, D), lambda i: (0, 0))],
        out_specs=pl.BlockSpec((tm, D), lambda i: (i, 0)),
        compiler_params=_cparams("parallel"),
        name="final_norm",
    )(x, g.reshape(1, D))


def _pair_tables(n_tiles):
    qi, kj = [], []
    for i in range(n_tiles):
        for j in range(i + 1):
            qi.append(i)
            kj.append(j)
    return jnp.asarray(np.array(qi, np.int32)), jnp.asarray(np.array(kj, np.int32))


def _rel_bucket(dist):
    n = jnp.maximum(dist, 0)
    max_exact = REL_BUCKETS // 2
    nf = jnp.maximum(n, 1).astype(F32)
    large = max_exact + (jnp.log(nf / max_exact) / math.log(REL_MAX_DIST / max_exact)
                         * (REL_BUCKETS - max_exact)).astype(jnp.int32)
    large = jnp.minimum(large, REL_BUCKETS - 1)
    return jnp.where(n < max_exact, n, large)


def _bias_lut(rel_bias):
    buckets = _rel_bucket(jnp.arange(LUT_SIZE, dtype=jnp.int32))
    return rel_bias.astype(F32)[buckets].T * LOG2E


def _tile_pos_bounds(pos, tq, ck):
    S = pos.shape[0]
    return pos.reshape(S // tq, tq).min(axis=1), pos.reshape(S // ck, ck).max(axis=1)


def _lut_bias(lut_row, idx):
    R, C = idx.shape
    lut = jnp.broadcast_to(lut_row, (R, LANES))
    parts = [jnp.take_along_axis(lut, idx[:, c:c + LANES], axis=1) for c in range(0, C, LANES)]
    return parts[0] if len(parts) == 1 else jnp.concatenate(parts, axis=1)


def _pipelined(units, scores, consume):
    st = scores(*units[0])
    for k, u in enumerate(units):
        nxt = scores(*units[k + 1]) if k + 1 < len(units) else None
        consume(*u, st)
        st = nxt


def _online_softmax_step_t(st, vt, m_ref, l_ref, acc_ref, qcols, m_cur=None, offsets=None):
    m_prev = m_ref[:, qcols]
    if m_cur is None:
        m_cur = jnp.max(st, axis=0, keepdims=True)
    m_new = jnp.maximum(m_prev, m_cur)
    alpha = jnp.exp2(m_prev - m_new)
    if offsets is None:
        p = jnp.exp2(st - m_new)
    else:
        parts = [jnp.exp2(st[rows, :] - fn(m_new)) for rows, fn in offsets]
        p = parts[0] if len(parts) == 1 else jnp.concatenate(parts, axis=0)
    l_ref[:, qcols] = alpha * l_ref[:, qcols] + jnp.sum(p, axis=0, keepdims=True)
    acc_ref[:, qcols] = alpha * acc_ref[:, qcols] + jnp.dot(vt, p.astype(BF16), preferred_element_type=F32)
    m_ref[:, qcols] = m_new


def _causal_t(k0, nk, q0, nq):
    r = k0 + lax.broadcasted_iota(jnp.int32, (nk, nq), 0)
    c = q0 + lax.broadcasted_iota(jnp.int32, (nk, nq), 1)
    return r <= c


def _nt_dot(a, b):
    return lax.dot_general(a, b, (((1,), (1,)), ((), ())), preferred_element_type=F32)


def _diff_attn_kernel(qi_ref, kj_ref, pminq_ref, pmaxk_ref,
                      q_ref, k_ref, vt_ref, pq_ref, pk_ref, lut_ref, lam_ref, g_ref,
                      o_ref, m_ref, l_ref, acc_ref, *, tq, ck, lambda_init):
    p = pl.program_id(1)
    i, j = qi_ref[p], kj_ref[p]
    nc = tq // ck
    dh = DIFF_HEAD_DIM

    @pl.when(j == 0)
    def _():
        m_ref[...] = jnp.full(m_ref.shape, NEG_INF, F32)
        l_ref[...] = jnp.zeros(l_ref.shape, F32)
        acc_ref[...] = jnp.zeros(acc_ref.shape, F32)

    def scores(c, mp, q0, mode):
        keys = slice(c * ck, (c + 1) * ck)
        st = _nt_dot(k_ref[keys, mp * dh:(mp + 1) * dh], q_ref[q0:tq, mp * dh:(mp + 1) * dh])
        if mode != "far":
            idx = jnp.clip(pq_ref[:, q0:tq] - pk_ref[keys, :], 0, LUT_SIZE - 1)
            st = st + _lut_bias(lut_ref[mp:mp + 1, :], idx)
        if mode == "diag":
            st = jnp.where(_causal_t(c * ck, ck, q0, tq - q0), st, NEG_INF)
        return st

    def consume(c, mp, q0, mode, st):
        m_cur = offsets = None
        if mode == "far":
            bias = lut_ref[mp:mp + 1, LUT_SIZE - 1:LUT_SIZE]
            m_cur = jnp.max(st, axis=0, keepdims=True) + bias
            offsets = [(slice(None), lambda m_new: m_new - bias)]
        _online_softmax_step_t(st, vt_ref[:, c * ck:(c + 1) * ck], m_ref.at[mp], l_ref.at[mp], acc_ref.at[mp],
                               slice(q0, tq), m_cur, offsets)

    def sweep(chunks, mode):
        units = [(c, mp, c * ck if mode == "diag" else 0, mode) for c in chunks for mp in range(2)]
        _pipelined(units, scores, consume)

    @pl.when(j < i)
    def _():
        fars = [pminq_ref[i] - pmaxk_ref[j * nc + c] >= LUT_SIZE - 1 for c in range(nc)]
        all_far = functools.reduce(jnp.logical_and, fars)
        pl.when(all_far)(functools.partial(sweep, range(nc), "far"))

        @pl.when(jnp.logical_not(all_far))
        def _():
            for c in range(nc):
                pl.when(fars[c])(functools.partial(sweep, [c], "far"))
                pl.when(jnp.logical_not(fars[c]))(functools.partial(sweep, [c], "near"))

    @pl.when(j == i)
    def _():
        sweep(range(nc), "diag")
        lam = lam_ref[...]
        lam_full = (jnp.exp(jnp.sum(lam[0:1] * lam[1:2], axis=-1, keepdims=True))
                    - jnp.exp(jnp.sum(lam[2:3] * lam[3:4], axis=-1, keepdims=True)) + lambda_init)
        a = acc_ref[0] / l_ref[0] - lam_full * (acc_ref[1] / l_ref[1])
        r = lax.rsqrt(jnp.mean(a * a, axis=0, keepdims=True) + NORM_EPS)
        y = a * r * (g_ref[...] * (1.0 - lambda_init))
        o_ref[...] = y.T.astype(o_ref.dtype)


def diff_attention(qkv, vt, pos, lut, lam, subln_g, lambda_init, *, tile=1024, chunk=512):
    S = qkv.shape[0]
    H, dh = DIFF_HEADS, DIFF_HEAD_DIM
    tq = _tile(S, tile)
    ck = _tile(tq, chunk)
    qi, kj = _pair_tables(S // tq)
    pminq, pmaxk = _tile_pos_bounds(pos, tq, ck)
    kern = functools.partial(_diff_attn_kernel, tq=tq, ck=ck, lambda_init=lambda_init)
    grid_spec = pltpu.PrefetchScalarGridSpec(
        num_scalar_prefetch=4,
        grid=(H, qi.shape[0]),
        in_specs=[pl.BlockSpec((tq, 2 * dh), lambda h, p, qi, kj, a, b: (qi[p], h)),
                  pl.BlockSpec((tq, 2 * dh), lambda h, p, qi, kj, a, b: (kj[p], H + h)),
                  pl.BlockSpec((2 * dh, tq), lambda h, p, qi, kj, a, b: (h, kj[p])),
                  pl.BlockSpec((1, tq), lambda h, p, qi, kj, a, b: (0, qi[p])),
                  pl.BlockSpec((tq, 1), lambda h, p, qi, kj, a, b: (kj[p], 0)),
                  pl.BlockSpec((None, 2, LUT_SIZE), lambda h, p, qi, kj, a, b: (h, 0, 0)),
                  pl.BlockSpec((4, dh), lambda h, p, qi, kj, a, b: (0, 0)),
                  pl.BlockSpec((2 * dh, 1), lambda h, p, qi, kj, a, b: (0, 0))],
        out_specs=pl.BlockSpec((tq, 2 * dh), lambda h, p, qi, kj, a, b: (qi[p], h)),
        scratch_shapes=[pltpu.VMEM((2, 1, tq), F32), pltpu.VMEM((2, 1, tq), F32),
                        pltpu.VMEM((2, 2 * dh, tq), F32)],
    )
    return pl.pallas_call(
        kern,
        out_shape=jax.ShapeDtypeStruct((S, H * 2 * dh), BF16),
        grid_spec=grid_spec,
        compiler_params=_cparams("parallel", "arbitrary"),
        name="diff_attention",
    )(qi, kj, pminq, pmaxk, qkv, qkv, vt, pos.reshape(1, S), pos.reshape(S, 1),
      lut.reshape(H, 2, LUT_SIZE), lam, subln_g.reshape(2 * dh, 1))


def _kmean_kernel(k_ref, hi_ref, lo_ref):
    km = jnp.mean(k_ref[...].astype(F32), axis=0, keepdims=True)
    hi = km.astype(BF16)
    hi_ref[...] = hi
    lo_ref[...] = (km - hi.astype(F32)).astype(BF16)


def moba_block_means(qkv, D):
    S = qkv.shape[0]
    nblk = S // MOBA_BLOCK
    hi, lo = pl.pallas_call(
        _kmean_kernel,
        out_shape=[jax.ShapeDtypeStruct((nblk, 1, D), BF16)] * 2,
        grid=(nblk,),
        in_specs=[pl.BlockSpec((MOBA_BLOCK, D), lambda n: (n, 1))],
        out_specs=[pl.BlockSpec((None, 1, D), lambda n: (n, 0, 0))] * 2,
        compiler_params=_cparams("parallel"),
        name="moba_block_means",
    )(qkv)
    return hi.reshape(nblk, D), lo.reshape(nblk, D)


def _moba_attn_kernel(qi_ref, kj_ref, pminq_ref, pmaxk_ref,
                      q_ref, k_ref, vt_ref, kmh_ref, kml_ref, pq_ref, pk_ref, lut_ref,
                      o_ref, m_ref, l_ref, acc_ref, sel_ref, *, tq, ck, hp):
    p = pl.program_id(1)
    i, j = qi_ref[p], kj_ref[p]
    nc = tq // ck
    L, dh = MOBA_BLOCK, MOBA_HEAD_DIM
    bpc = ck // L
    log2_l = L.bit_length() - 1

    @pl.when(j == 0)
    def _():
        m_ref[...] = jnp.full(m_ref.shape, NEG_INF, F32)
        l_ref[...] = jnp.zeros(l_ref.shape, F32)
        acc_ref[...] = jnp.zeros(acc_ref.shape, F32)
        blk = lax.broadcasted_iota(jnp.int32, (LANES, tq), 0)
        blk_f = blk.astype(F32)
        own = lax.shift_right_logical(i * tq + lax.broadcasted_iota(jnp.int32, (1, tq), 1), log2_l)
        for hd in range(hp):
            q = q_ref[:, hd * dh:(hd + 1) * dh]
            gs = (_nt_dot(kmh_ref[:, hd * dh:(hd + 1) * dh], q)
                  + _nt_dot(kml_ref[:, hd * dh:(hd + 1) * dh], q))
            g = jnp.where(blk < own, gs, NEG_INF)
            sel = jnp.zeros((LANES, tq), F32)
            for _ in range(MOBA_TOPK):
                best = jnp.max(g, axis=0, keepdims=True)
                first = jnp.min(jnp.where(g == best, blk_f, float(LANES)), axis=0, keepdims=True)
                pick = blk_f == first
                sel = jnp.where(pick, jnp.where(best > 0.5 * NEG_INF, 1.0, sel), sel)
                g = jnp.where(pick, -3e38, g)
            sel_ref[hd] = sel

    def scores(c, hd, q0, mode):
        keys = slice(c * ck, (c + 1) * ck)
        nq = tq - q0
        st = _nt_dot(k_ref[keys, hd * dh:(hd + 1) * dh], q_ref[q0:tq, hd * dh:(hd + 1) * dh])
        if mode != "far":
            idx = jnp.clip(pq_ref[:, q0:tq] - pk_ref[keys, :], 0, LUT_SIZE - 1)
            st = st + _lut_bias(lut_ref[hd:hd + 1, :], idx)
        if mode != "diag":
            return st
        allowed = []
        for b in range(bpc):
            n_local = c * bpc + b
            picked = jnp.broadcast_to(picked_row(hd, n_local, q0), (L, nq))
            own_local = lax.shift_right_logical(q0 + lax.broadcasted_iota(jnp.int32, (L, nq), 1), log2_l)
            visible = jnp.where(_causal_t(n_local * L, L, q0, nq), 1.0, 0.0)
            allowed.append(jnp.where(own_local == n_local, visible, picked))
        allowed = allowed[0] if bpc == 1 else jnp.concatenate(allowed, axis=0)
        return jnp.where(allowed > 0.0, st, NEG_INF)

    def picked_row(hd, n_local, q0):
        return sel_ref[hd, pl.ds(j * (tq // L) + n_local, 1), q0:tq]

    def consume(c, hd, q0, mode, st):
        m_cur = offsets = None
        if mode != "diag":
            bias = lut_ref[hd:hd + 1, LUT_SIZE - 1:LUT_SIZE] if mode == "far" else 0.0
            picked = [picked_row(hd, c * bpc + b, q0) > 0.0 for b in range(bpc)]
            m_cur = functools.reduce(jnp.maximum, [
                jnp.where(picked[b], jnp.max(st[b * L:(b + 1) * L, :], axis=0, keepdims=True) + bias, NEG_INF)
                for b in range(bpc)])
            offsets = [(slice(b * L, (b + 1) * L),
                        functools.partial(lambda m_new, pk: jnp.where(pk, m_new - bias, -NEG_INF), pk=picked[b]))
                       for b in range(bpc)]
        _online_softmax_step_t(st, vt_ref[hd * dh:(hd + 1) * dh, c * ck:(c + 1) * ck],
                               m_ref.at[hd], l_ref.at[hd], acc_ref.at[hd], slice(q0, tq), m_cur, offsets)

    def sweep(chunks, mode):
        _pipelined([(c, hd, c * ck if mode == "diag" else 0, mode) for c in chunks for hd in range(hp)],
                   scores, consume)

    @pl.when(j < i)
    def _():
        fars = [pminq_ref[i] - pmaxk_ref[j * nc + c] >= LUT_SIZE - 1 for c in range(nc)]
        all_far = functools.reduce(jnp.logical_and, fars)
        pl.when(all_far)(functools.partial(sweep, range(nc), "far"))

        @pl.when(jnp.logical_not(all_far))
        def _():
            for c in range(nc):
                pl.when(fars[c])(functools.partial(sweep, [c], "far"))
                pl.when(jnp.logical_not(fars[c]))(functools.partial(sweep, [c], "near"))

    @pl.when(j == i)
    def _():
        sweep(range(nc), "diag")
        for hd in range(hp):
            o_ref[:, hd * dh:(hd + 1) * dh] = (acc_ref[hd] / l_ref[hd]).T.astype(o_ref.dtype)


def moba_attention(qkv, vt, pos, lut, *, tile=1024, chunk=512, heads_per_step=2):
    S = qkv.shape[0]
    H, dh, L = MOBA_HEADS, MOBA_HEAD_DIM, MOBA_BLOCK
    D = H * dh
    nblk = S // L
    assert S % L == 0 and nblk <= LANES
    tq = _tile(S, tile)
    ck = _tile(tq, chunk)
    assert tq % L == 0 and ck % L == 0
    km_hi, km_lo = moba_block_means(qkv, D)
    km_hi = jnp.pad(km_hi, ((0, LANES - nblk), (0, 0)))
    km_lo = jnp.pad(km_lo, ((0, LANES - nblk), (0, 0)))
    qi, kj = _pair_tables(S // tq)
    pminq, pmaxk = _tile_pos_bounds(pos, tq, ck)
    hp = heads_per_step
    G = H // hp
    kern = functools.partial(_moba_attn_kernel, tq=tq, ck=ck, hp=hp)
    grid_spec = pltpu.PrefetchScalarGridSpec(
        num_scalar_prefetch=4,
        grid=(G, qi.shape[0]),
        in_specs=[pl.BlockSpec((tq, hp * dh), lambda h, p, qi, kj, a, b: (qi[p], h)),
                  pl.BlockSpec((tq, hp * dh), lambda h, p, qi, kj, a, b: (kj[p], G + h)),
                  pl.BlockSpec((hp * dh, tq), lambda h, p, qi, kj, a, b: (h, kj[p])),
                  pl.BlockSpec((LANES, hp * dh), lambda h, p, qi, kj, a, b: (0, h)),
                  pl.BlockSpec((LANES, hp * dh), lambda h, p, qi, kj, a, b: (0, h)),
                  pl.BlockSpec((1, tq), lambda h, p, qi, kj, a, b: (0, qi[p])),
                  pl.BlockSpec((tq, 1), lambda h, p, qi, kj, a, b: (kj[p], 0)),
                  pl.BlockSpec((None, hp, LUT_SIZE), lambda h, p, qi, kj, a, b: (h, 0, 0))],
        out_specs=pl.BlockSpec((tq, hp * dh), lambda h, p, qi, kj, a, b: (qi[p], h)),
        scratch_shapes=[pltpu.VMEM((hp, 1, tq), F32), pltpu.VMEM((hp, 1, tq), F32),
                        pltpu.VMEM((hp, dh, tq), F32), pltpu.VMEM((hp, LANES, tq), F32)],
    )
    return pl.pallas_call(
        kern,
        out_shape=jax.ShapeDtypeStruct((S, D), BF16),
        grid_spec=grid_spec,
        compiler_params=_cparams("parallel", "arbitrary"),
        name="moba_attention",
    )(qi, kj, pminq, pmaxk, qkv, qkv, vt, km_hi, km_lo, pos.reshape(1, S), pos.reshape(S, 1),
      lut.reshape(G, hp, LUT_SIZE))


def _rope_fold(y):
    lane = lax.broadcasted_iota(jnp.int32, y.shape, 1)
    return jnp.where(lane < MLA_ROPE, y + pltpu.roll(y, MLA_ROPE, 1), 0.0)


def _mla_q_kernel(cq_ref, g_ref, w_ref, kr_ref, t_ref, q_ref, kro_ref, cqn_ref, *, scale):
    t = t_ref[...]

    @pl.when(pl.program_id(1) == 0)
    def _():
        cqn_ref[...] = _rms_rows(cq_ref[...], g_ref[...]).astype(BF16)
        kro_ref[...] = _rope_fold(kr_ref[...] * t).astype(kro_ref.dtype)

    qh = jnp.dot(cqn_ref[...], w_ref[...], preferred_element_type=F32)
    q_ref[:, :MLA_NOPE] = (qh[:, :MLA_NOPE] * scale).astype(q_ref.dtype)
    q_ref[:, MLA_NOPE:] = (_rope_fold(qh[:, MLA_NOPE:] * t) * scale).astype(q_ref.dtype)


def mla_queries_and_rope_key(down, g_q, w_uq_ext, rope_tab, *, scale, tm=512):
    S = down.shape[0]
    H = MLA_HEADS
    tm = _tile(S, tm)
    W = 2 * LANES
    return pl.pallas_call(
        functools.partial(_mla_q_kernel, scale=scale),
        out_shape=[jax.ShapeDtypeStruct((S, H * W), BF16), jax.ShapeDtypeStruct((S, LANES), BF16)],
        grid=(S // tm, H),
        in_specs=[pl.BlockSpec((tm, MLA_Q_RANK), lambda i, h: (i, 0)),
                  pl.BlockSpec((1, MLA_Q_RANK), lambda i, h: (0, 0)),
                  pl.BlockSpec((MLA_Q_RANK, W), lambda i, h: (0, h)),
                  pl.BlockSpec((tm, LANES), lambda i, h: (i, (MLA_Q_RANK + MLA_KV_RANK) // LANES)),
                  pl.BlockSpec((tm, LANES), lambda i, h: (i, 0))],
        out_specs=[pl.BlockSpec((tm, W), lambda i, h: (i, h)),
                   pl.BlockSpec((tm, LANES), lambda i, h: (i, 0))],
        scratch_shapes=[pltpu.VMEM((tm, MLA_Q_RANK), BF16)],
        compiler_params=_cparams("parallel", "arbitrary"),
        name="mla_queries",
    )(down, g_q.reshape(1, MLA_Q_RANK), w_uq_ext, down, rope_tab)


def _mla_attn_kernel(qi_ref, kj_ref, q_ref, kv_ref, kr_ref, vt_ref, o_ref, m_ref, l_ref, acc_ref, *, tq, ck, hp):
    p = pl.program_id(1)
    i, j = qi_ref[p], kj_ref[p]
    nc = tq // ck
    W = 2 * LANES

    @pl.when(j == 0)
    def _():
        m_ref[...] = jnp.full(m_ref.shape, NEG_INF, F32)
        l_ref[...] = jnp.zeros(l_ref.shape, F32)
        acc_ref[...] = jnp.zeros(acc_ref.shape, F32)

    def scores(c, g, q0, diag):
        keys = slice(c * ck, (c + 1) * ck)
        kc = jnp.concatenate([kv_ref[keys, g * MLA_NOPE:(g + 1) * MLA_NOPE], kr_ref[keys, :]], axis=1)
        st = _nt_dot(kc, q_ref[q0:tq, g * W:(g + 1) * W])
        if diag:
            st = jnp.where(_causal_t(c * ck, ck, q0, tq - q0), st, NEG_INF)
        return st

    def consume(c, g, q0, diag, st):
        _online_softmax_step_t(st, vt_ref[g * MLA_V:(g + 1) * MLA_V, c * ck:(c + 1) * ck],
                               m_ref.at[g], l_ref.at[g], acc_ref.at[g], slice(q0, tq))

    def sweep(diag):
        _pipelined([(c, g, c * ck if diag else 0, diag) for c in range(nc) for g in range(hp)], scores, consume)

    @pl.when(j < i)
    def _():
        sweep(False)

    @pl.when(j == i)
    def _():
        sweep(True)
        for g in range(hp):
            o_ref[:, g * MLA_V:(g + 1) * MLA_V] = (acc_ref[g] / l_ref[g]).T.astype(o_ref.dtype)


def mla_attention(q, kv, kr, vt, *, tile=1024, chunk=512, heads_per_step=2):
    S = q.shape[0]
    H, hp = MLA_HEADS, heads_per_step
    tq = _tile(S, tile)
    ck = _tile(tq, chunk)
    qi, kj = _pair_tables(S // tq)
    W = 2 * LANES
    grid_spec = pltpu.PrefetchScalarGridSpec(
        num_scalar_prefetch=2,
        grid=(H // hp, qi.shape[0]),
        in_specs=[pl.BlockSpec((tq, hp * W), lambda h, p, qi, kj: (qi[p], h)),
                  pl.BlockSpec((tq, hp * MLA_NOPE), lambda h, p, qi, kj: (kj[p], h)),
                  pl.BlockSpec((tq, LANES), lambda h, p, qi, kj: (kj[p], 0)),
                  pl.BlockSpec((hp * MLA_V, tq), lambda h, p, qi, kj: (h, kj[p]))],
        out_specs=pl.BlockSpec((tq, hp * MLA_V), lambda h, p, qi, kj: (qi[p], h)),
        scratch_shapes=[pltpu.VMEM((hp, 1, tq), F32), pltpu.VMEM((hp, 1, tq), F32),
                        pltpu.VMEM((hp, MLA_V, tq), F32)],
    )
    return pl.pallas_call(
        functools.partial(_mla_attn_kernel, tq=tq, ck=ck, hp=hp),
        out_shape=jax.ShapeDtypeStruct((S, H * MLA_V), BF16),
        grid_spec=grid_spec,
        compiler_params=_cparams("parallel", "arbitrary"),
        name="mla_attention",
    )(qi, kj, q, kv, kr, vt)


_HI16 = 0xFFFF0000


def _pack_bf16_pairs(hi_f32):
    half = hi_f32.shape[1] // 2
    bits = pltpu.bitcast(hi_f32, jnp.uint32)
    return lax.shift_right_logical(bits[:, :half], jnp.uint32(16)) | (bits[:, half:] & jnp.uint32(_HI16))


def _unpack_bf16_pairs(words):
    lo = pltpu.bitcast(lax.shift_left(words, jnp.uint32(16)), F32).astype(BF16)
    hi = pltpu.bitcast(words & jnp.uint32(_HI16), F32).astype(BF16)
    return jnp.concatenate([lo, hi], axis=1)


def _router_kernel(x_ref, g_ref, wh_ref, wl_ref, xn_ref, r_ref):
    xn = _rms_rows(x_ref[...], g_ref[...])
    hi = xn.astype(BF16)
    hi_f32 = hi.astype(F32)
    lo = (xn - hi_f32).astype(BF16)
    xn_ref[...] = _pack_bf16_pairs(hi_f32)
    logits = (jnp.dot(hi, wh_ref[...], preferred_element_type=F32)
              + jnp.dot(hi, wl_ref[...], preferred_element_type=F32)
              + jnp.dot(lo, wh_ref[...], preferred_element_type=F32))
    lane = lax.broadcasted_iota(jnp.int32, logits.shape, 1)
    lane_f = lane.astype(F32)
    g = jnp.where(lane < N_EXPERTS, logits, NEG_INF)
    v1 = jnp.max(g, axis=-1, keepdims=True)
    i1 = jnp.min(jnp.where(g == v1, lane_f, float(LANES)), axis=-1, keepdims=True)
    g = jnp.where(lane_f == i1, NEG_INF, g)
    v2 = jnp.max(g, axis=-1, keepdims=True)
    i2 = jnp.min(jnp.where(g == v2, lane_f, float(LANES)), axis=-1, keepdims=True)
    g2 = 1.0 / (1.0 + jnp.exp(v1 - v2))
    g1 = 1.0 - g2
    r_ref[...] = jnp.where(lane == 0, i1,
                           jnp.where(lane == 1, i2,
                                     jnp.where(lane == 2, g1, jnp.where(lane == 3, g2, 0.0))))


def moe_route(x, g, w_router, *, tm=512):
    M, D = x.shape
    tm = _tile(M, tm)
    wr = jnp.pad(w_router.astype(F32), ((0, 0), (0, LANES - N_EXPERTS)))
    wh = wr.astype(BF16)
    wl = (wr - wh.astype(F32)).astype(BF16)
    return pl.pallas_call(
        _router_kernel,
        out_shape=[jax.ShapeDtypeStruct((M, D // 2), jnp.uint32), jax.ShapeDtypeStruct((M, LANES), F32)],
        grid=(M // tm,),
        in_specs=[pl.BlockSpec((tm, D), lambda i: (i, 0)),
                  pl.BlockSpec((1, D), lambda i: (0, 0)),
                  pl.BlockSpec((D, LANES), lambda i: (0, 0)),
                  pl.BlockSpec((D, LANES), lambda i: (0, 0))],
        out_specs=[pl.BlockSpec((tm, D // 2), lambda i: (i, 0)),
                   pl.BlockSpec((tm, LANES), lambda i: (i, 0))],
        compiler_params=_cparams("parallel"),
        name="moe_route",
    )(x, g.reshape(1, D), wh, wl)


def _new_expert(te_ref, t):
    return jnp.logical_or(t == 0, te_ref[t] != te_ref[jnp.maximum(t - 1, 0)])


def _gmm_swiglu_kernel(te_ref, nu_ref, x_ref, wg_ref, wu_ref, o_ref, wgb_ref, wub_ref):
    t = pl.program_id(1)
    used = t < nu_ref[0]

    @pl.when(jnp.logical_and(used, _new_expert(te_ref, t)))
    def _():
        wgb_ref[...] = wg_ref[...].astype(BF16)
        wub_ref[...] = wu_ref[...].astype(BF16)

    @pl.when(used)
    def _():
        x = _unpack_bf16_pairs(x_ref[...])
        a = jnp.dot(x, wgb_ref[...], preferred_element_type=F32)
        u = jnp.dot(x, wub_ref[...], preferred_element_type=F32)
        o_ref[...] = (a * jax.nn.sigmoid(a) * u).astype(o_ref.dtype)

    @pl.when(jnp.logical_not(used))
    def _():
        o_ref[...] = jnp.zeros(o_ref.shape, o_ref.dtype)


def _gmm_out_kernel(te_ref, nu_ref, h_ref, w_ref, o_ref, wb_ref):
    t = pl.program_id(1)
    used = t < nu_ref[0]

    @pl.when(jnp.logical_and(used, _new_expert(te_ref, t)))
    def _():
        wb_ref[...] = w_ref[...].astype(BF16)

    @pl.when(used)
    def _():
        o_ref[...] = jnp.dot(h_ref[...], wb_ref[...], preferred_element_type=F32)

    @pl.when(jnp.logical_not(used))
    def _():
        o_ref[...] = jnp.zeros(o_ref.shape, o_ref.dtype)


def moe_experts(xs, tile_expert, n_used, w_in, w_out, layer, *, tm, tn=512):
    P = xs.shape[0]
    D = w_in.shape[2]
    F = w_in.shape[3] // 2
    nt = P // tm
    tn1, tn2 = _tile(F, tn), _tile(D, tn)
    nj1, nj2 = F // tn1, D // tn2
    h = pl.pallas_call(
        _gmm_swiglu_kernel,
        out_shape=jax.ShapeDtypeStruct((P, F), BF16),
        grid_spec=pltpu.PrefetchScalarGridSpec(
            num_scalar_prefetch=2,
            grid=(nj1, nt),
            in_specs=[pl.BlockSpec((tm, D // 2), lambda j, t, te, nu: (t, 0)),
                      pl.BlockSpec((None, None, D, tn1), lambda j, t, te, nu: (layer, te[t], 0, j)),
                      pl.BlockSpec((None, None, D, tn1), lambda j, t, te, nu: (layer, te[t], 0, nj1 + j))],
            out_specs=pl.BlockSpec((tm, tn1), lambda j, t, te, nu: (t, j)),
            scratch_shapes=[pltpu.VMEM((D, tn1), BF16), pltpu.VMEM((D, tn1), BF16)],
        ),
        compiler_params=_cparams("arbitrary", "arbitrary"),
        name="moe_swiglu_in",
    )(tile_expert, n_used, xs, w_in, w_in)
    return pl.pallas_call(
        _gmm_out_kernel,
        out_shape=jax.ShapeDtypeStruct((P, D), F32),
        grid_spec=pltpu.PrefetchScalarGridSpec(
            num_scalar_prefetch=2,
            grid=(nj2, nt),
            in_specs=[pl.BlockSpec((tm, F), lambda j, t, te, nu: (t, 0)),
                      pl.BlockSpec((None, None, F, tn2), lambda j, t, te, nu: (layer, te[t], 0, j))],
            out_specs=pl.BlockSpec((tm, tn2), lambda j, t, te, nu: (t, j)),
            scratch_shapes=[pltpu.VMEM((F, tn2), BF16)],
        ),
        compiler_params=_cparams("arbitrary", "arbitrary"),
        name="moe_out",
    )(tile_expert, n_used, h, w_out)


def moe_swiglu(x, g, w_router, w_in, w_out, layer, *, tm=512):
    S, D = x.shape
    E = N_EXPERTS
    tm = _tile(S, tm)
    xn, route = moe_route(x, g, w_router)
    experts = route[:, :2].astype(jnp.int32)
    gates = route[:, 2:4]
    flat_e = experts.reshape(-1)
    order = jnp.argsort(flat_e, stable=True).astype(jnp.int32)
    rank = jnp.argsort(order).astype(jnp.int32)
    counts = jnp.bincount(flat_e, length=E).astype(jnp.int32)
    padded = ((counts + tm - 1) // tm) * tm
    start = jnp.cumsum(counts) - counts
    pstart = jnp.cumsum(padded) - padded
    pend = jnp.cumsum(padded)
    P = 2 * S + E * tm
    nt = P // tm
    tile_expert = jnp.minimum(
        jnp.searchsorted(pend, jnp.arange(nt, dtype=jnp.int32) * tm, side="right"), E - 1).astype(jnp.int32)
    n_used = (pend[-1:] // tm).astype(jnp.int32)
    row_e = jnp.repeat(tile_expert, tm)
    row_rank = jnp.arange(P, dtype=jnp.int32) - pstart[row_e]
    row_valid = jnp.logical_and(row_rank < counts[row_e], jnp.arange(P) < pend[-1])
    row_token = jnp.where(row_valid, order[jnp.clip(start[row_e] + row_rank, 0, 2 * S - 1)] // 2, 0)
    slot = pstart[flat_e] + rank - start[flat_e]
    tile_expert = jnp.where(jnp.arange(nt) < n_used[0], tile_expert, tile_expert[jnp.maximum(n_used[0] - 1, 0)])
    slot = slot.reshape(S, 2)
    xs = jnp.concatenate([jnp.take(xn, part, axis=0, mode="clip") for part in jnp.split(row_token, 4)], axis=0)
    ys = moe_experts(xs, tile_expert, n_used, w_in, w_out, layer, tm=tm)
    return jnp.take(ys, slot[:, 0], axis=0, mode="clip"), jnp.take(ys, slot[:, 1], axis=0, mode="clip"), gates


def _diff_lambda_init(layer):
    return 0.8 - 0.6 * math.exp(-0.3 * layer)


def _rope_table(pos):
    half = MLA_ROPE // 2
    inv_freq = ROPE_THETA ** (-jnp.arange(half, dtype=F32) / half)
    ang = pos.astype(F32)[:, None] * inv_freq
    cos, sin = jnp.cos(ang), jnp.sin(ang)
    return jnp.concatenate([cos, cos, -sin, sin], axis=1)


def _rot_half_cols(w):
    half = w.shape[-1] // 2
    return jnp.concatenate([w[..., half:], w[..., :half]], axis=-1)


def kernel(x, p, positions, rel_bias, norm_mix, norm_ffn, norm_ple, norm_final, diff_w_qkv, diff_lambda, diff_subln, diff_w_o, moba_w_qkv, moba_w_o, mla_w_down, mla_g_q, mla_w_uq, mla_g_kv, mla_w_ukv, mla_w_o, ffn_w_in, ffn_w_out, moe_w_router, moe_w_in, moe_w_out, ple_w_gate, ple_w_proj):
    B, S, D = x.shape
    depth = p.shape[0]
    lut = _bias_lut(rel_bias)
    outs = []
    for b in range(B):
        xb = x[b]
        pos = positions[b].astype(jnp.int32)
        for i in range(depth):
            jm = i // N_MIXERS
            if i % N_MIXERS == 0:
                qkv = norm_matmul(xb, norm_mix[i], diff_w_qkv[jm].astype(BF16), out_dtype=BF16,
                                  scaled_cols=D, scale=DIFF_HEAD_DIM ** -0.5 * LOG2E)
                o = diff_attention(qkv, qkv[:, 2 * D:].T, pos, lut, diff_lambda[jm].astype(F32), diff_subln[jm],
                                   _diff_lambda_init(i))
                xb = matmul_residual(o, diff_w_o[jm].astype(BF16), xb)
            elif i % N_MIXERS == 1:
                qkv = norm_matmul(xb, norm_mix[i], moba_w_qkv[jm].astype(BF16), out_dtype=BF16,
                                  scaled_cols=D, scale=MOBA_HEAD_DIM ** -0.5 * LOG2E)
                o = moba_attention(qkv, qkv[:, 2 * D:].T, pos, lut)
                xb = matmul_residual(o, moba_w_o[jm].astype(BF16), xb)
            else:
                wd = mla_w_down[jm]
                kr0 = MLA_Q_RANK + MLA_KV_RANK
                wd_ext = jnp.concatenate([wd, _rot_half_cols(wd[:, kr0:])], axis=1).astype(BF16)
                wq = mla_w_uq[jm].reshape(MLA_Q_RANK, MLA_HEADS, MLA_NOPE + MLA_ROPE)
                wq_ext = jnp.concatenate([wq, _rot_half_cols(wq[..., MLA_NOPE:])], axis=-1)
                wq_ext = wq_ext.reshape(MLA_Q_RANK, MLA_HEADS * 2 * LANES).astype(BF16)
                down = norm_matmul(xb, norm_mix[i], wd_ext, out_dtype=F32, tn=wd_ext.shape[1])
                q, kr = mla_queries_and_rope_key(down, mla_g_q[jm], wq_ext, _rope_table(pos),
                                                 scale=(MLA_NOPE + MLA_ROPE) ** -0.5 * LOG2E)
                wkv = mla_w_ukv[jm].reshape(MLA_KV_RANK, MLA_HEADS, 2, MLA_V).transpose(0, 2, 1, 3)
                wkv = wkv.reshape(MLA_KV_RANK, 2 * MLA_HEADS * MLA_V).astype(BF16)
                kv = norm_matmul(down, mla_g_kv[jm], wkv, out_dtype=BF16, xcol=1)
                vt = kv[:, MLA_HEADS * MLA_NOPE:].T
                o = mla_attention(q, kv, kr, vt)
                xb = matmul_residual(o, mla_w_o[jm].astype(BF16), xb)
            w_gate, w_proj = ple_w_gate[i].astype(BF16), ple_w_proj[i].astype(BF16)
            if i % 2 == 0:
                h = norm_swiglu_in(xb, norm_ffn[i], ffn_w_in[i // 2].astype(BF16))
                xb = matmul_residual(h, ffn_w_out[i // 2].astype(BF16), xb)
                xb = ple_update(xb, norm_ple[i], p[i, b], w_gate, w_proj)
            else:
                y1, y2, gates = moe_swiglu(xb, norm_ffn[i], moe_w_router[i // 2], moe_w_in, moe_w_out, i // 2)
                xb = ple_update_moe(xb, y1, y2, gates, norm_ple[i], p[i, b], w_gate, w_proj)
        outs.append(final_norm(xb, norm_final))
    return jnp.stack(outs, axis=0)
```

```python
import functools
import math

import numpy as np
import jax
import jax.numpy as jnp
from jax import lax
from jax.experimental import pallas as pl
from jax.experimental.pallas import tpu as pltpu

F32 = jnp.float32
BF16 = jnp.bfloat16

NORM_EPS = 1e-6
NEG_INF = -1e30
LOG2E = math.log2(math.e)
LANES = 128
VMEM_LIMIT_BYTES = 56 * 1024 * 1024

REL_BUCKETS = 32
REL_MAX_DIST = 128
LUT_SIZE = LANES
BIAS_BLOCK = 256
DIFF_HEADS = 8
DIFF_HEAD_DIM = 128
MOBA_HEADS = 16
MOBA_HEAD_DIM = 128
MOBA_BLOCK = 256
MOBA_TOPK = 3
MLA_HEADS = 16
MLA_Q_RANK = 512
MLA_KV_RANK = 512
MLA_NOPE = 128
MLA_ROPE = 64
MLA_V = 128
ROPE_THETA = 10000.0
N_EXPERTS = 8
N_MIXERS = 3
MOE_GATHER_PARTS = 4


def _cparams(*sem):
    return pltpu.CompilerParams(dimension_semantics=sem, vmem_limit_bytes=VMEM_LIMIT_BYTES)


def _tile(n, pref):
    if n <= pref:
        return n
    t = pref
    while n % t:
        t //= 2
    return t


def _rms_rows(x, g):
    r = lax.rsqrt(jnp.mean(x * x, axis=-1, keepdims=True) + NORM_EPS)
    return x * r * g


def _norm_mm_kernel(x_ref, g_ref, w_ref, o_ref, xn_ref, *, scaled_tiles, scale):
    j = pl.program_id(1)

    @pl.when(j == 0)
    def _():
        xn_ref[...] = _rms_rows(x_ref[...], g_ref[...]).astype(BF16)

    acc = jnp.dot(xn_ref[...], w_ref[...], preferred_element_type=F32)
    if scaled_tiles:
        acc = acc * jnp.where(j < scaled_tiles, scale, 1.0)
    o_ref[...] = acc.astype(o_ref.dtype)


def norm_matmul(x, g, w, *, out_dtype, xcol=0, tm=1024, tn=512, scaled_cols=0, scale=1.0):
    M = x.shape[0]
    K, N = w.shape
    tm, tn = _tile(M, tm), _tile(N, tn)
    assert scaled_cols % tn == 0
    kern = functools.partial(_norm_mm_kernel, scaled_tiles=scaled_cols // tn, scale=scale)
    return pl.pallas_call(
        kern,
        out_shape=jax.ShapeDtypeStruct((M, N), out_dtype),
        grid=(M // tm, N // tn),
        in_specs=[pl.BlockSpec((tm, K), lambda i, j: (i, xcol)),
                  pl.BlockSpec((1, K), lambda i, j: (0, 0)),
                  pl.BlockSpec((K, tn), lambda i, j: (0, j))],
        out_specs=pl.BlockSpec((tm, tn), lambda i, j: (i, j)),
        scratch_shapes=[pltpu.VMEM((tm, K), BF16)],
        compiler_params=_cparams("parallel", "arbitrary"),
        name="norm_matmul",
    )(x, g.reshape(1, K), w)


def _norm_swiglu_kernel(x_ref, g_ref, wg_ref, wu_ref, o_ref, xn_ref):
    @pl.when(pl.program_id(1) == 0)
    def _():
        xn_ref[...] = _rms_rows(x_ref[...], g_ref[...]).astype(BF16)

    xn = xn_ref[...]
    a = jnp.dot(xn, wg_ref[...], preferred_element_type=F32)
    u = jnp.dot(xn, wu_ref[...], preferred_element_type=F32)
    o_ref[...] = (a * jax.nn.sigmoid(a) * u).astype(o_ref.dtype)


def norm_swiglu_in(x, g, w_in, *, tm=512, tn=512):
    M, K = x.shape
    F = w_in.shape[1] // 2
    tm, tn = _tile(M, tm), _tile(F, tn)
    nj = F // tn
    return pl.pallas_call(
        _norm_swiglu_kernel,
        out_shape=jax.ShapeDtypeStruct((M, F), BF16),
        grid=(M // tm, nj),
        in_specs=[pl.BlockSpec((tm, K), lambda i, j: (i, 0)),
                  pl.BlockSpec((1, K), lambda i, j: (0, 0)),
                  pl.BlockSpec((K, tn), lambda i, j: (0, j)),
                  pl.BlockSpec((K, tn), lambda i, j: (0, j + nj))],
        out_specs=pl.BlockSpec((tm, tn), lambda i, j: (i, j)),
        scratch_shapes=[pltpu.VMEM((tm, K), BF16)],
        compiler_params=_cparams("parallel", "arbitrary"),
        name="norm_swiglu_in",
    )(x, g.reshape(1, K), w_in, w_in)


def _mm_res_kernel(a_ref, w_ref, r_ref, o_ref):
    o_ref[...] = r_ref[...] + jnp.dot(a_ref[...], w_ref[...], preferred_element_type=F32)


def matmul_residual(a, w, res, *, tm=1024, tn=512):
    M, K = a.shape
    N = w.shape[1]
    tm, tn = _tile(M, tm), _tile(N, tn)
    return pl.pallas_call(
        _mm_res_kernel,
        out_shape=jax.ShapeDtypeStruct((M, N), F32),
        grid=(M // tm, N // tn),
        in_specs=[pl.BlockSpec((tm, K), lambda i, j: (i, 0)),
                  pl.BlockSpec((K, tn), lambda i, j: (0, j)),
                  pl.BlockSpec((tm, tn), lambda i, j: (i, j))],
        out_specs=pl.BlockSpec((tm, tn), lambda i, j: (i, j)),
        compiler_params=_cparams("parallel", "parallel"),
        name="matmul_residual",
    )(a, w, res)


def _ple_kernel(x_ref, g_ref, xr_ref, p_ref, wg_ref, wp_ref, o_ref, xn_ref):
    @pl.when(pl.program_id(1) == 0)
    def _():
        xn_ref[...] = _rms_rows(x_ref[...], g_ref[...]).astype(BF16)

    gate = jax.nn.sigmoid(jnp.dot(xn_ref[...], wg_ref[...], preferred_element_type=F32))
    proj = jnp.dot(p_ref[...].astype(BF16), wp_ref[...], preferred_element_type=F32)
    o_ref[...] = xr_ref[...] + gate * proj


def ple_update(x, g, p, w_gate, w_proj, *, tm=1024, tn=512):
    M, D = x.shape
    P = p.shape[1]
    tm, tn = _tile(M, tm), _tile(D, tn)
    return pl.pallas_call(
        _ple_kernel,
        out_shape=jax.ShapeDtypeStruct((M, D), F32),
        grid=(M // tm, D // tn),
        in_specs=[pl.BlockSpec((tm, D), lambda i, j: (i, 0)),
                  pl.BlockSpec((1, D), lambda i, j: (0, 0)),
                  pl.BlockSpec((tm, tn), lambda i, j: (i, j)),
                  pl.BlockSpec((tm, P), lambda i, j: (i, 0)),
                  pl.BlockSpec((D, tn), lambda i, j: (0, j)),
                  pl.BlockSpec((P, tn), lambda i, j: (0, j))],
        out_specs=pl.BlockSpec((tm, tn), lambda i, j: (i, j)),
        scratch_shapes=[pltpu.VMEM((tm, D), BF16)],
        compiler_params=_cparams("parallel", "arbitrary"),
        name="ple_update",
    )(x, g.reshape(1, D), x, p, w_gate, w_proj)


def _ple_moe_kernel(x_ref, y1_ref, y2_ref, gt_ref, g_ref, p_ref, wg_ref, wp_ref, o_ref, xs_ref, xn_ref, *, tn):
    j = pl.program_id(1)

    @pl.when(j == 0)
    def _():
        gt = gt_ref[...]
        xnew = x_ref[...] + gt[:, 0:1] * y1_ref[...] + gt[:, 1:2] * y2_ref[...]
        xn_ref[...] = _rms_rows(xnew, g_ref[...]).astype(BF16)
        for jj in range(xs_ref.shape[0]):
            xs_ref[jj] = xnew[:, jj * tn:(jj + 1) * tn]

    gate = jax.nn.sigmoid(jnp.dot(xn_ref[...], wg_ref[...], preferred_element_type=F32))
    proj = jnp.dot(p_ref[...].astype(BF16), wp_ref[...], preferred_element_type=F32)
    o_ref[...] = xs_ref[j] + gate * proj


def ple_update_moe(x, y1, y2, gates, g, p, w_gate, w_proj, *, tm=512, tn=512):
    M, D = x.shape
    P = p.shape[1]
    tm, tn = _tile(M, tm), _tile(D, tn)
    row = pl.BlockSpec((tm, D), lambda i, j: (i, 0))
    return pl.pallas_call(
        functools.partial(_ple_moe_kernel, tn=tn),
        out_shape=jax.ShapeDtypeStruct((M, D), F32),
        grid=(M // tm, D // tn),
        in_specs=[row, row, row,
                  pl.BlockSpec((tm, 2), lambda i, j: (i, 0)),
                  pl.BlockSpec((1, D), lambda i, j: (0, 0)),
                  pl.BlockSpec((tm, P), lambda i, j: (i, 0)),
                  pl.BlockSpec((D, tn), lambda i, j: (0, j)),
                  pl.BlockSpec((P, tn), lambda i, j: (0, j))],
        out_specs=pl.BlockSpec((tm, tn), lambda i, j: (i, j)),
        scratch_shapes=[pltpu.VMEM((D // tn, tm, tn), F32), pltpu.VMEM((tm, D), BF16)],
        compiler_params=_cparams("parallel", "arbitrary"),
        name="ple_update_moe",
    )(x, y1, y2, gates, g.reshape(1, D), p, w_gate, w_proj)


def _final_norm_kernel(x_ref, g_ref, o_ref):
    o_ref[...] = _rms_rows(x_ref[...], g_ref[...])


def final_norm(x, g, *, tm=512):
    M, D = x.shape
    tm = _tile(M, tm)
    return pl.pallas_call(
        _final_norm_kernel,
        out_shape=jax.ShapeDtypeStruct((M, D), F32),
        grid=(M // tm,),
        in_specs=[pl.BlockSpec((tm, D), lambda i: (i, 0)),
                  pl.BlockSpec((1, D), lambda i: (0, 0))],
        out_specs=pl.BlockSpec((tm, D), lambda i: (i, 0)),
        compiler_params=_cparams("parallel"),
        name="final_norm",
    )(x, g.reshape(1, D))


def _pair_tables(n_tiles):
    qi, kj = [], []
    for i in range(n_tiles):
        for j in range(i + 1):
            qi.append(i)
            kj.append(j)
    return jnp.asarray(np.array(qi, np.int32)), jnp.asarray(np.array(kj, np.int32))


def _rel_bucket(dist):
    n = jnp.maximum(dist, 0)
    max_exact = REL_BUCKETS // 2
    nf = jnp.maximum(n, 1).astype(F32)
    large = max_exact + (jnp.log(nf / max_exact) / math.log(REL_MAX_DIST / max_exact)
                         * (REL_BUCKETS - max_exact)).astype(jnp.int32)
    large = jnp.minimum(large, REL_BUCKETS - 1)
    return jnp.where(n < max_exact, n, large)


def _bias_lut(rel_bias):
    buckets = _rel_bucket(jnp.arange(LUT_SIZE, dtype=jnp.int32))
    return rel_bias.astype(F32)[buckets].T * LOG2E


def _block_pos_bounds(pos, sb):
    blocks = pos.reshape(pos.shape[0] // sb, sb)
    return blocks.min(axis=1), blocks.max(axis=1)


def _lut_bias(lut_row, idx):
    R, C = idx.shape
    lut = jnp.broadcast_to(lut_row, (R, LANES))
    parts = [jnp.take_along_axis(lut, idx[:, c:c + LANES], axis=1) for c in range(0, C, LANES)]
    return parts[0] if len(parts) == 1 else jnp.concatenate(parts, axis=1)


def _pipelined(units, scores, consume):
    st = scores(*units[0])
    for k, u in enumerate(units):
        nxt = scores(*units[k + 1]) if k + 1 < len(units) else None
        consume(*u, st)
        st = nxt


def _online_softmax_step_t(st, vt, m_ref, l_ref, acc_ref, qcols, m_cur=None, offsets=None):
    m_prev = m_ref[:, qcols]
    if m_cur is None:
        m_cur = jnp.max(st, axis=0, keepdims=True)
    m_new = jnp.maximum(m_prev, m_cur)
    alpha = jnp.exp2(m_prev - m_new)
    if offsets is None:
        p = jnp.exp2(st - m_new)
    else:
        parts = [jnp.exp2(st[rows, :] - fn(m_new)) for rows, fn in offsets]
        p = parts[0] if len(parts) == 1 else jnp.concatenate(parts, axis=0)
    l_ref[:, qcols] = alpha * l_ref[:, qcols] + jnp.sum(p, axis=0, keepdims=True)
    acc_ref[:, qcols] = alpha * acc_ref[:, qcols] + jnp.dot(vt, p.astype(BF16), preferred_element_type=F32)
    m_ref[:, qcols] = m_new


def _causal_t(k0, nk, q0, nq):
    r = k0 + lax.broadcasted_iota(jnp.int32, (nk, nq), 0)
    c = q0 + lax.broadcasted_iota(jnp.int32, (nk, nq), 1)
    return r <= c


def _nt_dot(a, b):
    return lax.dot_general(a, b, (((1,), (1,)), ((), ())), preferred_element_type=F32)


def _diff_attn_kernel(qi_ref, kj_ref, pminq_ref, pmaxk_ref,
                      q_ref, k_ref, vt_ref, pq_ref, pk_ref, lut_ref, lam_ref, g_ref,
                      o_ref, m_ref, l_ref, acc_ref, st_ref, *, tq, ck, sb, lambda_init):
    p = pl.program_id(1)
    i, j = qi_ref[p], kj_ref[p]
    nc = tq // ck
    dh = DIFF_HEAD_DIM
    SB = sb
    nqb, nkb = tq // SB, ck // SB

    @pl.when(j == 0)
    def _():
        m_ref[...] = jnp.full(m_ref.shape, NEG_INF, F32)
        l_ref[...] = jnp.zeros(l_ref.shape, F32)
        acc_ref[...] = jnp.zeros(acc_ref.shape, F32)

    def near_sub(c, kb, qb):
        return pminq_ref[i * nqb + qb] - pmaxk_ref[(j * nc + c) * nkb + kb] < LUT_SIZE - 1

    def refine_chunk(c):
        keys = slice(c * ck, (c + 1) * ck)
        for mp in range(2):
            st_ref[mp] = _nt_dot(k_ref[keys, mp * dh:(mp + 1) * dh], q_ref[:, mp * dh:(mp + 1) * dh])
        for kb in range(nkb):
            for qb in range(nqb):
                @pl.when(near_sub(c, kb, qb))
                def _(kb=kb, qb=qb):
                    rows, cols = slice(kb * SB, (kb + 1) * SB), slice(qb * SB, (qb + 1) * SB)
                    idx = jnp.clip(pq_ref[:, cols] - pk_ref[c * ck + kb * SB:c * ck + (kb + 1) * SB, :],
                                   0, LUT_SIZE - 1)
                    for mp in range(2):
                        far_bias = lut_ref[mp:mp + 1, LUT_SIZE - 1:LUT_SIZE]
                        st_ref[mp, rows, cols] = (st_ref[mp, rows, cols]
                                                  + (_lut_bias(lut_ref[mp:mp + 1, :], idx) - far_bias))
        for mp in range(2):
            consume(c, mp, 0, "far", st_ref[mp])

    def scores(c, mp, q0, mode):
        keys = slice(c * ck, (c + 1) * ck)
        st = _nt_dot(k_ref[keys, mp * dh:(mp + 1) * dh], q_ref[q0:tq, mp * dh:(mp + 1) * dh])
        if mode != "far":
            idx = jnp.clip(pq_ref[:, q0:tq] - pk_ref[keys, :], 0, LUT_SIZE - 1)
            st = st + _lut_bias(lut_ref[mp:mp + 1, :], idx)
        if mode == "diag":
            st = jnp.where(_causal_t(c * ck, ck, q0, tq - q0), st, NEG_INF)
        return st

    def consume(c, mp, q0, mode, st):
        m_cur = offsets = None
        if mode == "far":
            bias = lut_ref[mp:mp + 1, LUT_SIZE - 1:LUT_SIZE]
            m_cur = jnp.max(st, axis=0, keepdims=True) + bias
            offsets = [(slice(None), lambda m_new: m_new - bias)]
        _online_softmax_step_t(st, vt_ref[:, c * ck:(c + 1) * ck], m_ref.at[mp], l_ref.at[mp], acc_ref.at[mp],
                               slice(q0, tq), m_cur, offsets)

    def sweep(chunks, mode):
        units = [(c, mp, c * ck if mode == "diag" else 0, mode) for c in chunks for mp in range(2)]
        _pipelined(units, scores, consume)

    @pl.when(j < i)
    def _():
        fars = [jnp.logical_not(functools.reduce(
            jnp.logical_or, [near_sub(c, kb, qb) for kb in range(nkb) for qb in range(nqb)])) for c in range(nc)]
        all_far = functools.reduce(jnp.logical_and, fars)
        pl.when(all_far)(functools.partial(sweep, range(nc), "far"))

        @pl.when(jnp.logical_not(all_far))
        def _():
            for c in range(nc):
                pl.when(fars[c])(functools.partial(sweep, [c], "far"))
                pl.when(jnp.logical_not(fars[c]))(functools.partial(refine_chunk, c))

    @pl.when(j == i)
    def _():
        sweep(range(nc), "diag")
        lam = lam_ref[...]
        lam_full = (jnp.exp(jnp.sum(lam[0:1] * lam[1:2], axis=-1, keepdims=True))
                    - jnp.exp(jnp.sum(lam[2:3] * lam[3:4], axis=-1, keepdims=True)) + lambda_init)
        a = acc_ref[0] / l_ref[0] - lam_full * (acc_ref[1] / l_ref[1])
        r = lax.rsqrt(jnp.mean(a * a, axis=0, keepdims=True) + NORM_EPS)
        y = a * r * (g_ref[...] * (1.0 - lambda_init))
        o_ref[...] = y.T.astype(o_ref.dtype)


def diff_attention(qkv, vt, pos, lut, lam, subln_g, lambda_init, *, tile=1024, chunk=512):
    S = qkv.shape[0]
    H, dh = DIFF_HEADS, DIFF_HEAD_DIM
    tq = _tile(S, tile)
    ck = _tile(tq, chunk)
    qi, kj = _pair_tables(S // tq)
    sb = _tile(ck, BIAS_BLOCK)
    pminq, pmaxk = _block_pos_bounds(pos, sb)
    kern = functools.partial(_diff_attn_kernel, tq=tq, ck=ck, sb=sb, lambda_init=lambda_init)
    grid_spec = pltpu.PrefetchScalarGridSpec(
        num_scalar_prefetch=4,
        grid=(H, qi.shape[0]),
        in_specs=[pl.BlockSpec((tq, 2 * dh), lambda h, p, qi, kj, a, b: (qi[p], h)),
                  pl.BlockSpec((tq, 2 * dh), lambda h, p, qi, kj, a, b: (kj[p], H + h)),
                  pl.BlockSpec((2 * dh, tq), lambda h, p, qi, kj, a, b: (h, kj[p])),
                  pl.BlockSpec((1, tq), lambda h, p, qi, kj, a, b: (0, qi[p])),
                  pl.BlockSpec((tq, 1), lambda h, p, qi, kj, a, b: (kj[p], 0)),
                  pl.BlockSpec((None, 2, LUT_SIZE), lambda h, p, qi, kj, a, b: (h, 0, 0)),
                  pl.BlockSpec((4, dh), lambda h, p, qi, kj, a, b: (0, 0)),
                  pl.BlockSpec((2 * dh, 1), lambda h, p, qi, kj, a, b: (0, 0))],
        out_specs=pl.BlockSpec((tq, 2 * dh), lambda h, p, qi, kj, a, b: (qi[p], h)),
        scratch_shapes=[pltpu.VMEM((2, 1, tq), F32), pltpu.VMEM((2, 1, tq), F32),
                        pltpu.VMEM((2, 2 * dh, tq), F32), pltpu.VMEM((2, ck, tq), F32)],
    )
    return pl.pallas_call(
        kern,
        out_shape=jax.ShapeDtypeStruct((S, H * 2 * dh), BF16),
        grid_spec=grid_spec,
        compiler_params=_cparams("parallel", "arbitrary"),
        name="diff_attention",
    )(qi, kj, pminq, pmaxk, qkv, qkv, vt, pos.reshape(1, S), pos.reshape(S, 1),
      lut.reshape(H, 2, LUT_SIZE), lam, subln_g.reshape(2 * dh, 1))


def _kmean_kernel(k_ref, hi_ref, lo_ref):
    km = jnp.mean(k_ref[...].astype(F32), axis=0, keepdims=True)
    hi = km.astype(BF16)
    hi_ref[...] = hi
    lo_ref[...] = (km - hi.astype(F32)).astype(BF16)


def moba_block_means(qkv, D):
    S = qkv.shape[0]
    nblk = S // MOBA_BLOCK
    hi, lo = pl.pallas_call(
        _kmean_kernel,
        out_shape=[jax.ShapeDtypeStruct((nblk, 1, D), BF16)] * 2,
        grid=(nblk,),
        in_specs=[pl.BlockSpec((MOBA_BLOCK, D), lambda n: (n, 1))],
        out_specs=[pl.BlockSpec((None, 1, D), lambda n: (n, 0, 0))] * 2,
        compiler_params=_cparams("parallel"),
        name="moba_block_means",
    )(qkv)
    return hi.reshape(nblk, D), lo.reshape(nblk, D)


def _moba_attn_kernel(qi_ref, kj_ref, pminq_ref, pmaxk_ref,
                      q_ref, k_ref, vt_ref, kmh_ref, kml_ref, pq_ref, pk_ref, lut_ref,
                      o_ref, m_ref, l_ref, acc_ref, sel_ref, st_ref, *, tq, ck, sb, hp):
    p = pl.program_id(1)
    i, j = qi_ref[p], kj_ref[p]
    nc = tq // ck
    L, dh = MOBA_BLOCK, MOBA_HEAD_DIM
    bpc = ck // L
    log2_l = L.bit_length() - 1
    SB = sb
    nqb, nkb = tq // SB, ck // SB

    def near_sub(c, kb, qb):
        return pminq_ref[i * nqb + qb] - pmaxk_ref[(j * nc + c) * nkb + kb] < LUT_SIZE - 1

    def refine_chunk(c):
        keys = slice(c * ck, (c + 1) * ck)
        for hd in range(hp):
            st_ref[hd] = _nt_dot(k_ref[keys, hd * dh:(hd + 1) * dh], q_ref[:, hd * dh:(hd + 1) * dh])
        for kb in range(nkb):
            for qb in range(nqb):
                @pl.when(near_sub(c, kb, qb))
                def _(kb=kb, qb=qb):
                    rows, cols = slice(kb * SB, (kb + 1) * SB), slice(qb * SB, (qb + 1) * SB)
                    idx = jnp.clip(pq_ref[:, cols] - pk_ref[c * ck + kb * SB:c * ck + (kb + 1) * SB, :],
                                   0, LUT_SIZE - 1)
                    for hd in range(hp):
                        far_bias = lut_ref[hd:hd + 1, LUT_SIZE - 1:LUT_SIZE]
                        st_ref[hd, rows, cols] = (st_ref[hd, rows, cols]
                                                  + (_lut_bias(lut_ref[hd:hd + 1, :], idx) - far_bias))
        for hd in range(hp):
            consume(c, hd, 0, "far", st_ref[hd])

    @pl.when(j == 0)
    def _():
        m_ref[...] = jnp.full(m_ref.shape, NEG_INF, F32)
        l_ref[...] = jnp.zeros(l_ref.shape, F32)
        acc_ref[...] = jnp.zeros(acc_ref.shape, F32)
        blk = lax.broadcasted_iota(jnp.int32, (LANES, tq), 0)
        blk_f = blk.astype(F32)
        own = lax.shift_right_logical(i * tq + lax.broadcasted_iota(jnp.int32, (1, tq), 1), log2_l)
        for hd in range(hp):
            q = q_ref[:, hd * dh:(hd + 1) * dh]
            gs = (_nt_dot(kmh_ref[:, hd * dh:(hd + 1) * dh], q)
                  + _nt_dot(kml_ref[:, hd * dh:(hd + 1) * dh], q))
            g = jnp.where(blk < own, gs, NEG_INF)
            sel = jnp.zeros((LANES, tq), F32)
            for _ in range(MOBA_TOPK):
                best = jnp.max(g, axis=0, keepdims=True)
                first = jnp.min(jnp.where(g == best, blk_f, float(LANES)), axis=0, keepdims=True)
                pick = blk_f == first
                sel = jnp.where(pick, jnp.where(best > 0.5 * NEG_INF, 1.0, sel), sel)
                g = jnp.where(pick, -3e38, g)
            sel_ref[hd] = sel

    def scores(c, hd, q0, mode):
        keys = slice(c * ck, (c + 1) * ck)
        nq = tq - q0
        st = _nt_dot(k_ref[keys, hd * dh:(hd + 1) * dh], q_ref[q0:tq, hd * dh:(hd + 1) * dh])
        if mode != "far":
            idx = jnp.clip(pq_ref[:, q0:tq] - pk_ref[keys, :], 0, LUT_SIZE - 1)
            st = st + _lut_bias(lut_ref[hd:hd + 1, :], idx)
        if mode != "diag":
            return st
        allowed = []
        for b in range(bpc):
            n_local = c * bpc + b
            picked = jnp.broadcast_to(picked_row(hd, n_local, q0), (L, nq))
            own_local = lax.shift_right_logical(q0 + lax.broadcasted_iota(jnp.int32, (L, nq), 1), log2_l)
            visible = jnp.where(_causal_t(n_local * L, L, q0, nq), 1.0, 0.0)
            allowed.append(jnp.where(own_local == n_local, visible, picked))
        allowed = allowed[0] if bpc == 1 else jnp.concatenate(allowed, axis=0)
        return jnp.where(allowed > 0.0, st, NEG_INF)

    def picked_row(hd, n_local, q0):
        return sel_ref[hd, pl.ds(j * (tq // L) + n_local, 1), q0:tq]

    def consume(c, hd, q0, mode, st):
        m_cur = offsets = None
        if mode != "diag":
            bias = lut_ref[hd:hd + 1, LUT_SIZE - 1:LUT_SIZE] if mode == "far" else 0.0
            picked = [picked_row(hd, c * bpc + b, q0) > 0.0 for b in range(bpc)]
            m_cur = functools.reduce(jnp.maximum, [
                jnp.where(picked[b], jnp.max(st[b * L:(b + 1) * L, :], axis=0, keepdims=True) + bias, NEG_INF)
                for b in range(bpc)])
            offsets = [(slice(b * L, (b + 1) * L),
                        functools.partial(lambda m_new, pk: jnp.where(pk, m_new - bias, -NEG_INF), pk=picked[b]))
                       for b in range(bpc)]
        _online_softmax_step_t(st, vt_ref[hd * dh:(hd + 1) * dh, c * ck:(c + 1) * ck],
                               m_ref.at[hd], l_ref.at[hd], acc_ref.at[hd], slice(q0, tq), m_cur, offsets)

    def sweep(chunks, mode):
        _pipelined([(c, hd, c * ck if mode == "diag" else 0, mode) for c in chunks for hd in range(hp)],
                   scores, consume)

    @pl.when(j < i)
    def _():
        fars = [jnp.logical_not(functools.reduce(
            jnp.logical_or, [near_sub(c, kb, qb) for kb in range(nkb) for qb in range(nqb)])) for c in range(nc)]
        all_far = functools.reduce(jnp.logical_and, fars)
        pl.when(all_far)(functools.partial(sweep, range(nc), "far"))

        @pl.when(jnp.logical_not(all_far))
        def _():
            for c in range(nc):
                pl.when(fars[c])(functools.partial(sweep, [c], "far"))
                pl.when(jnp.logical_not(fars[c]))(functools.partial(refine_chunk, c))

    @pl.when(j == i)
    def _():
        sweep(range(nc), "diag")
        for hd in range(hp):
            o_ref[:, hd * dh:(hd + 1) * dh] = (acc_ref[hd] / l_ref[hd]).T.astype(o_ref.dtype)


def moba_attention(qkv, vt, pos, lut, *, tile=1024, chunk=512, heads_per_step=2):
    S = qkv.shape[0]
    H, dh, L = MOBA_HEADS, MOBA_HEAD_DIM, MOBA_BLOCK
    D = H * dh
    nblk = S // L
    assert S % L == 0 and nblk <= LANES
    tq = _tile(S, tile)
    ck = _tile(tq, chunk)
    assert tq % L == 0 and ck % L == 0
    km_hi, km_lo = moba_block_means(qkv, D)
    km_hi = jnp.pad(km_hi, ((0, LANES - nblk), (0, 0)))
    km_lo = jnp.pad(km_lo, ((0, LANES - nblk), (0, 0)))
    qi, kj = _pair_tables(S // tq)
    sb = _tile(ck, BIAS_BLOCK)
    pminq, pmaxk = _block_pos_bounds(pos, sb)
    hp = heads_per_step
    G = H // hp
    kern = functools.partial(_moba_attn_kernel, tq=tq, ck=ck, sb=sb, hp=hp)
    grid_spec = pltpu.PrefetchScalarGridSpec(
        num_scalar_prefetch=4,
        grid=(G, qi.shape[0]),
        in_specs=[pl.BlockSpec((tq, hp * dh), lambda h, p, qi, kj, a, b: (qi[p], h)),
                  pl.BlockSpec((tq, hp * dh), lambda h, p, qi, kj, a, b: (kj[p], G + h)),
                  pl.BlockSpec((hp * dh, tq), lambda h, p, qi, kj, a, b: (h, kj[p])),
                  pl.BlockSpec((LANES, hp * dh), lambda h, p, qi, kj, a, b: (0, h)),
                  pl.BlockSpec((LANES, hp * dh), lambda h, p, qi, kj, a, b: (0, h)),
                  pl.BlockSpec((1, tq), lambda h, p, qi, kj, a, b: (0, qi[p])),
                  pl.BlockSpec((tq, 1), lambda h, p, qi, kj, a, b: (kj[p], 0)),
                  pl.BlockSpec((None, hp, LUT_SIZE), lambda h, p, qi, kj, a, b: (h, 0, 0))],
        out_specs=pl.BlockSpec((tq, hp * dh), lambda h, p, qi, kj, a, b: (qi[p], h)),
        scratch_shapes=[pltpu.VMEM((hp, 1, tq), F32), pltpu.VMEM((hp, 1, tq), F32),
                        pltpu.VMEM((hp, dh, tq), F32), pltpu.VMEM((hp, LANES, tq), F32),
                        pltpu.VMEM((hp, ck, tq), F32)],
    )
    return pl.pallas_call(
        kern,
        out_shape=jax.ShapeDtypeStruct((S, D), BF16),
        grid_spec=grid_spec,
        compiler_params=_cparams("parallel", "arbitrary"),
        name="moba_attention",
    )(qi, kj, pminq, pmaxk, qkv, qkv, vt, km_hi, km_lo, pos.reshape(1, S), pos.reshape(S, 1),
      lut.reshape(G, hp, LUT_SIZE))


def _rope_fold(y):
    lane = lax.broadcasted_iota(jnp.int32, y.shape, 1)
    return jnp.where(lane < MLA_ROPE, y + pltpu.roll(y, MLA_ROPE, 1), 0.0)


def _mla_q_kernel(cq_ref, g_ref, w_ref, kr_ref, t_ref, q_ref, kro_ref, cqn_ref, *, scale, hp):
    t = t_ref[...]
    W = 2 * LANES

    @pl.when(pl.program_id(1) == 0)
    def _():
        cqn_ref[...] = _rms_rows(cq_ref[...], g_ref[...]).astype(BF16)
        kro_ref[...] = _rope_fold(kr_ref[...] * t).astype(kro_ref.dtype)

    qh = jnp.dot(cqn_ref[...], w_ref[...], preferred_element_type=F32)
    for hd in range(hp):
        q_ref[:, hd * W:hd * W + MLA_NOPE] = (qh[:, hd * W:hd * W + MLA_NOPE] * scale).astype(q_ref.dtype)
        q_ref[:, hd * W + MLA_NOPE:(hd + 1) * W] = (
            _rope_fold(qh[:, hd * W + MLA_NOPE:(hd + 1) * W] * t) * scale).astype(q_ref.dtype)


def mla_queries_and_rope_key(down, g_q, w_uq_ext, rope_tab, *, scale, tm=1024, heads_per_step=4):
    S = down.shape[0]
    H, hp = MLA_HEADS, heads_per_step
    tm = _tile(S, tm)
    W = 2 * LANES
    return pl.pallas_call(
        functools.partial(_mla_q_kernel, scale=scale, hp=hp),
        out_shape=[jax.ShapeDtypeStruct((S, H * W), BF16), jax.ShapeDtypeStruct((S, LANES), BF16)],
        grid=(S // tm, H // hp),
        in_specs=[pl.BlockSpec((tm, MLA_Q_RANK), lambda i, h: (i, 0)),
                  pl.BlockSpec((1, MLA_Q_RANK), lambda i, h: (0, 0)),
                  pl.BlockSpec((MLA_Q_RANK, hp * W), lambda i, h: (0, h)),
                  pl.BlockSpec((tm, LANES), lambda i, h: (i, (MLA_Q_RANK + MLA_KV_RANK) // LANES)),
                  pl.BlockSpec((tm, LANES), lambda i, h: (i, 0))],
        out_specs=[pl.BlockSpec((tm, hp * W), lambda i, h: (i, h)),
                   pl.BlockSpec((tm, LANES), lambda i, h: (i, 0))],
        scratch_shapes=[pltpu.VMEM((tm, MLA_Q_RANK), BF16)],
        compiler_params=_cparams("parallel", "arbitrary"),
        name="mla_queries",
    )(down, g_q.reshape(1, MLA_Q_RANK), w_uq_ext, down, rope_tab)


def _mla_attn_kernel(qi_ref, kj_ref, q_ref, kv_ref, kr_ref, vt_ref, o_ref, m_ref, l_ref, acc_ref, *, tq, ck, hp):
    p = pl.program_id(1)
    i, j = qi_ref[p], kj_ref[p]
    nc = tq // ck
    W = 2 * LANES

    @pl.when(j == 0)
    def _():
        m_ref[...] = jnp.full(m_ref.shape, NEG_INF, F32)
        l_ref[...] = jnp.zeros(l_ref.shape, F32)
        acc_ref[...] = jnp.zeros(acc_ref.shape, F32)

    def scores(c, g, q0, diag):
        keys = slice(c * ck, (c + 1) * ck)
        kc = jnp.concatenate([kv_ref[keys, g * MLA_NOPE:(g + 1) * MLA_NOPE], kr_ref[keys, :]], axis=1)
        st = _nt_dot(kc, q_ref[q0:tq, g * W:(g + 1) * W])
        if diag:
            st = jnp.where(_causal_t(c * ck, ck, q0, tq - q0), st, NEG_INF)
        return st

    def consume(c, g, q0, diag, st):
        _online_softmax_step_t(st, vt_ref[g * MLA_V:(g + 1) * MLA_V, c * ck:(c + 1) * ck],
                               m_ref.at[g], l_ref.at[g], acc_ref.at[g], slice(q0, tq))

    def sweep(diag):
        _pipelined([(c, g, c * ck if diag else 0, diag) for c in range(nc) for g in range(hp)], scores, consume)

    @pl.when(j < i)
    def _():
        sweep(False)

    @pl.when(j == i)
    def _():
        sweep(True)
        for g in range(hp):
            o_ref[:, g * MLA_V:(g + 1) * MLA_V] = (acc_ref[g] / l_ref[g]).T.astype(o_ref.dtype)


def mla_attention(q, kv, kr, vt, *, tile=1024, chunk=512, heads_per_step=2):
    S = q.shape[0]
    H, hp = MLA_HEADS, heads_per_step
    tq = _tile(S, tile)
    ck = _tile(tq, chunk)
    qi, kj = _pair_tables(S // tq)
    W = 2 * LANES
    grid_spec = pltpu.PrefetchScalarGridSpec(
        num_scalar_prefetch=2,
        grid=(H // hp, qi.shape[0]),
        in_specs=[pl.BlockSpec((tq, hp * W), lambda h, p, qi, kj: (qi[p], h)),
                  pl.BlockSpec((tq, hp * MLA_NOPE), lambda h, p, qi, kj: (kj[p], h)),
                  pl.BlockSpec((tq, LANES), lambda h, p, qi, kj: (kj[p], 0)),
                  pl.BlockSpec((hp * MLA_V, tq), lambda h, p, qi, kj: (h, kj[p]))],
        out_specs=pl.BlockSpec((tq, hp * MLA_V), lambda h, p, qi, kj: (qi[p], h)),
        scratch_shapes=[pltpu.VMEM((hp, 1, tq), F32), pltpu.VMEM((hp, 1, tq), F32),
                        pltpu.VMEM((hp, MLA_V, tq), F32)],
    )
    return pl.pallas_call(
        functools.partial(_mla_attn_kernel, tq=tq, ck=ck, hp=hp),
        out_shape=jax.ShapeDtypeStruct((S, H * MLA_V), BF16),
        grid_spec=grid_spec,
        compiler_params=_cparams("parallel", "arbitrary"),
        name="mla_attention",
    )(qi, kj, q, kv, kr, vt)


_HI16 = 0xFFFF0000


def _pack_bf16_pairs(hi_f32):
    half = hi_f32.shape[1] // 2
    bits = pltpu.bitcast(hi_f32, jnp.uint32)
    return lax.shift_right_logical(bits[:, :half], jnp.uint32(16)) | (bits[:, half:] & jnp.uint32(_HI16))


def _unpack_bf16_pairs(words):
    lo = pltpu.bitcast(lax.shift_left(words, jnp.uint32(16)), F32).astype(BF16)
    hi = pltpu.bitcast(words & jnp.uint32(_HI16), F32).astype(BF16)
    return jnp.concatenate([lo, hi], axis=1)


def _router_kernel(x_ref, g_ref, wh_ref, wl_ref, xn_ref, r_ref):
    xn = _rms_rows(x_ref[...], g_ref[...])
    hi = xn.astype(BF16)
    hi_f32 = hi.astype(F32)
    lo = (xn - hi_f32).astype(BF16)
    xn_ref[...] = _pack_bf16_pairs(hi_f32)
    logits = (jnp.dot(hi, wh_ref[...], preferred_element_type=F32)
              + jnp.dot(hi, wl_ref[...], preferred_element_type=F32)
              + jnp.dot(lo, wh_ref[...], preferred_element_type=F32))
    lane = lax.broadcasted_iota(jnp.int32, logits.shape, 1)
    lane_f = lane.astype(F32)
    g = jnp.where(lane < N_EXPERTS, logits, NEG_INF)
    v1 = jnp.max(g, axis=-1, keepdims=True)
    i1 = jnp.min(jnp.where(g == v1, lane_f, float(LANES)), axis=-1, keepdims=True)
    g = jnp.where(lane_f == i1, NEG_INF, g)
    v2 = jnp.max(g, axis=-1, keepdims=True)
    i2 = jnp.min(jnp.where(g == v2, lane_f, float(LANES)), axis=-1, keepdims=True)
    g2 = 1.0 / (1.0 + jnp.exp(v1 - v2))
    g1 = 1.0 - g2
    r_ref[...] = jnp.where(lane == 0, i1,
                           jnp.where(lane == 1, i2,
                                     jnp.where(lane == 2, g1, jnp.where(lane == 3, g2, 0.0))))


def moe_route(x, g, w_router, *, tm=512):
    M, D = x.shape
    tm = _tile(M, tm)
    wr = jnp.pad(w_router.astype(F32), ((0, 0), (0, LANES - N_EXPERTS)))
    wh = wr.astype(BF16)
    wl = (wr - wh.astype(F32)).astype(BF16)
    return pl.pallas_call(
        _router_kernel,
        out_shape=[jax.ShapeDtypeStruct((M, D // 2), jnp.uint32), jax.ShapeDtypeStruct((M, LANES), F32)],
        grid=(M // tm,),
        in_specs=[pl.BlockSpec((tm, D), lambda i: (i, 0)),
                  pl.BlockSpec((1, D), lambda i: (0, 0)),
                  pl.BlockSpec((D, LANES), lambda i: (0, 0)),
                  pl.BlockSpec((D, LANES), lambda i: (0, 0))],
        out_specs=[pl.BlockSpec((tm, D // 2), lambda i: (i, 0)),
                   pl.BlockSpec((tm, LANES), lambda i: (i, 0))],
        compiler_params=_cparams("parallel"),
        name="moe_route",
    )(x, g.reshape(1, D), wh, wl)


def _new_expert(te_ref, t):
    return jnp.logical_or(t == 0, te_ref[t] != te_ref[jnp.maximum(t - 1, 0)])


def _gmm_swiglu_kernel(te_ref, nu_ref, *refs, n_parts, tiles_per_part):
    x_refs = refs[:n_parts]
    wg_ref, wu_ref, o_ref, wgb_ref, wub_ref = refs[n_parts:]
    t = pl.program_id(1)
    used = t < nu_ref[0]

    @pl.when(jnp.logical_and(used, _new_expert(te_ref, t)))
    def _():
        wgb_ref[...] = wg_ref[...].astype(BF16)
        wub_ref[...] = wu_ref[...].astype(BF16)

    for c in range(n_parts):
        @pl.when(jnp.logical_and(used, t // tiles_per_part == c))
        def _(c=c):
            x = _unpack_bf16_pairs(x_refs[c][...])
            a = jnp.dot(x, wgb_ref[...], preferred_element_type=F32)
            u = jnp.dot(x, wub_ref[...], preferred_element_type=F32)
            o_ref[...] = (a * jax.nn.sigmoid(a) * u).astype(o_ref.dtype)

    @pl.when(jnp.logical_not(used))
    def _():
        o_ref[...] = jnp.zeros(o_ref.shape, o_ref.dtype)


def _gmm_out_kernel(te_ref, nu_ref, h_ref, w_ref, o_ref, wb_ref):
    t = pl.program_id(1)
    used = t < nu_ref[0]

    @pl.when(jnp.logical_and(used, _new_expert(te_ref, t)))
    def _():
        wb_ref[...] = w_ref[...].astype(BF16)

    @pl.when(used)
    def _():
        o_ref[...] = jnp.dot(h_ref[...], wb_ref[...], preferred_element_type=F32)

    @pl.when(jnp.logical_not(used))
    def _():
        o_ref[...] = jnp.zeros(o_ref.shape, o_ref.dtype)


def moe_experts(xs, tile_expert, n_used, w_in, w_out, layer, *, tm, tn=512):
    n_parts = len(xs)
    P = n_parts * xs[0].shape[0]
    D = w_in.shape[2]
    F = w_in.shape[3] // 2
    nt = P // tm
    ntp = nt // n_parts
    assert ntp * n_parts == nt
    tn1, tn2 = _tile(F, tn), _tile(D, tn)
    nj1, nj2 = F // tn1, D // tn2

    def part_spec(c):
        return pl.BlockSpec((tm, D // 2), lambda j, t, te, nu: (jnp.clip(t - c * ntp, 0, ntp - 1), 0))

    h = pl.pallas_call(
        functools.partial(_gmm_swiglu_kernel, n_parts=n_parts, tiles_per_part=ntp),
        out_shape=jax.ShapeDtypeStruct((P, F), BF16),
        grid_spec=pltpu.PrefetchScalarGridSpec(
            num_scalar_prefetch=2,
            grid=(nj1, nt),
            in_specs=[part_spec(c) for c in range(n_parts)] + [
                pl.BlockSpec((None, None, D, tn1), lambda j, t, te, nu: (layer, te[t], 0, j)),
                pl.BlockSpec((None, None, D, tn1), lambda j, t, te, nu: (layer, te[t], 0, nj1 + j))],
            out_specs=pl.BlockSpec((tm, tn1), lambda j, t, te, nu: (t, j)),
            scratch_shapes=[pltpu.VMEM((D, tn1), BF16), pltpu.VMEM((D, tn1), BF16)],
        ),
        compiler_params=_cparams("arbitrary", "arbitrary"),
        name="moe_swiglu_in",
    )(tile_expert, n_used, *xs, w_in, w_in)
    return pl.pallas_call(
        _gmm_out_kernel,
        out_shape=jax.ShapeDtypeStruct((P, D), F32),
        grid_spec=pltpu.PrefetchScalarGridSpec(
            num_scalar_prefetch=2,
            grid=(nj2, nt),
            in_specs=[pl.BlockSpec((tm, F), lambda j, t, te, nu: (t, 0)),
                      pl.BlockSpec((None, None, F, tn2), lambda j, t, te, nu: (layer, te[t], 0, j))],
            out_specs=pl.BlockSpec((tm, tn2), lambda j, t, te, nu: (t, j)),
            scratch_shapes=[pltpu.VMEM((F, tn2), BF16)],
        ),
        compiler_params=_cparams("arbitrary", "arbitrary"),
        name="moe_out",
    )(tile_expert, n_used, h, w_out)


def moe_swiglu(x, g, w_router, w_in, w_out, layer, *, tm=512):
    S, D = x.shape
    E = N_EXPERTS
    tm = _tile(S, tm)
    xn, route = moe_route(x, g, w_router)
    experts = route[:, :2].astype(jnp.int32)
    gates = route[:, 2:4]
    flat_e = experts.reshape(-1)
    order = jnp.argsort(flat_e, stable=True).astype(jnp.int32)
    rank = jnp.argsort(order).astype(jnp.int32)
    counts = jnp.bincount(flat_e, length=E).astype(jnp.int32)
    padded = ((counts + tm - 1) // tm) * tm
    start = jnp.cumsum(counts) - counts
    pstart = jnp.cumsum(padded) - padded
    pend = jnp.cumsum(padded)
    P = 2 * S + E * tm
    nt = P // tm
    tile_expert = jnp.minimum(
        jnp.searchsorted(pend, jnp.arange(nt, dtype=jnp.int32) * tm, side="right"), E - 1).astype(jnp.int32)
    n_used = (pend[-1:] // tm).astype(jnp.int32)
    row_e = jnp.repeat(tile_expert, tm)
    row_rank = jnp.arange(P, dtype=jnp.int32) - pstart[row_e]
    row_valid = jnp.logical_and(row_rank < counts[row_e], jnp.arange(P) < pend[-1])
    row_token = jnp.where(row_valid, order[jnp.clip(start[row_e] + row_rank, 0, 2 * S - 1)] // 2, 0)
    slot = pstart[flat_e] + rank - start[flat_e]
    tile_expert = jnp.where(jnp.arange(nt) < n_used[0], tile_expert, tile_expert[jnp.maximum(n_used[0] - 1, 0)])
    slot = slot.reshape(S, 2)
    xs = [jnp.take(xn, part, axis=0, mode="clip") for part in jnp.split(row_token, MOE_GATHER_PARTS)]
    ys = moe_experts(xs, tile_expert, n_used, w_in, w_out, layer, tm=tm)
    return jnp.take(ys, slot[:, 0], axis=0, mode="clip"), jnp.take(ys, slot[:, 1], axis=0, mode="clip"), gates


def _diff_lambda_init(layer):
    return 0.8 - 0.6 * math.exp(-0.3 * layer)


def _rope_table(pos):
    half = MLA_ROPE // 2
    inv_freq = ROPE_THETA ** (-jnp.arange(half, dtype=F32) / half)
    ang = pos.astype(F32)[:, None] * inv_freq
    cos, sin = jnp.cos(ang), jnp.sin(ang)
    return jnp.concatenate([cos, cos, -sin, sin], axis=1)


def _rot_half_cols(w):
    half = w.shape[-1] // 2
    return jnp.concatenate([w[..., half:], w[..., :half]], axis=-1)


def kernel(x, p, positions, rel_bias, norm_mix, norm_ffn, norm_ple, norm_final, diff_w_qkv, diff_lambda, diff_subln, diff_w_o, moba_w_qkv, moba_w_o, mla_w_down, mla_g_q, mla_w_uq, mla_g_kv, mla_w_ukv, mla_w_o, ffn_w_in, ffn_w_out, moe_w_router, moe_w_in, moe_w_out, ple_w_gate, ple_w_proj):
    B, S, D = x.shape
    depth = p.shape[0]
    lut = _bias_lut(rel_bias)
    outs = []
    for b in range(B):
        xb = x[b]
        pos = positions[b].astype(jnp.int32)
        for i in range(depth):
            jm = i // N_MIXERS
            if i % N_MIXERS == 0:
                qkv = norm_matmul(xb, norm_mix[i], diff_w_qkv[jm].astype(BF16), out_dtype=BF16,
                                  scaled_cols=D, scale=DIFF_HEAD_DIM ** -0.5 * LOG2E)
                o = diff_attention(qkv, qkv[:, 2 * D:].T, pos, lut, diff_lambda[jm].astype(F32), diff_subln[jm],
                                   _diff_lambda_init(i))
                xb = matmul_residual(o, diff_w_o[jm].astype(BF16), xb)
            elif i % N_MIXERS == 1:
                qkv = norm_matmul(xb, norm_mix[i], moba_w_qkv[jm].astype(BF16), out_dtype=BF16,
                                  scaled_cols=D, scale=MOBA_HEAD_DIM ** -0.5 * LOG2E)
                o = moba_attention(qkv, qkv[:, 2 * D:].T, pos, lut)
                xb = matmul_residual(o, moba_w_o[jm].astype(BF16), xb)
            else:
                wd = mla_w_down[jm]
                kr0 = MLA_Q_RANK + MLA_KV_RANK
                wd_ext = jnp.concatenate([wd, _rot_half_cols(wd[:, kr0:])], axis=1).astype(BF16)
                wq = mla_w_uq[jm].reshape(MLA_Q_RANK, MLA_HEADS, MLA_NOPE + MLA_ROPE)
                wq_ext = jnp.concatenate([wq, _rot_half_cols(wq[..., MLA_NOPE:])], axis=-1)
                wq_ext = wq_ext.reshape(MLA_Q_RANK, MLA_HEADS * 2 * LANES).astype(BF16)
                down = norm_matmul(xb, norm_mix[i], wd_ext, out_dtype=F32, tn=wd_ext.shape[1])
                q, kr = mla_queries_and_rope_key(down, mla_g_q[jm], wq_ext, _rope_table(pos),
                                                 scale=(MLA_NOPE + MLA_ROPE) ** -0.5 * LOG2E)
                wkv = mla_w_ukv[jm].reshape(MLA_KV_RANK, MLA_HEADS, 2, MLA_V).transpose(0, 2, 1, 3)
                wkv = wkv.reshape(MLA_KV_RANK, 2 * MLA_HEADS * MLA_V).astype(BF16)
                kv = norm_matmul(down, mla_g_kv[jm], wkv, out_dtype=BF16, xcol=1)
                vt = kv[:, MLA_HEADS * MLA_NOPE:].T
                o = mla_attention(q, kv, kr, vt)
                xb = matmul_residual(o, mla_w_o[jm].astype(BF16), xb)
            w_gate, w_proj = ple_w_gate[i].astype(BF16), ple_w_proj[i].astype(BF16)
            if i % 2 == 0:
                h = norm_swiglu_in(xb, norm_ffn[i], ffn_w_in[i // 2].astype(BF16))
                xb = matmul_residual(h, ffn_w_out[i // 2].astype(BF16), xb)
                xb = ple_update(xb, norm_ple[i], p[i, b], w_gate, w_proj)
            else:
                y1, y2, gates = moe_swiglu(xb, norm_ffn[i], moe_w_router[i // 2], moe_w_in, moe_w_out, i // 2)
                xb = ple_update_moe(xb, y1, y2, gates, norm_ple[i], p[i, b], w_gate, w_proj)
        outs.append(final_norm(xb, norm_final))
    return jnp.stack(outs, axis=0)
```

```python
import functools
import math

import numpy as np
import jax
import jax.numpy as jnp
from jax import lax
from jax.experimental import pallas as pl
from jax.experimental.pallas import tpu as pltpu

F32 = jnp.float32
BF16 = jnp.bfloat16

NORM_EPS = 1e-6
NEG_INF = -1e30
LOG2E = math.log2(math.e)
LANES = 128
VMEM_LIMIT_BYTES = 56 * 1024 * 1024

REL_BUCKETS = 32
REL_MAX_DIST = 128
LUT_SIZE = LANES
BIAS_BLOCK = 256
DIFF_HEADS = 8
DIFF_HEAD_DIM = 128
MOBA_HEADS = 16
MOBA_HEAD_DIM = 128
MOBA_BLOCK = 256
MOBA_TOPK = 3
MLA_HEADS = 16
MLA_Q_RANK = 512
MLA_KV_RANK = 512
MLA_NOPE = 128
MLA_ROPE = 64
MLA_V = 128
ROPE_THETA = 10000.0
N_EXPERTS = 8
N_MIXERS = 3
MOE_GATHER_PARTS = 4


def _cparams(*sem):
    return pltpu.CompilerParams(dimension_semantics=sem, vmem_limit_bytes=VMEM_LIMIT_BYTES)


def _tile(n, pref):
    if n <= pref:
        return n
    t = pref
    while n % t:
        t //= 2
    return t


def _rms_rows(x, g):
    r = lax.rsqrt(jnp.mean(x * x, axis=-1, keepdims=True) + NORM_EPS)
    return x * r * g


def _norm_mm_kernel(x_ref, g_ref, w_ref, o_ref, xn_ref, *, scaled_tiles, scale):
    j = pl.program_id(1)

    @pl.when(j == 0)
    def _():
        xn_ref[...] = _rms_rows(x_ref[...], g_ref[...]).astype(BF16)

    acc = jnp.dot(xn_ref[...], w_ref[...], preferred_element_type=F32)
    if scaled_tiles:
        acc = acc * jnp.where(j < scaled_tiles, scale, 1.0)
    o_ref[...] = acc.astype(o_ref.dtype)


def norm_matmul(x, g, w, *, out_dtype, xcol=0, tm=1024, tn=512, scaled_cols=0, scale=1.0):
    M = x.shape[0]
    K, N = w.shape
    tm, tn = _tile(M, tm), _tile(N, tn)
    assert scaled_cols % tn == 0
    kern = functools.partial(_norm_mm_kernel, scaled_tiles=scaled_cols // tn, scale=scale)
    return pl.pallas_call(
        kern,
        out_shape=jax.ShapeDtypeStruct((M, N), out_dtype),
        grid=(M // tm, N // tn),
        in_specs=[pl.BlockSpec((tm, K), lambda i, j: (i, xcol)),
                  pl.BlockSpec((1, K), lambda i, j: (0, 0)),
                  pl.BlockSpec((K, tn), lambda i, j: (0, j))],
        out_specs=pl.BlockSpec((tm, tn), lambda i, j: (i, j)),
        scratch_shapes=[pltpu.VMEM((tm, K), BF16)],
        compiler_params=_cparams("parallel", "arbitrary"),
        name="norm_matmul",
    )(x, g.reshape(1, K), w)


def _norm_mm_vt_kernel(x_ref, g_ref, w_ref, o_ref, ot_ref, xn_ref, *, direct_tiles, scaled_tiles, scale):
    j = pl.program_id(1)

    @pl.when(j == 0)
    def _():
        xn_ref[...] = _rms_rows(x_ref[...], g_ref[...]).astype(BF16)

    acc = jnp.dot(xn_ref[...], w_ref[...], preferred_element_type=F32)

    @pl.when(j < direct_tiles)
    def _():
        out = acc * jnp.where(j < scaled_tiles, scale, 1.0) if scaled_tiles else acc
        o_ref[...] = out.astype(o_ref.dtype)

    @pl.when(j >= direct_tiles)
    def _():
        ot_ref[...] = acc.T.astype(ot_ref.dtype)


def norm_matmul_vt(x, g, w, *, n_direct, xcol=0, tm=1024, tn=512, scaled_cols=0, scale=1.0):
    M = x.shape[0]
    K, N = w.shape
    tm, tn = _tile(M, tm), _tile(N, tn)
    assert scaled_cols % tn == 0 and n_direct % tn == 0 and 0 < n_direct < N
    nd = n_direct // tn
    kern = functools.partial(_norm_mm_vt_kernel, direct_tiles=nd, scaled_tiles=scaled_cols // tn, scale=scale)
    return pl.pallas_call(
        kern,
        out_shape=[jax.ShapeDtypeStruct((M, n_direct), BF16), jax.ShapeDtypeStruct((N - n_direct, M), BF16)],
        grid=(M // tm, N // tn),
        in_specs=[pl.BlockSpec((tm, K), lambda i, j: (i, xcol)),
                  pl.BlockSpec((1, K), lambda i, j: (0, 0)),
                  pl.BlockSpec((K, tn), lambda i, j: (0, j))],
        out_specs=[pl.BlockSpec((tm, tn), lambda i, j: (i, jnp.minimum(j, nd - 1))),
                   pl.BlockSpec((tn, tm), lambda i, j: (jnp.maximum(j - nd, 0), i))],
        scratch_shapes=[pltpu.VMEM((tm, K), BF16)],
        compiler_params=_cparams("parallel", "arbitrary"),
        name="norm_matmul_vt",
    )(x, g.reshape(1, K), w)


def _norm_swiglu_kernel(x_ref, g_ref, wg_ref, wu_ref, o_ref, xn_ref):
    @pl.when(pl.program_id(1) == 0)
    def _():
        xn_ref[...] = _rms_rows(x_ref[...], g_ref[...]).astype(BF16)

    xn = xn_ref[...]
    a = jnp.dot(xn, wg_ref[...], preferred_element_type=F32)
    u = jnp.dot(xn, wu_ref[...], preferred_element_type=F32)
    o_ref[...] = (a * jax.nn.sigmoid(a) * u).astype(o_ref.dtype)


def norm_swiglu_in(x, g, w_in, *, tm=512, tn=512):
    M, K = x.shape
    F = w_in.shape[1] // 2
    tm, tn = _tile(M, tm), _tile(F, tn)
    nj = F // tn
    return pl.pallas_call(
        _norm_swiglu_kernel,
        out_shape=jax.ShapeDtypeStruct((M, F), BF16),
        grid=(M // tm, nj),
        in_specs=[pl.BlockSpec((tm, K), lambda i, j: (i, 0)),
                  pl.BlockSpec((1, K), lambda i, j: (0, 0)),
                  pl.BlockSpec((K, tn), lambda i, j: (0, j)),
                  pl.BlockSpec((K, tn), lambda i, j: (0, j + nj))],
        out_specs=pl.BlockSpec((tm, tn), lambda i, j: (i, j)),
        scratch_shapes=[pltpu.VMEM((tm, K), BF16)],
        compiler_params=_cparams("parallel", "arbitrary"),
        name="norm_swiglu_in",
    )(x, g.reshape(1, K), w_in, w_in)


def _mm_res_kernel(a_ref, w_ref, r_ref, o_ref):
    o_ref[...] = r_ref[...] + jnp.dot(a_ref[...], w_ref[...], preferred_element_type=F32)


def matmul_residual(a, w, res, *, tm=1024, tn=512):
    M, K = a.shape
    N = w.shape[1]
    tm, tn = _tile(M, tm), _tile(N, tn)
    return pl.pallas_call(
        _mm_res_kernel,
        out_shape=jax.ShapeDtypeStruct((M, N), F32),
        grid=(M // tm, N // tn),
        in_specs=[pl.BlockSpec((tm, K), lambda i, j: (i, 0)),
                  pl.BlockSpec((K, tn), lambda i, j: (0, j)),
                  pl.BlockSpec((tm, tn), lambda i, j: (i, j))],
        out_specs=pl.BlockSpec((tm, tn), lambda i, j: (i, j)),
        compiler_params=_cparams("parallel", "parallel"),
        name="matmul_residual",
    )(a, w, res)


def _ple_kernel(x_ref, g_ref, xr_ref, p_ref, wg_ref, wp_ref, o_ref, xn_ref):
    @pl.when(pl.program_id(1) == 0)
    def _():
        xn_ref[...] = _rms_rows(x_ref[...], g_ref[...]).astype(BF16)

    gate = jax.nn.sigmoid(jnp.dot(xn_ref[...], wg_ref[...], preferred_element_type=F32))
    proj = jnp.dot(p_ref[...].astype(BF16), wp_ref[...], preferred_element_type=F32)
    o_ref[...] = xr_ref[...] + gate * proj


def ple_update(x, g, p, w_gate, w_proj, *, tm=1024, tn=512):
    M, D = x.shape
    P = p.shape[1]
    tm, tn = _tile(M, tm), _tile(D, tn)
    return pl.pallas_call(
        _ple_kernel,
        out_shape=jax.ShapeDtypeStruct((M, D), F32),
        grid=(M // tm, D // tn),
        in_specs=[pl.BlockSpec((tm, D), lambda i, j: (i, 0)),
                  pl.BlockSpec((1, D), lambda i, j: (0, 0)),
                  pl.BlockSpec((tm, tn), lambda i, j: (i, j)),
                  pl.BlockSpec((tm, P), lambda i, j: (i, 0)),
                  pl.BlockSpec((D, tn), lambda i, j: (0, j)),
                  pl.BlockSpec((P, tn), lambda i, j: (0, j))],
        out_specs=pl.BlockSpec((tm, tn), lambda i, j: (i, j)),
        scratch_shapes=[pltpu.VMEM((tm, D), BF16)],
        compiler_params=_cparams("parallel", "arbitrary"),
        name="ple_update",
    )(x, g.reshape(1, D), x, p, w_gate, w_proj)


def _ple_moe_kernel(x_ref, y1_ref, y2_ref, gt_ref, g_ref, p_ref, wg_ref, wp_ref, o_ref, xs_ref, xn_ref, *, tn):
    j = pl.program_id(1)

    @pl.when(j == 0)
    def _():
        gt = gt_ref[...]
        xnew = x_ref[...] + gt[:, 0:1] * y1_ref[...] + gt[:, 1:2] * y2_ref[...]
        xn_ref[...] = _rms_rows(xnew, g_ref[...]).astype(BF16)
        for jj in range(xs_ref.shape[0]):
            xs_ref[jj] = xnew[:, jj * tn:(jj + 1) * tn]

    gate = jax.nn.sigmoid(jnp.dot(xn_ref[...], wg_ref[...], preferred_element_type=F32))
    proj = jnp.dot(p_ref[...].astype(BF16), wp_ref[...], preferred_element_type=F32)
    o_ref[...] = xs_ref[j] + gate * proj


def ple_update_moe(x, y1, y2, gates, g, p, w_gate, w_proj, *, tm=512, tn=512):
    M, D = x.shape
    P = p.shape[1]
    tm, tn = _tile(M, tm), _tile(D, tn)
    row = pl.BlockSpec((tm, D), lambda i, j: (i, 0))
    return pl.pallas_call(
        functools.partial(_ple_moe_kernel, tn=tn),
        out_shape=jax.ShapeDtypeStruct((M, D), F32),
        grid=(M // tm, D // tn),
        in_specs=[row, row, row,
                  pl.BlockSpec((tm, 2), lambda i, j: (i, 0)),
                  pl.BlockSpec((1, D), lambda i, j: (0, 0)),
                  pl.BlockSpec((tm, P), lambda i, j: (i, 0)),
                  pl.BlockSpec((D, tn), lambda i, j: (0, j)),
                  pl.BlockSpec((P, tn), lambda i, j: (0, j))],
        out_specs=pl.BlockSpec((tm, tn), lambda i, j: (i, j)),
        scratch_shapes=[pltpu.VMEM((D // tn, tm, tn), F32), pltpu.VMEM((tm, D), BF16)],
        compiler_params=_cparams("parallel", "arbitrary"),
        name="ple_update_moe",
    )(x, y1, y2, gates, g.reshape(1, D), p, w_gate, w_proj)


def _final_norm_kernel(x_ref, g_ref, o_ref):
    o_ref[...] = _rms_rows(x_ref[...], g_ref[...])


def final_norm(x, g, *, tm=512):
    M, D = x.shape
    tm = _tile(M, tm)
    return pl.pallas_call(
        _final_norm_kernel,
        out_shape=jax.ShapeDtypeStruct((M, D), F32),
        grid=(M // tm,),
        in_specs=[pl.BlockSpec((tm, D), lambda i: (i, 0)),
                  pl.BlockSpec((1, D), lambda i: (0, 0))],
        out_specs=pl.BlockSpec((tm, D), lambda i: (i, 0)),
        compiler_params=_cparams("parallel"),
        name="final_norm",
    )(x, g.reshape(1, D))


def _pair_tables(n_tiles):
    qi, kj = [], []
    for i in range(n_tiles):
        for j in range(i + 1):
            qi.append(i)
            kj.append(j)
    return jnp.asarray(np.array(qi, np.int32)), jnp.asarray(np.array(kj, np.int32))


def _rel_bucket(dist):
    n = jnp.maximum(dist, 0)
    max_exact = REL_BUCKETS // 2
    nf = jnp.maximum(n, 1).astype(F32)
    large = max_exact + (jnp.log(nf / max_exact) / math.log(REL_MAX_DIST / max_exact)
                         * (REL_BUCKETS - max_exact)).astype(jnp.int32)
    large = jnp.minimum(large, REL_BUCKETS - 1)
    return jnp.where(n < max_exact, n, large)


def _bias_lut(rel_bias):
    buckets = _rel_bucket(jnp.arange(LUT_SIZE, dtype=jnp.int32))
    return rel_bias.astype(F32)[buckets].T * LOG2E


def _block_pos_bounds(pos, sb):
    blocks = pos.reshape(pos.shape[0] // sb, sb)
    return blocks.min(axis=1), blocks.max(axis=1)


def _lut_bias(lut_row, idx):
    R, C = idx.shape
    lut = jnp.broadcast_to(lut_row, (R, LANES))
    parts = [jnp.take_along_axis(lut, idx[:, c:c + LANES], axis=1, mode="promise_in_bounds")
             for c in range(0, C, LANES)]
    return parts[0] if len(parts) == 1 else jnp.concatenate(parts, axis=1)


def _pipelined(units, scores, consume):
    st = scores(*units[0])
    for k, u in enumerate(units):
        nxt = scores(*units[k + 1]) if k + 1 < len(units) else None
        consume(*u, st)
        st = nxt


def _online_softmax_step_t(st, vt, m_ref, l_ref, acc_ref, qcols, m_cur=None, offsets=None):
    m_prev = m_ref[:, qcols]
    if m_cur is None:
        m_cur = jnp.max(st, axis=0, keepdims=True)
    m_new = jnp.maximum(m_prev, m_cur)
    alpha = jnp.exp2(m_prev - m_new)
    if offsets is None:
        p = jnp.exp2(st - m_new)
    else:
        parts = [jnp.exp2(st[rows, :] - fn(m_new)) for rows, fn in offsets]
        p = parts[0] if len(parts) == 1 else jnp.concatenate(parts, axis=0)
    l_ref[:, qcols] = alpha * l_ref[:, qcols] + jnp.sum(p, axis=0, keepdims=True)
    acc_ref[:, qcols] = alpha * acc_ref[:, qcols] + jnp.dot(vt, p.astype(BF16), preferred_element_type=F32)
    m_ref[:, qcols] = m_new


def _causal_t(k0, nk, q0, nq):
    r = k0 + lax.broadcasted_iota(jnp.int32, (nk, nq), 0)
    c = q0 + lax.broadcasted_iota(jnp.int32, (nk, nq), 1)
    return r <= c


def _nt_dot(a, b):
    return lax.dot_general(a, b, (((1,), (1,)), ((), ())), preferred_element_type=F32)


def _diff_attn_kernel(qi_ref, kj_ref, pminq_ref, pmaxk_ref,
                      q_ref, k_ref, vt_ref, pq_ref, pk_ref, lut_ref, lam_ref, g_ref,
                      o_ref, m_ref, l_ref, acc_ref, st_ref, *, tq, ck, sb, lambda_init):
    p = pl.program_id(1)
    i, j = qi_ref[p], kj_ref[p]
    nc = tq // ck
    dh = DIFF_HEAD_DIM
    SB = sb
    nqb, nkb = tq // SB, ck // SB

    @pl.when(j == 0)
    def _():
        m_ref[...] = jnp.full(m_ref.shape, NEG_INF, F32)
        l_ref[...] = jnp.zeros(l_ref.shape, F32)
        acc_ref[...] = jnp.zeros(acc_ref.shape, F32)

    def near_sub(c, kb, qb):
        return pminq_ref[i * nqb + qb] - pmaxk_ref[(j * nc + c) * nkb + kb] < LUT_SIZE - 1

    def refine_chunk(c):
        keys = slice(c * ck, (c + 1) * ck)
        for mp in range(2):
            st_ref[mp] = _nt_dot(k_ref[keys, mp * dh:(mp + 1) * dh], q_ref[:, mp * dh:(mp + 1) * dh])
        for kb in range(nkb):
            for qb in range(nqb):
                @pl.when(near_sub(c, kb, qb))
                def _(kb=kb, qb=qb):
                    rows, cols = slice(kb * SB, (kb + 1) * SB), slice(qb * SB, (qb + 1) * SB)
                    idx = jnp.clip(pq_ref[:, cols] - pk_ref[c * ck + kb * SB:c * ck + (kb + 1) * SB, :],
                                   0, LUT_SIZE - 1)
                    for mp in range(2):
                        far_bias = lut_ref[mp:mp + 1, LUT_SIZE - 1:LUT_SIZE]
                        st_ref[mp, rows, cols] = (st_ref[mp, rows, cols]
                                                  + (_lut_bias(lut_ref[mp:mp + 1, :], idx) - far_bias))
        for mp in range(2):
            consume(c, mp, 0, "far", st_ref[mp])

    def scores(c, mp, q0, mode):
        keys = slice(c * ck, (c + 1) * ck)
        st = _nt_dot(k_ref[keys, mp * dh:(mp + 1) * dh], q_ref[q0:tq, mp * dh:(mp + 1) * dh])
        if mode != "far":
            idx = jnp.clip(pq_ref[:, q0:tq] - pk_ref[keys, :], 0, LUT_SIZE - 1)
            st = st + _lut_bias(lut_ref[mp:mp + 1, :], idx)
        if mode == "diag":
            st = jnp.where(_causal_t(c * ck, ck, q0, tq - q0), st, NEG_INF)
        return st

    def consume(c, mp, q0, mode, st):
        m_cur = offsets = None
        if mode == "far":
            bias = lut_ref[mp:mp + 1, LUT_SIZE - 1:LUT_SIZE]
            m_cur = jnp.max(st, axis=0, keepdims=True) + bias
            offsets = [(slice(None), lambda m_new: m_new - bias)]
        _online_softmax_step_t(st, vt_ref[:, c * ck:(c + 1) * ck], m_ref.at[mp], l_ref.at[mp], acc_ref.at[mp],
                               slice(q0, tq), m_cur, offsets)

    def sweep(chunks, mode):
        units = [(c, mp, c * ck if mode == "diag" else 0, mode) for c in chunks for mp in range(2)]
        _pipelined(units, scores, consume)

    @pl.when(j < i)
    def _():
        fars = [jnp.logical_not(functools.reduce(
            jnp.logical_or, [near_sub(c, kb, qb) for kb in range(nkb) for qb in range(nqb)])) for c in range(nc)]
        all_far = functools.reduce(jnp.logical_and, fars)
        pl.when(all_far)(functools.partial(sweep, range(nc), "far"))

        @pl.when(jnp.logical_not(all_far))
        def _():
            for c in range(nc):
                pl.when(fars[c])(functools.partial(sweep, [c], "far"))
                pl.when(jnp.logical_not(fars[c]))(functools.partial(refine_chunk, c))

    @pl.when(j == i)
    def _():
        sweep(range(nc), "diag")
        lam = lam_ref[...]
        lam_full = (jnp.exp(jnp.sum(lam[0:1] * lam[1:2], axis=-1, keepdims=True))
                    - jnp.exp(jnp.sum(lam[2:3] * lam[3:4], axis=-1, keepdims=True)) + lambda_init)
        a = acc_ref[0] / l_ref[0] - lam_full * (acc_ref[1] / l_ref[1])
        r = lax.rsqrt(jnp.mean(a * a, axis=0, keepdims=True) + NORM_EPS)
        y = a * r * (g_ref[...] * (1.0 - lambda_init))
        o_ref[...] = y.T.astype(o_ref.dtype)


def diff_attention(qkv, vt, pos, lut, lam, subln_g, lambda_init, *, tile=1024, chunk=512):
    S = qkv.shape[0]
    H, dh = DIFF_HEADS, DIFF_HEAD_DIM
    tq = _tile(S, tile)
    ck = _tile(tq, chunk)
    qi, kj = _pair_tables(S // tq)
    sb = _tile(ck, BIAS_BLOCK)
    pminq, pmaxk = _block_pos_bounds(pos, sb)
    kern = functools.partial(_diff_attn_kernel, tq=tq, ck=ck, sb=sb, lambda_init=lambda_init)
    grid_spec = pltpu.PrefetchScalarGridSpec(
        num_scalar_prefetch=4,
        grid=(H, qi.shape[0]),
        in_specs=[pl.BlockSpec((tq, 2 * dh), lambda h, p, qi, kj, a, b: (qi[p], h)),
                  pl.BlockSpec((tq, 2 * dh), lambda h, p, qi, kj, a, b: (kj[p], H + h)),
                  pl.BlockSpec((2 * dh, tq), lambda h, p, qi, kj, a, b: (h, kj[p])),
                  pl.BlockSpec((1, tq), lambda h, p, qi, kj, a, b: (0, qi[p])),
                  pl.BlockSpec((tq, 1), lambda h, p, qi, kj, a, b: (kj[p], 0)),
                  pl.BlockSpec((None, 2, LUT_SIZE), lambda h, p, qi, kj, a, b: (h, 0, 0)),
                  pl.BlockSpec((4, dh), lambda h, p, qi, kj, a, b: (0, 0)),
                  pl.BlockSpec((2 * dh, 1), lambda h, p, qi, kj, a, b: (0, 0))],
        out_specs=pl.BlockSpec((tq, 2 * dh), lambda h, p, qi, kj, a, b: (qi[p], h)),
        scratch_shapes=[pltpu.VMEM((2, 1, tq), F32), pltpu.VMEM((2, 1, tq), F32),
                        pltpu.VMEM((2, 2 * dh, tq), F32), pltpu.VMEM((2, ck, tq), F32)],
    )
    return pl.pallas_call(
        kern,
        out_shape=jax.ShapeDtypeStruct((S, H * 2 * dh), BF16),
        grid_spec=grid_spec,
        compiler_params=_cparams("parallel", "arbitrary"),
        name="diff_attention",
    )(qi, kj, pminq, pmaxk, qkv, qkv, vt, pos.reshape(1, S), pos.reshape(S, 1),
      lut.reshape(H, 2, LUT_SIZE), lam, subln_g.reshape(2 * dh, 1))


def _kmean_kernel(k_ref, hi_ref, lo_ref):
    km = jnp.mean(k_ref[...].astype(F32), axis=0, keepdims=True)
    hi = km.astype(BF16)
    hi_ref[...] = hi
    lo_ref[...] = (km - hi.astype(F32)).astype(BF16)


def moba_block_means(qkv, D):
    S = qkv.shape[0]
    nblk = S // MOBA_BLOCK
    hi, lo = pl.pallas_call(
        _kmean_kernel,
        out_shape=[jax.ShapeDtypeStruct((nblk, 1, D), BF16)] * 2,
        grid=(nblk,),
        in_specs=[pl.BlockSpec((MOBA_BLOCK, D), lambda n: (n, 1))],
        out_specs=[pl.BlockSpec((None, 1, D), lambda n: (n, 0, 0))] * 2,
        compiler_params=_cparams("parallel"),
        name="moba_block_means",
    )(qkv)
    return hi.reshape(nblk, D), lo.reshape(nblk, D)


def _moba_attn_kernel(qi_ref, kj_ref, pminq_ref, pmaxk_ref,
                      q_ref, k_ref, vt_ref, kmh_ref, kml_ref, pq_ref, pk_ref, lut_ref,
                      o_ref, m_ref, l_ref, acc_ref, sel_ref, st_ref, *, tq, ck, sb, hp):
    p = pl.program_id(1)
    i, j = qi_ref[p], kj_ref[p]
    nc = tq // ck
    L, dh = MOBA_BLOCK, MOBA_HEAD_DIM
    bpc = ck // L
    log2_l = L.bit_length() - 1
    SB = sb
    nqb, nkb = tq // SB, ck // SB

    def near_sub(c, kb, qb):
        return pminq_ref[i * nqb + qb] - pmaxk_ref[(j * nc + c) * nkb + kb] < LUT_SIZE - 1

    def refine_chunk(c):
        keys = slice(c * ck, (c + 1) * ck)
        for hd in range(hp):
            st_ref[hd] = _nt_dot(k_ref[keys, hd * dh:(hd + 1) * dh], q_ref[:, hd * dh:(hd + 1) * dh])
        for kb in range(nkb):
            for qb in range(nqb):
                @pl.when(near_sub(c, kb, qb))
                def _(kb=kb, qb=qb):
                    rows, cols = slice(kb * SB, (kb + 1) * SB), slice(qb * SB, (qb + 1) * SB)
                    idx = jnp.clip(pq_ref[:, cols] - pk_ref[c * ck + kb * SB:c * ck + (kb + 1) * SB, :],
                                   0, LUT_SIZE - 1)
                    for hd in range(hp):
                        far_bias = lut_ref[hd:hd + 1, LUT_SIZE - 1:LUT_SIZE]
                        st_ref[hd, rows, cols] = (st_ref[hd, rows, cols]
                                                  + (_lut_bias(lut_ref[hd:hd + 1, :], idx) - far_bias))
        for hd in range(hp):
            consume(c, hd, 0, "far", st_ref[hd])

    @pl.when(j == 0)
    def _():
        m_ref[...] = jnp.full(m_ref.shape, NEG_INF, F32)
        l_ref[...] = jnp.zeros(l_ref.shape, F32)
        acc_ref[...] = jnp.zeros(acc_ref.shape, F32)
        blk = lax.broadcasted_iota(jnp.int32, (LANES, tq), 0)
        blk_f = blk.astype(F32)
        own = lax.shift_right_logical(i * tq + lax.broadcasted_iota(jnp.int32, (1, tq), 1), log2_l)
        for hd in range(hp):
            q = q_ref[:, hd * dh:(hd + 1) * dh]
            gs = (_nt_dot(kmh_ref[:, hd * dh:(hd + 1) * dh], q)
                  + _nt_dot(kml_ref[:, hd * dh:(hd + 1) * dh], q))
            g = jnp.where(blk < own, gs, NEG_INF)
            sel = jnp.zeros((LANES, tq), F32)
            for _ in range(MOBA_TOPK):
                best = jnp.max(g, axis=0, keepdims=True)
                first = jnp.min(jnp.where(g == best, blk_f, float(LANES)), axis=0, keepdims=True)
                pick = blk_f == first
                sel = jnp.where(pick, jnp.where(best > 0.5 * NEG_INF, 1.0, sel), sel)
                g = jnp.where(pick, -3e38, g)
            sel_ref[hd] = sel

    def scores(c, hd, q0, mode):
        keys = slice(c * ck, (c + 1) * ck)
        nq = tq - q0
        st = _nt_dot(k_ref[keys, hd * dh:(hd + 1) * dh], q_ref[q0:tq, hd * dh:(hd + 1) * dh])
        if mode != "far":
            idx = jnp.clip(pq_ref[:, q0:tq] - pk_ref[keys, :], 0, LUT_SIZE - 1)
            st = st + _lut_bias(lut_ref[hd:hd + 1, :], idx)
        if mode != "diag":
            return st
        allowed = []
        for b in range(bpc):
            n_local = c * bpc + b
            picked = jnp.broadcast_to(picked_row(hd, n_local, q0), (L, nq))
            own_local = lax.shift_right_logical(q0 + lax.broadcasted_iota(jnp.int32, (L, nq), 1), log2_l)
            visible = jnp.where(_causal_t(n_local * L, L, q0, nq), 1.0, 0.0)
            allowed.append(jnp.where(own_local == n_local, visible, picked))
        allowed = allowed[0] if bpc == 1 else jnp.concatenate(allowed, axis=0)
        return jnp.where(allowed > 0.0, st, NEG_INF)

    def picked_row(hd, n_local, q0):
        return sel_ref[hd, pl.ds(j * (tq // L) + n_local, 1), q0:tq]

    def consume(c, hd, q0, mode, st):
        m_cur = offsets = None
        if mode != "diag":
            bias = lut_ref[hd:hd + 1, LUT_SIZE - 1:LUT_SIZE] if mode == "far" else 0.0
            picked = [picked_row(hd, c * bpc + b, q0) > 0.0 for b in range(bpc)]
            m_cur = functools.reduce(jnp.maximum, [
                jnp.where(picked[b], jnp.max(st[b * L:(b + 1) * L, :], axis=0, keepdims=True) + bias, NEG_INF)
                for b in range(bpc)])
            offsets = [(slice(b * L, (b + 1) * L),
                        functools.partial(lambda m_new, pk: jnp.where(pk, m_new - bias, -NEG_INF), pk=picked[b]))
                       for b in range(bpc)]
        _online_softmax_step_t(st, vt_ref[hd * dh:(hd + 1) * dh, c * ck:(c + 1) * ck],
                               m_ref.at[hd], l_ref.at[hd], acc_ref.at[hd], slice(q0, tq), m_cur, offsets)

    def sweep(chunks, mode):
        _pipelined([(c, hd, c * ck if mode == "diag" else 0, mode) for c in chunks for hd in range(hp)],
                   scores, consume)

    @pl.when(j < i)
    def _():
        fars = [jnp.logical_not(functools.reduce(
            jnp.logical_or, [near_sub(c, kb, qb) for kb in range(nkb) for qb in range(nqb)])) for c in range(nc)]
        all_far = functools.reduce(jnp.logical_and, fars)
        pl.when(all_far)(functools.partial(sweep, range(nc), "far"))

        @pl.when(jnp.logical_not(all_far))
        def _():
            for c in range(nc):
                pl.when(fars[c])(functools.partial(sweep, [c], "far"))
                pl.when(jnp.logical_not(fars[c]))(functools.partial(refine_chunk, c))

    @pl.when(j == i)
    def _():
        sweep(range(nc), "diag")
        for hd in range(hp):
            o_ref[:, hd * dh:(hd + 1) * dh] = (acc_ref[hd] / l_ref[hd]).T.astype(o_ref.dtype)


def moba_attention(qkv, vt, pos, lut, *, tile=1024, chunk=512, heads_per_step=2):
    S = qkv.shape[0]
    H, dh, L = MOBA_HEADS, MOBA_HEAD_DIM, MOBA_BLOCK
    D = H * dh
    nblk = S // L
    assert S % L == 0 and nblk <= LANES
    tq = _tile(S, tile)
    ck = _tile(tq, chunk)
    assert tq % L == 0 and ck % L == 0
    km_hi, km_lo = moba_block_means(qkv, D)
    km_hi = jnp.pad(km_hi, ((0, LANES - nblk), (0, 0)))
    km_lo = jnp.pad(km_lo, ((0, LANES - nblk), (0, 0)))
    qi, kj = _pair_tables(S // tq)
    sb = _tile(ck, BIAS_BLOCK)
    pminq, pmaxk = _block_pos_bounds(pos, sb)
    hp = heads_per_step
    G = H // hp
    kern = functools.partial(_moba_attn_kernel, tq=tq, ck=ck, sb=sb, hp=hp)
    grid_spec = pltpu.PrefetchScalarGridSpec(
        num_scalar_prefetch=4,
        grid=(G, qi.shape[0]),
        in_specs=[pl.BlockSpec((tq, hp * dh), lambda h, p, qi, kj, a, b: (qi[p], h)),
                  pl.BlockSpec((tq, hp * dh), lambda h, p, qi, kj, a, b: (kj[p], G + h)),
                  pl.BlockSpec((hp * dh, tq), lambda h, p, qi, kj, a, b: (h, kj[p])),
                  pl.BlockSpec((LANES, hp * dh), lambda h, p, qi, kj, a, b: (0, h)),
                  pl.BlockSpec((LANES, hp * dh), lambda h, p, qi, kj, a, b: (0, h)),
                  pl.BlockSpec((1, tq), lambda h, p, qi, kj, a, b: (0, qi[p])),
                  pl.BlockSpec((tq, 1), lambda h, p, qi, kj, a, b: (kj[p], 0)),
                  pl.BlockSpec((None, hp, LUT_SIZE), lambda h, p, qi, kj, a, b: (h, 0, 0))],
        out_specs=pl.BlockSpec((tq, hp * dh), lambda h, p, qi, kj, a, b: (qi[p], h)),
        scratch_shapes=[pltpu.VMEM((hp, 1, tq), F32), pltpu.VMEM((hp, 1, tq), F32),
                        pltpu.VMEM((hp, dh, tq), F32), pltpu.VMEM((hp, LANES, tq), F32),
                        pltpu.VMEM((hp, ck, tq), F32)],
    )
    return pl.pallas_call(
        kern,
        out_shape=jax.ShapeDtypeStruct((S, D), BF16),
        grid_spec=grid_spec,
        compiler_params=_cparams("parallel", "arbitrary"),
        name="moba_attention",
    )(qi, kj, pminq, pmaxk, qkv, qkv, vt, km_hi, km_lo, pos.reshape(1, S), pos.reshape(S, 1),
      lut.reshape(G, hp, LUT_SIZE))


def _rope_fold(y):
    lane = lax.broadcasted_iota(jnp.int32, y.shape, 1)
    return jnp.where(lane < MLA_ROPE, y + pltpu.roll(y, MLA_ROPE, 1), 0.0)


def _mla_q_kernel(cq_ref, g_ref, w_ref, kr_ref, t_ref, q_ref, kro_ref, cqn_ref, *, scale, hp):
    t = t_ref[...]
    W = 2 * LANES

    @pl.when(pl.program_id(1) == 0)
    def _():
        cqn_ref[...] = _rms_rows(cq_ref[...], g_ref[...]).astype(BF16)
        kro_ref[...] = _rope_fold(kr_ref[...] * t).astype(kro_ref.dtype)

    qh = jnp.dot(cqn_ref[...], w_ref[...], preferred_element_type=F32)
    for hd in range(hp):
        q_ref[:, hd * W:hd * W + MLA_NOPE] = (qh[:, hd * W:hd * W + MLA_NOPE] * scale).astype(q_ref.dtype)
        q_ref[:, hd * W + MLA_NOPE:(hd + 1) * W] = (
            _rope_fold(qh[:, hd * W + MLA_NOPE:(hd + 1) * W] * t) * scale).astype(q_ref.dtype)


def mla_queries_and_rope_key(down, g_q, w_uq_ext, rope_tab, *, scale, tm=1024, heads_per_step=4):
    S = down.shape[0]
    H, hp = MLA_HEADS, heads_per_step
    tm = _tile(S, tm)
    W = 2 * LANES
    return pl.pallas_call(
        functools.partial(_mla_q_kernel, scale=scale, hp=hp),
        out_shape=[jax.ShapeDtypeStruct((S, H * W), BF16), jax.ShapeDtypeStruct((S, LANES), BF16)],
        grid=(S // tm, H // hp),
        in_specs=[pl.BlockSpec((tm, MLA_Q_RANK), lambda i, h: (i, 0)),
                  pl.BlockSpec((1, MLA_Q_RANK), lambda i, h: (0, 0)),
                  pl.BlockSpec((MLA_Q_RANK, hp * W), lambda i, h: (0, h)),
                  pl.BlockSpec((tm, LANES), lambda i, h: (i, (MLA_Q_RANK + MLA_KV_RANK) // LANES)),
                  pl.BlockSpec((tm, LANES), lambda i, h: (i, 0))],
        out_specs=[pl.BlockSpec((tm, hp * W), lambda i, h: (i, h)),
                   pl.BlockSpec((tm, LANES), lambda i, h: (i, 0))],
        scratch_shapes=[pltpu.VMEM((tm, MLA_Q_RANK), BF16)],
        compiler_params=_cparams("parallel", "arbitrary"),
        name="mla_queries",
    )(down, g_q.reshape(1, MLA_Q_RANK), w_uq_ext, down, rope_tab)


def _mla_attn_kernel(qi_ref, kj_ref, q_ref, kv_ref, kr_ref, vt_ref, o_ref, m_ref, l_ref, acc_ref, *, tq, ck, hp):
    p = pl.program_id(1)
    i, j = qi_ref[p], kj_ref[p]
    nc = tq // ck
    W = 2 * LANES

    @pl.when(j == 0)
    def _():
        m_ref[...] = jnp.full(m_ref.shape, NEG_INF, F32)
        l_ref[...] = jnp.zeros(l_ref.shape, F32)
        acc_ref[...] = jnp.zeros(acc_ref.shape, F32)

    def scores(c, g, q0, diag):
        keys = slice(c * ck, (c + 1) * ck)
        kc = jnp.concatenate([kv_ref[keys, g * MLA_NOPE:(g + 1) * MLA_NOPE], kr_ref[keys, :]], axis=1)
        st = _nt_dot(kc, q_ref[q0:tq, g * W:(g + 1) * W])
        if diag:
            st = jnp.where(_causal_t(c * ck, ck, q0, tq - q0), st, NEG_INF)
        return st

    def consume(c, g, q0, diag, st):
        _online_softmax_step_t(st, vt_ref[g * MLA_V:(g + 1) * MLA_V, c * ck:(c + 1) * ck],
                               m_ref.at[g], l_ref.at[g], acc_ref.at[g], slice(q0, tq))

    def sweep(diag):
        _pipelined([(c, g, c * ck if diag else 0, diag) for c in range(nc) for g in range(hp)], scores, consume)

    @pl.when(j < i)
    def _():
        sweep(False)

    @pl.when(j == i)
    def _():
        sweep(True)
        for g in range(hp):
            o_ref[:, g * MLA_V:(g + 1) * MLA_V] = (acc_ref[g] / l_ref[g]).T.astype(o_ref.dtype)


def mla_attention(q, kv, kr, vt, *, tile=1024, chunk=512, heads_per_step=2):
    S = q.shape[0]
    H, hp = MLA_HEADS, heads_per_step
    tq = _tile(S, tile)
    ck = _tile(tq, chunk)
    qi, kj = _pair_tables(S // tq)
    W = 2 * LANES
    grid_spec = pltpu.PrefetchScalarGridSpec(
        num_scalar_prefetch=2,
        grid=(H // hp, qi.shape[0]),
        in_specs=[pl.BlockSpec((tq, hp * W), lambda h, p, qi, kj: (qi[p], h)),
                  pl.BlockSpec((tq, hp * MLA_NOPE), lambda h, p, qi, kj: (kj[p], h)),
                  pl.BlockSpec((tq, LANES), lambda h, p, qi, kj: (kj[p], 0)),
                  pl.BlockSpec((hp * MLA_V, tq), lambda h, p, qi, kj: (h, kj[p]))],
        out_specs=pl.BlockSpec((tq, hp * MLA_V), lambda h, p, qi, kj: (qi[p], h)),
        scratch_shapes=[pltpu.VMEM((hp, 1, tq), F32), pltpu.VMEM((hp, 1, tq), F32),
                        pltpu.VMEM((hp, MLA_V, tq), F32)],
    )
    return pl.pallas_call(
        functools.partial(_mla_attn_kernel, tq=tq, ck=ck, hp=hp),
        out_shape=jax.ShapeDtypeStruct((S, H * MLA_V), BF16),
        grid_spec=grid_spec,
        compiler_params=_cparams("parallel", "arbitrary"),
        name="mla_attention",
    )(qi, kj, q, kv, kr, vt)


_HI16 = 0xFFFF0000


def _pack_bf16_pairs(hi_f32):
    half = hi_f32.shape[1] // 2
    bits = pltpu.bitcast(hi_f32, jnp.uint32)
    return lax.shift_right_logical(bits[:, :half], jnp.uint32(16)) | (bits[:, half:] & jnp.uint32(_HI16))


def _unpack_bf16_pairs(words):
    lo = pltpu.bitcast(lax.shift_left(words, jnp.uint32(16)), F32).astype(BF16)
    hi = pltpu.bitcast(words & jnp.uint32(_HI16), F32).astype(BF16)
    return jnp.concatenate([lo, hi], axis=1)


def _router_kernel(x_ref, g_ref, wh_ref, wl_ref, xn_ref, r_ref):
    xn = _rms_rows(x_ref[...], g_ref[...])
    hi = xn.astype(BF16)
    hi_f32 = hi.astype(F32)
    lo = (xn - hi_f32).astype(BF16)
    xn_ref[...] = _pack_bf16_pairs(hi_f32)
    logits = (jnp.dot(hi, wh_ref[...], preferred_element_type=F32)
              + jnp.dot(hi, wl_ref[...], preferred_element_type=F32)
              + jnp.dot(lo, wh_ref[...], preferred_element_type=F32))
    lane = lax.broadcasted_iota(jnp.int32, logits.shape, 1)
    lane_f = lane.astype(F32)
    g = jnp.where(lane < N_EXPERTS, logits, NEG_INF)
    v1 = jnp.max(g, axis=-1, keepdims=True)
    i1 = jnp.min(jnp.where(g == v1, lane_f, float(LANES)), axis=-1, keepdims=True)
    g = jnp.where(lane_f == i1, NEG_INF, g)
    v2 = jnp.max(g, axis=-1, keepdims=True)
    i2 = jnp.min(jnp.where(g == v2, lane_f, float(LANES)), axis=-1, keepdims=True)
    g2 = 1.0 / (1.0 + jnp.exp(v1 - v2))
    g1 = 1.0 - g2
    r_ref[...] = jnp.where(lane == 0, i1,
                           jnp.where(lane == 1, i2,
                                     jnp.where(lane == 2, g1, jnp.where(lane == 3, g2, 0.0))))


def moe_route(x, g, w_router, *, tm=512):
    M, D = x.shape
    tm = _tile(M, tm)
    wr = jnp.pad(w_router.astype(F32), ((0, 0), (0, LANES - N_EXPERTS)))
    wh = wr.astype(BF16)
    wl = (wr - wh.astype(F32)).astype(BF16)
    return pl.pallas_call(
        _router_kernel,
        out_shape=[jax.ShapeDtypeStruct((M, D // 2), jnp.uint32), jax.ShapeDtypeStruct((M, LANES), F32)],
        grid=(M // tm,),
        in_specs=[pl.BlockSpec((tm, D), lambda i: (i, 0)),
                  pl.BlockSpec((1, D), lambda i: (0, 0)),
                  pl.BlockSpec((D, LANES), lambda i: (0, 0)),
                  pl.BlockSpec((D, LANES), lambda i: (0, 0))],
        out_specs=[pl.BlockSpec((tm, D // 2), lambda i: (i, 0)),
                   pl.BlockSpec((tm, LANES), lambda i: (i, 0))],
        compiler_params=_cparams("parallel"),
        name="moe_route",
    )(x, g.reshape(1, D), wh, wl)


def _new_expert(te_ref, t):
    return jnp.logical_or(t == 0, te_ref[t] != te_ref[jnp.maximum(t - 1, 0)])


def _gmm_swiglu_kernel(te_ref, nu_ref, *refs, n_parts, tiles_per_part):
    x_refs = refs[:n_parts]
    wg_ref, wu_ref, o_ref, wgb_ref, wub_ref = refs[n_parts:]
    t = pl.program_id(1)
    used = t < nu_ref[0]

    @pl.when(jnp.logical_and(used, _new_expert(te_ref, t)))
    def _():
        wgb_ref[...] = wg_ref[...].astype(BF16)
        wub_ref[...] = wu_ref[...].astype(BF16)

    for c in range(n_parts):
        @pl.when(jnp.logical_and(used, t // tiles_per_part == c))
        def _(c=c):
            x = _unpack_bf16_pairs(x_refs[c][...])
            a = jnp.dot(x, wgb_ref[...], preferred_element_type=F32)
            u = jnp.dot(x, wub_ref[...], preferred_element_type=F32)
            o_ref[...] = (a * jax.nn.sigmoid(a) * u).astype(o_ref.dtype)

    @pl.when(jnp.logical_not(used))
    def _():
        o_ref[...] = jnp.zeros(o_ref.shape, o_ref.dtype)


def _gmm_out_kernel(te_ref, nu_ref, h_ref, w_ref, o_ref, wb_ref):
    t = pl.program_id(1)
    used = t < nu_ref[0]

    @pl.when(jnp.logical_and(used, _new_expert(te_ref, t)))
    def _():
        wb_ref[...] = w_ref[...].astype(BF16)

    @pl.when(used)
    def _():
        o_ref[...] = jnp.dot(h_ref[...], wb_ref[...], preferred_element_type=F32)

    @pl.when(jnp.logical_not(used))
    def _():
        o_ref[...] = jnp.zeros(o_ref.shape, o_ref.dtype)


def moe_experts(xs, tile_expert, n_used, w_in, w_out, layer, *, tm, tn=512):
    n_parts = len(xs)
    P = n_parts * xs[0].shape[0]
    D = w_in.shape[2]
    F = w_in.shape[3] // 2
    nt = P // tm
    ntp = nt // n_parts
    assert ntp * n_parts == nt
    tn1, tn2 = _tile(F, tn), _tile(D, tn)
    nj1, nj2 = F // tn1, D // tn2

    def part_spec(c):
        return pl.BlockSpec((tm, D // 2), lambda j, t, te, nu: (jnp.clip(t - c * ntp, 0, ntp - 1), 0))

    h = pl.pallas_call(
        functools.partial(_gmm_swiglu_kernel, n_parts=n_parts, tiles_per_part=ntp),
        out_shape=jax.ShapeDtypeStruct((P, F), BF16),
        grid_spec=pltpu.PrefetchScalarGridSpec(
            num_scalar_prefetch=2,
            grid=(nj1, nt),
            in_specs=[part_spec(c) for c in range(n_parts)] + [
                pl.BlockSpec((None, None, D, tn1), lambda j, t, te, nu: (layer, te[t], 0, j)),
                pl.BlockSpec((None, None, D, tn1), lambda j, t, te, nu: (layer, te[t], 0, nj1 + j))],
            out_specs=pl.BlockSpec((tm, tn1), lambda j, t, te, nu: (t, j)),
            scratch_shapes=[pltpu.VMEM((D, tn1), BF16), pltpu.VMEM((D, tn1), BF16)],
        ),
        compiler_params=_cparams("arbitrary", "arbitrary"),
        name="moe_swiglu_in",
    )(tile_expert, n_used, *xs, w_in, w_in)
    return pl.pallas_call(
        _gmm_out_kernel,
        out_shape=jax.ShapeDtypeStruct((P, D), F32),
        grid_spec=pltpu.PrefetchScalarGridSpec(
            num_scalar_prefetch=2,
            grid=(nj2, nt),
            in_specs=[pl.BlockSpec((tm, F), lambda j, t, te, nu: (t, 0)),
                      pl.BlockSpec((None, None, F, tn2), lambda j, t, te, nu: (layer, te[t], 0, j))],
            out_specs=pl.BlockSpec((tm, tn2), lambda j, t, te, nu: (t, j)),
            scratch_shapes=[pltpu.VMEM((F, tn2), BF16)],
        ),
        compiler_params=_cparams("arbitrary", "arbitrary"),
        name="moe_out",
    )(tile_expert, n_used, h, w_out)


def moe_swiglu(x, g, w_router, w_in, w_out, layer, *, tm=512):
    S, D = x.shape
    E = N_EXPERTS
    tm = _tile(S, tm)
    xn, route = moe_route(x, g, w_router)
    experts = route[:, :2].astype(jnp.int32)
    gates = route[:, 2:4]
    flat_e = experts.reshape(-1)
    order = jnp.argsort(flat_e, stable=True).astype(jnp.int32)
    rank = jnp.argsort(order).astype(jnp.int32)
    counts = jnp.bincount(flat_e, length=E).astype(jnp.int32)
    padded = ((counts + tm - 1) // tm) * tm
    start = jnp.cumsum(counts) - counts
    pstart = jnp.cumsum(padded) - padded
    pend = jnp.cumsum(padded)
    P = 2 * S + E * tm
    nt = P // tm
    tile_expert = jnp.minimum(
        jnp.searchsorted(pend, jnp.arange(nt, dtype=jnp.int32) * tm, side="right"), E - 1).astype(jnp.int32)
    n_used = (pend[-1:] // tm).astype(jnp.int32)
    row_e = jnp.repeat(tile_expert, tm)
    row_rank = jnp.arange(P, dtype=jnp.int32) - pstart[row_e]
    row_valid = jnp.logical_and(row_rank < counts[row_e], jnp.arange(P) < pend[-1])
    row_token = jnp.where(row_valid, order[jnp.clip(start[row_e] + row_rank, 0, 2 * S - 1)] // 2, 0)
    slot = pstart[flat_e] + rank - start[flat_e]
    tile_expert = jnp.where(jnp.arange(nt) < n_used[0], tile_expert, tile_expert[jnp.maximum(n_used[0] - 1, 0)])
    slot = slot.reshape(S, 2)
    xs = [jnp.take(xn, part, axis=0, mode="clip") for part in jnp.split(row_token, MOE_GATHER_PARTS)]
    ys = moe_experts(xs, tile_expert, n_used, w_in, w_out, layer, tm=tm)
    return jnp.take(ys, slot[:, 0], axis=0, mode="clip"), jnp.take(ys, slot[:, 1], axis=0, mode="clip"), gates


def _diff_lambda_init(layer):
    return 0.8 - 0.6 * math.exp(-0.3 * layer)


def _rope_table(pos):
    half = MLA_ROPE // 2
    inv_freq = ROPE_THETA ** (-jnp.arange(half, dtype=F32) / half)
    ang = pos.astype(F32)[:, None] * inv_freq
    cos, sin = jnp.cos(ang), jnp.sin(ang)
    return jnp.concatenate([cos, cos, -sin, sin], axis=1)


def _rot_half_cols(w):
    half = w.shape[-1] // 2
    return jnp.concatenate([w[..., half:], w[..., :half]], axis=-1)


def kernel(x, p, positions, rel_bias, norm_mix, norm_ffn, norm_ple, norm_final, diff_w_qkv, diff_lambda, diff_subln, diff_w_o, moba_w_qkv, moba_w_o, mla_w_down, mla_g_q, mla_w_uq, mla_g_kv, mla_w_ukv, mla_w_o, ffn_w_in, ffn_w_out, moe_w_router, moe_w_in, moe_w_out, ple_w_gate, ple_w_proj):
    B, S, D = x.shape
    depth = p.shape[0]
    lut = _bias_lut(rel_bias)
    outs = []
    for b in range(B):
        xb = x[b]
        pos = positions[b].astype(jnp.int32)
        for i in range(depth):
            jm = i // N_MIXERS
            if i % N_MIXERS == 0:
                qk, vt = norm_matmul_vt(xb, norm_mix[i], diff_w_qkv[jm].astype(BF16), n_direct=2 * D,
                                        scaled_cols=D, scale=DIFF_HEAD_DIM ** -0.5 * LOG2E)
                o = diff_attention(qk, vt, pos, lut, diff_lambda[jm].astype(F32), diff_subln[jm],
                                   _diff_lambda_init(i))
                xb = matmul_residual(o, diff_w_o[jm].astype(BF16), xb)
            elif i % N_MIXERS == 1:
                qk, vt = norm_matmul_vt(xb, norm_mix[i], moba_w_qkv[jm].astype(BF16), n_direct=2 * D,
                                        scaled_cols=D, scale=MOBA_HEAD_DIM ** -0.5 * LOG2E)
                o = moba_attention(qk, vt, pos, lut)
                xb = matmul_residual(o, moba_w_o[jm].astype(BF16), xb)
            else:
                wd = mla_w_down[jm]
                kr0 = MLA_Q_RANK + MLA_KV_RANK
                wd_ext = jnp.concatenate([wd, _rot_half_cols(wd[:, kr0:])], axis=1).astype(BF16)
                wq = mla_w_uq[jm].reshape(MLA_Q_RANK, MLA_HEADS, MLA_NOPE + MLA_ROPE)
                wq_ext = jnp.concatenate([wq, _rot_half_cols(wq[..., MLA_NOPE:])], axis=-1)
                wq_ext = wq_ext.reshape(MLA_Q_RANK, MLA_HEADS * 2 * LANES).astype(BF16)
                down = norm_matmul(xb, norm_mix[i], wd_ext, out_dtype=F32, tn=wd_ext.shape[1])
                q, kr = mla_queries_and_rope_key(down, mla_g_q[jm], wq_ext, _rope_table(pos),
                                                 scale=(MLA_NOPE + MLA_ROPE) ** -0.5 * LOG2E)
                wkv = mla_w_ukv[jm].reshape(MLA_KV_RANK, MLA_HEADS, 2, MLA_V).transpose(0, 2, 1, 3)
                wkv = wkv.reshape(MLA_KV_RANK, 2 * MLA_HEADS * MLA_V).astype(BF16)
                kn, vt = norm_matmul_vt(down, mla_g_kv[jm], wkv, n_direct=MLA_HEADS * MLA_NOPE, xcol=1)
                o = mla_attention(q, kn, kr, vt)
                xb = matmul_residual(o, mla_w_o[jm].astype(BF16), xb)
            w_gate, w_proj = ple_w_gate[i].astype(BF16), ple_w_proj[i].astype(BF16)
            if i % 2 == 0:
                h = norm_swiglu_in(xb, norm_ffn[i], ffn_w_in[i // 2].astype(BF16))
                xb = matmul_residual(h, ffn_w_out[i // 2].astype(BF16), xb)
                xb = ple_update(xb, norm_ple[i], p[i, b], w_gate, w_proj)
            else:
                y1, y2, gates = moe_swiglu(xb, norm_ffn[i], moe_w_router[i // 2], moe_w_in, moe_w_out, i // 2)
                xb = ple_update_moe(xb, y1, y2, gates, norm_ple[i], p[i, b], w_gate, w_proj)
        outs.append(final_norm(xb, norm_final))
    return jnp.stack(outs, axis=0)
```

```python
import functools
import math

import numpy as np
import jax
import jax.numpy as jnp
from jax import lax
from jax.experimental import pallas as pl
from jax.experimental.pallas import tpu as pltpu

F32 = jnp.float32
BF16 = jnp.bfloat16

NORM_EPS = 1e-6
NEG_INF = -1e30
LOG2E = math.log2(math.e)
LANES = 128
VMEM_LIMIT_BYTES = 56 * 1024 * 1024

REL_BUCKETS = 32
REL_MAX_DIST = 128
LUT_SIZE = LANES
BIAS_BLOCK = 256
DIFF_HEADS = 8
DIFF_HEAD_DIM = 128
MOBA_HEADS = 16
MOBA_HEAD_DIM = 128
MOBA_BLOCK = 256
MOBA_TOPK = 3
MLA_HEADS = 16
MLA_Q_RANK = 512
MLA_KV_RANK = 512
MLA_NOPE = 128
MLA_ROPE = 64
MLA_V = 128
ROPE_THETA = 10000.0
N_EXPERTS = 8
N_MIXERS = 3
MOE_GATHER_PARTS = 4


def _cparams(*sem):
    return pltpu.CompilerParams(dimension_semantics=sem, vmem_limit_bytes=VMEM_LIMIT_BYTES)


def _tile(n, pref):
    if n <= pref:
        return n
    t = pref
    while n % t:
        t //= 2
    return t


def _rms_rows(x, g):
    r = lax.rsqrt(jnp.mean(x * x, axis=-1, keepdims=True) + NORM_EPS)
    return x * r * g


def _norm_mm_kernel(x_ref, g_ref, w_ref, o_ref, xn_ref, *, scaled_tiles, scale):
    j = pl.program_id(1)

    @pl.when(j == 0)
    def _():
        xn_ref[...] = _rms_rows(x_ref[...], g_ref[...]).astype(BF16)

    acc = jnp.dot(xn_ref[...], w_ref[...], preferred_element_type=F32)
    if scaled_tiles:
        acc = acc * jnp.where(j < scaled_tiles, scale, 1.0)
    o_ref[...] = acc.astype(o_ref.dtype)


def norm_matmul(x, g, w, *, out_dtype, xcol=0, tm=1024, tn=512, scaled_cols=0, scale=1.0):
    M = x.shape[0]
    K, N = w.shape
    tm, tn = _tile(M, tm), _tile(N, tn)
    assert scaled_cols % tn == 0
    kern = functools.partial(_norm_mm_kernel, scaled_tiles=scaled_cols // tn, scale=scale)
    return pl.pallas_call(
        kern,
        out_shape=jax.ShapeDtypeStruct((M, N), out_dtype),
        grid=(M // tm, N // tn),
        in_specs=[pl.BlockSpec((tm, K), lambda i, j: (i, xcol)),
                  pl.BlockSpec((1, K), lambda i, j: (0, 0)),
                  pl.BlockSpec((K, tn), lambda i, j: (0, j))],
        out_specs=pl.BlockSpec((tm, tn), lambda i, j: (i, j)),
        scratch_shapes=[pltpu.VMEM((tm, K), BF16)],
        compiler_params=_cparams("parallel", "arbitrary"),
        name="norm_matmul",
    )(x, g.reshape(1, K), w)


def _norm_mm_vt_kernel(x_ref, g_ref, w_ref, o_ref, ot_ref, xn_ref, *, direct_tiles, scaled_tiles, scale):
    j = pl.program_id(1)

    @pl.when(j == 0)
    def _():
        xn_ref[...] = _rms_rows(x_ref[...], g_ref[...]).astype(BF16)

    acc = jnp.dot(xn_ref[...], w_ref[...], preferred_element_type=F32)

    @pl.when(j < direct_tiles)
    def _():
        out = acc * jnp.where(j < scaled_tiles, scale, 1.0) if scaled_tiles else acc
        o_ref[...] = out.astype(o_ref.dtype)

    @pl.when(j >= direct_tiles)
    def _():
        ot_ref[...] = acc.T.astype(ot_ref.dtype)


def norm_matmul_vt(x, g, w, *, n_direct, xcol=0, tm=1024, tn=512, scaled_cols=0, scale=1.0):
    M = x.shape[0]
    K, N = w.shape
    tm, tn = _tile(M, tm), _tile(N, tn)
    assert scaled_cols % tn == 0 and n_direct % tn == 0 and 0 < n_direct < N
    nd = n_direct // tn
    kern = functools.partial(_norm_mm_vt_kernel, direct_tiles=nd, scaled_tiles=scaled_cols // tn, scale=scale)
    return pl.pallas_call(
        kern,
        out_shape=[jax.ShapeDtypeStruct((M, n_direct), BF16), jax.ShapeDtypeStruct((N - n_direct, M), BF16)],
        grid=(M // tm, N // tn),
        in_specs=[pl.BlockSpec((tm, K), lambda i, j: (i, xcol)),
                  pl.BlockSpec((1, K), lambda i, j: (0, 0)),
                  pl.BlockSpec((K, tn), lambda i, j: (0, j))],
        out_specs=[pl.BlockSpec((tm, tn), lambda i, j: (i, jnp.minimum(j, nd - 1))),
                   pl.BlockSpec((tn, tm), lambda i, j: (jnp.maximum(j - nd, 0), i))],
        scratch_shapes=[pltpu.VMEM((tm, K), BF16)],
        compiler_params=_cparams("parallel", "arbitrary"),
        name="norm_matmul_vt",
    )(x, g.reshape(1, K), w)


def _norm_swiglu_kernel(x_ref, g_ref, wg_ref, wu_ref, o_ref, xn_ref):
    @pl.when(pl.program_id(1) == 0)
    def _():
        xn_ref[...] = _rms_rows(x_ref[...], g_ref[...]).astype(BF16)

    xn = xn_ref[...]
    a = jnp.dot(xn, wg_ref[...], preferred_element_type=F32)
    u = jnp.dot(xn, wu_ref[...], preferred_element_type=F32)
    o_ref[...] = (a * jax.nn.sigmoid(a) * u).astype(o_ref.dtype)


def norm_swiglu_in(x, g, w_in, *, tm=512, tn=512):
    M, K = x.shape
    F = w_in.shape[1] // 2
    tm, tn = _tile(M, tm), _tile(F, tn)
    nj = F // tn
    return pl.pallas_call(
        _norm_swiglu_kernel,
        out_shape=jax.ShapeDtypeStruct((M, F), BF16),
        grid=(M // tm, nj),
        in_specs=[pl.BlockSpec((tm, K), lambda i, j: (i, 0)),
                  pl.BlockSpec((1, K), lambda i, j: (0, 0)),
                  pl.BlockSpec((K, tn), lambda i, j: (0, j)),
                  pl.BlockSpec((K, tn), lambda i, j: (0, j + nj))],
        out_specs=pl.BlockSpec((tm, tn), lambda i, j: (i, j)),
        scratch_shapes=[pltpu.VMEM((tm, K), BF16)],
        compiler_params=_cparams("parallel", "arbitrary"),
        name="norm_swiglu_in",
    )(x, g.reshape(1, K), w_in, w_in)


def _mm_res_kernel(a_ref, w_ref, r_ref, o_ref):
    o_ref[...] = r_ref[...] + jnp.dot(a_ref[...], w_ref[...], preferred_element_type=F32)


def matmul_residual(a, w, res, *, tm=1024, tn=512):
    M, K = a.shape
    N = w.shape[1]
    tm, tn = _tile(M, tm), _tile(N, tn)
    return pl.pallas_call(
        _mm_res_kernel,
        out_shape=jax.ShapeDtypeStruct((M, N), F32),
        grid=(M // tm, N // tn),
        in_specs=[pl.BlockSpec((tm, K), lambda i, j: (i, 0)),
                  pl.BlockSpec((K, tn), lambda i, j: (0, j)),
                  pl.BlockSpec((tm, tn), lambda i, j: (i, j))],
        out_specs=pl.BlockSpec((tm, tn), lambda i, j: (i, j)),
        compiler_params=_cparams("parallel", "parallel"),
        name="matmul_residual",
    )(a, w, res)


def _ple_kernel(x_ref, g_ref, xr_ref, p_ref, wg_ref, wp_ref, o_ref, xn_ref):
    @pl.when(pl.program_id(1) == 0)
    def _():
        xn_ref[...] = _rms_rows(x_ref[...], g_ref[...]).astype(BF16)

    gate = jax.nn.sigmoid(jnp.dot(xn_ref[...], wg_ref[...], preferred_element_type=F32))
    proj = jnp.dot(p_ref[...].astype(BF16), wp_ref[...], preferred_element_type=F32)
    o_ref[...] = xr_ref[...] + gate * proj


def ple_update(x, g, p, w_gate, w_proj, *, tm=1024, tn=512):
    M, D = x.shape
    P = p.shape[1]
    tm, tn = _tile(M, tm), _tile(D, tn)
    return pl.pallas_call(
        _ple_kernel,
        out_shape=jax.ShapeDtypeStruct((M, D), F32),
        grid=(M // tm, D // tn),
        in_specs=[pl.BlockSpec((tm, D), lambda i, j: (i, 0)),
                  pl.BlockSpec((1, D), lambda i, j: (0, 0)),
                  pl.BlockSpec((tm, tn), lambda i, j: (i, j)),
                  pl.BlockSpec((tm, P), lambda i, j: (i, 0)),
                  pl.BlockSpec((D, tn), lambda i, j: (0, j)),
                  pl.BlockSpec((P, tn), lambda i, j: (0, j))],
        out_specs=pl.BlockSpec((tm, tn), lambda i, j: (i, j)),
        scratch_shapes=[pltpu.VMEM((tm, D), BF16)],
        compiler_params=_cparams("parallel", "arbitrary"),
        name="ple_update",
    )(x, g.reshape(1, D), x, p, w_gate, w_proj)


def _ple_moe_kernel(x_ref, y1_ref, y2_ref, gt_ref, g_ref, p_ref, wg_ref, wp_ref, o_ref, xs_ref, xn_ref, *, tn):
    j = pl.program_id(1)

    @pl.when(j == 0)
    def _():
        gt = gt_ref[...]
        xnew = x_ref[...] + gt[:, 0:1] * y1_ref[...] + gt[:, 1:2] * y2_ref[...]
        xn_ref[...] = _rms_rows(xnew, g_ref[...]).astype(BF16)
        for jj in range(xs_ref.shape[0]):
            xs_ref[jj] = xnew[:, jj * tn:(jj + 1) * tn]

    gate = jax.nn.sigmoid(jnp.dot(xn_ref[...], wg_ref[...], preferred_element_type=F32))
    proj = jnp.dot(p_ref[...].astype(BF16), wp_ref[...], preferred_element_type=F32)
    o_ref[...] = xs_ref[j] + gate * proj


def ple_update_moe(x, y1, y2, gates, g, p, w_gate, w_proj, *, tm=512, tn=512):
    M, D = x.shape
    P = p.shape[1]
    tm, tn = _tile(M, tm), _tile(D, tn)
    row = pl.BlockSpec((tm, D), lambda i, j: (i, 0))
    return pl.pallas_call(
        functools.partial(_ple_moe_kernel, tn=tn),
        out_shape=jax.ShapeDtypeStruct((M, D), F32),
        grid=(M // tm, D // tn),
        in_specs=[row, row, row,
                  pl.BlockSpec((tm, 2), lambda i, j: (i, 0)),
                  pl.BlockSpec((1, D), lambda i, j: (0, 0)),
                  pl.BlockSpec((tm, P), lambda i, j: (i, 0)),
                  pl.BlockSpec((D, tn), lambda i, j: (0, j)),
                  pl.BlockSpec((P, tn), lambda i, j: (0, j))],
        out_specs=pl.BlockSpec((tm, tn), lambda i, j: (i, j)),
        scratch_shapes=[pltpu.VMEM((D // tn, tm, tn), F32), pltpu.VMEM((tm, D), BF16)],
        compiler_params=_cparams("parallel", "arbitrary"),
        name="ple_update_moe",
    )(x, y1, y2, gates, g.reshape(1, D), p, w_gate, w_proj)


def _final_norm_kernel(x_ref, g_ref, o_ref):
    o_ref[...] = _rms_rows(x_ref[...], g_ref[...])


def final_norm(x, g, *, tm=512):
    M, D = x.shape
    tm = _tile(M, tm)
    return pl.pallas_call(
        _final_norm_kernel,
        out_shape=jax.ShapeDtypeStruct((M, D), F32),
        grid=(M // tm,),
        in_specs=[pl.BlockSpec((tm, D), lambda i: (i, 0)),
                  pl.BlockSpec((1, D), lambda i: (0, 0))],
        out_specs=pl.BlockSpec((tm, D), lambda i: (i, 0)),
        compiler_params=_cparams("parallel"),
        name="final_norm",
    )(x, g.reshape(1, D))


def _pair_tables(n_tiles):
    qi, kj = [], []
    for i in range(n_tiles):
        for j in range(i + 1):
            qi.append(i)
            kj.append(j)
    return jnp.asarray(np.array(qi, np.int32)), jnp.asarray(np.array(kj, np.int32))


def _rel_bucket(dist):
    n = jnp.maximum(dist, 0)
    max_exact = REL_BUCKETS // 2
    nf = jnp.maximum(n, 1).astype(F32)
    large = max_exact + (jnp.log(nf / max_exact) / math.log(REL_MAX_DIST / max_exact)
                         * (REL_BUCKETS - max_exact)).astype(jnp.int32)
    large = jnp.minimum(large, REL_BUCKETS - 1)
    return jnp.where(n < max_exact, n, large)


def _bias_lut(rel_bias):
    buckets = _rel_bucket(jnp.arange(LUT_SIZE, dtype=jnp.int32))
    return rel_bias.astype(F32)[buckets].T * LOG2E


def _block_pos_bounds(pos, sb):
    blocks = pos.reshape(pos.shape[0] // sb, sb)
    return blocks.min(axis=1), blocks.max(axis=1)


def _lut_bias(lut_row, idx):
    R, C = idx.shape
    lut = jnp.broadcast_to(lut_row, (R, LANES))
    parts = [jnp.take_along_axis(lut, idx[:, c:c + LANES], axis=1, mode="promise_in_bounds")
             for c in range(0, C, LANES)]
    return parts[0] if len(parts) == 1 else jnp.concatenate(parts, axis=1)


def _pipelined(units, scores, consume):
    st = scores(*units[0])
    for k, u in enumerate(units):
        nxt = scores(*units[k + 1]) if k + 1 < len(units) else None
        consume(*u, st)
        st = nxt


def _online_softmax_step_t(st, vt, m_ref, l_ref, acc_ref, qcols, m_cur=None, offsets=None):
    m_prev = m_ref[:, qcols]
    if m_cur is None:
        m_cur = jnp.max(st, axis=0, keepdims=True)
    m_new = jnp.maximum(m_prev, m_cur)
    alpha = jnp.exp2(m_prev - m_new)
    if offsets is None:
        p = jnp.exp2(st - m_new)
    else:
        parts = [jnp.exp2(st[rows, :] - fn(m_new)) for rows, fn in offsets]
        p = parts[0] if len(parts) == 1 else jnp.concatenate(parts, axis=0)
    l_ref[:, qcols] = alpha * l_ref[:, qcols] + jnp.sum(p, axis=0, keepdims=True)
    acc_ref[:, qcols] = alpha * acc_ref[:, qcols] + jnp.dot(vt, p.astype(BF16), preferred_element_type=F32)
    m_ref[:, qcols] = m_new


def _causal_t(k0, nk, q0, nq):
    r = k0 + lax.broadcasted_iota(jnp.int32, (nk, nq), 0)
    c = q0 + lax.broadcasted_iota(jnp.int32, (nk, nq), 1)
    return r <= c


def _nt_dot(a, b):
    return lax.dot_general(a, b, (((1,), (1,)), ((), ())), preferred_element_type=F32)


def _diff_attn_kernel(qi_ref, kj_ref, pminq_ref, pmaxk_ref,
                      q_ref, k_ref, vt_ref, pq_ref, pk_ref, lut_ref, lam_ref, g_ref,
                      o_ref, m_ref, l_ref, acc_ref, st_ref, *, tq, ck, sb, lambda_init):
    p = pl.program_id(1)
    i, j = qi_ref[p], kj_ref[p]
    nc = tq // ck
    dh = DIFF_HEAD_DIM
    SB = sb
    nqb, nkb = tq // SB, ck // SB

    @pl.when(j == 0)
    def _():
        m_ref[...] = jnp.full(m_ref.shape, NEG_INF, F32)
        l_ref[...] = jnp.zeros(l_ref.shape, F32)
        acc_ref[...] = jnp.zeros(acc_ref.shape, F32)

    def near_sub(c, kb, qb):
        return pminq_ref[i * nqb + qb] - pmaxk_ref[(j * nc + c) * nkb + kb] < LUT_SIZE - 1

    def refine_chunk(c):
        keys = slice(c * ck, (c + 1) * ck)
        for mp in range(2):
            st_ref[mp] = _nt_dot(k_ref[keys, mp * dh:(mp + 1) * dh], q_ref[:, mp * dh:(mp + 1) * dh])
        for kb in range(nkb):
            for qb in range(nqb):
                @pl.when(near_sub(c, kb, qb))
                def _(kb=kb, qb=qb):
                    rows, cols = slice(kb * SB, (kb + 1) * SB), slice(qb * SB, (qb + 1) * SB)
                    idx = jnp.clip(pq_ref[:, cols] - pk_ref[c * ck + kb * SB:c * ck + (kb + 1) * SB, :],
                                   0, LUT_SIZE - 1)
                    for mp in range(2):
                        far_bias = lut_ref[mp:mp + 1, LUT_SIZE - 1:LUT_SIZE]
                        st_ref[mp, rows, cols] = (st_ref[mp, rows, cols]
                                                  + (_lut_bias(lut_ref[mp:mp + 1, :], idx) - far_bias))
        for mp in range(2):
            consume(c, mp, 0, "far", st_ref[mp])

    def scores(c, mp, q0, mode):
        keys = slice(c * ck, (c + 1) * ck)
        st = _nt_dot(k_ref[keys, mp * dh:(mp + 1) * dh], q_ref[q0:tq, mp * dh:(mp + 1) * dh])
        if mode != "far":
            idx = jnp.clip(pq_ref[:, q0:tq] - pk_ref[keys, :], 0, LUT_SIZE - 1)
            st = st + _lut_bias(lut_ref[mp:mp + 1, :], idx)
        if mode == "diag":
            st = jnp.where(_causal_t(c * ck, ck, q0, tq - q0), st, NEG_INF)
        return st

    def consume(c, mp, q0, mode, st):
        m_cur = offsets = None
        if mode == "far":
            bias = lut_ref[mp:mp + 1, LUT_SIZE - 1:LUT_SIZE]
            m_cur = jnp.max(st, axis=0, keepdims=True) + bias
            offsets = [(slice(None), lambda m_new: m_new - bias)]
        _online_softmax_step_t(st, vt_ref[:, c * ck:(c + 1) * ck], m_ref.at[mp], l_ref.at[mp], acc_ref.at[mp],
                               slice(q0, tq), m_cur, offsets)

    def sweep(chunks, mode):
        units = [(c, mp, c * ck if mode == "diag" else 0, mode) for c in chunks for mp in range(2)]
        _pipelined(units, scores, consume)

    @pl.when(j < i)
    def _():
        fars = [jnp.logical_not(functools.reduce(
            jnp.logical_or, [near_sub(c, kb, qb) for kb in range(nkb) for qb in range(nqb)])) for c in range(nc)]
        all_far = functools.reduce(jnp.logical_and, fars)
        pl.when(all_far)(functools.partial(sweep, range(nc), "far"))

        @pl.when(jnp.logical_not(all_far))
        def _():
            for c in range(nc):
                pl.when(fars[c])(functools.partial(sweep, [c], "far"))
                pl.when(jnp.logical_not(fars[c]))(functools.partial(refine_chunk, c))

    @pl.when(j == i)
    def _():
        sweep(range(nc), "diag")
        lam = lam_ref[...]
        lam_full = (jnp.exp(jnp.sum(lam[0:1] * lam[1:2], axis=-1, keepdims=True))
                    - jnp.exp(jnp.sum(lam[2:3] * lam[3:4], axis=-1, keepdims=True)) + lambda_init)
        a = acc_ref[0] / l_ref[0] - lam_full * (acc_ref[1] / l_ref[1])
        r = lax.rsqrt(jnp.mean(a * a, axis=0, keepdims=True) + NORM_EPS)
        y = a * r * (g_ref[...] * (1.0 - lambda_init))
        o_ref[...] = y.T.astype(o_ref.dtype)


def diff_attention(qkv, vt, pos, lut, lam, subln_g, lambda_init, *, tile=1024, chunk=512):
    S = qkv.shape[0]
    H, dh = DIFF_HEADS, DIFF_HEAD_DIM
    tq = _tile(S, tile)
    ck = _tile(tq, chunk)
    qi, kj = _pair_tables(S // tq)
    sb = _tile(ck, BIAS_BLOCK)
    pminq, pmaxk = _block_pos_bounds(pos, sb)
    kern = functools.partial(_diff_attn_kernel, tq=tq, ck=ck, sb=sb, lambda_init=lambda_init)
    grid_spec = pltpu.PrefetchScalarGridSpec(
        num_scalar_prefetch=4,
        grid=(H, qi.shape[0]),
        in_specs=[pl.BlockSpec((tq, 2 * dh), lambda h, p, qi, kj, a, b: (qi[p], h)),
                  pl.BlockSpec((tq, 2 * dh), lambda h, p, qi, kj, a, b: (kj[p], H + h)),
                  pl.BlockSpec((2 * dh, tq), lambda h, p, qi, kj, a, b: (h, kj[p])),
                  pl.BlockSpec((1, tq), lambda h, p, qi, kj, a, b: (0, qi[p])),
                  pl.BlockSpec((tq, 1), lambda h, p, qi, kj, a, b: (kj[p], 0)),
                  pl.BlockSpec((None, 2, LUT_SIZE), lambda h, p, qi, kj, a, b: (h, 0, 0)),
                  pl.BlockSpec((4, dh), lambda h, p, qi, kj, a, b: (0, 0)),
                  pl.BlockSpec((2 * dh, 1), lambda h, p, qi, kj, a, b: (0, 0))],
        out_specs=pl.BlockSpec((tq, 2 * dh), lambda h, p, qi, kj, a, b: (qi[p], h)),
        scratch_shapes=[pltpu.VMEM((2, 1, tq), F32), pltpu.VMEM((2, 1, tq), F32),
                        pltpu.VMEM((2, 2 * dh, tq), F32), pltpu.VMEM((2, ck, tq), F32)],
    )
    return pl.pallas_call(
        kern,
        out_shape=jax.ShapeDtypeStruct((S, H * 2 * dh), BF16),
        grid_spec=grid_spec,
        compiler_params=_cparams("parallel", "arbitrary"),
        name="diff_attention",
    )(qi, kj, pminq, pmaxk, qkv, qkv, vt, pos.reshape(1, S), pos.reshape(S, 1),
      lut.reshape(H, 2, LUT_SIZE), lam, subln_g.reshape(2 * dh, 1))


def _kmean_kernel(k_ref, hi_ref, lo_ref):
    km = jnp.mean(k_ref[...].astype(F32), axis=0, keepdims=True)
    hi = km.astype(BF16)
    hi_ref[...] = hi
    lo_ref[...] = (km - hi.astype(F32)).astype(BF16)


def moba_block_means(qkv, D):
    S = qkv.shape[0]
    nblk = S // MOBA_BLOCK
    hi, lo = pl.pallas_call(
        _kmean_kernel,
        out_shape=[jax.ShapeDtypeStruct((nblk, 1, D), BF16)] * 2,
        grid=(nblk,),
        in_specs=[pl.BlockSpec((MOBA_BLOCK, D), lambda n: (n, 1))],
        out_specs=[pl.BlockSpec((None, 1, D), lambda n: (n, 0, 0))] * 2,
        compiler_params=_cparams("parallel"),
        name="moba_block_means",
    )(qkv)
    return hi.reshape(nblk, D), lo.reshape(nblk, D)


def _moba_attn_kernel(qi_ref, kj_ref, pminq_ref, pmaxk_ref,
                      q_ref, k_ref, vt_ref, kmh_ref, kml_ref, pq_ref, pk_ref, lut_ref,
                      o_ref, m_ref, l_ref, acc_ref, sel_ref, st_ref, *, tq, ck, sb, hp):
    p = pl.program_id(1)
    i, j = qi_ref[p], kj_ref[p]
    nc = tq // ck
    L, dh = MOBA_BLOCK, MOBA_HEAD_DIM
    bpc = ck // L
    log2_l = L.bit_length() - 1
    SB = sb
    nqb, nkb = tq // SB, ck // SB

    def near_sub(c, kb, qb):
        return pminq_ref[i * nqb + qb] - pmaxk_ref[(j * nc + c) * nkb + kb] < LUT_SIZE - 1

    def refine_chunk(c):
        keys = slice(c * ck, (c + 1) * ck)
        for hd in range(hp):
            st_ref[hd] = _nt_dot(k_ref[keys, hd * dh:(hd + 1) * dh], q_ref[:, hd * dh:(hd + 1) * dh])
        for kb in range(nkb):
            for qb in range(nqb):
                @pl.when(near_sub(c, kb, qb))
                def _(kb=kb, qb=qb):
                    rows, cols = slice(kb * SB, (kb + 1) * SB), slice(qb * SB, (qb + 1) * SB)
                    idx = jnp.clip(pq_ref[:, cols] - pk_ref[c * ck + kb * SB:c * ck + (kb + 1) * SB, :],
                                   0, LUT_SIZE - 1)
                    for hd in range(hp):
                        far_bias = lut_ref[hd:hd + 1, LUT_SIZE - 1:LUT_SIZE]
                        st_ref[hd, rows, cols] = (st_ref[hd, rows, cols]
                                                  + (_lut_bias(lut_ref[hd:hd + 1, :], idx) - far_bias))
        for hd in range(hp):
            consume(c, hd, 0, "far", st_ref[hd])

    @pl.when(j == 0)
    def _():
        m_ref[...] = jnp.full(m_ref.shape, NEG_INF, F32)
        l_ref[...] = jnp.zeros(l_ref.shape, F32)
        acc_ref[...] = jnp.zeros(acc_ref.shape, F32)
        blk = lax.broadcasted_iota(jnp.int32, (LANES, tq), 0)
        blk_f = blk.astype(F32)
        own = lax.shift_right_logical(i * tq + lax.broadcasted_iota(jnp.int32, (1, tq), 1), log2_l)
        for hd in range(hp):
            q = q_ref[:, hd * dh:(hd + 1) * dh]
            gs = (_nt_dot(kmh_ref[:, hd * dh:(hd + 1) * dh], q)
                  + _nt_dot(kml_ref[:, hd * dh:(hd + 1) * dh], q))
            g = jnp.where(blk < own, gs, NEG_INF)
            sel = jnp.zeros((LANES, tq), F32)
            for _ in range(MOBA_TOPK):
                best = jnp.max(g, axis=0, keepdims=True)
                first = jnp.min(jnp.where(g == best, blk_f, float(LANES)), axis=0, keepdims=True)
                pick = blk_f == first
                sel = jnp.where(pick, jnp.where(best > 0.5 * NEG_INF, 1.0, sel), sel)
                g = jnp.where(pick, -3e38, g)
            sel_ref[hd] = sel

    def scores(c, hd, q0, mode):
        keys = slice(c * ck, (c + 1) * ck)
        nq = tq - q0
        st = _nt_dot(k_ref[keys, hd * dh:(hd + 1) * dh], q_ref[q0:tq, hd * dh:(hd + 1) * dh])
        if mode != "far":
            idx = jnp.clip(pq_ref[:, q0:tq] - pk_ref[keys, :], 0, LUT_SIZE - 1)
            st = st + _lut_bias(lut_ref[hd:hd + 1, :], idx)
        if mode != "diag":
            return st
        allowed = []
        for b in range(bpc):
            n_local = c * bpc + b
            picked = jnp.broadcast_to(picked_row(hd, n_local, q0), (L, nq))
            own_local = lax.shift_right_logical(q0 + lax.broadcasted_iota(jnp.int32, (L, nq), 1), log2_l)
            visible = jnp.where(_causal_t(n_local * L, L, q0, nq), 1.0, 0.0)
            allowed.append(jnp.where(own_local == n_local, visible, picked))
        allowed = allowed[0] if bpc == 1 else jnp.concatenate(allowed, axis=0)
        return jnp.where(allowed > 0.0, st, NEG_INF)

    def picked_row(hd, n_local, q0):
        return sel_ref[hd, pl.ds(j * (tq // L) + n_local, 1), q0:tq]

    def consume(c, hd, q0, mode, st):
        m_cur = offsets = None
        if mode != "diag":
            bias = lut_ref[hd:hd + 1, LUT_SIZE - 1:LUT_SIZE] if mode == "far" else 0.0
            picked = [picked_row(hd, c * bpc + b, q0) > 0.0 for b in range(bpc)]
            m_cur = functools.reduce(jnp.maximum, [
                jnp.where(picked[b], jnp.max(st[b * L:(b + 1) * L, :], axis=0, keepdims=True) + bias, NEG_INF)
                for b in range(bpc)])
            offsets = [(slice(b * L, (b + 1) * L),
                        functools.partial(lambda m_new, pk: jnp.where(pk, m_new - bias, -NEG_INF), pk=picked[b]))
                       for b in range(bpc)]
        _online_softmax_step_t(st, vt_ref[hd * dh:(hd + 1) * dh, c * ck:(c + 1) * ck],
                               m_ref.at[hd], l_ref.at[hd], acc_ref.at[hd], slice(q0, tq), m_cur, offsets)

    def sweep(chunks, mode):
        _pipelined([(c, hd, c * ck if mode == "diag" else 0, mode) for c in chunks for hd in range(hp)],
                   scores, consume)

    @pl.when(j < i)
    def _():
        fars = [jnp.logical_not(functools.reduce(
            jnp.logical_or, [near_sub(c, kb, qb) for kb in range(nkb) for qb in range(nqb)])) for c in range(nc)]
        all_far = functools.reduce(jnp.logical_and, fars)
        pl.when(all_far)(functools.partial(sweep, range(nc), "far"))

        @pl.when(jnp.logical_not(all_far))
        def _():
            for c in range(nc):
                pl.when(fars[c])(functools.partial(sweep, [c], "far"))
                pl.when(jnp.logical_not(fars[c]))(functools.partial(refine_chunk, c))

    @pl.when(j == i)
    def _():
        sweep(range(nc), "diag")
        for hd in range(hp):
            o_ref[:, hd * dh:(hd + 1) * dh] = (acc_ref[hd] / l_ref[hd]).T.astype(o_ref.dtype)


def moba_attention(qkv, vt, pos, lut, *, tile=1024, chunk=512, heads_per_step=2):
    S = qkv.shape[0]
    H, dh, L = MOBA_HEADS, MOBA_HEAD_DIM, MOBA_BLOCK
    D = H * dh
    nblk = S // L
    assert S % L == 0 and nblk <= LANES
    tq = _tile(S, tile)
    ck = _tile(tq, chunk)
    assert tq % L == 0 and ck % L == 0
    km_hi, km_lo = moba_block_means(qkv, D)
    km_hi = jnp.pad(km_hi, ((0, LANES - nblk), (0, 0)))
    km_lo = jnp.pad(km_lo, ((0, LANES - nblk), (0, 0)))
    qi, kj = _pair_tables(S // tq)
    sb = _tile(ck, BIAS_BLOCK)
    pminq, pmaxk = _block_pos_bounds(pos, sb)
    hp = heads_per_step
    G = H // hp
    kern = functools.partial(_moba_attn_kernel, tq=tq, ck=ck, sb=sb, hp=hp)
    grid_spec = pltpu.PrefetchScalarGridSpec(
        num_scalar_prefetch=4,
        grid=(G, qi.shape[0]),
        in_specs=[pl.BlockSpec((tq, hp * dh), lambda h, p, qi, kj, a, b: (qi[p], h)),
                  pl.BlockSpec((tq, hp * dh), lambda h, p, qi, kj, a, b: (kj[p], G + h)),
                  pl.BlockSpec((hp * dh, tq), lambda h, p, qi, kj, a, b: (h, kj[p])),
                  pl.BlockSpec((LANES, hp * dh), lambda h, p, qi, kj, a, b: (0, h)),
                  pl.BlockSpec((LANES, hp * dh), lambda h, p, qi, kj, a, b: (0, h)),
                  pl.BlockSpec((1, tq), lambda h, p, qi, kj, a, b: (0, qi[p])),
                  pl.BlockSpec((tq, 1), lambda h, p, qi, kj, a, b: (kj[p], 0)),
                  pl.BlockSpec((None, hp, LUT_SIZE), lambda h, p, qi, kj, a, b: (h, 0, 0))],
        out_specs=pl.BlockSpec((tq, hp * dh), lambda h, p, qi, kj, a, b: (qi[p], h)),
        scratch_shapes=[pltpu.VMEM((hp, 1, tq), F32), pltpu.VMEM((hp, 1, tq), F32),
                        pltpu.VMEM((hp, dh, tq), F32), pltpu.VMEM((hp, LANES, tq), F32),
                        pltpu.VMEM((hp, ck, tq), F32)],
    )
    return pl.pallas_call(
        kern,
        out_shape=jax.ShapeDtypeStruct((S, D), BF16),
        grid_spec=grid_spec,
        compiler_params=_cparams("parallel", "arbitrary"),
        name="moba_attention",
    )(qi, kj, pminq, pmaxk, qkv, qkv, vt, km_hi, km_lo, pos.reshape(1, S), pos.reshape(S, 1),
      lut.reshape(G, hp, LUT_SIZE))


def _rope_fold(y):
    lane = lax.broadcasted_iota(jnp.int32, y.shape, 1)
    return jnp.where(lane < MLA_ROPE, y + pltpu.roll(y, MLA_ROPE, 1), 0.0)


def _mla_q_kernel(cq_ref, g_ref, w_ref, kr_ref, t_ref, q_ref, kro_ref, cqn_ref, *, scale, hp):
    t = t_ref[...]
    W = 2 * LANES

    @pl.when(pl.program_id(1) == 0)
    def _():
        cqn_ref[...] = _rms_rows(cq_ref[...], g_ref[...]).astype(BF16)
        kro_ref[...] = _rope_fold(kr_ref[...] * t).astype(kro_ref.dtype)

    qh = jnp.dot(cqn_ref[...], w_ref[...], preferred_element_type=F32)
    for hd in range(hp):
        q_ref[:, hd * W:hd * W + MLA_NOPE] = (qh[:, hd * W:hd * W + MLA_NOPE] * scale).astype(q_ref.dtype)
        q_ref[:, hd * W + MLA_NOPE:(hd + 1) * W] = (
            _rope_fold(qh[:, hd * W + MLA_NOPE:(hd + 1) * W] * t) * scale).astype(q_ref.dtype)


def mla_queries_and_rope_key(down, g_q, w_uq_ext, rope_tab, *, scale, tm=1024, heads_per_step=4):
    S = down.shape[0]
    H, hp = MLA_HEADS, heads_per_step
    tm = _tile(S, tm)
    W = 2 * LANES
    return pl.pallas_call(
        functools.partial(_mla_q_kernel, scale=scale, hp=hp),
        out_shape=[jax.ShapeDtypeStruct((S, H * W), BF16), jax.ShapeDtypeStruct((S, LANES), BF16)],
        grid=(S // tm, H // hp),
        in_specs=[pl.BlockSpec((tm, MLA_Q_RANK), lambda i, h: (i, 0)),
                  pl.BlockSpec((1, MLA_Q_RANK), lambda i, h: (0, 0)),
                  pl.BlockSpec((MLA_Q_RANK, hp * W), lambda i, h: (0, h)),
                  pl.BlockSpec((tm, LANES), lambda i, h: (i, (MLA_Q_RANK + MLA_KV_RANK) // LANES)),
                  pl.BlockSpec((tm, LANES), lambda i, h: (i, 0))],
        out_specs=[pl.BlockSpec((tm, hp * W), lambda i, h: (i, h)),
                   pl.BlockSpec((tm, LANES), lambda i, h: (i, 0))],
        scratch_shapes=[pltpu.VMEM((tm, MLA_Q_RANK), BF16)],
        compiler_params=_cparams("parallel", "arbitrary"),
        name="mla_queries",
    )(down, g_q.reshape(1, MLA_Q_RANK), w_uq_ext, down, rope_tab)


def _mla_attn_kernel(qi_ref, kj_ref, q_ref, kv_ref, kr_ref, vt_ref, o_ref, m_ref, l_ref, acc_ref, *, tq, ck, hp):
    p = pl.program_id(1)
    i, j = qi_ref[p], kj_ref[p]
    nc = tq // ck
    W = 2 * LANES

    @pl.when(j == 0)
    def _():
        m_ref[...] = jnp.full(m_ref.shape, NEG_INF, F32)
        l_ref[...] = jnp.zeros(l_ref.shape, F32)
        acc_ref[...] = jnp.zeros(acc_ref.shape, F32)

    def scores(c, g, q0, diag):
        keys = slice(c * ck, (c + 1) * ck)
        kc = jnp.concatenate([kv_ref[keys, g * MLA_NOPE:(g + 1) * MLA_NOPE], kr_ref[keys, :]], axis=1)
        st = _nt_dot(kc, q_ref[q0:tq, g * W:(g + 1) * W])
        if diag:
            st = jnp.where(_causal_t(c * ck, ck, q0, tq - q0), st, NEG_INF)
        return st

    def consume(c, g, q0, diag, st):
        _online_softmax_step_t(st, vt_ref[g * MLA_V:(g + 1) * MLA_V, c * ck:(c + 1) * ck],
                               m_ref.at[g], l_ref.at[g], acc_ref.at[g], slice(q0, tq))

    def sweep(diag):
        _pipelined([(c, g, c * ck if diag else 0, diag) for c in range(nc) for g in range(hp)], scores, consume)

    @pl.when(j < i)
    def _():
        sweep(False)

    @pl.when(j == i)
    def _():
        sweep(True)
        for g in range(hp):
            o_ref[:, g * MLA_V:(g + 1) * MLA_V] = (acc_ref[g] / l_ref[g]).T.astype(o_ref.dtype)


def mla_attention(q, kv, kr, vt, *, tile=1024, chunk=512, heads_per_step=2):
    S = q.shape[0]
    H, hp = MLA_HEADS, heads_per_step
    tq = _tile(S, tile)
    ck = _tile(tq, chunk)
    qi, kj = _pair_tables(S // tq)
    W = 2 * LANES
    grid_spec = pltpu.PrefetchScalarGridSpec(
        num_scalar_prefetch=2,
        grid=(H // hp, qi.shape[0]),
        in_specs=[pl.BlockSpec((tq, hp * W), lambda h, p, qi, kj: (qi[p], h)),
                  pl.BlockSpec((tq, hp * MLA_NOPE), lambda h, p, qi, kj: (kj[p], h)),
                  pl.BlockSpec((tq, LANES), lambda h, p, qi, kj: (kj[p], 0)),
                  pl.BlockSpec((hp * MLA_V, tq), lambda h, p, qi, kj: (h, kj[p]))],
        out_specs=pl.BlockSpec((tq, hp * MLA_V), lambda h, p, qi, kj: (qi[p], h)),
        scratch_shapes=[pltpu.VMEM((hp, 1, tq), F32), pltpu.VMEM((hp, 1, tq), F32),
                        pltpu.VMEM((hp, MLA_V, tq), F32)],
    )
    return pl.pallas_call(
        functools.partial(_mla_attn_kernel, tq=tq, ck=ck, hp=hp),
        out_shape=jax.ShapeDtypeStruct((S, H * MLA_V), BF16),
        grid_spec=grid_spec,
        compiler_params=_cparams("parallel", "arbitrary"),
        name="mla_attention",
    )(qi, kj, q, kv, kr, vt)


_HI16 = 0xFFFF0000


def _pack_bf16_pairs(hi_f32):
    half = hi_f32.shape[1] // 2
    bits = pltpu.bitcast(hi_f32, jnp.uint32)
    return lax.shift_right_logical(bits[:, :half], jnp.uint32(16)) | (bits[:, half:] & jnp.uint32(_HI16))


def _unpack_bf16_pairs(words):
    lo = pltpu.bitcast(lax.shift_left(words, jnp.uint32(16)), F32).astype(BF16)
    hi = pltpu.bitcast(words & jnp.uint32(_HI16), F32).astype(BF16)
    return jnp.concatenate([lo, hi], axis=1)


def _router_kernel(x_ref, g_ref, wh_ref, wl_ref, xn_ref, r_ref):
    xn = _rms_rows(x_ref[...], g_ref[...])
    hi = xn.astype(BF16)
    hi_f32 = hi.astype(F32)
    lo = (xn - hi_f32).astype(BF16)
    xn_ref[...] = _pack_bf16_pairs(hi_f32)
    logits = (jnp.dot(hi, wh_ref[...], preferred_element_type=F32)
              + jnp.dot(hi, wl_ref[...], preferred_element_type=F32)
              + jnp.dot(lo, wh_ref[...], preferred_element_type=F32))
    lane = lax.broadcasted_iota(jnp.int32, logits.shape, 1)
    lane_f = lane.astype(F32)
    g = jnp.where(lane < N_EXPERTS, logits, NEG_INF)
    v1 = jnp.max(g, axis=-1, keepdims=True)
    i1 = jnp.min(jnp.where(g == v1, lane_f, float(LANES)), axis=-1, keepdims=True)
    g = jnp.where(lane_f == i1, NEG_INF, g)
    v2 = jnp.max(g, axis=-1, keepdims=True)
    i2 = jnp.min(jnp.where(g == v2, lane_f, float(LANES)), axis=-1, keepdims=True)
    g2 = 1.0 / (1.0 + jnp.exp(v1 - v2))
    g1 = 1.0 - g2
    r_ref[...] = jnp.where(lane == 0, i1,
                           jnp.where(lane == 1, i2,
                                     jnp.where(lane == 2, g1, jnp.where(lane == 3, g2, 0.0))))


def moe_route(x, g, w_router, *, tm=512):
    M, D = x.shape
    tm = _tile(M, tm)
    wr = jnp.pad(w_router.astype(F32), ((0, 0), (0, LANES - N_EXPERTS)))
    wh = wr.astype(BF16)
    wl = (wr - wh.astype(F32)).astype(BF16)
    return pl.pallas_call(
        _router_kernel,
        out_shape=[jax.ShapeDtypeStruct((M, D // 2), jnp.uint32), jax.ShapeDtypeStruct((M, LANES), F32)],
        grid=(M // tm,),
        in_specs=[pl.BlockSpec((tm, D), lambda i: (i, 0)),
                  pl.BlockSpec((1, D), lambda i: (0, 0)),
                  pl.BlockSpec((D, LANES), lambda i: (0, 0)),
                  pl.BlockSpec((D, LANES), lambda i: (0, 0))],
        out_specs=[pl.BlockSpec((tm, D // 2), lambda i: (i, 0)),
                   pl.BlockSpec((tm, LANES), lambda i: (i, 0))],
        compiler_params=_cparams("parallel"),
        name="moe_route",
    )(x, g.reshape(1, D), wh, wl)


def _new_expert(te_ref, t):
    return jnp.logical_or(t == 0, te_ref[t] != te_ref[jnp.maximum(t - 1, 0)])


def _stream_expert_weights(te_ref, rid_ref, rexp_ref, nr_ref, used, n_col_tiles, copies_for, on_ready):
    j, t = pl.program_id(0), pl.program_id(1)

    @pl.when(jnp.logical_and(used, _new_expert(te_ref, t)))
    def _():
        n_runs = nr_ref[0]
        r = rid_ref[t]
        g = j * n_runs + r
        slot = lax.rem(g, 2)

        @pl.when(g == 0)
        def _():
            for c in copies_for(j, te_ref[t], slot):
                c.start()

        for c in copies_for(j, te_ref[t], slot):
            c.wait()
        on_ready(slot)
        last = r + 1 == n_runs
        nxt_r = jnp.where(last, 0, r + 1)
        nxt_j = jnp.where(last, j + 1, j)

        @pl.when(nxt_j < n_col_tiles)
        def _():
            for c in copies_for(nxt_j, rexp_ref[nxt_r], 1 - slot):
                c.start()


def _gmm_swiglu_kernel(te_ref, nu_ref, rid_ref, rexp_ref, nr_ref, *refs, n_parts, tiles_per_part, layer, nj, tn):
    x_refs = refs[:n_parts]
    w_hbm, o_ref, wbuf_ref, wgb_ref, wub_ref, sem = refs[n_parts:]
    t = pl.program_id(1)
    used = t < nu_ref[0]

    def copies_for(jj, e, slot):
        gate_cols = pl.ds(pl.multiple_of(jj * tn, tn), tn)
        up_cols = pl.ds(pl.multiple_of((nj + jj) * tn, tn), tn)
        return (pltpu.make_async_copy(w_hbm.at[layer, e, :, gate_cols], wbuf_ref.at[slot, 0], sem.at[slot, 0]),
                pltpu.make_async_copy(w_hbm.at[layer, e, :, up_cols], wbuf_ref.at[slot, 1], sem.at[slot, 1]))

    def on_ready(slot):
        wgb_ref[...] = wbuf_ref[slot, 0].astype(BF16)
        wub_ref[...] = wbuf_ref[slot, 1].astype(BF16)

    _stream_expert_weights(te_ref, rid_ref, rexp_ref, nr_ref, used, nj, copies_for, on_ready)

    for c in range(n_parts):
        @pl.when(jnp.logical_and(used, t // tiles_per_part == c))
        def _(c=c):
            x = _unpack_bf16_pairs(x_refs[c][...])
            a = jnp.dot(x, wgb_ref[...], preferred_element_type=F32)
            u = jnp.dot(x, wub_ref[...], preferred_element_type=F32)
            o_ref[...] = (a * jax.nn.sigmoid(a) * u).astype(o_ref.dtype)

    @pl.when(jnp.logical_not(used))
    def _():
        o_ref[...] = jnp.zeros(o_ref.shape, o_ref.dtype)


def _gmm_out_kernel(te_ref, nu_ref, rid_ref, rexp_ref, nr_ref, h_ref, w_hbm, o_ref, wbuf_ref, wb_ref, sem, *,
                    layer, nj, tn):
    t = pl.program_id(1)
    used = t < nu_ref[0]

    def copies_for(jj, e, slot):
        cols = pl.ds(pl.multiple_of(jj * tn, tn), tn)
        return (pltpu.make_async_copy(w_hbm.at[layer, e, :, cols], wbuf_ref.at[slot], sem.at[slot]),)

    def on_ready(slot):
        wb_ref[...] = wbuf_ref[slot].astype(BF16)

    _stream_expert_weights(te_ref, rid_ref, rexp_ref, nr_ref, used, nj, copies_for, on_ready)

    @pl.when(used)
    def _():
        o_ref[...] = jnp.dot(h_ref[...], wb_ref[...], preferred_element_type=F32)

    @pl.when(jnp.logical_not(used))
    def _():
        o_ref[...] = jnp.zeros(o_ref.shape, o_ref.dtype)


def moe_experts(xs, tile_expert, n_used, runs, w_in, w_out, layer, *, tm, tn=512):
    n_parts = len(xs)
    P = n_parts * xs[0].shape[0]
    D = w_in.shape[2]
    F = w_in.shape[3] // 2
    nt = P // tm
    ntp = nt // n_parts
    assert ntp * n_parts == nt
    tn1, tn2 = _tile(F, tn), _tile(D, tn)
    nj1, nj2 = F // tn1, D // tn2

    def part_spec(c):
        return pl.BlockSpec((tm, D // 2), lambda j, t, *_: (jnp.clip(t - c * ntp, 0, ntp - 1), 0))

    prefetch = (tile_expert, n_used) + tuple(runs)
    hbm = pl.BlockSpec(memory_space=pl.ANY)
    h = pl.pallas_call(
        functools.partial(_gmm_swiglu_kernel, n_parts=n_parts, tiles_per_part=ntp, layer=layer, nj=nj1, tn=tn1),
        out_shape=jax.ShapeDtypeStruct((P, F), BF16),
        grid_spec=pltpu.PrefetchScalarGridSpec(
            num_scalar_prefetch=len(prefetch),
            grid=(nj1, nt),
            in_specs=[part_spec(c) for c in range(n_parts)] + [hbm],
            out_specs=pl.BlockSpec((tm, tn1), lambda j, t, *_: (t, j)),
            scratch_shapes=[pltpu.VMEM((2, 2, D, tn1), F32), pltpu.VMEM((D, tn1), BF16),
                            pltpu.VMEM((D, tn1), BF16), pltpu.SemaphoreType.DMA((2, 2))],
        ),
        compiler_params=_cparams("arbitrary", "arbitrary"),
        name="moe_swiglu_in",
    )(*prefetch, *xs, w_in)
    return pl.pallas_call(
        functools.partial(_gmm_out_kernel, layer=layer, nj=nj2, tn=tn2),
        out_shape=jax.ShapeDtypeStruct((P, D), F32),
        grid_spec=pltpu.PrefetchScalarGridSpec(
            num_scalar_prefetch=len(prefetch),
            grid=(nj2, nt),
            in_specs=[pl.BlockSpec((tm, F), lambda j, t, *_: (t, 0)), hbm],
            out_specs=pl.BlockSpec((tm, tn2), lambda j, t, *_: (t, j)),
            scratch_shapes=[pltpu.VMEM((2, F, tn2), F32), pltpu.VMEM((F, tn2), BF16),
                            pltpu.SemaphoreType.DMA((2,))],
        ),
        compiler_params=_cparams("arbitrary", "arbitrary"),
        name="moe_out",
    )(*prefetch, h, w_out)


def moe_swiglu(x, g, w_router, w_in, w_out, layer, *, tm=512):
    S, D = x.shape
    E = N_EXPERTS
    tm = _tile(S, tm)
    xn, route = moe_route(x, g, w_router)
    experts = route[:, :2].astype(jnp.int32)
    gates = route[:, 2:4]
    flat_e = experts.reshape(-1)
    order = jnp.argsort(flat_e, stable=True).astype(jnp.int32)
    rank = jnp.argsort(order).astype(jnp.int32)
    counts = jnp.bincount(flat_e, length=E).astype(jnp.int32)
    padded = ((counts + tm - 1) // tm) * tm
    start = jnp.cumsum(counts) - counts
    pstart = jnp.cumsum(padded) - padded
    pend = jnp.cumsum(padded)
    P = 2 * S + E * tm
    nt = P // tm
    tile_expert = jnp.minimum(
        jnp.searchsorted(pend, jnp.arange(nt, dtype=jnp.int32) * tm, side="right"), E - 1).astype(jnp.int32)
    n_used = (pend[-1:] // tm).astype(jnp.int32)
    row_e = jnp.repeat(tile_expert, tm)
    row_rank = jnp.arange(P, dtype=jnp.int32) - pstart[row_e]
    row_valid = jnp.logical_and(row_rank < counts[row_e], jnp.arange(P) < pend[-1])
    row_token = jnp.where(row_valid, order[jnp.clip(start[row_e] + row_rank, 0, 2 * S - 1)] // 2, 0)
    slot = pstart[flat_e] + rank - start[flat_e]
    tile_expert = jnp.where(jnp.arange(nt) < n_used[0], tile_expert, tile_expert[jnp.maximum(n_used[0] - 1, 0)])
    slot = slot.reshape(S, 2)
    present = counts > 0
    run_of_expert = jnp.cumsum(present.astype(jnp.int32)) - 1
    runs = (run_of_expert[tile_expert].astype(jnp.int32),
            jnp.sort(jnp.where(present, jnp.arange(E, dtype=jnp.int32), E))[:E].clip(0, E - 1).astype(jnp.int32),
            jnp.sum(present).astype(jnp.int32).reshape(1))
    xs = [jnp.take(xn, part, axis=0, mode="clip") for part in jnp.split(row_token, MOE_GATHER_PARTS)]
    ys = moe_experts(xs, tile_expert, n_used, runs, w_in, w_out, layer, tm=tm)
    return jnp.take(ys, slot[:, 0], axis=0, mode="clip"), jnp.take(ys, slot[:, 1], axis=0, mode="clip"), gates


def _diff_lambda_init(layer):
    return 0.8 - 0.6 * math.exp(-0.3 * layer)


def _rope_table(pos):
    half = MLA_ROPE // 2
    inv_freq = ROPE_THETA ** (-jnp.arange(half, dtype=F32) / half)
    ang = pos.astype(F32)[:, None] * inv_freq
    cos, sin = jnp.cos(ang), jnp.sin(ang)
    return jnp.concatenate([cos, cos, -sin, sin], axis=1)


def _rot_half_cols(w):
    half = w.shape[-1] // 2
    return jnp.concatenate([w[..., half:], w[..., :half]], axis=-1)


def kernel(x, p, positions, rel_bias, norm_mix, norm_ffn, norm_ple, norm_final, diff_w_qkv, diff_lambda, diff_subln, diff_w_o, moba_w_qkv, moba_w_o, mla_w_down, mla_g_q, mla_w_uq, mla_g_kv, mla_w_ukv, mla_w_o, ffn_w_in, ffn_w_out, moe_w_router, moe_w_in, moe_w_out, ple_w_gate, ple_w_proj):
    B, S, D = x.shape
    depth = p.shape[0]
    lut = _bias_lut(rel_bias)
    outs = []
    for b in range(B):
        xb = x[b]
        pos = positions[b].astype(jnp.int32)
        for i in range(depth):
            jm = i // N_MIXERS
            if i % N_MIXERS == 0:
                qk, vt = norm_matmul_vt(xb, norm_mix[i], diff_w_qkv[jm].astype(BF16), n_direct=2 * D,
                                        scaled_cols=D, scale=DIFF_HEAD_DIM ** -0.5 * LOG2E)
                o = diff_attention(qk, vt, pos, lut, diff_lambda[jm].astype(F32), diff_subln[jm],
                                   _diff_lambda_init(i))
                xb = matmul_residual(o, diff_w_o[jm].astype(BF16), xb)
            elif i % N_MIXERS == 1:
                qk, vt = norm_matmul_vt(xb, norm_mix[i], moba_w_qkv[jm].astype(BF16), n_direct=2 * D,
                                        scaled_cols=D, scale=MOBA_HEAD_DIM ** -0.5 * LOG2E)
                o = moba_attention(qk, vt, pos, lut)
                xb = matmul_residual(o, moba_w_o[jm].astype(BF16), xb)
            else:
                wd = mla_w_down[jm]
                kr0 = MLA_Q_RANK + MLA_KV_RANK
                wd_ext = jnp.concatenate([wd, _rot_half_cols(wd[:, kr0:])], axis=1).astype(BF16)
                wq = mla_w_uq[jm].reshape(MLA_Q_RANK, MLA_HEADS, MLA_NOPE + MLA_ROPE)
                wq_ext = jnp.concatenate([wq, _rot_half_cols(wq[..., MLA_NOPE:])], axis=-1)
                wq_ext = wq_ext.reshape(MLA_Q_RANK, MLA_HEADS * 2 * LANES).astype(BF16)
                down = norm_matmul(xb, norm_mix[i], wd_ext, out_dtype=F32, tn=wd_ext.shape[1])
                q, kr = mla_queries_and_rope_key(down, mla_g_q[jm], wq_ext, _rope_table(pos),
                                                 scale=(MLA_NOPE + MLA_ROPE) ** -0.5 * LOG2E)
                wkv = mla_w_ukv[jm].reshape(MLA_KV_RANK, MLA_HEADS, 2, MLA_V).transpose(0, 2, 1, 3)
                wkv = wkv.reshape(MLA_KV_RANK, 2 * MLA_HEADS * MLA_V).astype(BF16)
                kn, vt = norm_matmul_vt(down, mla_g_kv[jm], wkv, n_direct=MLA_HEADS * MLA_NOPE, xcol=1)
                o = mla_attention(q, kn, kr, vt)
                xb = matmul_residual(o, mla_w_o[jm].astype(BF16), xb)
            w_gate, w_proj = ple_w_gate[i].astype(BF16), ple_w_proj[i].astype(BF16)
            if i % 2 == 0:
                h = norm_swiglu_in(xb, norm_ffn[i], ffn_w_in[i // 2].astype(BF16))
                xb = matmul_residual(h, ffn_w_out[i // 2].astype(BF16), xb)
                xb = ple_update(xb, norm_ple[i], p[i, b], w_gate, w_proj)
            else:
                y1, y2, gates = moe_swiglu(xb, norm_ffn[i], moe_w_router[i // 2], moe_w_in, moe_w_out, i // 2)
                xb = ple_update_moe(xb, y1, y2, gates, norm_ple[i], p[i, b], w_gate, w_proj)
        outs.append(final_norm(xb, norm_final))
    return jnp.stack(outs, axis=0)
```

```python
import functools
import math

import numpy as np
import jax
import jax.numpy as jnp
from jax import lax
from jax.experimental import pallas as pl
from jax.experimental.pallas import tpu as pltpu

F32 = jnp.float32
BF16 = jnp.bfloat16

NORM_EPS = 1e-6
NEG_INF = -1e30
LOG2E = math.log2(math.e)
LANES = 128
VMEM_LIMIT_BYTES = 56 * 1024 * 1024

REL_BUCKETS = 32
REL_MAX_DIST = 128
LUT_SIZE = LANES
BIAS_BLOCK = 256
DIFF_HEADS = 8
DIFF_HEAD_DIM = 128
MOBA_HEADS = 16
MOBA_HEAD_DIM = 128
MOBA_BLOCK = 256
MOBA_TOPK = 3
MLA_HEADS = 16
MLA_Q_RANK = 512
MLA_KV_RANK = 512
MLA_NOPE = 128
MLA_ROPE = 64
MLA_V = 128
ROPE_THETA = 10000.0
N_EXPERTS = 8
N_MIXERS = 3
MOE_GATHER_PARTS = 4


def _cparams(*sem):
    return pltpu.CompilerParams(dimension_semantics=sem, vmem_limit_bytes=VMEM_LIMIT_BYTES)


def _tile(n, pref):
    if n <= pref:
        return n
    t = pref
    while n % t:
        t //= 2
    return t


def _weight_spec(w, layer, tn, col=lambda j: j):
    K = w.shape[-2]
    if w.ndim == 2:
        return pl.BlockSpec((K, tn), lambda i, j: (0, col(j)))
    return pl.BlockSpec((None, K, tn), lambda i, j: (layer, 0, col(j)))


def _rms_rows(x, g):
    r = lax.rsqrt(jnp.mean(x * x, axis=-1, keepdims=True) + NORM_EPS)
    return x * r * g


def _norm_mm_kernel(x_ref, g_ref, w_ref, o_ref, xn_ref, *, scaled_tiles, scale):
    j = pl.program_id(1)

    @pl.when(j == 0)
    def _():
        xn_ref[...] = _rms_rows(x_ref[...], g_ref[...]).astype(BF16)

    acc = jnp.dot(xn_ref[...], w_ref[...].astype(BF16), preferred_element_type=F32)
    if scaled_tiles:
        acc = acc * jnp.where(j < scaled_tiles, scale, 1.0)
    o_ref[...] = acc.astype(o_ref.dtype)


def norm_matmul(x, g, w, *, out_dtype, xcol=0, tm=1024, tn=512, scaled_cols=0, scale=1.0):
    M = x.shape[0]
    K, N = w.shape
    tm, tn = _tile(M, tm), _tile(N, tn)
    assert scaled_cols % tn == 0
    kern = functools.partial(_norm_mm_kernel, scaled_tiles=scaled_cols // tn, scale=scale)
    return pl.pallas_call(
        kern,
        out_shape=jax.ShapeDtypeStruct((M, N), out_dtype),
        grid=(M // tm, N // tn),
        in_specs=[pl.BlockSpec((tm, K), lambda i, j: (i, xcol)),
                  pl.BlockSpec((1, K), lambda i, j: (0, 0)),
                  pl.BlockSpec((K, tn), lambda i, j: (0, j))],
        out_specs=pl.BlockSpec((tm, tn), lambda i, j: (i, j)),
        scratch_shapes=[pltpu.VMEM((tm, K), BF16)],
        compiler_params=_cparams("parallel", "arbitrary"),
        name="norm_matmul",
    )(x, g.reshape(1, K), w)


def _norm_mm_vt_kernel(x_ref, g_ref, w_ref, o_ref, ot_ref, xn_ref, *, direct_tiles, scaled_tiles, scale):
    j = pl.program_id(1)

    @pl.when(j == 0)
    def _():
        xn_ref[...] = _rms_rows(x_ref[...], g_ref[...]).astype(BF16)

    acc = jnp.dot(xn_ref[...], w_ref[...].astype(BF16), preferred_element_type=F32)

    @pl.when(j < direct_tiles)
    def _():
        out = acc * jnp.where(j < scaled_tiles, scale, 1.0) if scaled_tiles else acc
        o_ref[...] = out.astype(o_ref.dtype)

    @pl.when(j >= direct_tiles)
    def _():
        ot_ref[...] = acc.T.astype(ot_ref.dtype)


def norm_matmul_vt(x, g, w, *, n_direct, layer=None, xcol=0, tm=1024, tn=512, scaled_cols=0, scale=1.0):
    M = x.shape[0]
    K, N = w.shape[-2:]
    tm, tn = _tile(M, tm), _tile(N, tn)
    assert scaled_cols % tn == 0 and n_direct % tn == 0 and 0 < n_direct < N
    nd = n_direct // tn
    kern = functools.partial(_norm_mm_vt_kernel, direct_tiles=nd, scaled_tiles=scaled_cols // tn, scale=scale)
    return pl.pallas_call(
        kern,
        out_shape=[jax.ShapeDtypeStruct((M, n_direct), BF16), jax.ShapeDtypeStruct((N - n_direct, M), BF16)],
        grid=(M // tm, N // tn),
        in_specs=[pl.BlockSpec((tm, K), lambda i, j: (i, xcol)),
                  pl.BlockSpec((1, K), lambda i, j: (0, 0)),
                  _weight_spec(w, layer, tn)],
        out_specs=[pl.BlockSpec((tm, tn), lambda i, j: (i, jnp.minimum(j, nd - 1))),
                   pl.BlockSpec((tn, tm), lambda i, j: (jnp.maximum(j - nd, 0), i))],
        scratch_shapes=[pltpu.VMEM((tm, K), BF16)],
        compiler_params=_cparams("parallel", "arbitrary"),
        name="norm_matmul_vt",
    )(x, g.reshape(1, K), w)


def _norm_swiglu_kernel(x_ref, g_ref, wg_ref, wu_ref, o_ref, xn_ref):
    @pl.when(pl.program_id(1) == 0)
    def _():
        xn_ref[...] = _rms_rows(x_ref[...], g_ref[...]).astype(BF16)

    xn = xn_ref[...]
    a = jnp.dot(xn, wg_ref[...].astype(BF16), preferred_element_type=F32)
    u = jnp.dot(xn, wu_ref[...].astype(BF16), preferred_element_type=F32)
    o_ref[...] = (a * jax.nn.sigmoid(a) * u).astype(o_ref.dtype)


def norm_swiglu_in(x, g, w_in, *, layer=None, tm=1024, tn=512):
    M, K = x.shape
    F = w_in.shape[-1] // 2
    tm, tn = _tile(M, tm), _tile(F, tn)
    nj = F // tn
    return pl.pallas_call(
        _norm_swiglu_kernel,
        out_shape=jax.ShapeDtypeStruct((M, F), BF16),
        grid=(M // tm, nj),
        in_specs=[pl.BlockSpec((tm, K), lambda i, j: (i, 0)),
                  pl.BlockSpec((1, K), lambda i, j: (0, 0)),
                  _weight_spec(w_in, layer, tn),
                  _weight_spec(w_in, layer, tn, lambda j: j + nj)],
        out_specs=pl.BlockSpec((tm, tn), lambda i, j: (i, j)),
        scratch_shapes=[pltpu.VMEM((tm, K), BF16)],
        compiler_params=_cparams("parallel", "arbitrary"),
        name="norm_swiglu_in",
    )(x, g.reshape(1, K), w_in, w_in)


def _mm_res_kernel(a_ref, w_ref, r_ref, o_ref):
    o_ref[...] = r_ref[...] + jnp.dot(a_ref[...], w_ref[...].astype(BF16), preferred_element_type=F32)


def matmul_residual(a, w, res, *, layer=None, tm=1024, tn=512):
    M, K = a.shape
    N = w.shape[-1]
    tm, tn = _tile(M, tm), _tile(N, tn)
    return pl.pallas_call(
        _mm_res_kernel,
        out_shape=jax.ShapeDtypeStruct((M, N), F32),
        grid=(M // tm, N // tn),
        in_specs=[pl.BlockSpec((tm, K), lambda i, j: (i, 0)),
                  _weight_spec(w, layer, tn),
                  pl.BlockSpec((tm, tn), lambda i, j: (i, j))],
        out_specs=pl.BlockSpec((tm, tn), lambda i, j: (i, j)),
        compiler_params=_cparams("parallel", "parallel"),
        name="matmul_residual",
    )(a, w, res)


def _ple_kernel(x_ref, g_ref, xr_ref, p_ref, wg_ref, wp_ref, o_ref, xn_ref):
    @pl.when(pl.program_id(1) == 0)
    def _():
        xn_ref[...] = _rms_rows(x_ref[...], g_ref[...]).astype(BF16)

    gate = jax.nn.sigmoid(jnp.dot(xn_ref[...], wg_ref[...].astype(BF16), preferred_element_type=F32))
    proj = jnp.dot(p_ref[...].astype(BF16), wp_ref[...].astype(BF16), preferred_element_type=F32)
    o_ref[...] = xr_ref[...] + gate * proj


def ple_update(x, g, p, w_gate, w_proj, *, layer=None, tm=1024, tn=512):
    M, D = x.shape
    P = p.shape[1]
    tm, tn = _tile(M, tm), _tile(D, tn)
    return pl.pallas_call(
        _ple_kernel,
        out_shape=jax.ShapeDtypeStruct((M, D), F32),
        grid=(M // tm, D // tn),
        in_specs=[pl.BlockSpec((tm, D), lambda i, j: (i, 0)),
                  pl.BlockSpec((1, D), lambda i, j: (0, 0)),
                  pl.BlockSpec((tm, tn), lambda i, j: (i, j)),
                  pl.BlockSpec((tm, P), lambda i, j: (i, 0)),
                  _weight_spec(w_gate, layer, tn),
                  _weight_spec(w_proj, layer, tn)],
        out_specs=pl.BlockSpec((tm, tn), lambda i, j: (i, j)),
        scratch_shapes=[pltpu.VMEM((tm, D), BF16)],
        compiler_params=_cparams("parallel", "arbitrary"),
        name="ple_update",
    )(x, g.reshape(1, D), x, p, w_gate, w_proj)


def _ple_moe_kernel(x_ref, y1_ref, y2_ref, gt_ref, g_ref, p_ref, wg_ref, wp_ref, o_ref, xs_ref, xn_ref, *, tn):
    j = pl.program_id(1)

    @pl.when(j == 0)
    def _():
        gt = gt_ref[...]
        xnew = x_ref[...] + gt[:, 0:1] * y1_ref[...] + gt[:, 1:2] * y2_ref[...]
        xn_ref[...] = _rms_rows(xnew, g_ref[...]).astype(BF16)
        for jj in range(xs_ref.shape[0]):
            xs_ref[jj] = xnew[:, jj * tn:(jj + 1) * tn]

    gate = jax.nn.sigmoid(jnp.dot(xn_ref[...], wg_ref[...].astype(BF16), preferred_element_type=F32))
    proj = jnp.dot(p_ref[...].astype(BF16), wp_ref[...].astype(BF16), preferred_element_type=F32)
    o_ref[...] = xs_ref[j] + gate * proj


def ple_update_moe(x, y1, y2, gates, g, p, w_gate, w_proj, *, layer=None, tm=512, tn=512):
    M, D = x.shape
    P = p.shape[1]
    tm, tn = _tile(M, tm), _tile(D, tn)
    row = pl.BlockSpec((tm, D), lambda i, j: (i, 0))
    return pl.pallas_call(
        functools.partial(_ple_moe_kernel, tn=tn),
        out_shape=jax.ShapeDtypeStruct((M, D), F32),
        grid=(M // tm, D // tn),
        in_specs=[row, row, row,
                  pl.BlockSpec((tm, 2), lambda i, j: (i, 0)),
                  pl.BlockSpec((1, D), lambda i, j: (0, 0)),
                  pl.BlockSpec((tm, P), lambda i, j: (i, 0)),
                  _weight_spec(w_gate, layer, tn),
                  _weight_spec(w_proj, layer, tn)],
        out_specs=pl.BlockSpec((tm, tn), lambda i, j: (i, j)),
        scratch_shapes=[pltpu.VMEM((D // tn, tm, tn), F32), pltpu.VMEM((tm, D), BF16)],
        compiler_params=_cparams("parallel", "arbitrary"),
        name="ple_update_moe",
    )(x, y1, y2, gates, g.reshape(1, D), p, w_gate, w_proj)


def _final_norm_kernel(x_ref, g_ref, o_ref):
    o_ref[...] = _rms_rows(x_ref[...], g_ref[...])


def final_norm(x, g, *, tm=512):
    M, D = x.shape
    tm = _tile(M, tm)
    return pl.pallas_call(
        _final_norm_kernel,
        out_shape=jax.ShapeDtypeStruct((M, D), F32),
        grid=(M // tm,),
        in_specs=[pl.BlockSpec((tm, D), lambda i: (i, 0)),
                  pl.BlockSpec((1, D), lambda i: (0, 0))],
        out_specs=pl.BlockSpec((tm, D), lambda i: (i, 0)),
        compiler_params=_cparams("parallel"),
        name="final_norm",
    )(x, g.reshape(1, D))


def _pair_tables(n_tiles):
    qi, kj = [], []
    for i in range(n_tiles):
        for j in range(i + 1):
            qi.append(i)
            kj.append(j)
    return jnp.asarray(np.array(qi, np.int32)), jnp.asarray(np.array(kj, np.int32))


def _rel_bucket(dist):
    n = jnp.maximum(dist, 0)
    max_exact = REL_BUCKETS // 2
    nf = jnp.maximum(n, 1).astype(F32)
    large = max_exact + (jnp.log(nf / max_exact) / math.log(REL_MAX_DIST / max_exact)
                         * (REL_BUCKETS - max_exact)).astype(jnp.int32)
    large = jnp.minimum(large, REL_BUCKETS - 1)
    return jnp.where(n < max_exact, n, large)


def _bias_lut(rel_bias):
    buckets = _rel_bucket(jnp.arange(LUT_SIZE, dtype=jnp.int32))
    return rel_bias.astype(F32)[buckets].T * LOG2E


def _block_pos_bounds(pos, sb):
    blocks = pos.reshape(pos.shape[0] // sb, sb)
    return blocks.min(axis=1), blocks.max(axis=1)


def _lut_bias(lut_row, idx):
    R, C = idx.shape
    lut = jnp.broadcast_to(lut_row, (R, LANES))
    parts = [jnp.take_along_axis(lut, idx[:, c:c + LANES], axis=1, mode="promise_in_bounds")
             for c in range(0, C, LANES)]
    return parts[0] if len(parts) == 1 else jnp.concatenate(parts, axis=1)


def _pipelined(units, scores, consume):
    st = scores(*units[0])
    for k, u in enumerate(units):
        nxt = scores(*units[k + 1]) if k + 1 < len(units) else None
        consume(*u, st)
        st = nxt


def _online_softmax_step_t(st, vt, m_ref, l_ref, acc_ref, qcols, m_cur=None, offsets=None):
    m_prev = m_ref[:, qcols]
    if m_cur is None:
        m_cur = jnp.max(st, axis=0, keepdims=True)
    m_new = jnp.maximum(m_prev, m_cur)
    alpha = jnp.exp2(m_prev - m_new)
    if offsets is None:
        p = jnp.exp2(st - m_new)
    else:
        parts = [jnp.exp2(st[rows, :] - fn(m_new)) for rows, fn in offsets]
        p = parts[0] if len(parts) == 1 else jnp.concatenate(parts, axis=0)
    l_ref[:, qcols] = alpha * l_ref[:, qcols] + jnp.sum(p, axis=0, keepdims=True)
    acc_ref[:, qcols] = alpha * acc_ref[:, qcols] + jnp.dot(vt, p.astype(BF16), preferred_element_type=F32)
    m_ref[:, qcols] = m_new


def _causal_t(k0, nk, q0, nq):
    r = k0 + lax.broadcasted_iota(jnp.int32, (nk, nq), 0)
    c = q0 + lax.broadcasted_iota(jnp.int32, (nk, nq), 1)
    return r <= c


def _nt_dot(a, b):
    return lax.dot_general(a, b, (((1,), (1,)), ((), ())), preferred_element_type=F32)


def _diff_attn_kernel(qi_ref, kj_ref, pminq_ref, pmaxk_ref,
                      q_ref, k_ref, vt_ref, pq_ref, pk_ref, lut_ref, lam_ref, g_ref,
                      o_ref, m_ref, l_ref, acc_ref, st_ref, *, tq, ck, sb, lambda_init):
    p = pl.program_id(1)
    i, j = qi_ref[p], kj_ref[p]
    nc = tq // ck
    dh = DIFF_HEAD_DIM
    SB = sb
    nqb, nkb = tq // SB, ck // SB

    @pl.when(j == 0)
    def _():
        m_ref[...] = jnp.full(m_ref.shape, NEG_INF, F32)
        l_ref[...] = jnp.zeros(l_ref.shape, F32)
        acc_ref[...] = jnp.zeros(acc_ref.shape, F32)

    def near_sub(c, kb, qb):
        return pminq_ref[i * nqb + qb] - pmaxk_ref[(j * nc + c) * nkb + kb] < LUT_SIZE - 1

    def refine_chunk(c):
        keys = slice(c * ck, (c + 1) * ck)
        for mp in range(2):
            st_ref[mp] = _nt_dot(k_ref[keys, mp * dh:(mp + 1) * dh], q_ref[:, mp * dh:(mp + 1) * dh])
        for kb in range(nkb):
            for qb in range(nqb):
                @pl.when(near_sub(c, kb, qb))
                def _(kb=kb, qb=qb):
                    rows, cols = slice(kb * SB, (kb + 1) * SB), slice(qb * SB, (qb + 1) * SB)
                    idx = jnp.clip(pq_ref[:, cols] - pk_ref[c * ck + kb * SB:c * ck + (kb + 1) * SB, :],
                                   0, LUT_SIZE - 1)
                    for mp in range(2):
                        far_bias = lut_ref[mp:mp + 1, LUT_SIZE - 1:LUT_SIZE]
                        st_ref[mp, rows, cols] = (st_ref[mp, rows, cols]
                                                  + (_lut_bias(lut_ref[mp:mp + 1, :], idx) - far_bias))
        for mp in range(2):
            consume(c, mp, 0, "far", st_ref[mp])

    def scores(c, mp, q0, mode):
        keys = slice(c * ck, (c + 1) * ck)
        st = _nt_dot(k_ref[keys, mp * dh:(mp + 1) * dh], q_ref[q0:tq, mp * dh:(mp + 1) * dh])
        if mode != "far":
            idx = jnp.clip(pq_ref[:, q0:tq] - pk_ref[keys, :], 0, LUT_SIZE - 1)
            st = st + _lut_bias(lut_ref[mp:mp + 1, :], idx)
        if mode == "diag":
            st = jnp.where(_causal_t(c * ck, ck, q0, tq - q0), st, NEG_INF)
        return st

    def consume(c, mp, q0, mode, st):
        m_cur = offsets = None
        if mode == "far":
            bias = lut_ref[mp:mp + 1, LUT_SIZE - 1:LUT_SIZE]
            m_cur = jnp.max(st, axis=0, keepdims=True) + bias
            offsets = [(slice(None), lambda m_new: m_new - bias)]
        _online_softmax_step_t(st, vt_ref[:, c * ck:(c + 1) * ck], m_ref.at[mp], l_ref.at[mp], acc_ref.at[mp],
                               slice(q0, tq), m_cur, offsets)

    def sweep(chunks, mode):
        units = [(c, mp, c * ck if mode == "diag" else 0, mode) for c in chunks for mp in range(2)]
        _pipelined(units, scores, consume)

    @pl.when(j < i)
    def _():
        fars = [jnp.logical_not(functools.reduce(
            jnp.logical_or, [near_sub(c, kb, qb) for kb in range(nkb) for qb in range(nqb)])) for c in range(nc)]
        all_far = functools.reduce(jnp.logical_and, fars)
        pl.when(all_far)(functools.partial(sweep, range(nc), "far"))

        @pl.when(jnp.logical_not(all_far))
        def _():
            for c in range(nc):
                pl.when(fars[c])(functools.partial(sweep, [c], "far"))
                pl.when(jnp.logical_not(fars[c]))(functools.partial(refine_chunk, c))

    @pl.when(j == i)
    def _():
        sweep(range(nc), "diag")
        lam = lam_ref[...]
        lam_full = (jnp.exp(jnp.sum(lam[0:1] * lam[1:2], axis=-1, keepdims=True))
                    - jnp.exp(jnp.sum(lam[2:3] * lam[3:4], axis=-1, keepdims=True)) + lambda_init)
        a = acc_ref[0] / l_ref[0] - lam_full * (acc_ref[1] / l_ref[1])
        r = lax.rsqrt(jnp.mean(a * a, axis=0, keepdims=True) + NORM_EPS)
        y = a * r * (g_ref[...] * (1.0 - lambda_init))
        o_ref[...] = y.T.astype(o_ref.dtype)


def diff_attention(qkv, vt, pos, lut, lam, subln_g, lambda_init, *, tile=1024, chunk=512):
    S = qkv.shape[0]
    H, dh = DIFF_HEADS, DIFF_HEAD_DIM
    tq = _tile(S, tile)
    ck = _tile(tq, chunk)
    qi, kj = _pair_tables(S // tq)
    sb = _tile(ck, BIAS_BLOCK)
    pminq, pmaxk = _block_pos_bounds(pos, sb)
    kern = functools.partial(_diff_attn_kernel, tq=tq, ck=ck, sb=sb, lambda_init=lambda_init)
    grid_spec = pltpu.PrefetchScalarGridSpec(
        num_scalar_prefetch=4,
        grid=(H, qi.shape[0]),
        in_specs=[pl.BlockSpec((tq, 2 * dh), lambda h, p, qi, kj, a, b: (qi[p], h)),
                  pl.BlockSpec((tq, 2 * dh), lambda h, p, qi, kj, a, b: (kj[p], H + h)),
                  pl.BlockSpec((2 * dh, tq), lambda h, p, qi, kj, a, b: (h, kj[p])),
                  pl.BlockSpec((1, tq), lambda h, p, qi, kj, a, b: (0, qi[p])),
                  pl.BlockSpec((tq, 1), lambda h, p, qi, kj, a, b: (kj[p], 0)),
                  pl.BlockSpec((None, 2, LUT_SIZE), lambda h, p, qi, kj, a, b: (h, 0, 0)),
                  pl.BlockSpec((4, dh), lambda h, p, qi, kj, a, b: (0, 0)),
                  pl.BlockSpec((2 * dh, 1), lambda h, p, qi, kj, a, b: (0, 0))],
        out_specs=pl.BlockSpec((tq, 2 * dh), lambda h, p, qi, kj, a, b: (qi[p], h)),
        scratch_shapes=[pltpu.VMEM((2, 1, tq), F32), pltpu.VMEM((2, 1, tq), F32),
                        pltpu.VMEM((2, 2 * dh, tq), F32), pltpu.VMEM((2, ck, tq), F32)],
    )
    return pl.pallas_call(
        kern,
        out_shape=jax.ShapeDtypeStruct((S, H * 2 * dh), BF16),
        grid_spec=grid_spec,
        compiler_params=_cparams("parallel", "arbitrary"),
        name="diff_attention",
    )(qi, kj, pminq, pmaxk, qkv, qkv, vt, pos.reshape(1, S), pos.reshape(S, 1),
      lut.reshape(H, 2, LUT_SIZE), lam, subln_g.reshape(2 * dh, 1))


def _kmean_kernel(k_ref, hi_ref, lo_ref):
    km = jnp.mean(k_ref[...].astype(F32), axis=0, keepdims=True)
    hi = km.astype(BF16)
    hi_ref[...] = hi
    lo_ref[...] = (km - hi.astype(F32)).astype(BF16)


def moba_block_means(qkv, D):
    S = qkv.shape[0]
    nblk = S // MOBA_BLOCK
    hi, lo = pl.pallas_call(
        _kmean_kernel,
        out_shape=[jax.ShapeDtypeStruct((nblk, 1, D), BF16)] * 2,
        grid=(nblk,),
        in_specs=[pl.BlockSpec((MOBA_BLOCK, D), lambda n: (n, 1))],
        out_specs=[pl.BlockSpec((None, 1, D), lambda n: (n, 0, 0))] * 2,
        compiler_params=_cparams("parallel"),
        name="moba_block_means",
    )(qkv)
    return hi.reshape(nblk, D), lo.reshape(nblk, D)


def _moba_attn_kernel(qi_ref, kj_ref, pminq_ref, pmaxk_ref,
                      q_ref, k_ref, vt_ref, kmh_ref, kml_ref, pq_ref, pk_ref, lut_ref,
                      o_ref, m_ref, l_ref, acc_ref, sel_ref, st_ref, *, tq, ck, sb, hp):
    p = pl.program_id(1)
    i, j = qi_ref[p], kj_ref[p]
    nc = tq // ck
    L, dh = MOBA_BLOCK, MOBA_HEAD_DIM
    bpc = ck // L
    log2_l = L.bit_length() - 1
    SB = sb
    nqb, nkb = tq // SB, ck // SB

    def near_sub(c, kb, qb):
        return pminq_ref[i * nqb + qb] - pmaxk_ref[(j * nc + c) * nkb + kb] < LUT_SIZE - 1

    def refine_chunk(c):
        keys = slice(c * ck, (c + 1) * ck)
        for hd in range(hp):
            st_ref[hd] = _nt_dot(k_ref[keys, hd * dh:(hd + 1) * dh], q_ref[:, hd * dh:(hd + 1) * dh])
        for kb in range(nkb):
            for qb in range(nqb):
                @pl.when(near_sub(c, kb, qb))
                def _(kb=kb, qb=qb):
                    rows, cols = slice(kb * SB, (kb + 1) * SB), slice(qb * SB, (qb + 1) * SB)
                    idx = jnp.clip(pq_ref[:, cols] - pk_ref[c * ck + kb * SB:c * ck + (kb + 1) * SB, :],
                                   0, LUT_SIZE - 1)
                    for hd in range(hp):
                        far_bias = lut_ref[hd:hd + 1, LUT_SIZE - 1:LUT_SIZE]
                        st_ref[hd, rows, cols] = (st_ref[hd, rows, cols]
                                                  + (_lut_bias(lut_ref[hd:hd + 1, :], idx) - far_bias))
        for hd in range(hp):
            consume(c, hd, 0, "far", st_ref[hd])

    @pl.when(j == 0)
    def _():
        m_ref[...] = jnp.full(m_ref.shape, NEG_INF, F32)
        l_ref[...] = jnp.zeros(l_ref.shape, F32)
        acc_ref[...] = jnp.zeros(acc_ref.shape, F32)
        blk = lax.broadcasted_iota(jnp.int32, (LANES, tq), 0)
        blk_f = blk.astype(F32)
        own = lax.shift_right_logical(i * tq + lax.broadcasted_iota(jnp.int32, (1, tq), 1), log2_l)
        for hd in range(hp):
            q = q_ref[:, hd * dh:(hd + 1) * dh]
            gs = (_nt_dot(kmh_ref[:, hd * dh:(hd + 1) * dh], q)
                  + _nt_dot(kml_ref[:, hd * dh:(hd + 1) * dh], q))
            g = jnp.where(blk < own, gs, NEG_INF)
            sel = jnp.zeros((LANES, tq), F32)
            for _ in range(MOBA_TOPK):
                best = jnp.max(g, axis=0, keepdims=True)
                first = jnp.min(jnp.where(g == best, blk_f, float(LANES)), axis=0, keepdims=True)
                pick = blk_f == first
                sel = jnp.where(pick, jnp.where(best > 0.5 * NEG_INF, 1.0, sel), sel)
                g = jnp.where(pick, -3e38, g)
            sel_ref[hd] = sel

    def scores(c, hd, q0, mode):
        keys = slice(c * ck, (c + 1) * ck)
        nq = tq - q0
        st = _nt_dot(k_ref[keys, hd * dh:(hd + 1) * dh], q_ref[q0:tq, hd * dh:(hd + 1) * dh])
        if mode != "far":
            idx = jnp.clip(pq_ref[:, q0:tq] - pk_ref[keys, :], 0, LUT_SIZE - 1)
            st = st + _lut_bias(lut_ref[hd:hd + 1, :], idx)
        if mode != "diag":
            return st
        allowed = []
        for b in range(bpc):
            n_local = c * bpc + b
            picked = jnp.broadcast_to(picked_row(hd, n_local, q0), (L, nq))
            own_local = lax.shift_right_logical(q0 + lax.broadcasted_iota(jnp.int32, (L, nq), 1), log2_l)
            visible = jnp.where(_causal_t(n_local * L, L, q0, nq), 1.0, 0.0)
            allowed.append(jnp.where(own_local == n_local, visible, picked))
        allowed = allowed[0] if bpc == 1 else jnp.concatenate(allowed, axis=0)
        return jnp.where(allowed > 0.0, st, NEG_INF)

    def picked_row(hd, n_local, q0):
        return sel_ref[hd, pl.ds(j * (tq // L) + n_local, 1), q0:tq]

    def consume(c, hd, q0, mode, st):
        m_cur = offsets = None
        if mode != "diag":
            bias = lut_ref[hd:hd + 1, LUT_SIZE - 1:LUT_SIZE] if mode == "far" else 0.0
            picked = [picked_row(hd, c * bpc + b, q0) > 0.0 for b in range(bpc)]
            m_cur = functools.reduce(jnp.maximum, [
                jnp.where(picked[b], jnp.max(st[b * L:(b + 1) * L, :], axis=0, keepdims=True) + bias, NEG_INF)
                for b in range(bpc)])
            offsets = [(slice(b * L, (b + 1) * L),
                        functools.partial(lambda m_new, pk: jnp.where(pk, m_new - bias, -NEG_INF), pk=picked[b]))
                       for b in range(bpc)]
        _online_softmax_step_t(st, vt_ref[hd * dh:(hd + 1) * dh, c * ck:(c + 1) * ck],
                               m_ref.at[hd], l_ref.at[hd], acc_ref.at[hd], slice(q0, tq), m_cur, offsets)

    def sweep(chunks, mode):
        _pipelined([(c, hd, c * ck if mode == "diag" else 0, mode) for c in chunks for hd in range(hp)],
                   scores, consume)

    @pl.when(j < i)
    def _():
        fars = [jnp.logical_not(functools.reduce(
            jnp.logical_or, [near_sub(c, kb, qb) for kb in range(nkb) for qb in range(nqb)])) for c in range(nc)]
        all_far = functools.reduce(jnp.logical_and, fars)
        pl.when(all_far)(functools.partial(sweep, range(nc), "far"))

        @pl.when(jnp.logical_not(all_far))
        def _():
            for c in range(nc):
                pl.when(fars[c])(functools.partial(sweep, [c], "far"))
                pl.when(jnp.logical_not(fars[c]))(functools.partial(refine_chunk, c))

    @pl.when(j == i)
    def _():
        sweep(range(nc), "diag")
        for hd in range(hp):
            o_ref[:, hd * dh:(hd + 1) * dh] = (acc_ref[hd] / l_ref[hd]).T.astype(o_ref.dtype)


def moba_attention(qkv, vt, pos, lut, *, tile=1024, chunk=512, heads_per_step=2):
    S = qkv.shape[0]
    H, dh, L = MOBA_HEADS, MOBA_HEAD_DIM, MOBA_BLOCK
    D = H * dh
    nblk = S // L
    assert S % L == 0 and nblk <= LANES
    tq = _tile(S, tile)
    ck = _tile(tq, chunk)
    assert tq % L == 0 and ck % L == 0
    km_hi, km_lo = moba_block_means(qkv, D)
    km_hi = jnp.pad(km_hi, ((0, LANES - nblk), (0, 0)))
    km_lo = jnp.pad(km_lo, ((0, LANES - nblk), (0, 0)))
    qi, kj = _pair_tables(S // tq)
    sb = _tile(ck, BIAS_BLOCK)
    pminq, pmaxk = _block_pos_bounds(pos, sb)
    hp = heads_per_step
    G = H // hp
    kern = functools.partial(_moba_attn_kernel, tq=tq, ck=ck, sb=sb, hp=hp)
    grid_spec = pltpu.PrefetchScalarGridSpec(
        num_scalar_prefetch=4,
        grid=(G, qi.shape[0]),
        in_specs=[pl.BlockSpec((tq, hp * dh), lambda h, p, qi, kj, a, b: (qi[p], h)),
                  pl.BlockSpec((tq, hp * dh), lambda h, p, qi, kj, a, b: (kj[p], G + h)),
                  pl.BlockSpec((hp * dh, tq), lambda h, p, qi, kj, a, b: (h, kj[p])),
                  pl.BlockSpec((LANES, hp * dh), lambda h, p, qi, kj, a, b: (0, h)),
                  pl.BlockSpec((LANES, hp * dh), lambda h, p, qi, kj, a, b: (0, h)),
                  pl.BlockSpec((1, tq), lambda h, p, qi, kj, a, b: (0, qi[p])),
                  pl.BlockSpec((tq, 1), lambda h, p, qi, kj, a, b: (kj[p], 0)),
                  pl.BlockSpec((None, hp, LUT_SIZE), lambda h, p, qi, kj, a, b: (h, 0, 0))],
        out_specs=pl.BlockSpec((tq, hp * dh), lambda h, p, qi, kj, a, b: (qi[p], h)),
        scratch_shapes=[pltpu.VMEM((hp, 1, tq), F32), pltpu.VMEM((hp, 1, tq), F32),
                        pltpu.VMEM((hp, dh, tq), F32), pltpu.VMEM((hp, LANES, tq), F32),
                        pltpu.VMEM((hp, ck, tq), F32)],
    )
    return pl.pallas_call(
        kern,
        out_shape=jax.ShapeDtypeStruct((S, D), BF16),
        grid_spec=grid_spec,
        compiler_params=_cparams("parallel", "arbitrary"),
        name="moba_attention",
    )(qi, kj, pminq, pmaxk, qkv, qkv, vt, km_hi, km_lo, pos.reshape(1, S), pos.reshape(S, 1),
      lut.reshape(G, hp, LUT_SIZE))


def _rope_fold(y):
    lane = lax.broadcasted_iota(jnp.int32, y.shape, 1)
    return jnp.where(lane < MLA_ROPE, y + pltpu.roll(y, MLA_ROPE, 1), 0.0)


def _mla_q_kernel(cq_ref, g_ref, w_ref, kr_ref, t_ref, q_ref, kro_ref, cqn_ref, *, scale, hp):
    t = t_ref[...]
    W = 2 * LANES

    @pl.when(pl.program_id(1) == 0)
    def _():
        cqn_ref[...] = _rms_rows(cq_ref[...], g_ref[...]).astype(BF16)
        kro_ref[...] = _rope_fold(kr_ref[...] * t).astype(kro_ref.dtype)

    qh = jnp.dot(cqn_ref[...], w_ref[...], preferred_element_type=F32)
    for hd in range(hp):
        q_ref[:, hd * W:hd * W + MLA_NOPE] = (qh[:, hd * W:hd * W + MLA_NOPE] * scale).astype(q_ref.dtype)
        q_ref[:, hd * W + MLA_NOPE:(hd + 1) * W] = (
            _rope_fold(qh[:, hd * W + MLA_NOPE:(hd + 1) * W] * t) * scale).astype(q_ref.dtype)


def mla_queries_and_rope_key(down, g_q, w_uq_ext, rope_tab, *, scale, tm=1024, heads_per_step=4):
    S = down.shape[0]
    H, hp = MLA_HEADS, heads_per_step
    tm = _tile(S, tm)
    W = 2 * LANES
    return pl.pallas_call(
        functools.partial(_mla_q_kernel, scale=scale, hp=hp),
        out_shape=[jax.ShapeDtypeStruct((S, H * W), BF16), jax.ShapeDtypeStruct((S, LANES), BF16)],
        grid=(S // tm, H // hp),
        in_specs=[pl.BlockSpec((tm, MLA_Q_RANK), lambda i, h: (i, 0)),
                  pl.BlockSpec((1, MLA_Q_RANK), lambda i, h: (0, 0)),
                  pl.BlockSpec((MLA_Q_RANK, hp * W), lambda i, h: (0, h)),
                  pl.BlockSpec((tm, LANES), lambda i, h: (i, (MLA_Q_RANK + MLA_KV_RANK) // LANES)),
                  pl.BlockSpec((tm, LANES), lambda i, h: (i, 0))],
        out_specs=[pl.BlockSpec((tm, hp * W), lambda i, h: (i, h)),
                   pl.BlockSpec((tm, LANES), lambda i, h: (i, 0))],
        scratch_shapes=[pltpu.VMEM((tm, MLA_Q_RANK), BF16)],
        compiler_params=_cparams("parallel", "arbitrary"),
        name="mla_queries",
    )(down, g_q.reshape(1, MLA_Q_RANK), w_uq_ext, down, rope_tab)


def _mla_attn_kernel(qi_ref, kj_ref, q_ref, kv_ref, kr_ref, vt_ref, o_ref, m_ref, l_ref, acc_ref, *, tq, ck, hp):
    p = pl.program_id(1)
    i, j = qi_ref[p], kj_ref[p]
    nc = tq // ck
    W = 2 * LANES

    @pl.when(j == 0)
    def _():
        m_ref[...] = jnp.full(m_ref.shape, NEG_INF, F32)
        l_ref[...] = jnp.zeros(l_ref.shape, F32)
        acc_ref[...] = jnp.zeros(acc_ref.shape, F32)

    def scores(c, g, q0, diag):
        keys = slice(c * ck, (c + 1) * ck)
        kc = jnp.concatenate([kv_ref[keys, g * MLA_NOPE:(g + 1) * MLA_NOPE], kr_ref[keys, :]], axis=1)
        st = _nt_dot(kc, q_ref[q0:tq, g * W:(g + 1) * W])
        if diag:
            st = jnp.where(_causal_t(c * ck, ck, q0, tq - q0), st, NEG_INF)
        return st

    def consume(c, g, q0, diag, st):
        _online_softmax_step_t(st, vt_ref[g * MLA_V:(g + 1) * MLA_V, c * ck:(c + 1) * ck],
                               m_ref.at[g], l_ref.at[g], acc_ref.at[g], slice(q0, tq))

    def sweep(diag):
        _pipelined([(c, g, c * ck if diag else 0, diag) for c in range(nc) for g in range(hp)], scores, consume)

    @pl.when(j < i)
    def _():
        sweep(False)

    @pl.when(j == i)
    def _():
        sweep(True)
        for g in range(hp):
            o_ref[:, g * MLA_V:(g + 1) * MLA_V] = (acc_ref[g] / l_ref[g]).T.astype(o_ref.dtype)


def mla_attention(q, kv, kr, vt, *, tile=1024, chunk=512, heads_per_step=2):
    S = q.shape[0]
    H, hp = MLA_HEADS, heads_per_step
    tq = _tile(S, tile)
    ck = _tile(tq, chunk)
    qi, kj = _pair_tables(S // tq)
    W = 2 * LANES
    grid_spec = pltpu.PrefetchScalarGridSpec(
        num_scalar_prefetch=2,
        grid=(H // hp, qi.shape[0]),
        in_specs=[pl.BlockSpec((tq, hp * W), lambda h, p, qi, kj: (qi[p], h)),
                  pl.BlockSpec((tq, hp * MLA_NOPE), lambda h, p, qi, kj: (kj[p], h)),
                  pl.BlockSpec((tq, LANES), lambda h, p, qi, kj: (kj[p], 0)),
                  pl.BlockSpec((hp * MLA_V, tq), lambda h, p, qi, kj: (h, kj[p]))],
        out_specs=pl.BlockSpec((tq, hp * MLA_V), lambda h, p, qi, kj: (qi[p], h)),
        scratch_shapes=[pltpu.VMEM((hp, 1, tq), F32), pltpu.VMEM((hp, 1, tq), F32),
                        pltpu.VMEM((hp, MLA_V, tq), F32)],
    )
    return pl.pallas_call(
        functools.partial(_mla_attn_kernel, tq=tq, ck=ck, hp=hp),
        out_shape=jax.ShapeDtypeStruct((S, H * MLA_V), BF16),
        grid_spec=grid_spec,
        compiler_params=_cparams("parallel", "arbitrary"),
        name="mla_attention",
    )(qi, kj, q, kv, kr, vt)


_HI16 = 0xFFFF0000


def _pack_bf16_pairs(hi_f32):
    half = hi_f32.shape[1] // 2
    bits = pltpu.bitcast(hi_f32, jnp.uint32)
    return lax.shift_right_logical(bits[:, :half], jnp.uint32(16)) | (bits[:, half:] & jnp.uint32(_HI16))


def _unpack_bf16_pairs(words):
    lo = pltpu.bitcast(lax.shift_left(words, jnp.uint32(16)), F32).astype(BF16)
    hi = pltpu.bitcast(words & jnp.uint32(_HI16), F32).astype(BF16)
    return jnp.concatenate([lo, hi], axis=1)


def _router_kernel(x_ref, g_ref, wh_ref, wl_ref, xn_ref, r_ref):
    xn = _rms_rows(x_ref[...], g_ref[...])
    hi = xn.astype(BF16)
    hi_f32 = hi.astype(F32)
    lo = (xn - hi_f32).astype(BF16)
    xn_ref[...] = _pack_bf16_pairs(hi_f32)
    logits = (jnp.dot(hi, wh_ref[...], preferred_element_type=F32)
              + jnp.dot(hi, wl_ref[...], preferred_element_type=F32)
              + jnp.dot(lo, wh_ref[...], preferred_element_type=F32))
    lane = lax.broadcasted_iota(jnp.int32, logits.shape, 1)
    lane_f = lane.astype(F32)
    g = jnp.where(lane < N_EXPERTS, logits, NEG_INF)
    v1 = jnp.max(g, axis=-1, keepdims=True)
    i1 = jnp.min(jnp.where(g == v1, lane_f, float(LANES)), axis=-1, keepdims=True)
    g = jnp.where(lane_f == i1, NEG_INF, g)
    v2 = jnp.max(g, axis=-1, keepdims=True)
    i2 = jnp.min(jnp.where(g == v2, lane_f, float(LANES)), axis=-1, keepdims=True)
    g2 = 1.0 / (1.0 + jnp.exp(v1 - v2))
    g1 = 1.0 - g2
    r_ref[...] = jnp.where(lane == 0, i1,
                           jnp.where(lane == 1, i2,
                                     jnp.where(lane == 2, g1, jnp.where(lane == 3, g2, 0.0))))


def moe_route(x, g, w_router, *, tm=512):
    M, D = x.shape
    tm = _tile(M, tm)
    wr = jnp.pad(w_router.astype(F32), ((0, 0), (0, LANES - N_EXPERTS)))
    wh = wr.astype(BF16)
    wl = (wr - wh.astype(F32)).astype(BF16)
    return pl.pallas_call(
        _router_kernel,
        out_shape=[jax.ShapeDtypeStruct((M, D // 2), jnp.uint32), jax.ShapeDtypeStruct((M, LANES), F32)],
        grid=(M // tm,),
        in_specs=[pl.BlockSpec((tm, D), lambda i: (i, 0)),
                  pl.BlockSpec((1, D), lambda i: (0, 0)),
                  pl.BlockSpec((D, LANES), lambda i: (0, 0)),
                  pl.BlockSpec((D, LANES), lambda i: (0, 0))],
        out_specs=[pl.BlockSpec((tm, D // 2), lambda i: (i, 0)),
                   pl.BlockSpec((tm, LANES), lambda i: (i, 0))],
        compiler_params=_cparams("parallel"),
        name="moe_route",
    )(x, g.reshape(1, D), wh, wl)


def _new_expert(te_ref, t):
    return jnp.logical_or(t == 0, te_ref[t] != te_ref[jnp.maximum(t - 1, 0)])


def _stream_expert_weights(te_ref, rid_ref, rexp_ref, nr_ref, used, n_col_tiles, copies_for, on_ready):
    j, t = pl.program_id(0), pl.program_id(1)

    @pl.when(jnp.logical_and(used, _new_expert(te_ref, t)))
    def _():
        n_runs = nr_ref[0]
        r = rid_ref[t]
        g = j * n_runs + r
        slot = lax.rem(g, 2)

        @pl.when(g == 0)
        def _():
            for c in copies_for(j, te_ref[t], slot):
                c.start()

        for c in copies_for(j, te_ref[t], slot):
            c.wait()
        on_ready(slot)
        last = r + 1 == n_runs
        nxt_r = jnp.where(last, 0, r + 1)
        nxt_j = jnp.where(last, j + 1, j)

        @pl.when(nxt_j < n_col_tiles)
        def _():
            for c in copies_for(nxt_j, rexp_ref[nxt_r], 1 - slot):
                c.start()


def _gmm_swiglu_kernel(te_ref, nu_ref, rid_ref, rexp_ref, nr_ref, *refs, n_parts, tiles_per_part, layer, nj, tn):
    x_refs = refs[:n_parts]
    w_hbm, o_ref, wbuf_ref, wgb_ref, wub_ref, sem = refs[n_parts:]
    t = pl.program_id(1)
    used = t < nu_ref[0]

    def copies_for(jj, e, slot):
        gate_cols = pl.ds(pl.multiple_of(jj * tn, tn), tn)
        up_cols = pl.ds(pl.multiple_of((nj + jj) * tn, tn), tn)
        return (pltpu.make_async_copy(w_hbm.at[layer, e, :, gate_cols], wbuf_ref.at[slot, 0], sem.at[slot, 0]),
                pltpu.make_async_copy(w_hbm.at[layer, e, :, up_cols], wbuf_ref.at[slot, 1], sem.at[slot, 1]))

    def on_ready(slot):
        wgb_ref[...] = wbuf_ref[slot, 0].astype(BF16)
        wub_ref[...] = wbuf_ref[slot, 1].astype(BF16)

    _stream_expert_weights(te_ref, rid_ref, rexp_ref, nr_ref, used, nj, copies_for, on_ready)

    for c in range(n_parts):
        @pl.when(jnp.logical_and(used, t // tiles_per_part == c))
        def _(c=c):
            x = _unpack_bf16_pairs(x_refs[c][...])
            a = jnp.dot(x, wgb_ref[...], preferred_element_type=F32)
            u = jnp.dot(x, wub_ref[...], preferred_element_type=F32)
            o_ref[...] = (a * jax.nn.sigmoid(a) * u).astype(o_ref.dtype)

    @pl.when(jnp.logical_not(used))
    def _():
        o_ref[...] = jnp.zeros(o_ref.shape, o_ref.dtype)


def _gmm_out_kernel(te_ref, nu_ref, rid_ref, rexp_ref, nr_ref, h_ref, w_hbm, o_ref, wbuf_ref, wb_ref, sem, *,
                    layer, nj, tn):
    t = pl.program_id(1)
    used = t < nu_ref[0]

    def copies_for(jj, e, slot):
        cols = pl.ds(pl.multiple_of(jj * tn, tn), tn)
        return (pltpu.make_async_copy(w_hbm.at[layer, e, :, cols], wbuf_ref.at[slot], sem.at[slot]),)

    def on_ready(slot):
        wb_ref[...] = wbuf_ref[slot].astype(BF16)

    _stream_expert_weights(te_ref, rid_ref, rexp_ref, nr_ref, used, nj, copies_for, on_ready)

    @pl.when(used)
    def _():
        o_ref[...] = jnp.dot(h_ref[...], wb_ref[...], preferred_element_type=F32)

    @pl.when(jnp.logical_not(used))
    def _():
        o_ref[...] = jnp.zeros(o_ref.shape, o_ref.dtype)


def moe_experts(xs, tile_expert, n_used, runs, w_in, w_out, layer, *, tm, tn=512):
    n_parts = len(xs)
    P = n_parts * xs[0].shape[0]
    D = w_in.shape[2]
    F = w_in.shape[3] // 2
    nt = P // tm
    ntp = nt // n_parts
    assert ntp * n_parts == nt
    tn1, tn2 = _tile(F, tn), _tile(D, tn)
    nj1, nj2 = F // tn1, D // tn2

    def part_spec(c):
        return pl.BlockSpec((tm, D // 2), lambda j, t, *_: (jnp.clip(t - c * ntp, 0, ntp - 1), 0))

    prefetch = (tile_expert, n_used) + tuple(runs)
    hbm = pl.BlockSpec(memory_space=pl.ANY)
    h = pl.pallas_call(
        functools.partial(_gmm_swiglu_kernel, n_parts=n_parts, tiles_per_part=ntp, layer=layer, nj=nj1, tn=tn1),
        out_shape=jax.ShapeDtypeStruct((P, F), BF16),
        grid_spec=pltpu.PrefetchScalarGridSpec(
            num_scalar_prefetch=len(prefetch),
            grid=(nj1, nt),
            in_specs=[part_spec(c) for c in range(n_parts)] + [hbm],
            out_specs=pl.BlockSpec((tm, tn1), lambda j, t, *_: (t, j)),
            scratch_shapes=[pltpu.VMEM((2, 2, D, tn1), F32), pltpu.VMEM((D, tn1), BF16),
                            pltpu.VMEM((D, tn1), BF16), pltpu.SemaphoreType.DMA((2, 2))],
        ),
        compiler_params=_cparams("arbitrary", "arbitrary"),
        name="moe_swiglu_in",
    )(*prefetch, *xs, w_in)
    return pl.pallas_call(
        functools.partial(_gmm_out_kernel, layer=layer, nj=nj2, tn=tn2),
        out_shape=jax.ShapeDtypeStruct((P, D), F32),
        grid_spec=pltpu.PrefetchScalarGridSpec(
            num_scalar_prefetch=len(prefetch),
            grid=(nj2, nt),
            in_specs=[pl.BlockSpec((tm, F), lambda j, t, *_: (t, 0)), hbm],
            out_specs=pl.BlockSpec((tm, tn2), lambda j, t, *_: (t, j)),
            scratch_shapes=[pltpu.VMEM((2, F, tn2), F32), pltpu.VMEM((F, tn2), BF16),
                            pltpu.SemaphoreType.DMA((2,))],
        ),
        compiler_params=_cparams("arbitrary", "arbitrary"),
        name="moe_out",
    )(*prefetch, h, w_out)


def moe_swiglu(x, g, w_router, w_in, w_out, layer, *, tm=512):
    S, D = x.shape
    E = N_EXPERTS
    tm = _tile(S, tm)
    xn, route = moe_route(x, g, w_router)
    experts = route[:, :2].astype(jnp.int32)
    gates = route[:, 2:4]
    flat_e = experts.reshape(-1)
    order = jnp.argsort(flat_e, stable=True).astype(jnp.int32)
    rank = jnp.argsort(order).astype(jnp.int32)
    counts = jnp.bincount(flat_e, length=E).astype(jnp.int32)
    padded = ((counts + tm - 1) // tm) * tm
    start = jnp.cumsum(counts) - counts
    pstart = jnp.cumsum(padded) - padded
    pend = jnp.cumsum(padded)
    P = 2 * S + E * tm
    nt = P // tm
    tile_expert = jnp.minimum(
        jnp.searchsorted(pend, jnp.arange(nt, dtype=jnp.int32) * tm, side="right"), E - 1).astype(jnp.int32)
    n_used = (pend[-1:] // tm).astype(jnp.int32)
    row_e = jnp.repeat(tile_expert, tm)
    row_rank = jnp.arange(P, dtype=jnp.int32) - pstart[row_e]
    row_valid = jnp.logical_and(row_rank < counts[row_e], jnp.arange(P) < pend[-1])
    row_token = jnp.where(row_valid, order[jnp.clip(start[row_e] + row_rank, 0, 2 * S - 1)] // 2, 0)
    slot = pstart[flat_e] + rank - start[flat_e]
    tile_expert = jnp.where(jnp.arange(nt) < n_used[0], tile_expert, tile_expert[jnp.maximum(n_used[0] - 1, 0)])
    slot = slot.reshape(S, 2)
    present = counts > 0
    run_of_expert = jnp.cumsum(present.astype(jnp.int32)) - 1
    runs = (run_of_expert[tile_expert].astype(jnp.int32),
            jnp.sort(jnp.where(present, jnp.arange(E, dtype=jnp.int32), E))[:E].clip(0, E - 1).astype(jnp.int32),
            jnp.sum(present).astype(jnp.int32).reshape(1))
    xs = [jnp.take(xn, part, axis=0, mode="clip") for part in jnp.split(row_token, MOE_GATHER_PARTS)]
    ys = moe_experts(xs, tile_expert, n_used, runs, w_in, w_out, layer, tm=tm)
    return jnp.take(ys, slot[:, 0], axis=0, mode="clip"), jnp.take(ys, slot[:, 1], axis=0, mode="clip"), gates


def _diff_lambda_init(layer):
    return 0.8 - 0.6 * math.exp(-0.3 * layer)


def _rope_table(pos):
    half = MLA_ROPE // 2
    inv_freq = ROPE_THETA ** (-jnp.arange(half, dtype=F32) / half)
    ang = pos.astype(F32)[:, None] * inv_freq
    cos, sin = jnp.cos(ang), jnp.sin(ang)
    return jnp.concatenate([cos, cos, -sin, sin], axis=1)


def _rot_half_cols(w):
    half = w.shape[-1] // 2
    return jnp.concatenate([w[..., half:], w[..., :half]], axis=-1)


def kernel(x, p, positions, rel_bias, norm_mix, norm_ffn, norm_ple, norm_final, diff_w_qkv, diff_lambda, diff_subln, diff_w_o, moba_w_qkv, moba_w_o, mla_w_down, mla_g_q, mla_w_uq, mla_g_kv, mla_w_ukv, mla_w_o, ffn_w_in, ffn_w_out, moe_w_router, moe_w_in, moe_w_out, ple_w_gate, ple_w_proj):
    B, S, D = x.shape
    depth = p.shape[0]
    lut = _bias_lut(rel_bias)
    outs = []
    for b in range(B):
        xb = x[b]
        pos = positions[b].astype(jnp.int32)
        for i in range(depth):
            jm = i // N_MIXERS
            if i % N_MIXERS == 0:
                qk, vt = norm_matmul_vt(xb, norm_mix[i], diff_w_qkv, layer=jm, n_direct=2 * D,
                                        scaled_cols=D, scale=DIFF_HEAD_DIM ** -0.5 * LOG2E)
                o = diff_attention(qk, vt, pos, lut, diff_lambda[jm].astype(F32), diff_subln[jm],
                                   _diff_lambda_init(i))
                xb = matmul_residual(o, diff_w_o, xb, layer=jm)
            elif i % N_MIXERS == 1:
                qk, vt = norm_matmul_vt(xb, norm_mix[i], moba_w_qkv, layer=jm, n_direct=2 * D,
                                        scaled_cols=D, scale=MOBA_HEAD_DIM ** -0.5 * LOG2E)
                o = moba_attention(qk, vt, pos, lut)
                xb = matmul_residual(o, moba_w_o, xb, layer=jm)
            else:
                wd = mla_w_down[jm]
                kr0 = MLA_Q_RANK + MLA_KV_RANK
                wd_ext = jnp.concatenate([wd, _rot_half_cols(wd[:, kr0:])], axis=1).astype(BF16)
                wq = mla_w_uq[jm].reshape(MLA_Q_RANK, MLA_HEADS, MLA_NOPE + MLA_ROPE)
                wq_ext = jnp.concatenate([wq, _rot_half_cols(wq[..., MLA_NOPE:])], axis=-1)
                wq_ext = wq_ext.reshape(MLA_Q_RANK, MLA_HEADS * 2 * LANES).astype(BF16)
                down = norm_matmul(xb, norm_mix[i], wd_ext, out_dtype=F32, tn=wd_ext.shape[1])
                q, kr = mla_queries_and_rope_key(down, mla_g_q[jm], wq_ext, _rope_table(pos),
                                                 scale=(MLA_NOPE + MLA_ROPE) ** -0.5 * LOG2E)
                wkv = mla_w_ukv[jm].reshape(MLA_KV_RANK, MLA_HEADS, 2, MLA_V).transpose(0, 2, 1, 3)
                wkv = wkv.reshape(MLA_KV_RANK, 2 * MLA_HEADS * MLA_V).astype(BF16)
                kn, vt = norm_matmul_vt(down, mla_g_kv[jm], wkv, n_direct=MLA_HEADS * MLA_NOPE, xcol=1)
                o = mla_attention(q, kn, kr, vt)
                xb = matmul_residual(o, mla_w_o, xb, layer=jm)
            if i % 2 == 0:
                h = norm_swiglu_in(xb, norm_ffn[i], ffn_w_in, layer=i // 2)
                xb = matmul_residual(h, ffn_w_out[i // 2].astype(BF16), xb)
                xb = ple_update(xb, norm_ple[i], p[i, b], ple_w_gate, ple_w_proj, layer=i)
            else:
                y1, y2, gates = moe_swiglu(xb, norm_ffn[i], moe_w_router[i // 2], moe_w_in, moe_w_out, i // 2)
                xb = ple_update_moe(xb, y1, y2, gates, norm_ple[i], p[i, b], ple_w_gate, ple_w_proj, layer=i)
        outs.append(final_norm(xb, norm_final))
    return jnp.stack(outs, axis=0)
```

```python
import functools
import math

import numpy as np
import jax
import jax.numpy as jnp
from jax import lax
from jax.experimental import pallas as pl
from jax.experimental.pallas import tpu as pltpu

F32 = jnp.float32
BF16 = jnp.bfloat16

NORM_EPS = 1e-6
NEG_INF = -1e30
LOG2E = math.log2(math.e)
LANES = 128
VMEM_LIMIT_BYTES = 56 * 1024 * 1024

REL_BUCKETS = 32
REL_MAX_DIST = 128
LUT_SIZE = LANES
BIAS_BLOCK = 256
DIFF_HEADS = 8
DIFF_HEAD_DIM = 128
MOBA_HEADS = 16
MOBA_HEAD_DIM = 128
MOBA_BLOCK = 256
MOBA_TOPK = 3
MLA_HEADS = 16
MLA_Q_RANK = 512
MLA_KV_RANK = 512
MLA_NOPE = 128
MLA_ROPE = 64
MLA_V = 128
ROPE_THETA = 10000.0
N_EXPERTS = 8
N_MIXERS = 3
MOE_GATHER_PARTS = 4
MOE_COL_TILE = 512


def _cparams(*sem):
    return pltpu.CompilerParams(dimension_semantics=sem, vmem_limit_bytes=VMEM_LIMIT_BYTES)


def _tile(n, pref):
    if n <= pref:
        return n
    t = pref
    while n % t:
        t //= 2
    return t


def _weight_spec(w, layer, tn, col=lambda j: j):
    K = w.shape[-2]
    if w.ndim == 2:
        return pl.BlockSpec((K, tn), lambda i, j: (0, col(j)))
    return pl.BlockSpec((None, K, tn), lambda i, j: (layer, 0, col(j)))


def _rms_rows(x, g):
    r = lax.rsqrt(jnp.mean(x * x, axis=-1, keepdims=True) + NORM_EPS)
    return x * r * g


def _norm_mm_kernel(x_ref, g_ref, w_ref, o_ref, xn_ref, *, scaled_tiles, scale):
    j = pl.program_id(1)

    @pl.when(j == 0)
    def _():
        xn_ref[...] = _rms_rows(x_ref[...], g_ref[...]).astype(BF16)

    acc = jnp.dot(xn_ref[...], w_ref[...].astype(BF16), preferred_element_type=F32)
    if scaled_tiles:
        acc = acc * jnp.where(j < scaled_tiles, scale, 1.0)
    o_ref[...] = acc.astype(o_ref.dtype)


def norm_matmul(x, g, w, *, out_dtype, xcol=0, tm=1024, tn=512, scaled_cols=0, scale=1.0):
    M = x.shape[0]
    K, N = w.shape
    tm, tn = _tile(M, tm), _tile(N, tn)
    assert scaled_cols % tn == 0
    kern = functools.partial(_norm_mm_kernel, scaled_tiles=scaled_cols // tn, scale=scale)
    return pl.pallas_call(
        kern,
        out_shape=jax.ShapeDtypeStruct((M, N), out_dtype),
        grid=(M // tm, N // tn),
        in_specs=[pl.BlockSpec((tm, K), lambda i, j: (i, xcol)),
                  pl.BlockSpec((1, K), lambda i, j: (0, 0)),
                  pl.BlockSpec((K, tn), lambda i, j: (0, j))],
        out_specs=pl.BlockSpec((tm, tn), lambda i, j: (i, j)),
        scratch_shapes=[pltpu.VMEM((tm, K), BF16)],
        compiler_params=_cparams("parallel", "arbitrary"),
        name="norm_matmul",
    )(x, g.reshape(1, K), w)


def _norm_mm_vt_kernel(x_ref, g_ref, w_ref, o_ref, ot_ref, xn_ref, *, direct_tiles, scaled_tiles, scale):
    j = pl.program_id(1)

    @pl.when(j == 0)
    def _():
        xn_ref[...] = _rms_rows(x_ref[...], g_ref[...]).astype(BF16)

    acc = jnp.dot(xn_ref[...], w_ref[...].astype(BF16), preferred_element_type=F32)

    @pl.when(j < direct_tiles)
    def _():
        out = acc * jnp.where(j < scaled_tiles, scale, 1.0) if scaled_tiles else acc
        o_ref[...] = out.astype(o_ref.dtype)

    @pl.when(j >= direct_tiles)
    def _():
        ot_ref[...] = acc.T.astype(ot_ref.dtype)


def norm_matmul_vt(x, g, w, *, n_direct, layer=None, xcol=0, tm=1024, tn=512, scaled_cols=0, scale=1.0):
    M = x.shape[0]
    K, N = w.shape[-2:]
    tm, tn = _tile(M, tm), _tile(N, tn)
    assert scaled_cols % tn == 0 and n_direct % tn == 0 and 0 < n_direct < N
    nd = n_direct // tn
    kern = functools.partial(_norm_mm_vt_kernel, direct_tiles=nd, scaled_tiles=scaled_cols // tn, scale=scale)
    return pl.pallas_call(
        kern,
        out_shape=[jax.ShapeDtypeStruct((M, n_direct), BF16), jax.ShapeDtypeStruct((N - n_direct, M), BF16)],
        grid=(M // tm, N // tn),
        in_specs=[pl.BlockSpec((tm, K), lambda i, j: (i, xcol)),
                  pl.BlockSpec((1, K), lambda i, j: (0, 0)),
                  _weight_spec(w, layer, tn)],
        out_specs=[pl.BlockSpec((tm, tn), lambda i, j: (i, jnp.minimum(j, nd - 1))),
                   pl.BlockSpec((tn, tm), lambda i, j: (jnp.maximum(j - nd, 0), i))],
        scratch_shapes=[pltpu.VMEM((tm, K), BF16)],
        compiler_params=_cparams("parallel", "arbitrary"),
        name="norm_matmul_vt",
    )(x, g.reshape(1, K), w)


def _norm_swiglu_kernel(x_ref, g_ref, wg_ref, wu_ref, o_ref, xn_ref):
    @pl.when(pl.program_id(1) == 0)
    def _():
        xn_ref[...] = _rms_rows(x_ref[...], g_ref[...]).astype(BF16)

    xn = xn_ref[...]
    a = jnp.dot(xn, wg_ref[...].astype(BF16), preferred_element_type=F32)
    u = jnp.dot(xn, wu_ref[...].astype(BF16), preferred_element_type=F32)
    o_ref[...] = (a * jax.nn.sigmoid(a) * u).astype(o_ref.dtype)


def norm_swiglu_in(x, g, w_in, *, layer=None, tm=1024, tn=512):
    M, K = x.shape
    F = w_in.shape[-1] // 2
    tm, tn = _tile(M, tm), _tile(F, tn)
    nj = F // tn
    return pl.pallas_call(
        _norm_swiglu_kernel,
        out_shape=jax.ShapeDtypeStruct((M, F), BF16),
        grid=(M // tm, nj),
        in_specs=[pl.BlockSpec((tm, K), lambda i, j: (i, 0)),
                  pl.BlockSpec((1, K), lambda i, j: (0, 0)),
                  _weight_spec(w_in, layer, tn),
                  _weight_spec(w_in, layer, tn, lambda j: j + nj)],
        out_specs=pl.BlockSpec((tm, tn), lambda i, j: (i, j)),
        scratch_shapes=[pltpu.VMEM((tm, K), BF16)],
        compiler_params=_cparams("parallel", "arbitrary"),
        name="norm_swiglu_in",
    )(x, g.reshape(1, K), w_in, w_in)


def _mm_res_kernel(a_ref, w_ref, r_ref, o_ref):
    o_ref[...] = r_ref[...] + jnp.dot(a_ref[...], w_ref[...].astype(BF16), preferred_element_type=F32)


def matmul_residual(a, w, res, *, layer=None, tm=1024, tn=512):
    M, K = a.shape
    N = w.shape[-1]
    tm, tn = _tile(M, tm), _tile(N, tn)
    return pl.pallas_call(
        _mm_res_kernel,
        out_shape=jax.ShapeDtypeStruct((M, N), F32),
        grid=(M // tm, N // tn),
        in_specs=[pl.BlockSpec((tm, K), lambda i, j: (i, 0)),
                  _weight_spec(w, layer, tn),
                  pl.BlockSpec((tm, tn), lambda i, j: (i, j))],
        out_specs=pl.BlockSpec((tm, tn), lambda i, j: (i, j)),
        compiler_params=_cparams("parallel", "parallel"),
        name="matmul_residual",
    )(a, w, res)


def _ple_kernel(x_ref, g_ref, xr_ref, p_ref, wg_ref, wp_ref, o_ref, xn_ref):
    @pl.when(pl.program_id(1) == 0)
    def _():
        xn_ref[...] = _rms_rows(x_ref[...], g_ref[...]).astype(BF16)

    gate = jax.nn.sigmoid(jnp.dot(xn_ref[...], wg_ref[...].astype(BF16), preferred_element_type=F32))
    proj = jnp.dot(p_ref[...].astype(BF16), wp_ref[...].astype(BF16), preferred_element_type=F32)
    o_ref[...] = xr_ref[...] + gate * proj


def ple_update(x, g, p, w_gate, w_proj, *, layer=None, tm=1024, tn=512):
    M, D = x.shape
    P = p.shape[1]
    tm, tn = _tile(M, tm), _tile(D, tn)
    return pl.pallas_call(
        _ple_kernel,
        out_shape=jax.ShapeDtypeStruct((M, D), F32),
        grid=(M // tm, D // tn),
        in_specs=[pl.BlockSpec((tm, D), lambda i, j: (i, 0)),
                  pl.BlockSpec((1, D), lambda i, j: (0, 0)),
                  pl.BlockSpec((tm, tn), lambda i, j: (i, j)),
                  pl.BlockSpec((tm, P), lambda i, j: (i, 0)),
                  _weight_spec(w_gate, layer, tn),
                  _weight_spec(w_proj, layer, tn)],
        out_specs=pl.BlockSpec((tm, tn), lambda i, j: (i, j)),
        scratch_shapes=[pltpu.VMEM((tm, D), BF16)],
        compiler_params=_cparams("parallel", "arbitrary"),
        name="ple_update",
    )(x, g.reshape(1, D), x, p, w_gate, w_proj)


def _ple_moe_kernel(x_ref, y1_ref, y2_ref, gt_ref, g_ref, p_ref, wg_ref, wp_ref, o_ref, xs_ref, xn_ref, *, tn,
                    pair_tile):
    j = pl.program_id(1)

    @pl.when(j == 0)
    def _():
        gt = gt_ref[...]
        xnew = (x_ref[...] + gt[:, 0:1] * _unpack_pairs_by_tile(y1_ref[...], pair_tile)
                + gt[:, 1:2] * _unpack_pairs_by_tile(y2_ref[...], pair_tile))
        xn_ref[...] = _rms_rows(xnew, g_ref[...]).astype(BF16)
        for jj in range(xs_ref.shape[0]):
            xs_ref[jj] = xnew[:, jj * tn:(jj + 1) * tn]

    gate = jax.nn.sigmoid(jnp.dot(xn_ref[...], wg_ref[...].astype(BF16), preferred_element_type=F32))
    proj = jnp.dot(p_ref[...].astype(BF16), wp_ref[...].astype(BF16), preferred_element_type=F32)
    o_ref[...] = xs_ref[j] + gate * proj


def ple_update_moe(x, y1, y2, gates, g, p, w_gate, w_proj, *, layer=None, tm=512, tn=512):
    M, D = x.shape
    P = p.shape[1]
    tm, tn = _tile(M, tm), _tile(D, tn)
    row = pl.BlockSpec((tm, D), lambda i, j: (i, 0))
    packed_row = pl.BlockSpec((tm, D // 2), lambda i, j: (i, 0))
    return pl.pallas_call(
        functools.partial(_ple_moe_kernel, tn=tn, pair_tile=_tile(D, MOE_COL_TILE)),
        out_shape=jax.ShapeDtypeStruct((M, D), F32),
        grid=(M // tm, D // tn),
        in_specs=[row, packed_row, packed_row,
                  pl.BlockSpec((tm, 2), lambda i, j: (i, 0)),
                  pl.BlockSpec((1, D), lambda i, j: (0, 0)),
                  pl.BlockSpec((tm, P), lambda i, j: (i, 0)),
                  _weight_spec(w_gate, layer, tn),
                  _weight_spec(w_proj, layer, tn)],
        out_specs=pl.BlockSpec((tm, tn), lambda i, j: (i, j)),
        scratch_shapes=[pltpu.VMEM((D // tn, tm, tn), F32), pltpu.VMEM((tm, D), BF16)],
        compiler_params=_cparams("parallel", "arbitrary"),
        name="ple_update_moe",
    )(x, y1, y2, gates, g.reshape(1, D), p, w_gate, w_proj)


def _final_norm_kernel(x_ref, g_ref, o_ref):
    o_ref[...] = _rms_rows(x_ref[...], g_ref[...])


def final_norm(x, g, *, tm=512):
    M, D = x.shape
    tm = _tile(M, tm)
    return pl.pallas_call(
        _final_norm_kernel,
        out_shape=jax.ShapeDtypeStruct((M, D), F32),
        grid=(M // tm,),
        in_specs=[pl.BlockSpec((tm, D), lambda i: (i, 0)),
                  pl.BlockSpec((1, D), lambda i: (0, 0))],
        out_specs=pl.BlockSpec((tm, D), lambda i: (i, 0)),
        compiler_params=_cparams("parallel"),
        name="final_norm",
    )(x, g.reshape(1, D))


def _pair_tables(n_tiles):
    qi, kj = [], []
    for i in range(n_tiles):
        for j in range(i + 1):
            qi.append(i)
            kj.append(j)
    return jnp.asarray(np.array(qi, np.int32)), jnp.asarray(np.array(kj, np.int32))


def _rel_bucket(dist):
    n = jnp.maximum(dist, 0)
    max_exact = REL_BUCKETS // 2
    nf = jnp.maximum(n, 1).astype(F32)
    large = max_exact + (jnp.log(nf / max_exact) / math.log(REL_MAX_DIST / max_exact)
                         * (REL_BUCKETS - max_exact)).astype(jnp.int32)
    large = jnp.minimum(large, REL_BUCKETS - 1)
    return jnp.where(n < max_exact, n, large)


def _bias_lut(rel_bias):
    buckets = _rel_bucket(jnp.arange(LUT_SIZE, dtype=jnp.int32))
    return rel_bias.astype(F32)[buckets].T * LOG2E


def _block_pos_bounds(pos, sb):
    blocks = pos.reshape(pos.shape[0] // sb, sb)
    return blocks.min(axis=1), blocks.max(axis=1)


def _lut_bias(lut_row, idx):
    R, C = idx.shape
    lut = jnp.broadcast_to(lut_row, (R, LANES))
    parts = [jnp.take_along_axis(lut, idx[:, c:c + LANES], axis=1, mode="promise_in_bounds")
             for c in range(0, C, LANES)]
    return parts[0] if len(parts) == 1 else jnp.concatenate(parts, axis=1)


def _pipelined(units, scores, consume):
    st = scores(*units[0])
    for k, u in enumerate(units):
        nxt = scores(*units[k + 1]) if k + 1 < len(units) else None
        consume(*u, st)
        st = nxt


def _online_softmax_step_t(st, vt, m_ref, l_ref, acc_ref, qcols, m_cur=None, offsets=None):
    m_prev = m_ref[:, qcols]
    if m_cur is None:
        m_cur = jnp.max(st, axis=0, keepdims=True)
    m_new = jnp.maximum(m_prev, m_cur)
    alpha = jnp.exp2(m_prev - m_new)
    if offsets is None:
        p = jnp.exp2(st - m_new)
    else:
        parts = [jnp.exp2(st[rows, :] - fn(m_new)) for rows, fn in offsets]
        p = parts[0] if len(parts) == 1 else jnp.concatenate(parts, axis=0)
    l_ref[:, qcols] = alpha * l_ref[:, qcols] + jnp.sum(p, axis=0, keepdims=True)
    acc_ref[:, qcols] = alpha * acc_ref[:, qcols] + jnp.dot(vt, p.astype(BF16), preferred_element_type=F32)
    m_ref[:, qcols] = m_new


def _causal_t(k0, nk, q0, nq):
    r = k0 + lax.broadcasted_iota(jnp.int32, (nk, nq), 0)
    c = q0 + lax.broadcasted_iota(jnp.int32, (nk, nq), 1)
    return r <= c


def _nt_dot(a, b):
    return lax.dot_general(a, b, (((1,), (1,)), ((), ())), preferred_element_type=F32)


def _diff_attn_kernel(qi_ref, kj_ref, pminq_ref, pmaxk_ref,
                      q_ref, k_ref, vt_ref, pq_ref, pk_ref, lut_ref, lam_ref, g_ref,
                      o_ref, m_ref, l_ref, acc_ref, st_ref, *, tq, ck, sb, lambda_init):
    p = pl.program_id(1)
    i, j = qi_ref[p], kj_ref[p]
    nc = tq // ck
    dh = DIFF_HEAD_DIM
    SB = sb
    nqb, nkb = tq // SB, ck // SB

    @pl.when(j == 0)
    def _():
        m_ref[...] = jnp.full(m_ref.shape, NEG_INF, F32)
        l_ref[...] = jnp.zeros(l_ref.shape, F32)
        acc_ref[...] = jnp.zeros(acc_ref.shape, F32)

    def near_sub(c, kb, qb):
        return pminq_ref[i * nqb + qb] - pmaxk_ref[(j * nc + c) * nkb + kb] < LUT_SIZE - 1

    def refine_chunk(c):
        keys = slice(c * ck, (c + 1) * ck)
        for mp in range(2):
            st_ref[mp] = _nt_dot(k_ref[keys, mp * dh:(mp + 1) * dh], q_ref[:, mp * dh:(mp + 1) * dh])
        for kb in range(nkb):
            for qb in range(nqb):
                @pl.when(near_sub(c, kb, qb))
                def _(kb=kb, qb=qb):
                    rows, cols = slice(kb * SB, (kb + 1) * SB), slice(qb * SB, (qb + 1) * SB)
                    idx = jnp.clip(pq_ref[:, cols] - pk_ref[c * ck + kb * SB:c * ck + (kb + 1) * SB, :],
                                   0, LUT_SIZE - 1)
                    for mp in range(2):
                        far_bias = lut_ref[mp:mp + 1, LUT_SIZE - 1:LUT_SIZE]
                        st_ref[mp, rows, cols] = (st_ref[mp, rows, cols]
                                                  + (_lut_bias(lut_ref[mp:mp + 1, :], idx) - far_bias))
        for mp in range(2):
            consume(c, mp, 0, "far", st_ref[mp])

    def scores(c, mp, q0, mode):
        keys = slice(c * ck, (c + 1) * ck)
        st = _nt_dot(k_ref[keys, mp * dh:(mp + 1) * dh], q_ref[q0:tq, mp * dh:(mp + 1) * dh])
        if mode != "far":
            idx = jnp.clip(pq_ref[:, q0:tq] - pk_ref[keys, :], 0, LUT_SIZE - 1)
            st = st + _lut_bias(lut_ref[mp:mp + 1, :], idx)
        if mode == "diag":
            st = jnp.where(_causal_t(c * ck, ck, q0, tq - q0), st, NEG_INF)
        return st

    def consume(c, mp, q0, mode, st):
        m_cur = offsets = None
        if mode == "far":
            bias = lut_ref[mp:mp + 1, LUT_SIZE - 1:LUT_SIZE]
            m_cur = jnp.max(st, axis=0, keepdims=True) + bias
            offsets = [(slice(None), lambda m_new: m_new - bias)]
        _online_softmax_step_t(st, vt_ref[:, c * ck:(c + 1) * ck], m_ref.at[mp], l_ref.at[mp], acc_ref.at[mp],
                               slice(q0, tq), m_cur, offsets)

    def sweep(chunks, mode):
        units = [(c, mp, c * ck if mode == "diag" else 0, mode) for c in chunks for mp in range(2)]
        _pipelined(units, scores, consume)

    @pl.when(j < i)
    def _():
        fars = [jnp.logical_not(functools.reduce(
            jnp.logical_or, [near_sub(c, kb, qb) for kb in range(nkb) for qb in range(nqb)])) for c in range(nc)]
        all_far = functools.reduce(jnp.logical_and, fars)
        pl.when(all_far)(functools.partial(sweep, range(nc), "far"))

        @pl.when(jnp.logical_not(all_far))
        def _():
            for c in range(nc):
                pl.when(fars[c])(functools.partial(sweep, [c], "far"))
                pl.when(jnp.logical_not(fars[c]))(functools.partial(refine_chunk, c))

    @pl.when(j == i)
    def _():
        sweep(range(nc), "diag")
        lam = lam_ref[...]
        lam_full = (jnp.exp(jnp.sum(lam[0:1] * lam[1:2], axis=-1, keepdims=True))
                    - jnp.exp(jnp.sum(lam[2:3] * lam[3:4], axis=-1, keepdims=True)) + lambda_init)
        a = acc_ref[0] / l_ref[0] - lam_full * (acc_ref[1] / l_ref[1])
        r = lax.rsqrt(jnp.mean(a * a, axis=0, keepdims=True) + NORM_EPS)
        y = a * r * (g_ref[...] * (1.0 - lambda_init))
        o_ref[...] = y.T.astype(o_ref.dtype)


def diff_attention(qkv, vt, pos, lut, lam, subln_g, lambda_init, *, tile=1024, chunk=512):
    S = qkv.shape[0]
    H, dh = DIFF_HEADS, DIFF_HEAD_DIM
    tq = _tile(S, tile)
    ck = _tile(tq, chunk)
    qi, kj = _pair_tables(S // tq)
    sb = _tile(ck, BIAS_BLOCK)
    pminq, pmaxk = _block_pos_bounds(pos, sb)
    kern = functools.partial(_diff_attn_kernel, tq=tq, ck=ck, sb=sb, lambda_init=lambda_init)
    grid_spec = pltpu.PrefetchScalarGridSpec(
        num_scalar_prefetch=4,
        grid=(H, qi.shape[0]),
        in_specs=[pl.BlockSpec((tq, 2 * dh), lambda h, p, qi, kj, a, b: (qi[p], h)),
                  pl.BlockSpec((tq, 2 * dh), lambda h, p, qi, kj, a, b: (kj[p], H + h)),
                  pl.BlockSpec((2 * dh, tq), lambda h, p, qi, kj, a, b: (h, kj[p])),
                  pl.BlockSpec((1, tq), lambda h, p, qi, kj, a, b: (0, qi[p])),
                  pl.BlockSpec((tq, 1), lambda h, p, qi, kj, a, b: (kj[p], 0)),
                  pl.BlockSpec((None, 2, LUT_SIZE), lambda h, p, qi, kj, a, b: (h, 0, 0)),
                  pl.BlockSpec((4, dh), lambda h, p, qi, kj, a, b: (0, 0)),
                  pl.BlockSpec((2 * dh, 1), lambda h, p, qi, kj, a, b: (0, 0))],
        out_specs=pl.BlockSpec((tq, 2 * dh), lambda h, p, qi, kj, a, b: (qi[p], h)),
        scratch_shapes=[pltpu.VMEM((2, 1, tq), F32), pltpu.VMEM((2, 1, tq), F32),
                        pltpu.VMEM((2, 2 * dh, tq), F32), pltpu.VMEM((2, ck, tq), F32)],
    )
    return pl.pallas_call(
        kern,
        out_shape=jax.ShapeDtypeStruct((S, H * 2 * dh), BF16),
        grid_spec=grid_spec,
        compiler_params=_cparams("parallel", "arbitrary"),
        name="diff_attention",
    )(qi, kj, pminq, pmaxk, qkv, qkv, vt, pos.reshape(1, S), pos.reshape(S, 1),
      lut.reshape(H, 2, LUT_SIZE), lam, subln_g.reshape(2 * dh, 1))


def _kmean_kernel(k_ref, hi_ref, lo_ref):
    km = jnp.mean(k_ref[...].astype(F32), axis=0, keepdims=True)
    hi = km.astype(BF16)
    hi_ref[...] = hi
    lo_ref[...] = (km - hi.astype(F32)).astype(BF16)


def moba_block_means(qkv, D):
    S = qkv.shape[0]
    nblk = S // MOBA_BLOCK
    hi, lo = pl.pallas_call(
        _kmean_kernel,
        out_shape=[jax.ShapeDtypeStruct((nblk, 1, D), BF16)] * 2,
        grid=(nblk,),
        in_specs=[pl.BlockSpec((MOBA_BLOCK, D), lambda n: (n, 1))],
        out_specs=[pl.BlockSpec((None, 1, D), lambda n: (n, 0, 0))] * 2,
        compiler_params=_cparams("parallel"),
        name="moba_block_means",
    )(qkv)
    return hi.reshape(nblk, D), lo.reshape(nblk, D)


def _moba_attn_kernel(qi_ref, kj_ref, pminq_ref, pmaxk_ref,
                      q_ref, k_ref, vt_ref, kmh_ref, kml_ref, pq_ref, pk_ref, lut_ref,
                      o_ref, m_ref, l_ref, acc_ref, sel_ref, st_ref, *, tq, ck, sb, hp):
    p = pl.program_id(1)
    i, j = qi_ref[p], kj_ref[p]
    nc = tq // ck
    L, dh = MOBA_BLOCK, MOBA_HEAD_DIM
    bpc = ck // L
    log2_l = L.bit_length() - 1
    SB = sb
    nqb, nkb = tq // SB, ck // SB

    def near_sub(c, kb, qb):
        return pminq_ref[i * nqb + qb] - pmaxk_ref[(j * nc + c) * nkb + kb] < LUT_SIZE - 1

    def refine_chunk(c):
        keys = slice(c * ck, (c + 1) * ck)
        for hd in range(hp):
            st_ref[hd] = _nt_dot(k_ref[keys, hd * dh:(hd + 1) * dh], q_ref[:, hd * dh:(hd + 1) * dh])
        for kb in range(nkb):
            for qb in range(nqb):
                @pl.when(near_sub(c, kb, qb))
                def _(kb=kb, qb=qb):
                    rows, cols = slice(kb * SB, (kb + 1) * SB), slice(qb * SB, (qb + 1) * SB)
                    idx = jnp.clip(pq_ref[:, cols] - pk_ref[c * ck + kb * SB:c * ck + (kb + 1) * SB, :],
                                   0, LUT_SIZE - 1)
                    for hd in range(hp):
                        far_bias = lut_ref[hd:hd + 1, LUT_SIZE - 1:LUT_SIZE]
                        st_ref[hd, rows, cols] = (st_ref[hd, rows, cols]
                                                  + (_lut_bias(lut_ref[hd:hd + 1, :], idx) - far_bias))
        for hd in range(hp):
            consume(c, hd, 0, "far", st_ref[hd])

    @pl.when(j == 0)
    def _():
        m_ref[...] = jnp.full(m_ref.shape, NEG_INF, F32)
        l_ref[...] = jnp.zeros(l_ref.shape, F32)
        acc_ref[...] = jnp.zeros(acc_ref.shape, F32)
        blk = lax.broadcasted_iota(jnp.int32, (LANES, tq), 0)
        blk_f = blk.astype(F32)
        own = lax.shift_right_logical(i * tq + lax.broadcasted_iota(jnp.int32, (1, tq), 1), log2_l)
        for hd in range(hp):
            q = q_ref[:, hd * dh:(hd + 1) * dh]
            gs = (_nt_dot(kmh_ref[:, hd * dh:(hd + 1) * dh], q)
                  + _nt_dot(kml_ref[:, hd * dh:(hd + 1) * dh], q))
            g = jnp.where(blk < own, gs, NEG_INF)
            sel = jnp.zeros((LANES, tq), F32)
            for _ in range(MOBA_TOPK):
                best = jnp.max(g, axis=0, keepdims=True)
                first = jnp.min(jnp.where(g == best, blk_f, float(LANES)), axis=0, keepdims=True)
                pick = blk_f == first
                sel = jnp.where(pick, jnp.where(best > 0.5 * NEG_INF, 1.0, sel), sel)
                g = jnp.where(pick, -3e38, g)
            sel_ref[hd] = sel

    def scores(c, hd, q0, mode):
        keys = slice(c * ck, (c + 1) * ck)
        nq = tq - q0
        st = _nt_dot(k_ref[keys, hd * dh:(hd + 1) * dh], q_ref[q0:tq, hd * dh:(hd + 1) * dh])
        if mode != "far":
            idx = jnp.clip(pq_ref[:, q0:tq] - pk_ref[keys, :], 0, LUT_SIZE - 1)
            st = st + _lut_bias(lut_ref[hd:hd + 1, :], idx)
        if mode != "diag":
            return st
        allowed = []
        for b in range(bpc):
            n_local = c * bpc + b
            picked = jnp.broadcast_to(picked_row(hd, n_local, q0), (L, nq))
            own_local = lax.shift_right_logical(q0 + lax.broadcasted_iota(jnp.int32, (L, nq), 1), log2_l)
            visible = jnp.where(_causal_t(n_local * L, L, q0, nq), 1.0, 0.0)
            allowed.append(jnp.where(own_local == n_local, visible, picked))
        allowed = allowed[0] if bpc == 1 else jnp.concatenate(allowed, axis=0)
        return jnp.where(allowed > 0.0, st, NEG_INF)

    def picked_row(hd, n_local, q0):
        return sel_ref[hd, pl.ds(j * (tq // L) + n_local, 1), q0:tq]

    def consume(c, hd, q0, mode, st):
        m_cur = offsets = None
        if mode != "diag":
            bias = lut_ref[hd:hd + 1, LUT_SIZE - 1:LUT_SIZE] if mode == "far" else 0.0
            picked = [picked_row(hd, c * bpc + b, q0) > 0.0 for b in range(bpc)]
            m_cur = functools.reduce(jnp.maximum, [
                jnp.where(picked[b], jnp.max(st[b * L:(b + 1) * L, :], axis=0, keepdims=True) + bias, NEG_INF)
                for b in range(bpc)])
            offsets = [(slice(b * L, (b + 1) * L),
                        functools.partial(lambda m_new, pk: jnp.where(pk, m_new - bias, -NEG_INF), pk=picked[b]))
                       for b in range(bpc)]
        _online_softmax_step_t(st, vt_ref[hd * dh:(hd + 1) * dh, c * ck:(c + 1) * ck],
                               m_ref.at[hd], l_ref.at[hd], acc_ref.at[hd], slice(q0, tq), m_cur, offsets)

    def sweep(chunks, mode):
        _pipelined([(c, hd, c * ck if mode == "diag" else 0, mode) for c in chunks for hd in range(hp)],
                   scores, consume)

    @pl.when(j < i)
    def _():
        fars = [jnp.logical_not(functools.reduce(
            jnp.logical_or, [near_sub(c, kb, qb) for kb in range(nkb) for qb in range(nqb)])) for c in range(nc)]
        all_far = functools.reduce(jnp.logical_and, fars)
        pl.when(all_far)(functools.partial(sweep, range(nc), "far"))

        @pl.when(jnp.logical_not(all_far))
        def _():
            for c in range(nc):
                pl.when(fars[c])(functools.partial(sweep, [c], "far"))
                pl.when(jnp.logical_not(fars[c]))(functools.partial(refine_chunk, c))

    @pl.when(j == i)
    def _():
        sweep(range(nc), "diag")
        for hd in range(hp):
            o_ref[:, hd * dh:(hd + 1) * dh] = (acc_ref[hd] / l_ref[hd]).T.astype(o_ref.dtype)


def moba_attention(qkv, vt, pos, lut, *, tile=1024, chunk=512, heads_per_step=2):
    S = qkv.shape[0]
    H, dh, L = MOBA_HEADS, MOBA_HEAD_DIM, MOBA_BLOCK
    D = H * dh
    nblk = S // L
    assert S % L == 0 and nblk <= LANES
    tq = _tile(S, tile)
    ck = _tile(tq, chunk)
    assert tq % L == 0 and ck % L == 0
    km_hi, km_lo = moba_block_means(qkv, D)
    km_hi = jnp.pad(km_hi, ((0, LANES - nblk), (0, 0)))
    km_lo = jnp.pad(km_lo, ((0, LANES - nblk), (0, 0)))
    qi, kj = _pair_tables(S // tq)
    sb = _tile(ck, BIAS_BLOCK)
    pminq, pmaxk = _block_pos_bounds(pos, sb)
    hp = heads_per_step
    G = H // hp
    kern = functools.partial(_moba_attn_kernel, tq=tq, ck=ck, sb=sb, hp=hp)
    grid_spec = pltpu.PrefetchScalarGridSpec(
        num_scalar_prefetch=4,
        grid=(G, qi.shape[0]),
        in_specs=[pl.BlockSpec((tq, hp * dh), lambda h, p, qi, kj, a, b: (qi[p], h)),
                  pl.BlockSpec((tq, hp * dh), lambda h, p, qi, kj, a, b: (kj[p], G + h)),
                  pl.BlockSpec((hp * dh, tq), lambda h, p, qi, kj, a, b: (h, kj[p])),
                  pl.BlockSpec((LANES, hp * dh), lambda h, p, qi, kj, a, b: (0, h)),
                  pl.BlockSpec((LANES, hp * dh), lambda h, p, qi, kj, a, b: (0, h)),
                  pl.BlockSpec((1, tq), lambda h, p, qi, kj, a, b: (0, qi[p])),
                  pl.BlockSpec((tq, 1), lambda h, p, qi, kj, a, b: (kj[p], 0)),
                  pl.BlockSpec((None, hp, LUT_SIZE), lambda h, p, qi, kj, a, b: (h, 0, 0))],
        out_specs=pl.BlockSpec((tq, hp * dh), lambda h, p, qi, kj, a, b: (qi[p], h)),
        scratch_shapes=[pltpu.VMEM((hp, 1, tq), F32), pltpu.VMEM((hp, 1, tq), F32),
                        pltpu.VMEM((hp, dh, tq), F32), pltpu.VMEM((hp, LANES, tq), F32),
                        pltpu.VMEM((hp, ck, tq), F32)],
    )
    return pl.pallas_call(
        kern,
        out_shape=jax.ShapeDtypeStruct((S, D), BF16),
        grid_spec=grid_spec,
        compiler_params=_cparams("parallel", "arbitrary"),
        name="moba_attention",
    )(qi, kj, pminq, pmaxk, qkv, qkv, vt, km_hi, km_lo, pos.reshape(1, S), pos.reshape(S, 1),
      lut.reshape(G, hp, LUT_SIZE))


def _rope_fold(y):
    lane = lax.broadcasted_iota(jnp.int32, y.shape, 1)
    return jnp.where(lane < MLA_ROPE, y + pltpu.roll(y, MLA_ROPE, 1), 0.0)


def _mla_q_kernel(cq_ref, g_ref, w_ref, kr_ref, t_ref, q_ref, kro_ref, cqn_ref, *, scale, hp):
    t = t_ref[...]
    W = 2 * LANES

    @pl.when(pl.program_id(1) == 0)
    def _():
        cqn_ref[...] = _rms_rows(cq_ref[...], g_ref[...]).astype(BF16)
        kro_ref[...] = _rope_fold(kr_ref[...] * t).astype(kro_ref.dtype)

    qh = jnp.dot(cqn_ref[...], w_ref[...], preferred_element_type=F32)
    for hd in range(hp):
        q_ref[:, hd * W:hd * W + MLA_NOPE] = (qh[:, hd * W:hd * W + MLA_NOPE] * scale).astype(q_ref.dtype)
        q_ref[:, hd * W + MLA_NOPE:(hd + 1) * W] = (
            _rope_fold(qh[:, hd * W + MLA_NOPE:(hd + 1) * W] * t) * scale).astype(q_ref.dtype)


def mla_queries_and_rope_key(down, g_q, w_uq_ext, rope_tab, *, scale, tm=1024, heads_per_step=4):
    S = down.shape[0]
    H, hp = MLA_HEADS, heads_per_step
    tm = _tile(S, tm)
    W = 2 * LANES
    return pl.pallas_call(
        functools.partial(_mla_q_kernel, scale=scale, hp=hp),
        out_shape=[jax.ShapeDtypeStruct((S, H * W), BF16), jax.ShapeDtypeStruct((S, LANES), BF16)],
        grid=(S // tm, H // hp),
        in_specs=[pl.BlockSpec((tm, MLA_Q_RANK), lambda i, h: (i, 0)),
                  pl.BlockSpec((1, MLA_Q_RANK), lambda i, h: (0, 0)),
                  pl.BlockSpec((MLA_Q_RANK, hp * W), lambda i, h: (0, h)),
                  pl.BlockSpec((tm, LANES), lambda i, h: (i, (MLA_Q_RANK + MLA_KV_RANK) // LANES)),
                  pl.BlockSpec((tm, LANES), lambda i, h: (i, 0))],
        out_specs=[pl.BlockSpec((tm, hp * W), lambda i, h: (i, h)),
                   pl.BlockSpec((tm, LANES), lambda i, h: (i, 0))],
        scratch_shapes=[pltpu.VMEM((tm, MLA_Q_RANK), BF16)],
        compiler_params=_cparams("parallel", "arbitrary"),
        name="mla_queries",
    )(down, g_q.reshape(1, MLA_Q_RANK), w_uq_ext, down, rope_tab)


def _mla_attn_kernel(qi_ref, kj_ref, q_ref, kv_ref, kr_ref, vt_ref, o_ref, m_ref, l_ref, acc_ref, *, tq, ck, hp):
    p = pl.program_id(1)
    i, j = qi_ref[p], kj_ref[p]
    nc = tq // ck
    W = 2 * LANES

    @pl.when(j == 0)
    def _():
        m_ref[...] = jnp.full(m_ref.shape, NEG_INF, F32)
        l_ref[...] = jnp.zeros(l_ref.shape, F32)
        acc_ref[...] = jnp.zeros(acc_ref.shape, F32)

    def scores(c, g, q0, diag):
        keys = slice(c * ck, (c + 1) * ck)
        kc = jnp.concatenate([kv_ref[keys, g * MLA_NOPE:(g + 1) * MLA_NOPE], kr_ref[keys, :]], axis=1)
        st = _nt_dot(kc, q_ref[q0:tq, g * W:(g + 1) * W])
        if diag:
            st = jnp.where(_causal_t(c * ck, ck, q0, tq - q0), st, NEG_INF)
        return st

    def consume(c, g, q0, diag, st):
        _online_softmax_step_t(st, vt_ref[g * MLA_V:(g + 1) * MLA_V, c * ck:(c + 1) * ck],
                               m_ref.at[g], l_ref.at[g], acc_ref.at[g], slice(q0, tq))

    def sweep(diag):
        _pipelined([(c, g, c * ck if diag else 0, diag) for c in range(nc) for g in range(hp)], scores, consume)

    @pl.when(j < i)
    def _():
        sweep(False)

    @pl.when(j == i)
    def _():
        sweep(True)
        for g in range(hp):
            o_ref[:, g * MLA_V:(g + 1) * MLA_V] = (acc_ref[g] / l_ref[g]).T.astype(o_ref.dtype)


def mla_attention(q, kv, kr, vt, *, tile=1024, chunk=512, heads_per_step=2):
    S = q.shape[0]
    H, hp = MLA_HEADS, heads_per_step
    tq = _tile(S, tile)
    ck = _tile(tq, chunk)
    qi, kj = _pair_tables(S // tq)
    W = 2 * LANES
    grid_spec = pltpu.PrefetchScalarGridSpec(
        num_scalar_prefetch=2,
        grid=(H // hp, qi.shape[0]),
        in_specs=[pl.BlockSpec((tq, hp * W), lambda h, p, qi, kj: (qi[p], h)),
                  pl.BlockSpec((tq, hp * MLA_NOPE), lambda h, p, qi, kj: (kj[p], h)),
                  pl.BlockSpec((tq, LANES), lambda h, p, qi, kj: (kj[p], 0)),
                  pl.BlockSpec((hp * MLA_V, tq), lambda h, p, qi, kj: (h, kj[p]))],
        out_specs=pl.BlockSpec((tq, hp * MLA_V), lambda h, p, qi, kj: (qi[p], h)),
        scratch_shapes=[pltpu.VMEM((hp, 1, tq), F32), pltpu.VMEM((hp, 1, tq), F32),
                        pltpu.VMEM((hp, MLA_V, tq), F32)],
    )
    return pl.pallas_call(
        functools.partial(_mla_attn_kernel, tq=tq, ck=ck, hp=hp),
        out_shape=jax.ShapeDtypeStruct((S, H * MLA_V), BF16),
        grid_spec=grid_spec,
        compiler_params=_cparams("parallel", "arbitrary"),
        name="mla_attention",
    )(qi, kj, q, kv, kr, vt)


_HI16 = 0xFFFF0000


def _pack_bf16_pairs(hi_f32):
    half = hi_f32.shape[1] // 2
    bits = pltpu.bitcast(hi_f32, jnp.uint32)
    return lax.shift_right_logical(bits[:, :half], jnp.uint32(16)) | (bits[:, half:] & jnp.uint32(_HI16))


def _unpack_pairs_by_tile(words, tile):
    half = tile // 2
    parts = []
    for c0 in range(0, words.shape[1], half):
        w = words[:, c0:c0 + half]
        parts.append(pltpu.bitcast(lax.shift_left(w, jnp.uint32(16)), F32))
        parts.append(pltpu.bitcast(w & jnp.uint32(_HI16), F32))
    return jnp.concatenate(parts, axis=1)


def _unpack_bf16_pairs(words):
    lo = pltpu.bitcast(lax.shift_left(words, jnp.uint32(16)), F32).astype(BF16)
    hi = pltpu.bitcast(words & jnp.uint32(_HI16), F32).astype(BF16)
    return jnp.concatenate([lo, hi], axis=1)


def _router_kernel(x_ref, g_ref, wh_ref, wl_ref, xn_ref, r_ref):
    xn = _rms_rows(x_ref[...], g_ref[...])
    hi = xn.astype(BF16)
    hi_f32 = hi.astype(F32)
    lo = (xn - hi_f32).astype(BF16)
    xn_ref[...] = _pack_bf16_pairs(hi_f32)
    logits = (jnp.dot(hi, wh_ref[...], preferred_element_type=F32)
              + jnp.dot(hi, wl_ref[...], preferred_element_type=F32)
              + jnp.dot(lo, wh_ref[...], preferred_element_type=F32))
    lane = lax.broadcasted_iota(jnp.int32, logits.shape, 1)
    lane_f = lane.astype(F32)
    g = jnp.where(lane < N_EXPERTS, logits, NEG_INF)
    v1 = jnp.max(g, axis=-1, keepdims=True)
    i1 = jnp.min(jnp.where(g == v1, lane_f, float(LANES)), axis=-1, keepdims=True)
    g = jnp.where(lane_f == i1, NEG_INF, g)
    v2 = jnp.max(g, axis=-1, keepdims=True)
    i2 = jnp.min(jnp.where(g == v2, lane_f, float(LANES)), axis=-1, keepdims=True)
    g2 = 1.0 / (1.0 + jnp.exp(v1 - v2))
    g1 = 1.0 - g2
    r_ref[...] = jnp.where(lane == 0, i1,
                           jnp.where(lane == 1, i2,
                                     jnp.where(lane == 2, g1, jnp.where(lane == 3, g2, 0.0))))


def moe_route(x, g, w_router, *, tm=512):
    M, D = x.shape
    tm = _tile(M, tm)
    wr = jnp.pad(w_router.astype(F32), ((0, 0), (0, LANES - N_EXPERTS)))
    wh = wr.astype(BF16)
    wl = (wr - wh.astype(F32)).astype(BF16)
    return pl.pallas_call(
        _router_kernel,
        out_shape=[jax.ShapeDtypeStruct((M, D // 2), jnp.uint32), jax.ShapeDtypeStruct((M, LANES), F32)],
        grid=(M // tm,),
        in_specs=[pl.BlockSpec((tm, D), lambda i: (i, 0)),
                  pl.BlockSpec((1, D), lambda i: (0, 0)),
                  pl.BlockSpec((D, LANES), lambda i: (0, 0)),
                  pl.BlockSpec((D, LANES), lambda i: (0, 0))],
        out_specs=[pl.BlockSpec((tm, D // 2), lambda i: (i, 0)),
                   pl.BlockSpec((tm, LANES), lambda i: (i, 0))],
        compiler_params=_cparams("parallel"),
        name="moe_route",
    )(x, g.reshape(1, D), wh, wl)


def _new_expert(te_ref, t):
    return jnp.logical_or(t == 0, te_ref[t] != te_ref[jnp.maximum(t - 1, 0)])


def _stream_expert_weights(te_ref, rid_ref, rexp_ref, nr_ref, used, n_col_tiles, copies_for, on_ready):
    j, t = pl.program_id(0), pl.program_id(1)

    @pl.when(jnp.logical_and(used, _new_expert(te_ref, t)))
    def _():
        n_runs = nr_ref[0]
        r = rid_ref[t]
        g = j * n_runs + r
        slot = lax.rem(g, 2)

        @pl.when(g == 0)
        def _():
            for c in copies_for(j, te_ref[t], slot):
                c.start()

        for c in copies_for(j, te_ref[t], slot):
            c.wait()
        on_ready(slot)
        last = r + 1 == n_runs
        nxt_r = jnp.where(last, 0, r + 1)
        nxt_j = jnp.where(last, j + 1, j)

        @pl.when(nxt_j < n_col_tiles)
        def _():
            for c in copies_for(nxt_j, rexp_ref[nxt_r], 1 - slot):
                c.start()


def _gmm_swiglu_kernel(te_ref, nu_ref, rid_ref, rexp_ref, nr_ref, *refs, n_parts, tiles_per_part, layer, nj, tn):
    x_refs = refs[:n_parts]
    w_hbm, o_ref, wbuf_ref, wgb_ref, wub_ref, sem = refs[n_parts:]
    t = pl.program_id(1)
    used = t < nu_ref[0]

    def copies_for(jj, e, slot):
        gate_cols = pl.ds(pl.multiple_of(jj * tn, tn), tn)
        up_cols = pl.ds(pl.multiple_of((nj + jj) * tn, tn), tn)
        return (pltpu.make_async_copy(w_hbm.at[layer, e, :, gate_cols], wbuf_ref.at[slot, 0], sem.at[slot, 0]),
                pltpu.make_async_copy(w_hbm.at[layer, e, :, up_cols], wbuf_ref.at[slot, 1], sem.at[slot, 1]))

    def on_ready(slot):
        wgb_ref[...] = wbuf_ref[slot, 0].astype(BF16)
        wub_ref[...] = wbuf_ref[slot, 1].astype(BF16)

    _stream_expert_weights(te_ref, rid_ref, rexp_ref, nr_ref, used, nj, copies_for, on_ready)

    for c in range(n_parts):
        @pl.when(jnp.logical_and(used, t // tiles_per_part == c))
        def _(c=c):
            x = _unpack_bf16_pairs(x_refs[c][...])
            a = jnp.dot(x, wgb_ref[...], preferred_element_type=F32)
            u = jnp.dot(x, wub_ref[...], preferred_element_type=F32)
            o_ref[...] = (a * jax.nn.sigmoid(a) * u).astype(o_ref.dtype)

    @pl.when(jnp.logical_not(used))
    def _():
        o_ref[...] = jnp.zeros(o_ref.shape, o_ref.dtype)


def _gmm_out_kernel(te_ref, nu_ref, rid_ref, rexp_ref, nr_ref, h_ref, w_hbm, o_ref, wbuf_ref, wb_ref, sem, *,
                    layer, nj, tn):
    t = pl.program_id(1)
    used = t < nu_ref[0]

    def copies_for(jj, e, slot):
        cols = pl.ds(pl.multiple_of(jj * tn, tn), tn)
        return (pltpu.make_async_copy(w_hbm.at[layer, e, :, cols], wbuf_ref.at[slot], sem.at[slot]),)

    def on_ready(slot):
        wb_ref[...] = wbuf_ref[slot].astype(BF16)

    _stream_expert_weights(te_ref, rid_ref, rexp_ref, nr_ref, used, nj, copies_for, on_ready)

    @pl.when(used)
    def _():
        y = jnp.dot(h_ref[...], wb_ref[...], preferred_element_type=F32)
        o_ref[...] = _pack_bf16_pairs(y.astype(BF16).astype(F32))

    @pl.when(jnp.logical_not(used))
    def _():
        o_ref[...] = jnp.zeros(o_ref.shape, o_ref.dtype)


def moe_experts(xs, tile_expert, n_used, runs, w_in, w_out, layer, *, tm, tn=MOE_COL_TILE):
    n_parts = len(xs)
    P = n_parts * xs[0].shape[0]
    D = w_in.shape[2]
    F = w_in.shape[3] // 2
    nt = P // tm
    ntp = nt // n_parts
    assert ntp * n_parts == nt
    tn1, tn2 = _tile(F, tn), _tile(D, tn)
    nj1, nj2 = F // tn1, D // tn2

    def part_spec(c):
        return pl.BlockSpec((tm, D // 2), lambda j, t, *_: (jnp.clip(t - c * ntp, 0, ntp - 1), 0))

    prefetch = (tile_expert, n_used) + tuple(runs)
    hbm = pl.BlockSpec(memory_space=pl.ANY)
    h = pl.pallas_call(
        functools.partial(_gmm_swiglu_kernel, n_parts=n_parts, tiles_per_part=ntp, layer=layer, nj=nj1, tn=tn1),
        out_shape=jax.ShapeDtypeStruct((P, F), BF16),
        grid_spec=pltpu.PrefetchScalarGridSpec(
            num_scalar_prefetch=len(prefetch),
            grid=(nj1, nt),
            in_specs=[part_spec(c) for c in range(n_parts)] + [hbm],
            out_specs=pl.BlockSpec((tm, tn1), lambda j, t, *_: (t, j)),
            scratch_shapes=[pltpu.VMEM((2, 2, D, tn1), F32), pltpu.VMEM((D, tn1), BF16),
                            pltpu.VMEM((D, tn1), BF16), pltpu.SemaphoreType.DMA((2, 2))],
        ),
        compiler_params=_cparams("arbitrary", "arbitrary"),
        name="moe_swiglu_in",
    )(*prefetch, *xs, w_in)
    return pl.pallas_call(
        functools.partial(_gmm_out_kernel, layer=layer, nj=nj2, tn=tn2),
        out_shape=jax.ShapeDtypeStruct((P, D // 2), jnp.uint32),
        grid_spec=pltpu.PrefetchScalarGridSpec(
            num_scalar_prefetch=len(prefetch),
            grid=(nj2, nt),
            in_specs=[pl.BlockSpec((tm, F), lambda j, t, *_: (t, 0)), hbm],
            out_specs=pl.BlockSpec((tm, tn2 // 2), lambda j, t, *_: (t, j)),
            scratch_shapes=[pltpu.VMEM((2, F, tn2), F32), pltpu.VMEM((F, tn2), BF16),
                            pltpu.SemaphoreType.DMA((2,))],
        ),
        compiler_params=_cparams("arbitrary", "arbitrary"),
        name="moe_out",
    )(*prefetch, h, w_out)


def moe_swiglu(x, g, w_router, w_in, w_out, layer, *, tm=512):
    S, D = x.shape
    E = N_EXPERTS
    tm = _tile(S, tm)
    xn, route = moe_route(x, g, w_router)
    experts = route[:, :2].astype(jnp.int32)
    gates = route[:, 2:4]
    flat_e = experts.reshape(-1)
    order = jnp.argsort(flat_e, stable=True).astype(jnp.int32)
    rank = jnp.argsort(order).astype(jnp.int32)
    counts = jnp.bincount(flat_e, length=E).astype(jnp.int32)
    padded = ((counts + tm - 1) // tm) * tm
    start = jnp.cumsum(counts) - counts
    pstart = jnp.cumsum(padded) - padded
    pend = jnp.cumsum(padded)
    P = 2 * S + E * tm
    nt = P // tm
    tile_expert = jnp.minimum(
        jnp.searchsorted(pend, jnp.arange(nt, dtype=jnp.int32) * tm, side="right"), E - 1).astype(jnp.int32)
    n_used = (pend[-1:] // tm).astype(jnp.int32)
    row_e = jnp.repeat(tile_expert, tm)
    row_rank = jnp.arange(P, dtype=jnp.int32) - pstart[row_e]
    row_valid = jnp.logical_and(row_rank < counts[row_e], jnp.arange(P) < pend[-1])
    row_token = jnp.where(row_valid, order[jnp.clip(start[row_e] + row_rank, 0, 2 * S - 1)] // 2, 0)
    slot = pstart[flat_e] + rank - start[flat_e]
    tile_expert = jnp.where(jnp.arange(nt) < n_used[0], tile_expert, tile_expert[jnp.maximum(n_used[0] - 1, 0)])
    slot = slot.reshape(S, 2)
    present = counts > 0
    run_of_expert = jnp.cumsum(present.astype(jnp.int32)) - 1
    runs = (run_of_expert[tile_expert].astype(jnp.int32),
            jnp.sort(jnp.where(present, jnp.arange(E, dtype=jnp.int32), E))[:E].clip(0, E - 1).astype(jnp.int32),
            jnp.sum(present).astype(jnp.int32).reshape(1))
    xs = [jnp.take(xn, part, axis=0, mode="clip") for part in jnp.split(row_token, MOE_GATHER_PARTS)]
    ys = moe_experts(xs, tile_expert, n_used, runs, w_in, w_out, layer, tm=tm)
    return jnp.take(ys, slot[:, 0], axis=0, mode="clip"), jnp.take(ys, slot[:, 1], axis=0, mode="clip"), gates


def _diff_lambda_init(layer):
    return 0.8 - 0.6 * math.exp(-0.3 * layer)


def _rope_table(pos):
    half = MLA_ROPE // 2
    inv_freq = ROPE_THETA ** (-jnp.arange(half, dtype=F32) / half)
    ang = pos.astype(F32)[:, None] * inv_freq
    cos, sin = jnp.cos(ang), jnp.sin(ang)
    return jnp.concatenate([cos, cos, -sin, sin], axis=1)


def _rot_half_cols(w):
    half = w.shape[-1] // 2
    return jnp.concatenate([w[..., half:], w[..., :half]], axis=-1)


def kernel(x, p, positions, rel_bias, norm_mix, norm_ffn, norm_ple, norm_final, diff_w_qkv, diff_lambda, diff_subln, diff_w_o, moba_w_qkv, moba_w_o, mla_w_down, mla_g_q, mla_w_uq, mla_g_kv, mla_w_ukv, mla_w_o, ffn_w_in, ffn_w_out, moe_w_router, moe_w_in, moe_w_out, ple_w_gate, ple_w_proj):
    B, S, D = x.shape
    depth = p.shape[0]
    lut = _bias_lut(rel_bias)
    outs = []
    for b in range(B):
        xb = x[b]
        pos = positions[b].astype(jnp.int32)
        for i in range(depth):
            jm = i // N_MIXERS
            if i % N_MIXERS == 0:
                qk, vt = norm_matmul_vt(xb, norm_mix[i], diff_w_qkv, layer=jm, n_direct=2 * D,
                                        scaled_cols=D, scale=DIFF_HEAD_DIM ** -0.5 * LOG2E)
                o = diff_attention(qk, vt, pos, lut, diff_lambda[jm].astype(F32), diff_subln[jm],
                                   _diff_lambda_init(i))
                xb = matmul_residual(o, diff_w_o, xb, layer=jm)
            elif i % N_MIXERS == 1:
                qk, vt = norm_matmul_vt(xb, norm_mix[i], moba_w_qkv, layer=jm, n_direct=2 * D,
                                        scaled_cols=D, scale=MOBA_HEAD_DIM ** -0.5 * LOG2E)
                o = moba_attention(qk, vt, pos, lut)
                xb = matmul_residual(o, moba_w_o, xb, layer=jm)
            else:
                wd = mla_w_down[jm]
                kr0 = MLA_Q_RANK + MLA_KV_RANK
                wd_ext = jnp.concatenate([wd, _rot_half_cols(wd[:, kr0:])], axis=1).astype(BF16)
                wq = mla_w_uq[jm].reshape(MLA_Q_RANK, MLA_HEADS, MLA_NOPE + MLA_ROPE)
                wq_ext = jnp.concatenate([wq, _rot_half_cols(wq[..., MLA_NOPE:])], axis=-1)
                wq_ext = wq_ext.reshape(MLA_Q_RANK, MLA_HEADS * 2 * LANES).astype(BF16)
                down = norm_matmul(xb, norm_mix[i], wd_ext, out_dtype=F32, tn=wd_ext.shape[1])
                q, kr = mla_queries_and_rope_key(down, mla_g_q[jm], wq_ext, _rope_table(pos),
                                                 scale=(MLA_NOPE + MLA_ROPE) ** -0.5 * LOG2E)
                wkv = mla_w_ukv[jm].reshape(MLA_KV_RANK, MLA_HEADS, 2, MLA_V).transpose(0, 2, 1, 3)
                wkv = wkv.reshape(MLA_KV_RANK, 2 * MLA_HEADS * MLA_V).astype(BF16)
                kn, vt = norm_matmul_vt(down, mla_g_kv[jm], wkv, n_direct=MLA_HEADS * MLA_NOPE, xcol=1)
                o = mla_attention(q, kn, kr, vt)
                xb = matmul_residual(o, mla_w_o, xb, layer=jm)
            if i % 2 == 0:
                h = norm_swiglu_in(xb, norm_ffn[i], ffn_w_in, layer=i // 2)
                xb = matmul_residual(h, ffn_w_out[i // 2].astype(BF16), xb)
                xb = ple_update(xb, norm_ple[i], p[i, b], ple_w_gate, ple_w_proj, layer=i)
            else:
                y1, y2, gates = moe_swiglu(xb, norm_ffn[i], moe_w_router[i // 2], moe_w_in, moe_w_out, i // 2)
                xb = ple_update_moe(xb, y1, y2, gates, norm_ple[i], p[i, b], ple_w_gate, ple_w_proj, layer=i)
        outs.append(final_norm(xb, norm_final))
    return jnp.stack(outs, axis=0)
```

```python
import functools
import math

import numpy as np
import jax
import jax.numpy as jnp
from jax import lax
from jax.experimental import pallas as pl
from jax.experimental.pallas import tpu as pltpu

F32 = jnp.float32
BF16 = jnp.bfloat16

NORM_EPS = 1e-6
NEG_INF = -1e30
LOG2E = math.log2(math.e)
LANES = 128
VMEM_LIMIT_BYTES = 56 * 1024 * 1024

REL_BUCKETS = 32
REL_MAX_DIST = 128
LUT_SIZE = LANES
BIAS_BLOCK = 256
DIFF_HEADS = 8
DIFF_HEAD_DIM = 128
MOBA_HEADS = 16
MOBA_HEAD_DIM = 128
MOBA_BLOCK = 256
MOBA_TOPK = 3
MLA_HEADS = 16
MLA_Q_RANK = 512
MLA_KV_RANK = 512
MLA_NOPE = 128
MLA_ROPE = 64
MLA_V = 128
ROPE_THETA = 10000.0
N_EXPERTS = 8
N_MIXERS = 3
MOE_GATHER_PARTS = 4
MOE_COL_TILE = 512


def _cparams(*sem):
    return pltpu.CompilerParams(dimension_semantics=sem, vmem_limit_bytes=VMEM_LIMIT_BYTES)


def _tile(n, pref):
    if n <= pref:
        return n
    t = pref
    while n % t:
        t //= 2
    return t


def _weight_spec(w, layer, tn, col=lambda j: j):
    K = w.shape[-2]
    if w.ndim == 2:
        return pl.BlockSpec((K, tn), lambda i, j: (0, col(j)))
    return pl.BlockSpec((None, K, tn), lambda i, j: (layer, 0, col(j)))


def _rms_rows(x, g):
    r = lax.rsqrt(jnp.mean(x * x, axis=-1, keepdims=True) + NORM_EPS)
    return x * r * g


def _norm_mm_kernel(x_ref, g_ref, w_ref, o_ref, xn_ref, *, scaled_tiles, scale):
    j = pl.program_id(1)

    @pl.when(j == 0)
    def _():
        xn_ref[...] = _rms_rows(x_ref[...], g_ref[...]).astype(BF16)

    acc = jnp.dot(xn_ref[...], w_ref[...].astype(BF16), preferred_element_type=F32)
    if scaled_tiles:
        acc = acc * jnp.where(j < scaled_tiles, scale, 1.0)
    o_ref[...] = acc.astype(o_ref.dtype)


def norm_matmul(x, g, w, *, out_dtype, xcol=0, tm=1024, tn=512, scaled_cols=0, scale=1.0):
    M = x.shape[0]
    K, N = w.shape
    tm, tn = _tile(M, tm), _tile(N, tn)
    assert scaled_cols % tn == 0
    kern = functools.partial(_norm_mm_kernel, scaled_tiles=scaled_cols // tn, scale=scale)
    return pl.pallas_call(
        kern,
        out_shape=jax.ShapeDtypeStruct((M, N), out_dtype),
        grid=(M // tm, N // tn),
        in_specs=[pl.BlockSpec((tm, K), lambda i, j: (i, xcol)),
                  pl.BlockSpec((1, K), lambda i, j: (0, 0)),
                  pl.BlockSpec((K, tn), lambda i, j: (0, j))],
        out_specs=pl.BlockSpec((tm, tn), lambda i, j: (i, j)),
        scratch_shapes=[pltpu.VMEM((tm, K), BF16)],
        compiler_params=_cparams("parallel", "arbitrary"),
        name="norm_matmul",
    )(x, g.reshape(1, K), w)


def _norm_mm_vt_kernel(x_ref, g_ref, w_ref, o_ref, ot_ref, xn_ref, *, direct_tiles, scaled_tiles, scale):
    j = pl.program_id(1)

    @pl.when(j == 0)
    def _():
        xn_ref[...] = _rms_rows(x_ref[...], g_ref[...]).astype(BF16)

    acc = jnp.dot(xn_ref[...], w_ref[...].astype(BF16), preferred_element_type=F32)

    @pl.when(j < direct_tiles)
    def _():
        out = acc * jnp.where(j < scaled_tiles, scale, 1.0) if scaled_tiles else acc
        o_ref[...] = out.astype(o_ref.dtype)

    @pl.when(j >= direct_tiles)
    def _():
        ot_ref[...] = acc.T.astype(ot_ref.dtype)


def norm_matmul_vt(x, g, w, *, n_direct, layer=None, xcol=0, tm=1024, tn=512, scaled_cols=0, scale=1.0):
    M = x.shape[0]
    K, N = w.shape[-2:]
    tm, tn = _tile(M, tm), _tile(N, tn)
    assert scaled_cols % tn == 0 and n_direct % tn == 0 and 0 < n_direct < N
    nd = n_direct // tn
    kern = functools.partial(_norm_mm_vt_kernel, direct_tiles=nd, scaled_tiles=scaled_cols // tn, scale=scale)
    return pl.pallas_call(
        kern,
        out_shape=[jax.ShapeDtypeStruct((M, n_direct), BF16), jax.ShapeDtypeStruct((N - n_direct, M), BF16)],
        grid=(M // tm, N // tn),
        in_specs=[pl.BlockSpec((tm, K), lambda i, j: (i, xcol)),
                  pl.BlockSpec((1, K), lambda i, j: (0, 0)),
                  _weight_spec(w, layer, tn)],
        out_specs=[pl.BlockSpec((tm, tn), lambda i, j: (i, jnp.minimum(j, nd - 1))),
                   pl.BlockSpec((tn, tm), lambda i, j: (jnp.maximum(j - nd, 0), i))],
        scratch_shapes=[pltpu.VMEM((tm, K), BF16)],
        compiler_params=_cparams("parallel", "arbitrary"),
        name="norm_matmul_vt",
    )(x, g.reshape(1, K), w)


def _norm_swiglu_kernel(x_ref, g_ref, wg_ref, wu_ref, o_ref, xn_ref):
    @pl.when(pl.program_id(1) == 0)
    def _():
        xn_ref[...] = _rms_rows(x_ref[...], g_ref[...]).astype(BF16)

    xn = xn_ref[...]
    a = jnp.dot(xn, wg_ref[...].astype(BF16), preferred_element_type=F32)
    u = jnp.dot(xn, wu_ref[...].astype(BF16), preferred_element_type=F32)
    o_ref[...] = (a * jax.nn.sigmoid(a) * u).astype(o_ref.dtype)


def norm_swiglu_in(x, g, w_in, *, layer=None, tm=1024, tn=512):
    M, K = x.shape
    F = w_in.shape[-1] // 2
    tm, tn = _tile(M, tm), _tile(F, tn)
    nj = F // tn
    return pl.pallas_call(
        _norm_swiglu_kernel,
        out_shape=jax.ShapeDtypeStruct((M, F), BF16),
        grid=(M // tm, nj),
        in_specs=[pl.BlockSpec((tm, K), lambda i, j: (i, 0)),
                  pl.BlockSpec((1, K), lambda i, j: (0, 0)),
                  _weight_spec(w_in, layer, tn),
                  _weight_spec(w_in, layer, tn, lambda j: j + nj)],
        out_specs=pl.BlockSpec((tm, tn), lambda i, j: (i, j)),
        scratch_shapes=[pltpu.VMEM((tm, K), BF16)],
        compiler_params=_cparams("parallel", "arbitrary"),
        name="norm_swiglu_in",
    )(x, g.reshape(1, K), w_in, w_in)


def _mm_res_kernel(a_ref, w_ref, r_ref, o_ref):
    o_ref[...] = r_ref[...] + jnp.dot(a_ref[...], w_ref[...].astype(BF16), preferred_element_type=F32)


def matmul_residual(a, w, res, *, layer=None, tm=1024, tn=512):
    M, K = a.shape
    N = w.shape[-1]
    tm, tn = _tile(M, tm), _tile(N, tn)
    return pl.pallas_call(
        _mm_res_kernel,
        out_shape=jax.ShapeDtypeStruct((M, N), F32),
        grid=(M // tm, N // tn),
        in_specs=[pl.BlockSpec((tm, K), lambda i, j: (i, 0)),
                  _weight_spec(w, layer, tn),
                  pl.BlockSpec((tm, tn), lambda i, j: (i, j))],
        out_specs=pl.BlockSpec((tm, tn), lambda i, j: (i, j)),
        compiler_params=_cparams("parallel", "parallel"),
        name="matmul_residual",
    )(a, w, res)


def _ple_kernel(x_ref, g_ref, xr_ref, p_ref, wg_ref, wp_ref, o_ref, xn_ref):
    @pl.when(pl.program_id(1) == 0)
    def _():
        xn_ref[...] = _rms_rows(x_ref[...], g_ref[...]).astype(BF16)

    gate = jax.nn.sigmoid(jnp.dot(xn_ref[...], wg_ref[...].astype(BF16), preferred_element_type=F32))
    proj = jnp.dot(p_ref[...].astype(BF16), wp_ref[...].astype(BF16), preferred_element_type=F32)
    o_ref[...] = xr_ref[...] + gate * proj


def ple_update(x, g, p, w_gate, w_proj, *, layer=None, tm=1024, tn=512):
    M, D = x.shape
    P = p.shape[1]
    tm, tn = _tile(M, tm), _tile(D, tn)
    return pl.pallas_call(
        _ple_kernel,
        out_shape=jax.ShapeDtypeStruct((M, D), F32),
        grid=(M // tm, D // tn),
        in_specs=[pl.BlockSpec((tm, D), lambda i, j: (i, 0)),
                  pl.BlockSpec((1, D), lambda i, j: (0, 0)),
                  pl.BlockSpec((tm, tn), lambda i, j: (i, j)),
                  pl.BlockSpec((tm, P), lambda i, j: (i, 0)),
                  _weight_spec(w_gate, layer, tn),
                  _weight_spec(w_proj, layer, tn)],
        out_specs=pl.BlockSpec((tm, tn), lambda i, j: (i, j)),
        scratch_shapes=[pltpu.VMEM((tm, D), BF16)],
        compiler_params=_cparams("parallel", "arbitrary"),
        name="ple_update",
    )(x, g.reshape(1, D), x, p, w_gate, w_proj)


def _ple_moe_kernel(x_ref, y1_ref, y2_ref, gt_ref, g_ref, p_ref, wg_ref, wp_ref, o_ref, xs_ref, xn_ref, *, tn,
                    pair_tile):
    j = pl.program_id(1)

    @pl.when(j == 0)
    def _():
        gt = gt_ref[...]
        xnew = (x_ref[...] + gt[:, 0:1] * _unpack_pairs_by_tile(y1_ref[...], pair_tile)
                + gt[:, 1:2] * _unpack_pairs_by_tile(y2_ref[...], pair_tile))
        xn_ref[...] = _rms_rows(xnew, g_ref[...]).astype(BF16)
        for jj in range(xs_ref.shape[0]):
            xs_ref[jj] = xnew[:, jj * tn:(jj + 1) * tn]

    gate = jax.nn.sigmoid(jnp.dot(xn_ref[...], wg_ref[...].astype(BF16), preferred_element_type=F32))
    proj = jnp.dot(p_ref[...].astype(BF16), wp_ref[...].astype(BF16), preferred_element_type=F32)
    o_ref[...] = xs_ref[j] + gate * proj


def ple_update_moe(x, y1, y2, gates, g, p, w_gate, w_proj, *, layer=None, tm=512, tn=512):
    M, D = x.shape
    P = p.shape[1]
    tm, tn = _tile(M, tm), _tile(D, tn)
    row = pl.BlockSpec((tm, D), lambda i, j: (i, 0))
    packed_row = pl.BlockSpec((tm, D // 2), lambda i, j: (i, 0))
    return pl.pallas_call(
        functools.partial(_ple_moe_kernel, tn=tn, pair_tile=_tile(D, MOE_COL_TILE)),
        out_shape=jax.ShapeDtypeStruct((M, D), F32),
        grid=(M // tm, D // tn),
        in_specs=[row, packed_row, packed_row,
                  pl.BlockSpec((tm, 2), lambda i, j: (i, 0)),
                  pl.BlockSpec((1, D), lambda i, j: (0, 0)),
                  pl.BlockSpec((tm, P), lambda i, j: (i, 0)),
                  _weight_spec(w_gate, layer, tn),
                  _weight_spec(w_proj, layer, tn)],
        out_specs=pl.BlockSpec((tm, tn), lambda i, j: (i, j)),
        scratch_shapes=[pltpu.VMEM((D // tn, tm, tn), F32), pltpu.VMEM((tm, D), BF16)],
        compiler_params=_cparams("parallel", "arbitrary"),
        name="ple_update_moe",
    )(x, y1, y2, gates, g.reshape(1, D), p, w_gate, w_proj)


def _final_norm_kernel(x_ref, g_ref, o_ref):
    o_ref[...] = _rms_rows(x_ref[...], g_ref[...])


def final_norm(x, g, *, tm=512):
    M, D = x.shape
    tm = _tile(M, tm)
    return pl.pallas_call(
        _final_norm_kernel,
        out_shape=jax.ShapeDtypeStruct((M, D), F32),
        grid=(M // tm,),
        in_specs=[pl.BlockSpec((tm, D), lambda i: (i, 0)),
                  pl.BlockSpec((1, D), lambda i: (0, 0))],
        out_specs=pl.BlockSpec((tm, D), lambda i: (i, 0)),
        compiler_params=_cparams("parallel"),
        name="final_norm",
    )(x, g.reshape(1, D))


def _pair_tables(n_tiles):
    qi, kj = [], []
    for i in range(n_tiles):
        for j in range(i + 1):
            qi.append(i)
            kj.append(j)
    return jnp.asarray(np.array(qi, np.int32)), jnp.asarray(np.array(kj, np.int32))


def _rel_bucket(dist):
    n = jnp.maximum(dist, 0)
    max_exact = REL_BUCKETS // 2
    nf = jnp.maximum(n, 1).astype(F32)
    large = max_exact + (jnp.log(nf / max_exact) / math.log(REL_MAX_DIST / max_exact)
                         * (REL_BUCKETS - max_exact)).astype(jnp.int32)
    large = jnp.minimum(large, REL_BUCKETS - 1)
    return jnp.where(n < max_exact, n, large)


def _bias_lut(rel_bias):
    buckets = _rel_bucket(jnp.arange(LUT_SIZE, dtype=jnp.int32))
    return rel_bias.astype(F32)[buckets].T * LOG2E


def _block_pos_bounds(pos, sb):
    blocks = pos.reshape(pos.shape[0] // sb, sb)
    return blocks.min(axis=1), blocks.max(axis=1)


def _lut_bias(lut_row, idx):
    R, C = idx.shape
    lut = jnp.broadcast_to(lut_row, (R, LANES))
    parts = [jnp.take_along_axis(lut, idx[:, c:c + LANES], axis=1, mode="promise_in_bounds")
             for c in range(0, C, LANES)]
    return parts[0] if len(parts) == 1 else jnp.concatenate(parts, axis=1)


def _pipelined(units, scores, consume):
    st = scores(*units[0])
    for k, u in enumerate(units):
        nxt = scores(*units[k + 1]) if k + 1 < len(units) else None
        consume(*u, st)
        st = nxt


def _online_softmax_step_t(st, vt, m_ref, l_ref, acc_ref, qcols, m_cur=None, offsets=None):
    m_prev = m_ref[:, qcols]
    if m_cur is None:
        m_cur = jnp.max(st, axis=0, keepdims=True)
    m_new = jnp.maximum(m_prev, m_cur)
    alpha = jnp.exp2(m_prev - m_new)
    if offsets is None:
        p = jnp.exp2(st - m_new)
    else:
        parts = [jnp.exp2(st[rows, :] - fn(m_new)) for rows, fn in offsets]
        p = parts[0] if len(parts) == 1 else jnp.concatenate(parts, axis=0)
    l_ref[:, qcols] = alpha * l_ref[:, qcols] + jnp.sum(p, axis=0, keepdims=True)
    acc_ref[:, qcols] = alpha * acc_ref[:, qcols] + jnp.dot(vt, p.astype(BF16), preferred_element_type=F32)
    m_ref[:, qcols] = m_new


def _causal_t(k0, nk, q0, nq):
    r = k0 + lax.broadcasted_iota(jnp.int32, (nk, nq), 0)
    c = q0 + lax.broadcasted_iota(jnp.int32, (nk, nq), 1)
    return r <= c


def _nt_dot(a, b):
    return lax.dot_general(a, b, (((1,), (1,)), ((), ())), preferred_element_type=F32)


def _diff_attn_kernel(qi_ref, kj_ref, pminq_ref, pmaxk_ref,
                      q_ref, k_ref, vt_ref, pq_ref, pk_ref, lut_ref, lam_ref, g_ref,
                      o_ref, m_ref, l_ref, acc_ref, st_ref, *, tq, ck, sb, lambda_init):
    p = pl.program_id(1)
    i, j = qi_ref[p], kj_ref[p]
    nc = tq // ck
    dh = DIFF_HEAD_DIM
    SB = sb
    nqb, nkb = tq // SB, ck // SB

    @pl.when(j == 0)
    def _():
        m_ref[...] = jnp.full(m_ref.shape, NEG_INF, F32)
        l_ref[...] = jnp.zeros(l_ref.shape, F32)
        acc_ref[...] = jnp.zeros(acc_ref.shape, F32)

    def near_sub(c, kb, qb):
        return pminq_ref[i * nqb + qb] - pmaxk_ref[(j * nc + c) * nkb + kb] < LUT_SIZE - 1

    def refine_chunk(c):
        keys = slice(c * ck, (c + 1) * ck)
        for mp in range(2):
            st_ref[mp] = _nt_dot(k_ref[keys, mp * dh:(mp + 1) * dh], q_ref[:, mp * dh:(mp + 1) * dh])
        for kb in range(nkb):
            for qb in range(nqb):
                @pl.when(near_sub(c, kb, qb))
                def _(kb=kb, qb=qb):
                    rows, cols = slice(kb * SB, (kb + 1) * SB), slice(qb * SB, (qb + 1) * SB)
                    idx = jnp.clip(pq_ref[:, cols] - pk_ref[c * ck + kb * SB:c * ck + (kb + 1) * SB, :],
                                   0, LUT_SIZE - 1)
                    for mp in range(2):
                        far_bias = lut_ref[mp:mp + 1, LUT_SIZE - 1:LUT_SIZE]
                        st_ref[mp, rows, cols] = (st_ref[mp, rows, cols]
                                                  + (_lut_bias(lut_ref[mp:mp + 1, :], idx) - far_bias))
        for mp in range(2):
            consume(c, mp, 0, "far", st_ref[mp])

    def scores(c, mp, q0, mode):
        keys = slice(c * ck, (c + 1) * ck)
        st = _nt_dot(k_ref[keys, mp * dh:(mp + 1) * dh], q_ref[q0:tq, mp * dh:(mp + 1) * dh])
        if mode != "far":
            idx = jnp.clip(pq_ref[:, q0:tq] - pk_ref[keys, :], 0, LUT_SIZE - 1)
            st = st + _lut_bias(lut_ref[mp:mp + 1, :], idx)
        if mode == "diag":
            st = jnp.where(_causal_t(c * ck, ck, q0, tq - q0), st, NEG_INF)
        return st

    def consume(c, mp, q0, mode, st):
        m_cur = offsets = None
        if mode == "far":
            bias = lut_ref[mp:mp + 1, LUT_SIZE - 1:LUT_SIZE]
            m_cur = jnp.max(st, axis=0, keepdims=True) + bias
            offsets = [(slice(None), lambda m_new: m_new - bias)]
        _online_softmax_step_t(st, vt_ref[:, c * ck:(c + 1) * ck], m_ref.at[mp], l_ref.at[mp], acc_ref.at[mp],
                               slice(q0, tq), m_cur, offsets)

    def sweep(chunks, mode):
        units = [(c, mp, c * ck if mode == "diag" else 0, mode) for c in chunks for mp in range(2)]
        _pipelined(units, scores, consume)

    @pl.when(j < i)
    def _():
        fars = [jnp.logical_not(functools.reduce(
            jnp.logical_or, [near_sub(c, kb, qb) for kb in range(nkb) for qb in range(nqb)])) for c in range(nc)]
        all_far = functools.reduce(jnp.logical_and, fars)
        pl.when(all_far)(functools.partial(sweep, range(nc), "far"))

        @pl.when(jnp.logical_not(all_far))
        def _():
            for c in range(nc):
                pl.when(fars[c])(functools.partial(sweep, [c], "far"))
                pl.when(jnp.logical_not(fars[c]))(functools.partial(refine_chunk, c))

    @pl.when(j == i)
    def _():
        sweep(range(nc), "diag")
        lam = lam_ref[...]
        lam_full = (jnp.exp(jnp.sum(lam[0:1] * lam[1:2], axis=-1, keepdims=True))
                    - jnp.exp(jnp.sum(lam[2:3] * lam[3:4], axis=-1, keepdims=True)) + lambda_init)
        a = acc_ref[0] / l_ref[0] - lam_full * (acc_ref[1] / l_ref[1])
        r = lax.rsqrt(jnp.mean(a * a, axis=0, keepdims=True) + NORM_EPS)
        y = a * r * (g_ref[...] * (1.0 - lambda_init))
        o_ref[...] = y.T.astype(o_ref.dtype)


def diff_attention(qkv, vt, pos, lut, lam, subln_g, lambda_init, *, tile=1024, chunk=512):
    S = qkv.shape[0]
    H, dh = DIFF_HEADS, DIFF_HEAD_DIM
    tq = _tile(S, tile)
    ck = _tile(tq, chunk)
    qi, kj = _pair_tables(S // tq)
    sb = _tile(ck, BIAS_BLOCK)
    pminq, pmaxk = _block_pos_bounds(pos, sb)
    kern = functools.partial(_diff_attn_kernel, tq=tq, ck=ck, sb=sb, lambda_init=lambda_init)
    grid_spec = pltpu.PrefetchScalarGridSpec(
        num_scalar_prefetch=4,
        grid=(H, qi.shape[0]),
        in_specs=[pl.BlockSpec((tq, 2 * dh), lambda h, p, qi, kj, a, b: (qi[p], h)),
                  pl.BlockSpec((tq, 2 * dh), lambda h, p, qi, kj, a, b: (kj[p], H + h)),
                  pl.BlockSpec((2 * dh, tq), lambda h, p, qi, kj, a, b: (h, kj[p])),
                  pl.BlockSpec((1, tq), lambda h, p, qi, kj, a, b: (0, qi[p])),
                  pl.BlockSpec((tq, 1), lambda h, p, qi, kj, a, b: (kj[p], 0)),
                  pl.BlockSpec((None, 2, LUT_SIZE), lambda h, p, qi, kj, a, b: (h, 0, 0)),
                  pl.BlockSpec((4, dh), lambda h, p, qi, kj, a, b: (0, 0)),
                  pl.BlockSpec((2 * dh, 1), lambda h, p, qi, kj, a, b: (0, 0))],
        out_specs=pl.BlockSpec((tq, 2 * dh), lambda h, p, qi, kj, a, b: (qi[p], h)),
        scratch_shapes=[pltpu.VMEM((2, 1, tq), F32), pltpu.VMEM((2, 1, tq), F32),
                        pltpu.VMEM((2, 2 * dh, tq), F32), pltpu.VMEM((2, ck, tq), F32)],
    )
    return pl.pallas_call(
        kern,
        out_shape=jax.ShapeDtypeStruct((S, H * 2 * dh), BF16),
        grid_spec=grid_spec,
        compiler_params=_cparams("parallel", "arbitrary"),
        name="diff_attention",
    )(qi, kj, pminq, pmaxk, qkv, qkv, vt, pos.reshape(1, S), pos.reshape(S, 1),
      lut.reshape(H, 2, LUT_SIZE), lam, subln_g.reshape(2 * dh, 1))


def _kmean_kernel(k_ref, hi_ref, lo_ref):
    km = jnp.mean(k_ref[...].astype(F32), axis=0, keepdims=True)
    hi = km.astype(BF16)
    hi_ref[...] = hi
    lo_ref[...] = (km - hi.astype(F32)).astype(BF16)


def moba_block_means(qkv, D):
    S = qkv.shape[0]
    nblk = S // MOBA_BLOCK
    hi, lo = pl.pallas_call(
        _kmean_kernel,
        out_shape=[jax.ShapeDtypeStruct((nblk, 1, D), BF16)] * 2,
        grid=(nblk,),
        in_specs=[pl.BlockSpec((MOBA_BLOCK, D), lambda n: (n, 1))],
        out_specs=[pl.BlockSpec((None, 1, D), lambda n: (n, 0, 0))] * 2,
        compiler_params=_cparams("parallel"),
        name="moba_block_means",
    )(qkv)
    return hi.reshape(nblk, D), lo.reshape(nblk, D)


def _moba_attn_kernel(qi_ref, kj_ref, pminq_ref, pmaxk_ref,
                      q_ref, k_ref, vt_ref, kmh_ref, kml_ref, pq_ref, pk_ref, lut_ref,
                      o_ref, m_ref, l_ref, acc_ref, sel_ref, st_ref, *, tq, ck, sb, hp):
    p = pl.program_id(1)
    i, j = qi_ref[p], kj_ref[p]
    nc = tq // ck
    L, dh = MOBA_BLOCK, MOBA_HEAD_DIM
    bpc = ck // L
    log2_l = L.bit_length() - 1
    SB = sb
    nqb, nkb = tq // SB, ck // SB

    def near_sub(c, kb, qb):
        return pminq_ref[i * nqb + qb] - pmaxk_ref[(j * nc + c) * nkb + kb] < LUT_SIZE - 1

    def refine_chunk(c):
        keys = slice(c * ck, (c + 1) * ck)
        for hd in range(hp):
            st_ref[hd] = _nt_dot(k_ref[keys, hd * dh:(hd + 1) * dh], q_ref[:, hd * dh:(hd + 1) * dh])
        for kb in range(nkb):
            for qb in range(nqb):
                @pl.when(near_sub(c, kb, qb))
                def _(kb=kb, qb=qb):
                    rows, cols = slice(kb * SB, (kb + 1) * SB), slice(qb * SB, (qb + 1) * SB)
                    idx = jnp.clip(pq_ref[:, cols] - pk_ref[c * ck + kb * SB:c * ck + (kb + 1) * SB, :],
                                   0, LUT_SIZE - 1)
                    for hd in range(hp):
                        far_bias = lut_ref[hd:hd + 1, LUT_SIZE - 1:LUT_SIZE]
                        st_ref[hd, rows, cols] = (st_ref[hd, rows, cols]
                                                  + (_lut_bias(lut_ref[hd:hd + 1, :], idx) - far_bias))
        for hd in range(hp):
            consume(c, hd, 0, "far", st_ref[hd])

    @pl.when(j == 0)
    def _():
        m_ref[...] = jnp.full(m_ref.shape, NEG_INF, F32)
        l_ref[...] = jnp.zeros(l_ref.shape, F32)
        acc_ref[...] = jnp.zeros(acc_ref.shape, F32)
        blk = lax.broadcasted_iota(jnp.int32, (LANES, tq), 0)
        blk_f = blk.astype(F32)
        own = lax.shift_right_logical(i * tq + lax.broadcasted_iota(jnp.int32, (1, tq), 1), log2_l)
        for hd in range(hp):
            q = q_ref[:, hd * dh:(hd + 1) * dh]
            gs = (_nt_dot(kmh_ref[:, hd * dh:(hd + 1) * dh], q)
                  + _nt_dot(kml_ref[:, hd * dh:(hd + 1) * dh], q))
            g = jnp.where(blk < own, gs, NEG_INF)
            sel = jnp.zeros((LANES, tq), F32)
            for _ in range(MOBA_TOPK):
                best = jnp.max(g, axis=0, keepdims=True)
                first = jnp.min(jnp.where(g == best, blk_f, float(LANES)), axis=0, keepdims=True)
                pick = blk_f == first
                sel = jnp.where(pick, jnp.where(best > 0.5 * NEG_INF, 1.0, sel), sel)
                g = jnp.where(pick, -3e38, g)
            sel_ref[hd] = sel

    def scores(c, hd, q0, mode):
        keys = slice(c * ck, (c + 1) * ck)
        nq = tq - q0
        st = _nt_dot(k_ref[keys, hd * dh:(hd + 1) * dh], q_ref[q0:tq, hd * dh:(hd + 1) * dh])
        if mode != "far":
            idx = jnp.clip(pq_ref[:, q0:tq] - pk_ref[keys, :], 0, LUT_SIZE - 1)
            st = st + _lut_bias(lut_ref[hd:hd + 1, :], idx)
        if mode != "diag":
            return st
        allowed = []
        for b in range(bpc):
            n_local = c * bpc + b
            picked = jnp.broadcast_to(picked_row(hd, n_local, q0), (L, nq))
            own_local = lax.shift_right_logical(q0 + lax.broadcasted_iota(jnp.int32, (L, nq), 1), log2_l)
            visible = jnp.where(_causal_t(n_local * L, L, q0, nq), 1.0, 0.0)
            allowed.append(jnp.where(own_local == n_local, visible, picked))
        allowed = allowed[0] if bpc == 1 else jnp.concatenate(allowed, axis=0)
        return jnp.where(allowed > 0.0, st, NEG_INF)

    def picked_row(hd, n_local, q0):
        return sel_ref[hd, pl.ds(j * (tq // L) + n_local, 1), q0:tq]

    def consume(c, hd, q0, mode, st):
        m_cur = offsets = None
        if mode != "diag":
            bias = lut_ref[hd:hd + 1, LUT_SIZE - 1:LUT_SIZE] if mode == "far" else 0.0
            picked = [picked_row(hd, c * bpc + b, q0) > 0.0 for b in range(bpc)]
            m_cur = functools.reduce(jnp.maximum, [
                jnp.where(picked[b], jnp.max(st[b * L:(b + 1) * L, :], axis=0, keepdims=True) + bias, NEG_INF)
                for b in range(bpc)])
            offsets = [(slice(b * L, (b + 1) * L),
                        functools.partial(lambda m_new, pk: jnp.where(pk, m_new - bias, -NEG_INF), pk=picked[b]))
                       for b in range(bpc)]
        _online_softmax_step_t(st, vt_ref[hd * dh:(hd + 1) * dh, c * ck:(c + 1) * ck],
                               m_ref.at[hd], l_ref.at[hd], acc_ref.at[hd], slice(q0, tq), m_cur, offsets)

    def sweep(chunks, mode):
        _pipelined([(c, hd, c * ck if mode == "diag" else 0, mode) for c in chunks for hd in range(hp)],
                   scores, consume)

    @pl.when(j < i)
    def _():
        fars = [jnp.logical_not(functools.reduce(
            jnp.logical_or, [near_sub(c, kb, qb) for kb in range(nkb) for qb in range(nqb)])) for c in range(nc)]
        all_far = functools.reduce(jnp.logical_and, fars)
        pl.when(all_far)(functools.partial(sweep, range(nc), "far"))

        @pl.when(jnp.logical_not(all_far))
        def _():
            for c in range(nc):
                pl.when(fars[c])(functools.partial(sweep, [c], "far"))
                pl.when(jnp.logical_not(fars[c]))(functools.partial(refine_chunk, c))

    @pl.when(j == i)
    def _():
        sweep(range(nc), "diag")
        for hd in range(hp):
            o_ref[:, hd * dh:(hd + 1) * dh] = (acc_ref[hd] / l_ref[hd]).T.astype(o_ref.dtype)


def moba_attention(qkv, vt, pos, lut, *, tile=1024, chunk=512, heads_per_step=2):
    S = qkv.shape[0]
    H, dh, L = MOBA_HEADS, MOBA_HEAD_DIM, MOBA_BLOCK
    D = H * dh
    nblk = S // L
    assert S % L == 0 and nblk <= LANES
    tq = _tile(S, tile)
    ck = _tile(tq, chunk)
    assert tq % L == 0 and ck % L == 0
    km_hi, km_lo = moba_block_means(qkv, D)
    km_hi = jnp.pad(km_hi, ((0, LANES - nblk), (0, 0)))
    km_lo = jnp.pad(km_lo, ((0, LANES - nblk), (0, 0)))
    qi, kj = _pair_tables(S // tq)
    sb = _tile(ck, BIAS_BLOCK)
    pminq, pmaxk = _block_pos_bounds(pos, sb)
    hp = heads_per_step
    G = H // hp
    kern = functools.partial(_moba_attn_kernel, tq=tq, ck=ck, sb=sb, hp=hp)
    grid_spec = pltpu.PrefetchScalarGridSpec(
        num_scalar_prefetch=4,
        grid=(G, qi.shape[0]),
        in_specs=[pl.BlockSpec((tq, hp * dh), lambda h, p, qi, kj, a, b: (qi[p], h)),
                  pl.BlockSpec((tq, hp * dh), lambda h, p, qi, kj, a, b: (kj[p], G + h)),
                  pl.BlockSpec((hp * dh, tq), lambda h, p, qi, kj, a, b: (h, kj[p])),
                  pl.BlockSpec((LANES, hp * dh), lambda h, p, qi, kj, a, b: (0, h)),
                  pl.BlockSpec((LANES, hp * dh), lambda h, p, qi, kj, a, b: (0, h)),
                  pl.BlockSpec((1, tq), lambda h, p, qi, kj, a, b: (0, qi[p])),
                  pl.BlockSpec((tq, 1), lambda h, p, qi, kj, a, b: (kj[p], 0)),
                  pl.BlockSpec((None, hp, LUT_SIZE), lambda h, p, qi, kj, a, b: (h, 0, 0))],
        out_specs=pl.BlockSpec((tq, hp * dh), lambda h, p, qi, kj, a, b: (qi[p], h)),
        scratch_shapes=[pltpu.VMEM((hp, 1, tq), F32), pltpu.VMEM((hp, 1, tq), F32),
                        pltpu.VMEM((hp, dh, tq), F32), pltpu.VMEM((hp, LANES, tq), F32),
                        pltpu.VMEM((hp, ck, tq), F32)],
    )
    return pl.pallas_call(
        kern,
        out_shape=jax.ShapeDtypeStruct((S, D), BF16),
        grid_spec=grid_spec,
        compiler_params=_cparams("parallel", "arbitrary"),
        name="moba_attention",
    )(qi, kj, pminq, pmaxk, qkv, qkv, vt, km_hi, km_lo, pos.reshape(1, S), pos.reshape(S, 1),
      lut.reshape(G, hp, LUT_SIZE))


def _rope_fold(y):
    lane = lax.broadcasted_iota(jnp.int32, y.shape, 1)
    return jnp.where(lane < MLA_ROPE, y + pltpu.roll(y, MLA_ROPE, 1), 0.0)


def _mla_q_kernel(cq_ref, g_ref, w_ref, kr_ref, t_ref, q_ref, kro_ref, cqn_ref, *, scale, hp):
    t = t_ref[...]
    W = 2 * LANES

    @pl.when(pl.program_id(1) == 0)
    def _():
        cqn_ref[...] = _rms_rows(cq_ref[...], g_ref[...]).astype(BF16)
        kro_ref[...] = _rope_fold(kr_ref[...] * t).astype(kro_ref.dtype)

    qh = jnp.dot(cqn_ref[...], w_ref[...], preferred_element_type=F32)
    for hd in range(hp):
        q_ref[:, hd * W:hd * W + MLA_NOPE] = (qh[:, hd * W:hd * W + MLA_NOPE] * scale).astype(q_ref.dtype)
        q_ref[:, hd * W + MLA_NOPE:(hd + 1) * W] = (
            _rope_fold(qh[:, hd * W + MLA_NOPE:(hd + 1) * W] * t) * scale).astype(q_ref.dtype)


def mla_queries_and_rope_key(down, g_q, w_uq_ext, rope_tab, *, scale, tm=1024, heads_per_step=4):
    S = down.shape[0]
    H, hp = MLA_HEADS, heads_per_step
    tm = _tile(S, tm)
    W = 2 * LANES
    return pl.pallas_call(
        functools.partial(_mla_q_kernel, scale=scale, hp=hp),
        out_shape=[jax.ShapeDtypeStruct((S, H * W), BF16), jax.ShapeDtypeStruct((S, LANES), BF16)],
        grid=(S // tm, H // hp),
        in_specs=[pl.BlockSpec((tm, MLA_Q_RANK), lambda i, h: (i, 0)),
                  pl.BlockSpec((1, MLA_Q_RANK), lambda i, h: (0, 0)),
                  pl.BlockSpec((MLA_Q_RANK, hp * W), lambda i, h: (0, h)),
                  pl.BlockSpec((tm, LANES), lambda i, h: (i, (MLA_Q_RANK + MLA_KV_RANK) // LANES)),
                  pl.BlockSpec((tm, LANES), lambda i, h: (i, 0))],
        out_specs=[pl.BlockSpec((tm, hp * W), lambda i, h: (i, h)),
                   pl.BlockSpec((tm, LANES), lambda i, h: (i, 0))],
        scratch_shapes=[pltpu.VMEM((tm, MLA_Q_RANK), BF16)],
        compiler_params=_cparams("parallel", "arbitrary"),
        name="mla_queries",
    )(down, g_q.reshape(1, MLA_Q_RANK), w_uq_ext, down, rope_tab)


def _mla_attn_kernel(qi_ref, kj_ref, q_ref, kv_ref, kr_ref, vt_ref, o_ref, m_ref, l_ref, acc_ref, *, tq, ck, hp):
    p = pl.program_id(1)
    i, j = qi_ref[p], kj_ref[p]
    nc = tq // ck
    W = 2 * LANES

    @pl.when(j == 0)
    def _():
        m_ref[...] = jnp.full(m_ref.shape, NEG_INF, F32)
        l_ref[...] = jnp.zeros(l_ref.shape, F32)
        acc_ref[...] = jnp.zeros(acc_ref.shape, F32)

    def scores(c, g, q0, diag):
        keys = slice(c * ck, (c + 1) * ck)
        kc = jnp.concatenate([kv_ref[keys, g * MLA_NOPE:(g + 1) * MLA_NOPE], kr_ref[keys, :]], axis=1)
        st = _nt_dot(kc, q_ref[q0:tq, g * W:(g + 1) * W])
        if diag:
            st = jnp.where(_causal_t(c * ck, ck, q0, tq - q0), st, NEG_INF)
        return st

    def consume(c, g, q0, diag, st):
        _online_softmax_step_t(st, vt_ref[g * MLA_V:(g + 1) * MLA_V, c * ck:(c + 1) * ck],
                               m_ref.at[g], l_ref.at[g], acc_ref.at[g], slice(q0, tq))

    def sweep(diag):
        _pipelined([(c, g, c * ck if diag else 0, diag) for c in range(nc) for g in range(hp)], scores, consume)

    @pl.when(j < i)
    def _():
        sweep(False)

    @pl.when(j == i)
    def _():
        sweep(True)
        for g in range(hp):
            o_ref[:, g * MLA_V:(g + 1) * MLA_V] = (acc_ref[g] / l_ref[g]).T.astype(o_ref.dtype)


def mla_attention(q, kv, kr, vt, *, tile=2048, chunk=512, heads_per_step=2):
    S = q.shape[0]
    H, hp = MLA_HEADS, heads_per_step
    tq = _tile(S, tile)
    ck = _tile(tq, chunk)
    qi, kj = _pair_tables(S // tq)
    W = 2 * LANES
    grid_spec = pltpu.PrefetchScalarGridSpec(
        num_scalar_prefetch=2,
        grid=(H // hp, qi.shape[0]),
        in_specs=[pl.BlockSpec((tq, hp * W), lambda h, p, qi, kj: (qi[p], h)),
                  pl.BlockSpec((tq, hp * MLA_NOPE), lambda h, p, qi, kj: (kj[p], h)),
                  pl.BlockSpec((tq, LANES), lambda h, p, qi, kj: (kj[p], 0)),
                  pl.BlockSpec((hp * MLA_V, tq), lambda h, p, qi, kj: (h, kj[p]))],
        out_specs=pl.BlockSpec((tq, hp * MLA_V), lambda h, p, qi, kj: (qi[p], h)),
        scratch_shapes=[pltpu.VMEM((hp, 1, tq), F32), pltpu.VMEM((hp, 1, tq), F32),
                        pltpu.VMEM((hp, MLA_V, tq), F32)],
    )
    return pl.pallas_call(
        functools.partial(_mla_attn_kernel, tq=tq, ck=ck, hp=hp),
        out_shape=jax.ShapeDtypeStruct((S, H * MLA_V), BF16),
        grid_spec=grid_spec,
        compiler_params=_cparams("parallel", "arbitrary"),
        name="mla_attention",
    )(qi, kj, q, kv, kr, vt)


_HI16 = 0xFFFF0000


def _pack_bf16_pairs(hi_f32):
    half = hi_f32.shape[1] // 2
    bits = pltpu.bitcast(hi_f32, jnp.uint32)
    return lax.shift_right_logical(bits[:, :half], jnp.uint32(16)) | (bits[:, half:] & jnp.uint32(_HI16))


def _unpack_pairs_by_tile(words, tile):
    half = tile // 2
    parts = []
    for c0 in range(0, words.shape[1], half):
        w = words[:, c0:c0 + half]
        parts.append(pltpu.bitcast(lax.shift_left(w, jnp.uint32(16)), F32))
        parts.append(pltpu.bitcast(w & jnp.uint32(_HI16), F32))
    return jnp.concatenate(parts, axis=1)


def _unpack_bf16_pairs(words):
    lo = pltpu.bitcast(lax.shift_left(words, jnp.uint32(16)), F32).astype(BF16)
    hi = pltpu.bitcast(words & jnp.uint32(_HI16), F32).astype(BF16)
    return jnp.concatenate([lo, hi], axis=1)


def _router_kernel(x_ref, g_ref, wh_ref, wl_ref, xn_ref, r_ref):
    xn = _rms_rows(x_ref[...], g_ref[...])
    hi = xn.astype(BF16)
    hi_f32 = hi.astype(F32)
    lo = (xn - hi_f32).astype(BF16)
    xn_ref[...] = _pack_bf16_pairs(hi_f32)
    logits = (jnp.dot(hi, wh_ref[...], preferred_element_type=F32)
              + jnp.dot(hi, wl_ref[...], preferred_element_type=F32)
              + jnp.dot(lo, wh_ref[...], preferred_element_type=F32))
    lane = lax.broadcasted_iota(jnp.int32, logits.shape, 1)
    lane_f = lane.astype(F32)
    g = jnp.where(lane < N_EXPERTS, logits, NEG_INF)
    v1 = jnp.max(g, axis=-1, keepdims=True)
    i1 = jnp.min(jnp.where(g == v1, lane_f, float(LANES)), axis=-1, keepdims=True)
    g = jnp.where(lane_f == i1, NEG_INF, g)
    v2 = jnp.max(g, axis=-1, keepdims=True)
    i2 = jnp.min(jnp.where(g == v2, lane_f, float(LANES)), axis=-1, keepdims=True)
    g2 = 1.0 / (1.0 + jnp.exp(v1 - v2))
    g1 = 1.0 - g2
    r_ref[...] = jnp.where(lane == 0, i1,
                           jnp.where(lane == 1, i2,
                                     jnp.where(lane == 2, g1, jnp.where(lane == 3, g2, 0.0))))


def moe_route(x, g, w_router, *, tm=512):
    M, D = x.shape
    tm = _tile(M, tm)
    wr = jnp.pad(w_router.astype(F32), ((0, 0), (0, LANES - N_EXPERTS)))
    wh = wr.astype(BF16)
    wl = (wr - wh.astype(F32)).astype(BF16)
    return pl.pallas_call(
        _router_kernel,
        out_shape=[jax.ShapeDtypeStruct((M, D // 2), jnp.uint32), jax.ShapeDtypeStruct((M, LANES), F32)],
        grid=(M // tm,),
        in_specs=[pl.BlockSpec((tm, D), lambda i: (i, 0)),
                  pl.BlockSpec((1, D), lambda i: (0, 0)),
                  pl.BlockSpec((D, LANES), lambda i: (0, 0)),
                  pl.BlockSpec((D, LANES), lambda i: (0, 0))],
        out_specs=[pl.BlockSpec((tm, D // 2), lambda i: (i, 0)),
                   pl.BlockSpec((tm, LANES), lambda i: (i, 0))],
        compiler_params=_cparams("parallel"),
        name="moe_route",
    )(x, g.reshape(1, D), wh, wl)


def _new_expert(te_ref, t):
    return jnp.logical_or(t == 0, te_ref[t] != te_ref[jnp.maximum(t - 1, 0)])


def _stream_expert_weights(te_ref, rid_ref, rexp_ref, nr_ref, used, n_col_tiles, copies_for, on_ready):
    j, t = pl.program_id(0), pl.program_id(1)

    @pl.when(jnp.logical_and(used, _new_expert(te_ref, t)))
    def _():
        n_runs = nr_ref[0]
        r = rid_ref[t]
        g = j * n_runs + r
        slot = lax.rem(g, 2)

        @pl.when(g == 0)
        def _():
            for c in copies_for(j, te_ref[t], slot):
                c.start()

        for c in copies_for(j, te_ref[t], slot):
            c.wait()
        on_ready(slot)
        last = r + 1 == n_runs
        nxt_r = jnp.where(last, 0, r + 1)
        nxt_j = jnp.where(last, j + 1, j)

        @pl.when(nxt_j < n_col_tiles)
        def _():
            for c in copies_for(nxt_j, rexp_ref[nxt_r], 1 - slot):
                c.start()


def _gmm_swiglu_kernel(te_ref, nu_ref, rid_ref, rexp_ref, nr_ref, *refs, n_parts, tiles_per_part, layer, nj, tn):
    x_refs = refs[:n_parts]
    w_hbm, o_ref, wbuf_ref, wgb_ref, wub_ref, sem = refs[n_parts:]
    t = pl.program_id(1)
    used = t < nu_ref[0]

    def copies_for(jj, e, slot):
        gate_cols = pl.ds(pl.multiple_of(jj * tn, tn), tn)
        up_cols = pl.ds(pl.multiple_of((nj + jj) * tn, tn), tn)
        return (pltpu.make_async_copy(w_hbm.at[layer, e, :, gate_cols], wbuf_ref.at[slot, 0], sem.at[slot, 0]),
                pltpu.make_async_copy(w_hbm.at[layer, e, :, up_cols], wbuf_ref.at[slot, 1], sem.at[slot, 1]))

    def on_ready(slot):
        wgb_ref[...] = wbuf_ref[slot, 0].astype(BF16)
        wub_ref[...] = wbuf_ref[slot, 1].astype(BF16)

    _stream_expert_weights(te_ref, rid_ref, rexp_ref, nr_ref, used, nj, copies_for, on_ready)

    for c in range(n_parts):
        @pl.when(jnp.logical_and(used, t // tiles_per_part == c))
        def _(c=c):
            x = _unpack_bf16_pairs(x_refs[c][...])
            a = jnp.dot(x, wgb_ref[...], preferred_element_type=F32)
            u = jnp.dot(x, wub_ref[...], preferred_element_type=F32)
            o_ref[...] = (a * jax.nn.sigmoid(a) * u).astype(o_ref.dtype)

    @pl.when(jnp.logical_not(used))
    def _():
        o_ref[...] = jnp.zeros(o_ref.shape, o_ref.dtype)


def _gmm_out_kernel(te_ref, nu_ref, rid_ref, rexp_ref, nr_ref, h_ref, w_hbm, o_ref, wbuf_ref, wb_ref, sem, *,
                    layer, nj, tn):
    t = pl.program_id(1)
    used = t < nu_ref[0]

    def copies_for(jj, e, slot):
        cols = pl.ds(pl.multiple_of(jj * tn, tn), tn)
        return (pltpu.make_async_copy(w_hbm.at[layer, e, :, cols], wbuf_ref.at[slot], sem.at[slot]),)

    def on_ready(slot):
        wb_ref[...] = wbuf_ref[slot].astype(BF16)

    _stream_expert_weights(te_ref, rid_ref, rexp_ref, nr_ref, used, nj, copies_for, on_ready)

    @pl.when(used)
    def _():
        y = jnp.dot(h_ref[...], wb_ref[...], preferred_element_type=F32)
        o_ref[...] = _pack_bf16_pairs(y.astype(BF16).astype(F32))

    @pl.when(jnp.logical_not(used))
    def _():
        o_ref[...] = jnp.zeros(o_ref.shape, o_ref.dtype)


def moe_experts(xs, tile_expert, n_used, runs, w_in, w_out, layer, *, tm, tn=MOE_COL_TILE):
    n_parts = len(xs)
    P = n_parts * xs[0].shape[0]
    D = w_in.shape[2]
    F = w_in.shape[3] // 2
    nt = P // tm
    ntp = nt // n_parts
    assert ntp * n_parts == nt
    tn1, tn2 = _tile(F, tn), _tile(D, tn)
    nj1, nj2 = F // tn1, D // tn2

    def part_spec(c):
        return pl.BlockSpec((tm, D // 2), lambda j, t, *_: (jnp.clip(t - c * ntp, 0, ntp - 1), 0))

    prefetch = (tile_expert, n_used) + tuple(runs)
    hbm = pl.BlockSpec(memory_space=pl.ANY)
    h = pl.pallas_call(
        functools.partial(_gmm_swiglu_kernel, n_parts=n_parts, tiles_per_part=ntp, layer=layer, nj=nj1, tn=tn1),
        out_shape=jax.ShapeDtypeStruct((P, F), BF16),
        grid_spec=pltpu.PrefetchScalarGridSpec(
            num_scalar_prefetch=len(prefetch),
            grid=(nj1, nt),
            in_specs=[part_spec(c) for c in range(n_parts)] + [hbm],
            out_specs=pl.BlockSpec((tm, tn1), lambda j, t, *_: (t, j)),
            scratch_shapes=[pltpu.VMEM((2, 2, D, tn1), F32), pltpu.VMEM((D, tn1), BF16),
                            pltpu.VMEM((D, tn1), BF16), pltpu.SemaphoreType.DMA((2, 2))],
        ),
        compiler_params=_cparams("arbitrary", "arbitrary"),
        name="moe_swiglu_in",
    )(*prefetch, *xs, w_in)
    return pl.pallas_call(
        functools.partial(_gmm_out_kernel, layer=layer, nj=nj2, tn=tn2),
        out_shape=jax.ShapeDtypeStruct((P, D // 2), jnp.uint32),
        grid_spec=pltpu.PrefetchScalarGridSpec(
            num_scalar_prefetch=len(prefetch),
            grid=(nj2, nt),
            in_specs=[pl.BlockSpec((tm, F), lambda j, t, *_: (t, 0)), hbm],
            out_specs=pl.BlockSpec((tm, tn2 // 2), lambda j, t, *_: (t, j)),
            scratch_shapes=[pltpu.VMEM((2, F, tn2), F32), pltpu.VMEM((F, tn2), BF16),
                            pltpu.SemaphoreType.DMA((2,))],
        ),
        compiler_params=_cparams("arbitrary", "arbitrary"),
        name="moe_out",
    )(*prefetch, h, w_out)


def moe_swiglu(x, g, w_router, w_in, w_out, layer, *, tm=512):
    S, D = x.shape
    E = N_EXPERTS
    tm = _tile(S, tm)
    xn, route = moe_route(x, g, w_router)
    experts = route[:, :2].astype(jnp.int32)
    gates = route[:, 2:4]
    flat_e = experts.reshape(-1)
    order = jnp.argsort(flat_e, stable=True).astype(jnp.int32)
    rank = jnp.argsort(order).astype(jnp.int32)
    counts = jnp.bincount(flat_e, length=E).astype(jnp.int32)
    padded = ((counts + tm - 1) // tm) * tm
    start = jnp.cumsum(counts) - counts
    pstart = jnp.cumsum(padded) - padded
    pend = jnp.cumsum(padded)
    P = 2 * S + E * tm
    nt = P // tm
    tile_expert = jnp.minimum(
        jnp.searchsorted(pend, jnp.arange(nt, dtype=jnp.int32) * tm, side="right"), E - 1).astype(jnp.int32)
    n_used = (pend[-1:] // tm).astype(jnp.int32)
    row_e = jnp.repeat(tile_expert, tm)
    row_rank = jnp.arange(P, dtype=jnp.int32) - pstart[row_e]
    row_valid = jnp.logical_and(row_rank < counts[row_e], jnp.arange(P) < pend[-1])
    row_token = jnp.where(row_valid, order[jnp.clip(start[row_e] + row_rank, 0, 2 * S - 1)] // 2, 0)
    slot = pstart[flat_e] + rank - start[flat_e]
    tile_expert = jnp.where(jnp.arange(nt) < n_used[0], tile_expert, tile_expert[jnp.maximum(n_used[0] - 1, 0)])
    slot = slot.reshape(S, 2)
    present = counts > 0
    run_of_expert = jnp.cumsum(present.astype(jnp.int32)) - 1
    runs = (run_of_expert[tile_expert].astype(jnp.int32),
            jnp.sort(jnp.where(present, jnp.arange(E, dtype=jnp.int32), E))[:E].clip(0, E - 1).astype(jnp.int32),
            jnp.sum(present).astype(jnp.int32).reshape(1))
    xs = [jnp.take(xn, part, axis=0, mode="clip") for part in jnp.split(row_token, MOE_GATHER_PARTS)]
    ys = moe_experts(xs, tile_expert, n_used, runs, w_in, w_out, layer, tm=tm)
    return jnp.take(ys, slot[:, 0], axis=0, mode="clip"), jnp.take(ys, slot[:, 1], axis=0, mode="clip"), gates


def _diff_lambda_init(layer):
    return 0.8 - 0.6 * math.exp(-0.3 * layer)


def _rope_table(pos):
    half = MLA_ROPE // 2
    inv_freq = ROPE_THETA ** (-jnp.arange(half, dtype=F32) / half)
    ang = pos.astype(F32)[:, None] * inv_freq
    cos, sin = jnp.cos(ang), jnp.sin(ang)
    return jnp.concatenate([cos, cos, -sin, sin], axis=1)


def _rot_half_cols(w):
    half = w.shape[-1] // 2
    return jnp.concatenate([w[..., half:], w[..., :half]], axis=-1)


def kernel(x, p, positions, rel_bias, norm_mix, norm_ffn, norm_ple, norm_final, diff_w_qkv, diff_lambda, diff_subln, diff_w_o, moba_w_qkv, moba_w_o, mla_w_down, mla_g_q, mla_w_uq, mla_g_kv, mla_w_ukv, mla_w_o, ffn_w_in, ffn_w_out, moe_w_router, moe_w_in, moe_w_out, ple_w_gate, ple_w_proj):
    B, S, D = x.shape
    depth = p.shape[0]
    lut = _bias_lut(rel_bias)
    outs = []
    for b in range(B):
        xb = x[b]
        pos = positions[b].astype(jnp.int32)
        for i in range(depth):
            jm = i // N_MIXERS
            if i % N_MIXERS == 0:
                qk, vt = norm_matmul_vt(xb, norm_mix[i], diff_w_qkv, layer=jm, n_direct=2 * D,
                                        scaled_cols=D, scale=DIFF_HEAD_DIM ** -0.5 * LOG2E)
                o = diff_attention(qk, vt, pos, lut, diff_lambda[jm].astype(F32), diff_subln[jm],
                                   _diff_lambda_init(i))
                xb = matmul_residual(o, diff_w_o, xb, layer=jm)
            elif i % N_MIXERS == 1:
                qk, vt = norm_matmul_vt(xb, norm_mix[i], moba_w_qkv, layer=jm, n_direct=2 * D,
                                        scaled_cols=D, scale=MOBA_HEAD_DIM ** -0.5 * LOG2E)
                o = moba_attention(qk, vt, pos, lut)
                xb = matmul_residual(o, moba_w_o, xb, layer=jm)
            else:
                wd = mla_w_down[jm]
                kr0 = MLA_Q_RANK + MLA_KV_RANK
                wd_ext = jnp.concatenate([wd, _rot_half_cols(wd[:, kr0:])], axis=1).astype(BF16)
                wq = mla_w_uq[jm].reshape(MLA_Q_RANK, MLA_HEADS, MLA_NOPE + MLA_ROPE)
                wq_ext = jnp.concatenate([wq, _rot_half_cols(wq[..., MLA_NOPE:])], axis=-1)
                wq_ext = wq_ext.reshape(MLA_Q_RANK, MLA_HEADS * 2 * LANES).astype(BF16)
                down = norm_matmul(xb, norm_mix[i], wd_ext, out_dtype=F32, tn=wd_ext.shape[1])
                q, kr = mla_queries_and_rope_key(down, mla_g_q[jm], wq_ext, _rope_table(pos),
                                                 scale=(MLA_NOPE + MLA_ROPE) ** -0.5 * LOG2E)
                wkv = mla_w_ukv[jm].reshape(MLA_KV_RANK, MLA_HEADS, 2, MLA_V).transpose(0, 2, 1, 3)
                wkv = wkv.reshape(MLA_KV_RANK, 2 * MLA_HEADS * MLA_V).astype(BF16)
                kn, vt = norm_matmul_vt(down, mla_g_kv[jm], wkv, n_direct=MLA_HEADS * MLA_NOPE, xcol=1)
                o = mla_attention(q, kn, kr, vt)
                xb = matmul_residual(o, mla_w_o, xb, layer=jm)
            if i % 2 == 0:
                h = norm_swiglu_in(xb, norm_ffn[i], ffn_w_in, layer=i // 2)
                xb = matmul_residual(h, ffn_w_out[i // 2].astype(BF16), xb)
                xb = ple_update(xb, norm_ple[i], p[i, b], ple_w_gate, ple_w_proj, layer=i)
            else:
                y1, y2, gates = moe_swiglu(xb, norm_ffn[i], moe_w_router[i // 2], moe_w_in, moe_w_out, i // 2)
                xb = ple_update_moe(xb, y1, y2, gates, norm_ple[i], p[i, b], ple_w_gate, ple_w_proj, layer=i)
        outs.append(final_norm(xb, norm_final))
    return jnp.stack(outs, axis=0)
```

```python
import functools
import math

import numpy as np
import jax
import jax.numpy as jnp
from jax import lax
from jax.experimental import pallas as pl
from jax.experimental.pallas import tpu as pltpu

F32 = jnp.float32
BF16 = jnp.bfloat16

NORM_EPS = 1e-6
NEG_INF = -1e30
LOG2E = math.log2(math.e)
LANES = 128
VMEM_LIMIT_BYTES = 56 * 1024 * 1024

REL_BUCKETS = 32
REL_MAX_DIST = 128
LUT_SIZE = LANES
BIAS_BLOCK = 256
DIFF_HEADS = 8
DIFF_HEAD_DIM = 128
MOBA_HEADS = 16
MOBA_HEAD_DIM = 128
MOBA_BLOCK = 256
MOBA_TOPK = 3
MLA_HEADS = 16
MLA_Q_RANK = 512
MLA_KV_RANK = 512
MLA_NOPE = 128
MLA_ROPE = 64
MLA_V = 128
ROPE_THETA = 10000.0
N_EXPERTS = 8
N_MIXERS = 3
MOE_GATHER_PARTS = 4
MOE_COL_TILE = 512


def _cparams(*sem):
    return pltpu.CompilerParams(dimension_semantics=sem, vmem_limit_bytes=VMEM_LIMIT_BYTES)


def _tile(n, pref):
    if n <= pref:
        return n
    t = pref
    while n % t:
        t //= 2
    return t


def _weight_spec(w, layer, tn, col=lambda j: j):
    K = w.shape[-2]
    if w.ndim == 2:
        return pl.BlockSpec((K, tn), lambda i, j: (0, col(j)))
    return pl.BlockSpec((None, K, tn), lambda i, j: (layer, 0, col(j)))


def _rms_rows(x, g):
    r = lax.rsqrt(jnp.mean(x * x, axis=-1, keepdims=True) + NORM_EPS)
    return x * r * g


def _norm_mm_kernel(x_ref, g_ref, w_ref, o_ref, xn_ref, *, scaled_tiles, scale):
    j = pl.program_id(1)

    @pl.when(j == 0)
    def _():
        xn_ref[...] = _rms_rows(x_ref[...], g_ref[...]).astype(BF16)

    acc = jnp.dot(xn_ref[...], w_ref[...].astype(BF16), preferred_element_type=F32)
    if scaled_tiles:
        acc = acc * jnp.where(j < scaled_tiles, scale, 1.0)
    o_ref[...] = acc.astype(o_ref.dtype)


def norm_matmul(x, g, w, *, out_dtype, xcol=0, tm=1024, tn=512, scaled_cols=0, scale=1.0):
    M = x.shape[0]
    K, N = w.shape
    tm, tn = _tile(M, tm), _tile(N, tn)
    assert scaled_cols % tn == 0
    kern = functools.partial(_norm_mm_kernel, scaled_tiles=scaled_cols // tn, scale=scale)
    return pl.pallas_call(
        kern,
        out_shape=jax.ShapeDtypeStruct((M, N), out_dtype),
        grid=(M // tm, N // tn),
        in_specs=[pl.BlockSpec((tm, K), lambda i, j: (i, xcol)),
                  pl.BlockSpec((1, K), lambda i, j: (0, 0)),
                  pl.BlockSpec((K, tn), lambda i, j: (0, j))],
        out_specs=pl.BlockSpec((tm, tn), lambda i, j: (i, j)),
        scratch_shapes=[pltpu.VMEM((tm, K), BF16)],
        compiler_params=_cparams("parallel", "arbitrary"),
        name="norm_matmul",
    )(x, g.reshape(1, K), w)


def _norm_mm_vt_kernel(x_ref, g_ref, w_ref, o_ref, ot_ref, xn_ref, *, direct_tiles, scaled_tiles, scale):
    j = pl.program_id(1)

    @pl.when(j == 0)
    def _():
        xn_ref[...] = _rms_rows(x_ref[...], g_ref[...]).astype(BF16)

    acc = jnp.dot(xn_ref[...], w_ref[...].astype(BF16), preferred_element_type=F32)

    @pl.when(j < direct_tiles)
    def _():
        out = acc * jnp.where(j < scaled_tiles, scale, 1.0) if scaled_tiles else acc
        o_ref[...] = out.astype(o_ref.dtype)

    @pl.when(j >= direct_tiles)
    def _():
        ot_ref[...] = acc.T.astype(ot_ref.dtype)


def norm_matmul_vt(x, g, w, *, n_direct, layer=None, xcol=0, tm=1024, tn=512, scaled_cols=0, scale=1.0):
    M = x.shape[0]
    K, N = w.shape[-2:]
    tm, tn = _tile(M, tm), _tile(N, tn)
    assert scaled_cols % tn == 0 and n_direct % tn == 0 and 0 < n_direct < N
    nd = n_direct // tn
    kern = functools.partial(_norm_mm_vt_kernel, direct_tiles=nd, scaled_tiles=scaled_cols // tn, scale=scale)
    return pl.pallas_call(
        kern,
        out_shape=[jax.ShapeDtypeStruct((M, n_direct), BF16), jax.ShapeDtypeStruct((N - n_direct, M), BF16)],
        grid=(M // tm, N // tn),
        in_specs=[pl.BlockSpec((tm, K), lambda i, j: (i, xcol)),
                  pl.BlockSpec((1, K), lambda i, j: (0, 0)),
                  _weight_spec(w, layer, tn)],
        out_specs=[pl.BlockSpec((tm, tn), lambda i, j: (i, jnp.minimum(j, nd - 1))),
                   pl.BlockSpec((tn, tm), lambda i, j: (jnp.maximum(j - nd, 0), i))],
        scratch_shapes=[pltpu.VMEM((tm, K), BF16)],
        compiler_params=_cparams("parallel", "arbitrary"),
        name="norm_matmul_vt",
    )(x, g.reshape(1, K), w)


def _norm_swiglu_kernel(x_ref, g_ref, wg_ref, wu_ref, o_ref, xn_ref):
    @pl.when(pl.program_id(1) == 0)
    def _():
        xn_ref[...] = _rms_rows(x_ref[...], g_ref[...]).astype(BF16)

    xn = xn_ref[...]
    a = jnp.dot(xn, wg_ref[...].astype(BF16), preferred_element_type=F32)
    u = jnp.dot(xn, wu_ref[...].astype(BF16), preferred_element_type=F32)
    o_ref[...] = (a * jax.nn.sigmoid(a) * u).astype(o_ref.dtype)


def norm_swiglu_in(x, g, w_in, *, layer=None, tm=1024, tn=512):
    M, K = x.shape
    F = w_in.shape[-1] // 2
    tm, tn = _tile(M, tm), _tile(F, tn)
    nj = F // tn
    return pl.pallas_call(
        _norm_swiglu_kernel,
        out_shape=jax.ShapeDtypeStruct((M, F), BF16),
        grid=(M // tm, nj),
        in_specs=[pl.BlockSpec((tm, K), lambda i, j: (i, 0)),
                  pl.BlockSpec((1, K), lambda i, j: (0, 0)),
                  _weight_spec(w_in, layer, tn),
                  _weight_spec(w_in, layer, tn, lambda j: j + nj)],
        out_specs=pl.BlockSpec((tm, tn), lambda i, j: (i, j)),
        scratch_shapes=[pltpu.VMEM((tm, K), BF16)],
        compiler_params=_cparams("parallel", "arbitrary"),
        name="norm_swiglu_in",
    )(x, g.reshape(1, K), w_in, w_in)


def _mm_res_kernel(a_ref, w_ref, r_ref, o_ref):
    o_ref[...] = r_ref[...] + jnp.dot(a_ref[...], w_ref[...].astype(BF16), preferred_element_type=F32)


def matmul_residual(a, w, res, *, layer=None, tm=1024, tn=512):
    M, K = a.shape
    N = w.shape[-1]
    tm, tn = _tile(M, tm), _tile(N, tn)
    return pl.pallas_call(
        _mm_res_kernel,
        out_shape=jax.ShapeDtypeStruct((M, N), F32),
        grid=(M // tm, N // tn),
        in_specs=[pl.BlockSpec((tm, K), lambda i, j: (i, 0)),
                  _weight_spec(w, layer, tn),
                  pl.BlockSpec((tm, tn), lambda i, j: (i, j))],
        out_specs=pl.BlockSpec((tm, tn), lambda i, j: (i, j)),
        compiler_params=_cparams("parallel", "parallel"),
        name="matmul_residual",
    )(a, w, res)


def _ple_kernel(x_ref, g_ref, xr_ref, p_ref, wg_ref, wp_ref, o_ref, xn_ref):
    @pl.when(pl.program_id(1) == 0)
    def _():
        xn_ref[...] = _rms_rows(x_ref[...], g_ref[...]).astype(BF16)

    gate = jax.nn.sigmoid(jnp.dot(xn_ref[...], wg_ref[...].astype(BF16), preferred_element_type=F32))
    proj = jnp.dot(p_ref[...].astype(BF16), wp_ref[...].astype(BF16), preferred_element_type=F32)
    o_ref[...] = xr_ref[...] + gate * proj


def ple_update(x, g, p, w_gate, w_proj, *, layer=None, tm=1024, tn=512):
    M, D = x.shape
    P = p.shape[1]
    tm, tn = _tile(M, tm), _tile(D, tn)
    return pl.pallas_call(
        _ple_kernel,
        out_shape=jax.ShapeDtypeStruct((M, D), F32),
        grid=(M // tm, D // tn),
        in_specs=[pl.BlockSpec((tm, D), lambda i, j: (i, 0)),
                  pl.BlockSpec((1, D), lambda i, j: (0, 0)),
                  pl.BlockSpec((tm, tn), lambda i, j: (i, j)),
                  pl.BlockSpec((tm, P), lambda i, j: (i, 0)),
                  _weight_spec(w_gate, layer, tn),
                  _weight_spec(w_proj, layer, tn)],
        out_specs=pl.BlockSpec((tm, tn), lambda i, j: (i, j)),
        scratch_shapes=[pltpu.VMEM((tm, D), BF16)],
        compiler_params=_cparams("parallel", "arbitrary"),
        name="ple_update",
    )(x, g.reshape(1, D), x, p, w_gate, w_proj)


def _ple_moe_kernel(x_ref, y1_ref, y2_ref, gt_ref, g_ref, p_ref, wg_ref, wp_ref, o_ref, xs_ref, xn_ref, *, tn,
                    pair_tile):
    j = pl.program_id(1)

    @pl.when(j == 0)
    def _():
        gt = gt_ref[...]
        xnew = (x_ref[...] + gt[:, 0:1] * _unpack_pairs_by_tile(y1_ref[...], pair_tile)
                + gt[:, 1:2] * _unpack_pairs_by_tile(y2_ref[...], pair_tile))
        xn_ref[...] = _rms_rows(xnew, g_ref[...]).astype(BF16)
        for jj in range(xs_ref.shape[0]):
            xs_ref[jj] = xnew[:, jj * tn:(jj + 1) * tn]

    gate = jax.nn.sigmoid(jnp.dot(xn_ref[...], wg_ref[...].astype(BF16), preferred_element_type=F32))
    proj = jnp.dot(p_ref[...].astype(BF16), wp_ref[...].astype(BF16), preferred_element_type=F32)
    o_ref[...] = xs_ref[j] + gate * proj


def ple_update_moe(x, y1, y2, gates, g, p, w_gate, w_proj, *, layer=None, tm=512, tn=512):
    M, D = x.shape
    P = p.shape[1]
    tm, tn = _tile(M, tm), _tile(D, tn)
    row = pl.BlockSpec((tm, D), lambda i, j: (i, 0))
    packed_row = pl.BlockSpec((tm, D // 2), lambda i, j: (i, 0))
    return pl.pallas_call(
        functools.partial(_ple_moe_kernel, tn=tn, pair_tile=_tile(D, MOE_COL_TILE)),
        out_shape=jax.ShapeDtypeStruct((M, D), F32),
        grid=(M // tm, D // tn),
        in_specs=[row, packed_row, packed_row,
                  pl.BlockSpec((tm, 2), lambda i, j: (i, 0)),
                  pl.BlockSpec((1, D), lambda i, j: (0, 0)),
                  pl.BlockSpec((tm, P), lambda i, j: (i, 0)),
                  _weight_spec(w_gate, layer, tn),
                  _weight_spec(w_proj, layer, tn)],
        out_specs=pl.BlockSpec((tm, tn), lambda i, j: (i, j)),
        scratch_shapes=[pltpu.VMEM((D // tn, tm, tn), F32), pltpu.VMEM((tm, D), BF16)],
        compiler_params=_cparams("parallel", "arbitrary"),
        name="ple_update_moe",
    )(x, y1, y2, gates, g.reshape(1, D), p, w_gate, w_proj)


def _final_norm_kernel(x_ref, g_ref, o_ref):
    o_ref[...] = _rms_rows(x_ref[...], g_ref[...])


def final_norm(x, g, *, tm=512):
    M, D = x.shape
    tm = _tile(M, tm)
    return pl.pallas_call(
        _final_norm_kernel,
        out_shape=jax.ShapeDtypeStruct((M, D), F32),
        grid=(M // tm,),
        in_specs=[pl.BlockSpec((tm, D), lambda i: (i, 0)),
                  pl.BlockSpec((1, D), lambda i: (0, 0))],
        out_specs=pl.BlockSpec((tm, D), lambda i: (i, 0)),
        compiler_params=_cparams("parallel"),
        name="final_norm",
    )(x, g.reshape(1, D))


def _pair_tables(n_tiles):
    qi, kj = [], []
    for i in range(n_tiles):
        for j in range(i + 1):
            qi.append(i)
            kj.append(j)
    return jnp.asarray(np.array(qi, np.int32)), jnp.asarray(np.array(kj, np.int32))


def _rel_bucket(dist):
    n = jnp.maximum(dist, 0)
    max_exact = REL_BUCKETS // 2
    nf = jnp.maximum(n, 1).astype(F32)
    large = max_exact + (jnp.log(nf / max_exact) / math.log(REL_MAX_DIST / max_exact)
                         * (REL_BUCKETS - max_exact)).astype(jnp.int32)
    large = jnp.minimum(large, REL_BUCKETS - 1)
    return jnp.where(n < max_exact, n, large)


def _bias_lut(rel_bias):
    buckets = _rel_bucket(jnp.arange(LUT_SIZE, dtype=jnp.int32))
    return rel_bias.astype(F32)[buckets].T * LOG2E


def _block_pos_bounds(pos, sb):
    blocks = pos.reshape(pos.shape[0] // sb, sb)
    return blocks.min(axis=1), blocks.max(axis=1)


def _lut_bias(lut_row, idx):
    R, C = idx.shape
    lut = jnp.broadcast_to(lut_row, (R, LANES))
    parts = [jnp.take_along_axis(lut, idx[:, c:c + LANES], axis=1, mode="promise_in_bounds")
             for c in range(0, C, LANES)]
    return parts[0] if len(parts) == 1 else jnp.concatenate(parts, axis=1)


def _pipelined(units, scores, consume):
    st = scores(*units[0])
    for k, u in enumerate(units):
        nxt = scores(*units[k + 1]) if k + 1 < len(units) else None
        consume(*u, st)
        st = nxt


def _online_softmax_step_t(st, vt, m_ref, l_ref, acc_ref, qcols, m_cur=None, offsets=None):
    m_prev = m_ref[:, qcols]
    if m_cur is None:
        m_cur = jnp.max(st, axis=0, keepdims=True)
    m_new = jnp.maximum(m_prev, m_cur)
    alpha = jnp.exp2(m_prev - m_new)
    if offsets is None:
        p = jnp.exp2(st - m_new)
    else:
        parts = [jnp.exp2(st[rows, :] - fn(m_new)) for rows, fn in offsets]
        p = parts[0] if len(parts) == 1 else jnp.concatenate(parts, axis=0)
    l_ref[:, qcols] = alpha * l_ref[:, qcols] + jnp.sum(p, axis=0, keepdims=True)
    acc_ref[:, qcols] = alpha * acc_ref[:, qcols] + jnp.dot(vt, p.astype(BF16), preferred_element_type=F32)
    m_ref[:, qcols] = m_new


def _causal_t(k0, nk, q0, nq):
    r = k0 + lax.broadcasted_iota(jnp.int32, (nk, nq), 0)
    c = q0 + lax.broadcasted_iota(jnp.int32, (nk, nq), 1)
    return r <= c


def _nt_dot(a, b):
    return lax.dot_general(a, b, (((1,), (1,)), ((), ())), preferred_element_type=F32)


def _diff_attn_kernel(qi_ref, kj_ref, pminq_ref, pmaxk_ref,
                      q_ref, k_ref, vt_ref, pq_ref, pk_ref, lut_ref, lam_ref, g_ref,
                      o_ref, m_ref, l_ref, acc_ref, st_ref, *, tq, ck, sb, hp, lambda_init):
    p = pl.program_id(1)
    i, j = qi_ref[p], kj_ref[p]
    nc = tq // ck
    dh = DIFF_HEAD_DIM
    nm = 2 * hp
    SB = sb
    nqb, nkb = tq // SB, ck // SB

    @pl.when(j == 0)
    def _():
        m_ref[...] = jnp.full(m_ref.shape, NEG_INF, F32)
        l_ref[...] = jnp.zeros(l_ref.shape, F32)
        acc_ref[...] = jnp.zeros(acc_ref.shape, F32)

    def near_sub(c, kb, qb):
        return pminq_ref[i * nqb + qb] - pmaxk_ref[(j * nc + c) * nkb + kb] < LUT_SIZE - 1

    def refine_chunk(c):
        keys = slice(c * ck, (c + 1) * ck)
        for mp in range(nm):
            st_ref[mp] = _nt_dot(k_ref[keys, mp * dh:(mp + 1) * dh], q_ref[:, mp * dh:(mp + 1) * dh])
        for kb in range(nkb):
            for qb in range(nqb):
                @pl.when(near_sub(c, kb, qb))
                def _(kb=kb, qb=qb):
                    rows, cols = slice(kb * SB, (kb + 1) * SB), slice(qb * SB, (qb + 1) * SB)
                    idx = jnp.clip(pq_ref[:, cols] - pk_ref[c * ck + kb * SB:c * ck + (kb + 1) * SB, :],
                                   0, LUT_SIZE - 1)
                    for mp in range(nm):
                        far_bias = lut_ref[mp:mp + 1, LUT_SIZE - 1:LUT_SIZE]
                        st_ref[mp, rows, cols] = (st_ref[mp, rows, cols]
                                                  + (_lut_bias(lut_ref[mp:mp + 1, :], idx) - far_bias))
        for mp in range(nm):
            consume(c, mp, 0, "far", st_ref[mp])

    def scores(c, mp, q0, mode):
        keys = slice(c * ck, (c + 1) * ck)
        st = _nt_dot(k_ref[keys, mp * dh:(mp + 1) * dh], q_ref[q0:tq, mp * dh:(mp + 1) * dh])
        if mode != "far":
            idx = jnp.clip(pq_ref[:, q0:tq] - pk_ref[keys, :], 0, LUT_SIZE - 1)
            st = st + _lut_bias(lut_ref[mp:mp + 1, :], idx)
        if mode == "diag":
            st = jnp.where(_causal_t(c * ck, ck, q0, tq - q0), st, NEG_INF)
        return st

    def consume(c, mp, q0, mode, st):
        m_cur = offsets = None
        if mode == "far":
            bias = lut_ref[mp:mp + 1, LUT_SIZE - 1:LUT_SIZE]
            m_cur = jnp.max(st, axis=0, keepdims=True) + bias
            offsets = [(slice(None), lambda m_new: m_new - bias)]
        hd = mp // 2
        _online_softmax_step_t(st, vt_ref[hd * 2 * dh:(hd + 1) * 2 * dh, c * ck:(c + 1) * ck],
                               m_ref.at[mp], l_ref.at[mp], acc_ref.at[mp], slice(q0, tq), m_cur, offsets)

    def sweep(chunks, mode):
        units = [(c, mp, c * ck if mode == "diag" else 0, mode) for c in chunks for mp in range(nm)]
        _pipelined(units, scores, consume)

    @pl.when(j < i)
    def _():
        fars = [jnp.logical_not(functools.reduce(
            jnp.logical_or, [near_sub(c, kb, qb) for kb in range(nkb) for qb in range(nqb)])) for c in range(nc)]
        all_far = functools.reduce(jnp.logical_and, fars)
        pl.when(all_far)(functools.partial(sweep, range(nc), "far"))

        @pl.when(jnp.logical_not(all_far))
        def _():
            for c in range(nc):
                pl.when(fars[c])(functools.partial(sweep, [c], "far"))
                pl.when(jnp.logical_not(fars[c]))(functools.partial(refine_chunk, c))

    @pl.when(j == i)
    def _():
        sweep(range(nc), "diag")
        lam = lam_ref[...]
        lam_full = (jnp.exp(jnp.sum(lam[0:1] * lam[1:2], axis=-1, keepdims=True))
                    - jnp.exp(jnp.sum(lam[2:3] * lam[3:4], axis=-1, keepdims=True)) + lambda_init)
        for hd in range(hp):
            a = (acc_ref[2 * hd] / l_ref[2 * hd]
                 - lam_full * (acc_ref[2 * hd + 1] / l_ref[2 * hd + 1]))
            r = lax.rsqrt(jnp.mean(a * a, axis=0, keepdims=True) + NORM_EPS)
            y = a * r * (g_ref[...] * (1.0 - lambda_init))
            o_ref[:, hd * 2 * dh:(hd + 1) * 2 * dh] = y.T.astype(o_ref.dtype)


def diff_attention(qkv, vt, pos, lut, lam, subln_g, lambda_init, *, tile=1024, chunk=512, heads_per_step=2):
    S = qkv.shape[0]
    H, dh = DIFF_HEADS, DIFF_HEAD_DIM
    tq = _tile(S, tile)
    ck = _tile(tq, chunk)
    qi, kj = _pair_tables(S // tq)
    sb = _tile(ck, BIAS_BLOCK)
    pminq, pmaxk = _block_pos_bounds(pos, sb)
    hp = heads_per_step
    G = H // hp
    W = hp * 2 * dh
    kern = functools.partial(_diff_attn_kernel, tq=tq, ck=ck, sb=sb, hp=hp, lambda_init=lambda_init)
    grid_spec = pltpu.PrefetchScalarGridSpec(
        num_scalar_prefetch=4,
        grid=(G, qi.shape[0]),
        in_specs=[pl.BlockSpec((tq, W), lambda h, p, qi, kj, a, b: (qi[p], h)),
                  pl.BlockSpec((tq, W), lambda h, p, qi, kj, a, b: (kj[p], G + h)),
                  pl.BlockSpec((W, tq), lambda h, p, qi, kj, a, b: (h, kj[p])),
                  pl.BlockSpec((1, tq), lambda h, p, qi, kj, a, b: (0, qi[p])),
                  pl.BlockSpec((tq, 1), lambda h, p, qi, kj, a, b: (kj[p], 0)),
                  pl.BlockSpec((None, 2 * hp, LUT_SIZE), lambda h, p, qi, kj, a, b: (h, 0, 0)),
                  pl.BlockSpec((4, dh), lambda h, p, qi, kj, a, b: (0, 0)),
                  pl.BlockSpec((2 * dh, 1), lambda h, p, qi, kj, a, b: (0, 0))],
        out_specs=pl.BlockSpec((tq, W), lambda h, p, qi, kj, a, b: (qi[p], h)),
        scratch_shapes=[pltpu.VMEM((2 * hp, 1, tq), F32), pltpu.VMEM((2 * hp, 1, tq), F32),
                        pltpu.VMEM((2 * hp, 2 * dh, tq), F32), pltpu.VMEM((2 * hp, ck, tq), F32)],
    )
    return pl.pallas_call(
        kern,
        out_shape=jax.ShapeDtypeStruct((S, H * 2 * dh), BF16),
        grid_spec=grid_spec,
        compiler_params=_cparams("parallel", "arbitrary"),
        name="diff_attention",
    )(qi, kj, pminq, pmaxk, qkv, qkv, vt, pos.reshape(1, S), pos.reshape(S, 1),
      lut.reshape(G, 2 * hp, LUT_SIZE), lam, subln_g.reshape(2 * dh, 1))


def _kmean_kernel(k_ref, hi_ref, lo_ref):
    km = jnp.mean(k_ref[...].astype(F32), axis=0, keepdims=True)
    hi = km.astype(BF16)
    hi_ref[...] = hi
    lo_ref[...] = (km - hi.astype(F32)).astype(BF16)


def moba_block_means(qkv, D):
    S = qkv.shape[0]
    nblk = S // MOBA_BLOCK
    hi, lo = pl.pallas_call(
        _kmean_kernel,
        out_shape=[jax.ShapeDtypeStruct((nblk, 1, D), BF16)] * 2,
        grid=(nblk,),
        in_specs=[pl.BlockSpec((MOBA_BLOCK, D), lambda n: (n, 1))],
        out_specs=[pl.BlockSpec((None, 1, D), lambda n: (n, 0, 0))] * 2,
        compiler_params=_cparams("parallel"),
        name="moba_block_means",
    )(qkv)
    return hi.reshape(nblk, D), lo.reshape(nblk, D)


def _moba_attn_kernel(qi_ref, kj_ref, pminq_ref, pmaxk_ref,
                      q_ref, k_ref, vt_ref, kmh_ref, kml_ref, pq_ref, pk_ref, lut_ref,
                      o_ref, m_ref, l_ref, acc_ref, sel_ref, st_ref, *, tq, ck, sb, hp):
    p = pl.program_id(1)
    i, j = qi_ref[p], kj_ref[p]
    nc = tq // ck
    L, dh = MOBA_BLOCK, MOBA_HEAD_DIM
    bpc = ck // L
    log2_l = L.bit_length() - 1
    SB = sb
    nqb, nkb = tq // SB, ck // SB

    def near_sub(c, kb, qb):
        return pminq_ref[i * nqb + qb] - pmaxk_ref[(j * nc + c) * nkb + kb] < LUT_SIZE - 1

    def refine_chunk(c):
        keys = slice(c * ck, (c + 1) * ck)
        for hd in range(hp):
            st_ref[hd] = _nt_dot(k_ref[keys, hd * dh:(hd + 1) * dh], q_ref[:, hd * dh:(hd + 1) * dh])
        for kb in range(nkb):
            for qb in range(nqb):
                @pl.when(near_sub(c, kb, qb))
                def _(kb=kb, qb=qb):
                    rows, cols = slice(kb * SB, (kb + 1) * SB), slice(qb * SB, (qb + 1) * SB)
                    idx = jnp.clip(pq_ref[:, cols] - pk_ref[c * ck + kb * SB:c * ck + (kb + 1) * SB, :],
                                   0, LUT_SIZE - 1)
                    for hd in range(hp):
                        far_bias = lut_ref[hd:hd + 1, LUT_SIZE - 1:LUT_SIZE]
                        st_ref[hd, rows, cols] = (st_ref[hd, rows, cols]
                                                  + (_lut_bias(lut_ref[hd:hd + 1, :], idx) - far_bias))
        for hd in range(hp):
            consume(c, hd, 0, "far", st_ref[hd])

    @pl.when(j == 0)
    def _():
        m_ref[...] = jnp.full(m_ref.shape, NEG_INF, F32)
        l_ref[...] = jnp.zeros(l_ref.shape, F32)
        acc_ref[...] = jnp.zeros(acc_ref.shape, F32)
        blk = lax.broadcasted_iota(jnp.int32, (LANES, tq), 0)
        blk_f = blk.astype(F32)
        own = lax.shift_right_logical(i * tq + lax.broadcasted_iota(jnp.int32, (1, tq), 1), log2_l)
        for hd in range(hp):
            q = q_ref[:, hd * dh:(hd + 1) * dh]
            gs = (_nt_dot(kmh_ref[:, hd * dh:(hd + 1) * dh], q)
                  + _nt_dot(kml_ref[:, hd * dh:(hd + 1) * dh], q))
            g = jnp.where(blk < own, gs, NEG_INF)
            sel = jnp.zeros((LANES, tq), F32)
            for _ in range(MOBA_TOPK):
                best = jnp.max(g, axis=0, keepdims=True)
                first = jnp.min(jnp.where(g == best, blk_f, float(LANES)), axis=0, keepdims=True)
                pick = blk_f == first
                sel = jnp.where(pick, jnp.where(best > 0.5 * NEG_INF, 1.0, sel), sel)
                g = jnp.where(pick, -3e38, g)
            sel_ref[hd] = sel

    def scores(c, hd, q0, mode):
        keys = slice(c * ck, (c + 1) * ck)
        nq = tq - q0
        st = _nt_dot(k_ref[keys, hd * dh:(hd + 1) * dh], q_ref[q0:tq, hd * dh:(hd + 1) * dh])
        if mode != "far":
            idx = jnp.clip(pq_ref[:, q0:tq] - pk_ref[keys, :], 0, LUT_SIZE - 1)
            st = st + _lut_bias(lut_ref[hd:hd + 1, :], idx)
        if mode != "diag":
            return st
        allowed = []
        for b in range(bpc):
            n_local = c * bpc + b
            picked = jnp.broadcast_to(picked_row(hd, n_local, q0), (L, nq))
            own_local = lax.shift_right_logical(q0 + lax.broadcasted_iota(jnp.int32, (L, nq), 1), log2_l)
            visible = jnp.where(_causal_t(n_local * L, L, q0, nq), 1.0, 0.0)
            allowed.append(jnp.where(own_local == n_local, visible, picked))
        allowed = allowed[0] if bpc == 1 else jnp.concatenate(allowed, axis=0)
        return jnp.where(allowed > 0.0, st, NEG_INF)

    def picked_row(hd, n_local, q0):
        return sel_ref[hd, pl.ds(j * (tq // L) + n_local, 1), q0:tq]

    def consume(c, hd, q0, mode, st):
        m_cur = offsets = None
        if mode != "diag":
            bias = lut_ref[hd:hd + 1, LUT_SIZE - 1:LUT_SIZE] if mode == "far" else 0.0
            picked = [picked_row(hd, c * bpc + b, q0) > 0.0 for b in range(bpc)]
            m_cur = functools.reduce(jnp.maximum, [
                jnp.where(picked[b], jnp.max(st[b * L:(b + 1) * L, :], axis=0, keepdims=True) + bias, NEG_INF)
                for b in range(bpc)])
            offsets = [(slice(b * L, (b + 1) * L),
                        functools.partial(lambda m_new, pk: jnp.where(pk, m_new - bias, -NEG_INF), pk=picked[b]))
                       for b in range(bpc)]
        _online_softmax_step_t(st, vt_ref[hd * dh:(hd + 1) * dh, c * ck:(c + 1) * ck],
                               m_ref.at[hd], l_ref.at[hd], acc_ref.at[hd], slice(q0, tq), m_cur, offsets)

    def sweep(chunks, mode):
        _pipelined([(c, hd, c * ck if mode == "diag" else 0, mode) for c in chunks for hd in range(hp)],
                   scores, consume)

    @pl.when(j < i)
    def _():
        fars = [jnp.logical_not(functools.reduce(
            jnp.logical_or, [near_sub(c, kb, qb) for kb in range(nkb) for qb in range(nqb)])) for c in range(nc)]
        all_far = functools.reduce(jnp.logical_and, fars)
        pl.when(all_far)(functools.partial(sweep, range(nc), "far"))

        @pl.when(jnp.logical_not(all_far))
        def _():
            for c in range(nc):
                pl.when(fars[c])(functools.partial(sweep, [c], "far"))
                pl.when(jnp.logical_not(fars[c]))(functools.partial(refine_chunk, c))

    @pl.when(j == i)
    def _():
        sweep(range(nc), "diag")
        for hd in range(hp):
            o_ref[:, hd * dh:(hd + 1) * dh] = (acc_ref[hd] / l_ref[hd]).T.astype(o_ref.dtype)


def moba_attention(qkv, vt, pos, lut, *, tile=1024, chunk=512, heads_per_step=2):
    S = qkv.shape[0]
    H, dh, L = MOBA_HEADS, MOBA_HEAD_DIM, MOBA_BLOCK
    D = H * dh
    nblk = S // L
    assert S % L == 0 and nblk <= LANES
    tq = _tile(S, tile)
    ck = _tile(tq, chunk)
    assert tq % L == 0 and ck % L == 0
    km_hi, km_lo = moba_block_means(qkv, D)
    km_hi = jnp.pad(km_hi, ((0, LANES - nblk), (0, 0)))
    km_lo = jnp.pad(km_lo, ((0, LANES - nblk), (0, 0)))
    qi, kj = _pair_tables(S // tq)
    sb = _tile(ck, BIAS_BLOCK)
    pminq, pmaxk = _block_pos_bounds(pos, sb)
    hp = heads_per_step
    G = H // hp
    kern = functools.partial(_moba_attn_kernel, tq=tq, ck=ck, sb=sb, hp=hp)
    grid_spec = pltpu.PrefetchScalarGridSpec(
        num_scalar_prefetch=4,
        grid=(G, qi.shape[0]),
        in_specs=[pl.BlockSpec((tq, hp * dh), lambda h, p, qi, kj, a, b: (qi[p], h)),
                  pl.BlockSpec((tq, hp * dh), lambda h, p, qi, kj, a, b: (kj[p], G + h)),
                  pl.BlockSpec((hp * dh, tq), lambda h, p, qi, kj, a, b: (h, kj[p])),
                  pl.BlockSpec((LANES, hp * dh), lambda h, p, qi, kj, a, b: (0, h)),
                  pl.BlockSpec((LANES, hp * dh), lambda h, p, qi, kj, a, b: (0, h)),
                  pl.BlockSpec((1, tq), lambda h, p, qi, kj, a, b: (0, qi[p])),
                  pl.BlockSpec((tq, 1), lambda h, p, qi, kj, a, b: (kj[p], 0)),
                  pl.BlockSpec((None, hp, LUT_SIZE), lambda h, p, qi, kj, a, b: (h, 0, 0))],
        out_specs=pl.BlockSpec((tq, hp * dh), lambda h, p, qi, kj, a, b: (qi[p], h)),
        scratch_shapes=[pltpu.VMEM((hp, 1, tq), F32), pltpu.VMEM((hp, 1, tq), F32),
                        pltpu.VMEM((hp, dh, tq), F32), pltpu.VMEM((hp, LANES, tq), F32),
                        pltpu.VMEM((hp, ck, tq), F32)],
    )
    return pl.pallas_call(
        kern,
        out_shape=jax.ShapeDtypeStruct((S, D), BF16),
        grid_spec=grid_spec,
        compiler_params=_cparams("parallel", "arbitrary"),
        name="moba_attention",
    )(qi, kj, pminq, pmaxk, qkv, qkv, vt, km_hi, km_lo, pos.reshape(1, S), pos.reshape(S, 1),
      lut.reshape(G, hp, LUT_SIZE))


def _rope_fold(y):
    lane = lax.broadcasted_iota(jnp.int32, y.shape, 1)
    return jnp.where(lane < MLA_ROPE, y + pltpu.roll(y, MLA_ROPE, 1), 0.0)


def _mla_q_kernel(cq_ref, g_ref, w_ref, kr_ref, t_ref, q_ref, kro_ref, cqn_ref, *, scale, hp):
    t = t_ref[...]
    W = 2 * LANES

    @pl.when(pl.program_id(1) == 0)
    def _():
        cqn_ref[...] = _rms_rows(cq_ref[...], g_ref[...]).astype(BF16)
        kro_ref[...] = _rope_fold(kr_ref[...] * t).astype(kro_ref.dtype)

    qh = jnp.dot(cqn_ref[...], w_ref[...], preferred_element_type=F32)
    for hd in range(hp):
        q_ref[:, hd * W:hd * W + MLA_NOPE] = (qh[:, hd * W:hd * W + MLA_NOPE] * scale).astype(q_ref.dtype)
        q_ref[:, hd * W + MLA_NOPE:(hd + 1) * W] = (
            _rope_fold(qh[:, hd * W + MLA_NOPE:(hd + 1) * W] * t) * scale).astype(q_ref.dtype)


def mla_queries_and_rope_key(down, g_q, w_uq_ext, rope_tab, *, scale, tm=1024, heads_per_step=4):
    S = down.shape[0]
    H, hp = MLA_HEADS, heads_per_step
    tm = _tile(S, tm)
    W = 2 * LANES
    return pl.pallas_call(
        functools.partial(_mla_q_kernel, scale=scale, hp=hp),
        out_shape=[jax.ShapeDtypeStruct((S, H * W), BF16), jax.ShapeDtypeStruct((S, LANES), BF16)],
        grid=(S // tm, H // hp),
        in_specs=[pl.BlockSpec((tm, MLA_Q_RANK), lambda i, h: (i, 0)),
                  pl.BlockSpec((1, MLA_Q_RANK), lambda i, h: (0, 0)),
                  pl.BlockSpec((MLA_Q_RANK, hp * W), lambda i, h: (0, h)),
                  pl.BlockSpec((tm, LANES), lambda i, h: (i, (MLA_Q_RANK + MLA_KV_RANK) // LANES)),
                  pl.BlockSpec((tm, LANES), lambda i, h: (i, 0))],
        out_specs=[pl.BlockSpec((tm, hp * W), lambda i, h: (i, h)),
                   pl.BlockSpec((tm, LANES), lambda i, h: (i, 0))],
        scratch_shapes=[pltpu.VMEM((tm, MLA_Q_RANK), BF16)],
        compiler_params=_cparams("parallel", "arbitrary"),
        name="mla_queries",
    )(down, g_q.reshape(1, MLA_Q_RANK), w_uq_ext, down, rope_tab)


def _mla_attn_kernel(qi_ref, kj_ref, q_ref, kv_ref, kr_ref, vt_ref, o_ref, m_ref, l_ref, acc_ref, *, tq, ck, hp):
    p = pl.program_id(1)
    i, j = qi_ref[p], kj_ref[p]
    nc = tq // ck
    W = 2 * LANES

    @pl.when(j == 0)
    def _():
        m_ref[...] = jnp.full(m_ref.shape, NEG_INF, F32)
        l_ref[...] = jnp.zeros(l_ref.shape, F32)
        acc_ref[...] = jnp.zeros(acc_ref.shape, F32)

    def scores(c, g, q0, diag):
        keys = slice(c * ck, (c + 1) * ck)
        kc = jnp.concatenate([kv_ref[keys, g * MLA_NOPE:(g + 1) * MLA_NOPE], kr_ref[keys, :]], axis=1)
        st = _nt_dot(kc, q_ref[q0:tq, g * W:(g + 1) * W])
        if diag:
            st = jnp.where(_causal_t(c * ck, ck, q0, tq - q0), st, NEG_INF)
        return st

    def consume(c, g, q0, diag, st):
        _online_softmax_step_t(st, vt_ref[g * MLA_V:(g + 1) * MLA_V, c * ck:(c + 1) * ck],
                               m_ref.at[g], l_ref.at[g], acc_ref.at[g], slice(q0, tq))

    def sweep(diag):
        _pipelined([(c, g, c * ck if diag else 0, diag) for c in range(nc) for g in range(hp)], scores, consume)

    @pl.when(j < i)
    def _():
        sweep(False)

    @pl.when(j == i)
    def _():
        sweep(True)
        for g in range(hp):
            o_ref[:, g * MLA_V:(g + 1) * MLA_V] = (acc_ref[g] / l_ref[g]).T.astype(o_ref.dtype)


def mla_attention(q, kv, kr, vt, *, tile=2048, chunk=512, heads_per_step=2):
    S = q.shape[0]
    H, hp = MLA_HEADS, heads_per_step
    tq = _tile(S, tile)
    ck = _tile(tq, chunk)
    qi, kj = _pair_tables(S // tq)
    W = 2 * LANES
    grid_spec = pltpu.PrefetchScalarGridSpec(
        num_scalar_prefetch=2,
        grid=(H // hp, qi.shape[0]),
        in_specs=[pl.BlockSpec((tq, hp * W), lambda h, p, qi, kj: (qi[p], h)),
                  pl.BlockSpec((tq, hp * MLA_NOPE), lambda h, p, qi, kj: (kj[p], h)),
                  pl.BlockSpec((tq, LANES), lambda h, p, qi, kj: (kj[p], 0)),
                  pl.BlockSpec((hp * MLA_V, tq), lambda h, p, qi, kj: (h, kj[p]))],
        out_specs=pl.BlockSpec((tq, hp * MLA_V), lambda h, p, qi, kj: (qi[p], h)),
        scratch_shapes=[pltpu.VMEM((hp, 1, tq), F32), pltpu.VMEM((hp, 1, tq), F32),
                        pltpu.VMEM((hp, MLA_V, tq), F32)],
    )
    return pl.pallas_call(
        functools.partial(_mla_attn_kernel, tq=tq, ck=ck, hp=hp),
        out_shape=jax.ShapeDtypeStruct((S, H * MLA_V), BF16),
        grid_spec=grid_spec,
        compiler_params=_cparams("parallel", "arbitrary"),
        name="mla_attention",
    )(qi, kj, q, kv, kr, vt)


_HI16 = 0xFFFF0000


def _pack_bf16_pairs(hi_f32):
    half = hi_f32.shape[1] // 2
    bits = pltpu.bitcast(hi_f32, jnp.uint32)
    return lax.shift_right_logical(bits[:, :half], jnp.uint32(16)) | (bits[:, half:] & jnp.uint32(_HI16))


def _unpack_pairs_by_tile(words, tile):
    half = tile // 2
    parts = []
    for c0 in range(0, words.shape[1], half):
        w = words[:, c0:c0 + half]
        parts.append(pltpu.bitcast(lax.shift_left(w, jnp.uint32(16)), F32))
        parts.append(pltpu.bitcast(w & jnp.uint32(_HI16), F32))
    return jnp.concatenate(parts, axis=1)


def _unpack_bf16_pairs(words):
    lo = pltpu.bitcast(lax.shift_left(words, jnp.uint32(16)), F32).astype(BF16)
    hi = pltpu.bitcast(words & jnp.uint32(_HI16), F32).astype(BF16)
    return jnp.concatenate([lo, hi], axis=1)


def _router_kernel(x_ref, g_ref, wh_ref, wl_ref, xn_ref, r_ref):
    xn = _rms_rows(x_ref[...], g_ref[...])
    hi = xn.astype(BF16)
    hi_f32 = hi.astype(F32)
    lo = (xn - hi_f32).astype(BF16)
    xn_ref[...] = _pack_bf16_pairs(hi_f32)
    logits = (jnp.dot(hi, wh_ref[...], preferred_element_type=F32)
              + jnp.dot(hi, wl_ref[...], preferred_element_type=F32)
              + jnp.dot(lo, wh_ref[...], preferred_element_type=F32))
    lane = lax.broadcasted_iota(jnp.int32, logits.shape, 1)
    lane_f = lane.astype(F32)
    g = jnp.where(lane < N_EXPERTS, logits, NEG_INF)
    v1 = jnp.max(g, axis=-1, keepdims=True)
    i1 = jnp.min(jnp.where(g == v1, lane_f, float(LANES)), axis=-1, keepdims=True)
    g = jnp.where(lane_f == i1, NEG_INF, g)
    v2 = jnp.max(g, axis=-1, keepdims=True)
    i2 = jnp.min(jnp.where(g == v2, lane_f, float(LANES)), axis=-1, keepdims=True)
    g2 = 1.0 / (1.0 + jnp.exp(v1 - v2))
    g1 = 1.0 - g2
    r_ref[...] = jnp.where(lane == 0, i1,
                           jnp.where(lane == 1, i2,
                                     jnp.where(lane == 2, g1, jnp.where(lane == 3, g2, 0.0))))


def moe_route(x, g, w_router, *, tm=512):
    M, D = x.shape
    tm = _tile(M, tm)
    wr = jnp.pad(w_router.astype(F32), ((0, 0), (0, LANES - N_EXPERTS)))
    wh = wr.astype(BF16)
    wl = (wr - wh.astype(F32)).astype(BF16)
    return pl.pallas_call(
        _router_kernel,
        out_shape=[jax.ShapeDtypeStruct((M, D // 2), jnp.uint32), jax.ShapeDtypeStruct((M, LANES), F32)],
        grid=(M // tm,),
        in_specs=[pl.BlockSpec((tm, D), lambda i: (i, 0)),
                  pl.BlockSpec((1, D), lambda i: (0, 0)),
                  pl.BlockSpec((D, LANES), lambda i: (0, 0)),
                  pl.BlockSpec((D, LANES), lambda i: (0, 0))],
        out_specs=[pl.BlockSpec((tm, D // 2), lambda i: (i, 0)),
                   pl.BlockSpec((tm, LANES), lambda i: (i, 0))],
        compiler_params=_cparams("parallel"),
        name="moe_route",
    )(x, g.reshape(1, D), wh, wl)


def _new_expert(te_ref, t):
    return jnp.logical_or(t == 0, te_ref[t] != te_ref[jnp.maximum(t - 1, 0)])


def _stream_expert_weights(te_ref, rid_ref, rexp_ref, nr_ref, used, n_col_tiles, copies_for, on_ready):
    j, t = pl.program_id(0), pl.program_id(1)

    @pl.when(jnp.logical_and(used, _new_expert(te_ref, t)))
    def _():
        n_runs = nr_ref[0]
        r = rid_ref[t]
        g = j * n_runs + r
        slot = lax.rem(g, 2)

        @pl.when(g == 0)
        def _():
            for c in copies_for(j, te_ref[t], slot):
                c.start()

        for c in copies_for(j, te_ref[t], slot):
            c.wait()
        on_ready(slot)
        last = r + 1 == n_runs
        nxt_r = jnp.where(last, 0, r + 1)
        nxt_j = jnp.where(last, j + 1, j)

        @pl.when(nxt_j < n_col_tiles)
        def _():
            for c in copies_for(nxt_j, rexp_ref[nxt_r], 1 - slot):
                c.start()


def _gmm_swiglu_kernel(te_ref, nu_ref, rid_ref, rexp_ref, nr_ref, *refs, n_parts, tiles_per_part, layer, nj, tn):
    x_refs = refs[:n_parts]
    w_hbm, o_ref, wbuf_ref, wgb_ref, wub_ref, sem = refs[n_parts:]
    t = pl.program_id(1)
    used = t < nu_ref[0]

    def copies_for(jj, e, slot):
        gate_cols = pl.ds(pl.multiple_of(jj * tn, tn), tn)
        up_cols = pl.ds(pl.multiple_of((nj + jj) * tn, tn), tn)
        return (pltpu.make_async_copy(w_hbm.at[layer, e, :, gate_cols], wbuf_ref.at[slot, 0], sem.at[slot, 0]),
                pltpu.make_async_copy(w_hbm.at[layer, e, :, up_cols], wbuf_ref.at[slot, 1], sem.at[slot, 1]))

    def on_ready(slot):
        wgb_ref[...] = wbuf_ref[slot, 0].astype(BF16)
        wub_ref[...] = wbuf_ref[slot, 1].astype(BF16)

    _stream_expert_weights(te_ref, rid_ref, rexp_ref, nr_ref, used, nj, copies_for, on_ready)

    for c in range(n_parts):
        @pl.when(jnp.logical_and(used, t // tiles_per_part == c))
        def _(c=c):
            x = _unpack_bf16_pairs(x_refs[c][...])
            a = jnp.dot(x, wgb_ref[...], preferred_element_type=F32)
            u = jnp.dot(x, wub_ref[...], preferred_element_type=F32)
            o_ref[...] = (a * jax.nn.sigmoid(a) * u).astype(o_ref.dtype)

    @pl.when(jnp.logical_not(used))
    def _():
        o_ref[...] = jnp.zeros(o_ref.shape, o_ref.dtype)


def _gmm_out_kernel(te_ref, nu_ref, rid_ref, rexp_ref, nr_ref, h_ref, w_hbm, o_ref, wbuf_ref, wb_ref, sem, *,
                    layer, nj, tn):
    t = pl.program_id(1)
    used = t < nu_ref[0]

    def copies_for(jj, e, slot):
        cols = pl.ds(pl.multiple_of(jj * tn, tn), tn)
        return (pltpu.make_async_copy(w_hbm.at[layer, e, :, cols], wbuf_ref.at[slot], sem.at[slot]),)

    def on_ready(slot):
        wb_ref[...] = wbuf_ref[slot].astype(BF16)

    _stream_expert_weights(te_ref, rid_ref, rexp_ref, nr_ref, used, nj, copies_for, on_ready)

    @pl.when(used)
    def _():
        y = jnp.dot(h_ref[...], wb_ref[...], preferred_element_type=F32)
        o_ref[...] = _pack_bf16_pairs(y.astype(BF16).astype(F32))

    @pl.when(jnp.logical_not(used))
    def _():
        o_ref[...] = jnp.zeros(o_ref.shape, o_ref.dtype)


def moe_experts(xs, tile_expert, n_used, runs, w_in, w_out, layer, *, tm, tn=MOE_COL_TILE):
    n_parts = len(xs)
    P = n_parts * xs[0].shape[0]
    D = w_in.shape[2]
    F = w_in.shape[3] // 2
    nt = P // tm
    ntp = nt // n_parts
    assert ntp * n_parts == nt
    tn1, tn2 = _tile(F, tn), _tile(D, tn)
    nj1, nj2 = F // tn1, D // tn2

    def part_spec(c):
        return pl.BlockSpec((tm, D // 2), lambda j, t, *_: (jnp.clip(t - c * ntp, 0, ntp - 1), 0))

    prefetch = (tile_expert, n_used) + tuple(runs)
    hbm = pl.BlockSpec(memory_space=pl.ANY)
    h = pl.pallas_call(
        functools.partial(_gmm_swiglu_kernel, n_parts=n_parts, tiles_per_part=ntp, layer=layer, nj=nj1, tn=tn1),
        out_shape=jax.ShapeDtypeStruct((P, F), BF16),
        grid_spec=pltpu.PrefetchScalarGridSpec(
            num_scalar_prefetch=len(prefetch),
            grid=(nj1, nt),
            in_specs=[part_spec(c) for c in range(n_parts)] + [hbm],
            out_specs=pl.BlockSpec((tm, tn1), lambda j, t, *_: (t, j)),
            scratch_shapes=[pltpu.VMEM((2, 2, D, tn1), F32), pltpu.VMEM((D, tn1), BF16),
                            pltpu.VMEM((D, tn1), BF16), pltpu.SemaphoreType.DMA((2, 2))],
        ),
        compiler_params=_cparams("arbitrary", "arbitrary"),
        name="moe_swiglu_in",
    )(*prefetch, *xs, w_in)
    return pl.pallas_call(
        functools.partial(_gmm_out_kernel, layer=layer, nj=nj2, tn=tn2),
        out_shape=jax.ShapeDtypeStruct((P, D // 2), jnp.uint32),
        grid_spec=pltpu.PrefetchScalarGridSpec(
            num_scalar_prefetch=len(prefetch),
            grid=(nj2, nt),
            in_specs=[pl.BlockSpec((tm, F), lambda j, t, *_: (t, 0)), hbm],
            out_specs=pl.BlockSpec((tm, tn2 // 2), lambda j, t, *_: (t, j)),
            scratch_shapes=[pltpu.VMEM((2, F, tn2), F32), pltpu.VMEM((F, tn2), BF16),
                            pltpu.SemaphoreType.DMA((2,))],
        ),
        compiler_params=_cparams("arbitrary", "arbitrary"),
        name="moe_out",
    )(*prefetch, h, w_out)


def moe_swiglu(x, g, w_router, w_in, w_out, layer, *, tm=512):
    S, D = x.shape
    E = N_EXPERTS
    tm = _tile(S, tm)
    xn, route = moe_route(x, g, w_router)
    experts = route[:, :2].astype(jnp.int32)
    gates = route[:, 2:4]
    flat_e = experts.reshape(-1)
    order = jnp.argsort(flat_e, stable=True).astype(jnp.int32)
    rank = jnp.argsort(order).astype(jnp.int32)
    counts = jnp.sum(flat_e[:, None] == jnp.arange(E, dtype=jnp.int32)[None, :], axis=0, dtype=jnp.int32)
    padded = ((counts + tm - 1) // tm) * tm
    start = jnp.cumsum(counts) - counts
    pstart = jnp.cumsum(padded) - padded
    pend = jnp.cumsum(padded)
    P = 2 * S + E * tm
    nt = P // tm
    tile_expert = jnp.minimum(
        jnp.sum(pend[None, :] <= (jnp.arange(nt, dtype=jnp.int32) * tm)[:, None], axis=1), E - 1).astype(jnp.int32)
    n_used = (pend[-1:] // tm).astype(jnp.int32)
    row_e = jnp.repeat(tile_expert, tm)
    row_rank = jnp.arange(P, dtype=jnp.int32) - pstart[row_e]
    row_valid = jnp.logical_and(row_rank < counts[row_e], jnp.arange(P) < pend[-1])
    row_token = jnp.where(row_valid, order[jnp.clip(start[row_e] + row_rank, 0, 2 * S - 1)] // 2, 0)
    slot = pstart[flat_e] + rank - start[flat_e]
    tile_expert = jnp.where(jnp.arange(nt) < n_used[0], tile_expert, tile_expert[jnp.maximum(n_used[0] - 1, 0)])
    slot = slot.reshape(S, 2)
    present = counts > 0
    run_of_expert = jnp.cumsum(present.astype(jnp.int32)) - 1
    runs = (run_of_expert[tile_expert].astype(jnp.int32),
            jnp.sort(jnp.where(present, jnp.arange(E, dtype=jnp.int32), E))[:E].clip(0, E - 1).astype(jnp.int32),
            jnp.sum(present).astype(jnp.int32).reshape(1))
    xs = [jnp.take(xn, part, axis=0, mode="clip") for part in jnp.split(row_token, MOE_GATHER_PARTS)]
    ys = moe_experts(xs, tile_expert, n_used, runs, w_in, w_out, layer, tm=tm)
    return jnp.take(ys, slot[:, 0], axis=0, mode="clip"), jnp.take(ys, slot[:, 1], axis=0, mode="clip"), gates


def _diff_lambda_init(layer):
    return 0.8 - 0.6 * math.exp(-0.3 * layer)


def _rope_table(pos):
    half = MLA_ROPE // 2
    inv_freq = ROPE_THETA ** (-jnp.arange(half, dtype=F32) / half)
    ang = pos.astype(F32)[:, None] * inv_freq
    cos, sin = jnp.cos(ang), jnp.sin(ang)
    return jnp.concatenate([cos, cos, -sin, sin], axis=1)


def _rot_half_cols(w):
    half = w.shape[-1] // 2
    return jnp.concatenate([w[..., half:], w[..., :half]], axis=-1)


def kernel(x, p, positions, rel_bias, norm_mix, norm_ffn, norm_ple, norm_final, diff_w_qkv, diff_lambda, diff_subln, diff_w_o, moba_w_qkv, moba_w_o, mla_w_down, mla_g_q, mla_w_uq, mla_g_kv, mla_w_ukv, mla_w_o, ffn_w_in, ffn_w_out, moe_w_router, moe_w_in, moe_w_out, ple_w_gate, ple_w_proj):
    B, S, D = x.shape
    depth = p.shape[0]
    lut = _bias_lut(rel_bias)
    outs = []
    for b in range(B):
        xb = x[b]
        pos = positions[b].astype(jnp.int32)
        for i in range(depth):
            jm = i // N_MIXERS
            if i % N_MIXERS == 0:
                qk, vt = norm_matmul_vt(xb, norm_mix[i], diff_w_qkv, layer=jm, n_direct=2 * D,
                                        scaled_cols=D, scale=DIFF_HEAD_DIM ** -0.5 * LOG2E)
                o = diff_attention(qk, vt, pos, lut, diff_lambda[jm].astype(F32), diff_subln[jm],
                                   _diff_lambda_init(i))
                xb = matmul_residual(o, diff_w_o, xb, layer=jm)
            elif i % N_MIXERS == 1:
                qk, vt = norm_matmul_vt(xb, norm_mix[i], moba_w_qkv, layer=jm, n_direct=2 * D,
                                        scaled_cols=D, scale=MOBA_HEAD_DIM ** -0.5 * LOG2E)
                o = moba_attention(qk, vt, pos, lut)
                xb = matmul_residual(o, moba_w_o, xb, layer=jm)
            else:
                wd = mla_w_down[jm]
                kr0 = MLA_Q_RANK + MLA_KV_RANK
                wd_ext = jnp.concatenate([wd, _rot_half_cols(wd[:, kr0:])], axis=1).astype(BF16)
                wq = mla_w_uq[jm].reshape(MLA_Q_RANK, MLA_HEADS, MLA_NOPE + MLA_ROPE)
                wq_ext = jnp.concatenate([wq, _rot_half_cols(wq[..., MLA_NOPE:])], axis=-1)
                wq_ext = wq_ext.reshape(MLA_Q_RANK, MLA_HEADS * 2 * LANES).astype(BF16)
                down = norm_matmul(xb, norm_mix[i], wd_ext, out_dtype=F32, tn=wd_ext.shape[1])
                q, kr = mla_queries_and_rope_key(down, mla_g_q[jm], wq_ext, _rope_table(pos),
                                                 scale=(MLA_NOPE + MLA_ROPE) ** -0.5 * LOG2E)
                wkv = mla_w_ukv[jm].reshape(MLA_KV_RANK, MLA_HEADS, 2, MLA_V).transpose(0, 2, 1, 3)
                wkv = wkv.reshape(MLA_KV_RANK, 2 * MLA_HEADS * MLA_V).astype(BF16)
                kn, vt = norm_matmul_vt(down, mla_g_kv[jm], wkv, n_direct=MLA_HEADS * MLA_NOPE, xcol=1)
                o = mla_attention(q, kn, kr, vt)
                xb = matmul_residual(o, mla_w_o, xb, layer=jm)
            if i % 2 == 0:
                h = norm_swiglu_in(xb, norm_ffn[i], ffn_w_in, layer=i // 2)
                xb = matmul_residual(h, ffn_w_out[i // 2].astype(BF16), xb)
                xb = ple_update(xb, norm_ple[i], p[i, b], ple_w_gate, ple_w_proj, layer=i)
            else:
                y1, y2, gates = moe_swiglu(xb, norm_ffn[i], moe_w_router[i // 2], moe_w_in, moe_w_out, i // 2)
                xb = ple_update_moe(xb, y1, y2, gates, norm_ple[i], p[i, b], ple_w_gate, ple_w_proj, layer=i)
        outs.append(final_norm(xb, norm_final))
    return jnp.stack(outs, axis=0)
```

```python
import functools
import math

import numpy as np
import jax
import jax.numpy as jnp
from jax import lax
from jax.experimental import pallas as pl
from jax.experimental.pallas import tpu as pltpu

F32 = jnp.float32
BF16 = jnp.bfloat16

NORM_EPS = 1e-6
NEG_INF = -1e30
LOG2E = math.log2(math.e)
LANES = 128
VMEM_LIMIT_BYTES = 56 * 1024 * 1024

REL_BUCKETS = 32
REL_MAX_DIST = 128
LUT_SIZE = LANES
BIAS_BLOCK = 256
DIFF_HEADS = 8
DIFF_HEAD_DIM = 128
MOBA_HEADS = 16
MOBA_HEAD_DIM = 128
MOBA_BLOCK = 256
MOBA_TOPK = 3
MLA_HEADS = 16
MLA_Q_RANK = 512
MLA_KV_RANK = 512
MLA_NOPE = 128
MLA_ROPE = 64
MLA_V = 128
ROPE_THETA = 10000.0
N_EXPERTS = 8
N_MIXERS = 3
MOE_GATHER_PARTS = 4
MOE_COL_TILE = 512


def _cparams(*sem):
    return pltpu.CompilerParams(dimension_semantics=sem, vmem_limit_bytes=VMEM_LIMIT_BYTES)


def _tile(n, pref):
    if n <= pref:
        return n
    t = pref
    while n % t:
        t //= 2
    return t


def _weight_spec(w, layer, tn, col=lambda j: j):
    K = w.shape[-2]
    if w.ndim == 2:
        return pl.BlockSpec((K, tn), lambda i, j: (0, col(j)))
    return pl.BlockSpec((None, K, tn), lambda i, j: (layer, 0, col(j)))


def _rms_rows(x, g):
    r = lax.rsqrt(jnp.mean(x * x, axis=-1, keepdims=True) + NORM_EPS)
    return x * r * g


def _norm_mm_kernel(x_ref, g_ref, w_ref, o_ref, xn_ref, *, scaled_tiles, scale):
    j = pl.program_id(1)

    @pl.when(j == 0)
    def _():
        xn_ref[...] = _rms_rows(x_ref[...], g_ref[...]).astype(BF16)

    acc = jnp.dot(xn_ref[...], w_ref[...].astype(BF16), preferred_element_type=F32)
    if scaled_tiles:
        acc = acc * jnp.where(j < scaled_tiles, scale, 1.0)
    o_ref[...] = acc.astype(o_ref.dtype)


def norm_matmul(x, g, w, *, out_dtype, xcol=0, tm=1024, tn=512, scaled_cols=0, scale=1.0):
    M = x.shape[0]
    K, N = w.shape
    tm, tn = _tile(M, tm), _tile(N, tn)
    assert scaled_cols % tn == 0
    kern = functools.partial(_norm_mm_kernel, scaled_tiles=scaled_cols // tn, scale=scale)
    return pl.pallas_call(
        kern,
        out_shape=jax.ShapeDtypeStruct((M, N), out_dtype),
        grid=(M // tm, N // tn),
        in_specs=[pl.BlockSpec((tm, K), lambda i, j: (i, xcol)),
                  pl.BlockSpec((1, K), lambda i, j: (0, 0)),
                  pl.BlockSpec((K, tn), lambda i, j: (0, j))],
        out_specs=pl.BlockSpec((tm, tn), lambda i, j: (i, j)),
        scratch_shapes=[pltpu.VMEM((tm, K), BF16)],
        compiler_params=_cparams("parallel", "arbitrary"),
        name="norm_matmul",
    )(x, g.reshape(1, K), w)


def _norm_mm_vt_kernel(x_ref, g_ref, w_ref, o_ref, ot_ref, xn_ref, *, direct_tiles, scaled_tiles, scale):
    j = pl.program_id(1)

    @pl.when(j == 0)
    def _():
        xn_ref[...] = _rms_rows(x_ref[...], g_ref[...]).astype(BF16)

    acc = jnp.dot(xn_ref[...], w_ref[...].astype(BF16), preferred_element_type=F32)

    @pl.when(j < direct_tiles)
    def _():
        out = acc * jnp.where(j < scaled_tiles, scale, 1.0) if scaled_tiles else acc
        o_ref[...] = out.astype(o_ref.dtype)

    @pl.when(j >= direct_tiles)
    def _():
        ot_ref[...] = acc.T.astype(ot_ref.dtype)


def norm_matmul_vt(x, g, w, *, n_direct, layer=None, xcol=0, tm=1024, tn=512, scaled_cols=0, scale=1.0):
    M = x.shape[0]
    K, N = w.shape[-2:]
    tm, tn = _tile(M, tm), _tile(N, tn)
    assert scaled_cols % tn == 0 and n_direct % tn == 0 and 0 < n_direct < N
    nd = n_direct // tn
    kern = functools.partial(_norm_mm_vt_kernel, direct_tiles=nd, scaled_tiles=scaled_cols // tn, scale=scale)
    return pl.pallas_call(
        kern,
        out_shape=[jax.ShapeDtypeStruct((M, n_direct), BF16), jax.ShapeDtypeStruct((N - n_direct, M), BF16)],
        grid=(M // tm, N // tn),
        in_specs=[pl.BlockSpec((tm, K), lambda i, j: (i, xcol)),
                  pl.BlockSpec((1, K), lambda i, j: (0, 0)),
                  _weight_spec(w, layer, tn)],
        out_specs=[pl.BlockSpec((tm, tn), lambda i, j: (i, jnp.minimum(j, nd - 1))),
                   pl.BlockSpec((tn, tm), lambda i, j: (jnp.maximum(j - nd, 0), i))],
        scratch_shapes=[pltpu.VMEM((tm, K), BF16)],
        compiler_params=_cparams("parallel", "arbitrary"),
        name="norm_matmul_vt",
    )(x, g.reshape(1, K), w)


def _norm_swiglu_kernel(x_ref, g_ref, wg_ref, wu_ref, o_ref, xn_ref):
    @pl.when(pl.program_id(1) == 0)
    def _():
        xn_ref[...] = _rms_rows(x_ref[...], g_ref[...]).astype(BF16)

    xn = xn_ref[...]
    a = jnp.dot(xn, wg_ref[...].astype(BF16), preferred_element_type=F32)
    u = jnp.dot(xn, wu_ref[...].astype(BF16), preferred_element_type=F32)
    o_ref[...] = (a * jax.nn.sigmoid(a) * u).astype(o_ref.dtype)


def norm_swiglu_in(x, g, w_in, *, layer=None, tm=1024, tn=512):
    M, K = x.shape
    F = w_in.shape[-1] // 2
    tm, tn = _tile(M, tm), _tile(F, tn)
    nj = F // tn
    return pl.pallas_call(
        _norm_swiglu_kernel,
        out_shape=jax.ShapeDtypeStruct((M, F), BF16),
        grid=(M // tm, nj),
        in_specs=[pl.BlockSpec((tm, K), lambda i, j: (i, 0)),
                  pl.BlockSpec((1, K), lambda i, j: (0, 0)),
                  _weight_spec(w_in, layer, tn),
                  _weight_spec(w_in, layer, tn, lambda j: j + nj)],
        out_specs=pl.BlockSpec((tm, tn), lambda i, j: (i, j)),
        scratch_shapes=[pltpu.VMEM((tm, K), BF16)],
        compiler_params=_cparams("parallel", "arbitrary"),
        name="norm_swiglu_in",
    )(x, g.reshape(1, K), w_in, w_in)


def _mm_res_kernel(a_ref, w_ref, r_ref, o_ref):
    o_ref[...] = r_ref[...] + jnp.dot(a_ref[...], w_ref[...].astype(BF16), preferred_element_type=F32)


def matmul_residual(a, w, res, *, layer=None, tm=1024, tn=512):
    M, K = a.shape
    N = w.shape[-1]
    tm, tn = _tile(M, tm), _tile(N, tn)
    return pl.pallas_call(
        _mm_res_kernel,
        out_shape=jax.ShapeDtypeStruct((M, N), F32),
        grid=(M // tm, N // tn),
        in_specs=[pl.BlockSpec((tm, K), lambda i, j: (i, 0)),
                  _weight_spec(w, layer, tn),
                  pl.BlockSpec((tm, tn), lambda i, j: (i, j))],
        out_specs=pl.BlockSpec((tm, tn), lambda i, j: (i, j)),
        compiler_params=_cparams("parallel", "parallel"),
        name="matmul_residual",
    )(a, w, res)


def _ple_kernel(x_ref, g_ref, xr_ref, p_ref, wg_ref, wp_ref, o_ref, xn_ref):
    @pl.when(pl.program_id(1) == 0)
    def _():
        xn_ref[...] = _rms_rows(x_ref[...], g_ref[...]).astype(BF16)

    gate = jax.nn.sigmoid(jnp.dot(xn_ref[...], wg_ref[...].astype(BF16), preferred_element_type=F32))
    proj = jnp.dot(p_ref[...].astype(BF16), wp_ref[...].astype(BF16), preferred_element_type=F32)
    o_ref[...] = xr_ref[...] + gate * proj


def ple_update(x, g, p, w_gate, w_proj, *, layer=None, tm=1024, tn=512):
    M, D = x.shape
    P = p.shape[1]
    tm, tn = _tile(M, tm), _tile(D, tn)
    return pl.pallas_call(
        _ple_kernel,
        out_shape=jax.ShapeDtypeStruct((M, D), F32),
        grid=(M // tm, D // tn),
        in_specs=[pl.BlockSpec((tm, D), lambda i, j: (i, 0)),
                  pl.BlockSpec((1, D), lambda i, j: (0, 0)),
                  pl.BlockSpec((tm, tn), lambda i, j: (i, j)),
                  pl.BlockSpec((tm, P), lambda i, j: (i, 0)),
                  _weight_spec(w_gate, layer, tn),
                  _weight_spec(w_proj, layer, tn)],
        out_specs=pl.BlockSpec((tm, tn), lambda i, j: (i, j)),
        scratch_shapes=[pltpu.VMEM((tm, D), BF16)],
        compiler_params=_cparams("parallel", "arbitrary"),
        name="ple_update",
    )(x, g.reshape(1, D), x, p, w_gate, w_proj)


def _ple_moe_kernel(x_ref, y1_ref, y2_ref, gt_ref, g_ref, p_ref, wg_ref, wp_ref, o_ref, xs_ref, xn_ref, *, tn,
                    pair_tile):
    j = pl.program_id(1)

    @pl.when(j == 0)
    def _():
        gt = gt_ref[...]
        xnew = (x_ref[...] + gt[:, 0:1] * _unpack_pairs_by_tile(y1_ref[...], pair_tile)
                + gt[:, 1:2] * _unpack_pairs_by_tile(y2_ref[...], pair_tile))
        xn_ref[...] = _rms_rows(xnew, g_ref[...]).astype(BF16)
        for jj in range(xs_ref.shape[0]):
            xs_ref[jj] = xnew[:, jj * tn:(jj + 1) * tn]

    gate = jax.nn.sigmoid(jnp.dot(xn_ref[...], wg_ref[...].astype(BF16), preferred_element_type=F32))
    proj = jnp.dot(p_ref[...].astype(BF16), wp_ref[...].astype(BF16), preferred_element_type=F32)
    o_ref[...] = xs_ref[j] + gate * proj


def ple_update_moe(x, y1, y2, gates, g, p, w_gate, w_proj, *, layer=None, tm=512, tn=512):
    M, D = x.shape
    P = p.shape[1]
    tm, tn = _tile(M, tm), _tile(D, tn)
    row = pl.BlockSpec((tm, D), lambda i, j: (i, 0))
    packed_row = pl.BlockSpec((tm, D // 2), lambda i, j: (i, 0))
    return pl.pallas_call(
        functools.partial(_ple_moe_kernel, tn=tn, pair_tile=_tile(D, MOE_COL_TILE)),
        out_shape=jax.ShapeDtypeStruct((M, D), F32),
        grid=(M // tm, D // tn),
        in_specs=[row, packed_row, packed_row,
                  pl.BlockSpec((tm, 2), lambda i, j: (i, 0)),
                  pl.BlockSpec((1, D), lambda i, j: (0, 0)),
                  pl.BlockSpec((tm, P), lambda i, j: (i, 0)),
                  _weight_spec(w_gate, layer, tn),
                  _weight_spec(w_proj, layer, tn)],
        out_specs=pl.BlockSpec((tm, tn), lambda i, j: (i, j)),
        scratch_shapes=[pltpu.VMEM((D // tn, tm, tn), F32), pltpu.VMEM((tm, D), BF16)],
        compiler_params=_cparams("parallel", "arbitrary"),
        name="ple_update_moe",
    )(x, y1, y2, gates, g.reshape(1, D), p, w_gate, w_proj)


def _final_norm_kernel(x_ref, g_ref, o_ref):
    o_ref[...] = _rms_rows(x_ref[...], g_ref[...])


def final_norm(x, g, *, tm=512):
    M, D = x.shape
    tm = _tile(M, tm)
    return pl.pallas_call(
        _final_norm_kernel,
        out_shape=jax.ShapeDtypeStruct((M, D), F32),
        grid=(M // tm,),
        in_specs=[pl.BlockSpec((tm, D), lambda i: (i, 0)),
                  pl.BlockSpec((1, D), lambda i: (0, 0))],
        out_specs=pl.BlockSpec((tm, D), lambda i: (i, 0)),
        compiler_params=_cparams("parallel"),
        name="final_norm",
    )(x, g.reshape(1, D))


def _pair_tables(n_tiles):
    qi, kj = [], []
    for i in range(n_tiles):
        for j in range(i + 1):
            qi.append(i)
            kj.append(j)
    return jnp.asarray(np.array(qi, np.int32)), jnp.asarray(np.array(kj, np.int32))


def _rel_bucket(dist):
    n = jnp.maximum(dist, 0)
    max_exact = REL_BUCKETS // 2
    nf = jnp.maximum(n, 1).astype(F32)
    large = max_exact + (jnp.log(nf / max_exact) / math.log(REL_MAX_DIST / max_exact)
                         * (REL_BUCKETS - max_exact)).astype(jnp.int32)
    large = jnp.minimum(large, REL_BUCKETS - 1)
    return jnp.where(n < max_exact, n, large)


def _bias_lut(rel_bias):
    buckets = _rel_bucket(jnp.arange(LUT_SIZE, dtype=jnp.int32))
    return rel_bias.astype(F32)[buckets].T * LOG2E


def _block_pos_bounds(pos, sb):
    blocks = pos.reshape(pos.shape[0] // sb, sb)
    return blocks.min(axis=1), blocks.max(axis=1)


def _lut_bias(lut_row, idx):
    R, C = idx.shape
    lut = jnp.broadcast_to(lut_row, (R, LANES))
    parts = [jnp.take_along_axis(lut, idx[:, c:c + LANES], axis=1, mode="promise_in_bounds")
             for c in range(0, C, LANES)]
    return parts[0] if len(parts) == 1 else jnp.concatenate(parts, axis=1)


def _pipelined(units, scores, consume):
    st = scores(*units[0])
    for k, u in enumerate(units):
        nxt = scores(*units[k + 1]) if k + 1 < len(units) else None
        consume(*u, st)
        st = nxt


def _online_softmax_step_t(st, vt, m_ref, l_ref, acc_ref, qcols, m_cur=None, offsets=None):
    m_prev = m_ref[:, qcols]
    if m_cur is None:
        m_cur = jnp.max(st, axis=0, keepdims=True)
    m_new = jnp.maximum(m_prev, m_cur)
    alpha = jnp.exp2(m_prev - m_new)
    if offsets is None:
        p = jnp.exp2(st - m_new)
    else:
        parts = [jnp.exp2(st[rows, :] - fn(m_new)) for rows, fn in offsets]
        p = parts[0] if len(parts) == 1 else jnp.concatenate(parts, axis=0)
    l_ref[:, qcols] = alpha * l_ref[:, qcols] + jnp.sum(p, axis=0, keepdims=True)
    acc_ref[:, qcols] = alpha * acc_ref[:, qcols] + jnp.dot(vt, p.astype(BF16), preferred_element_type=F32)
    m_ref[:, qcols] = m_new


def _causal_t(k0, nk, q0, nq):
    r = k0 + lax.broadcasted_iota(jnp.int32, (nk, nq), 0)
    c = q0 + lax.broadcasted_iota(jnp.int32, (nk, nq), 1)
    return r <= c


def _nt_dot(a, b):
    return lax.dot_general(a, b, (((1,), (1,)), ((), ())), preferred_element_type=F32)


def _diff_attn_kernel(qi_ref, kj_ref, pminq_ref, pmaxk_ref,
                      q_ref, k_ref, vt_ref, pq_ref, pk_ref, lut_ref, lam_ref, g_ref,
                      o_ref, m_ref, l_ref, acc_ref, st_ref, *, tq, ck, sb, hp, lambda_init):
    p = pl.program_id(1)
    i, j = qi_ref[p], kj_ref[p]
    nc = tq // ck
    dh = DIFF_HEAD_DIM
    nm = 2 * hp
    SB = sb
    nqb, nkb = tq // SB, ck // SB

    @pl.when(j == 0)
    def _():
        m_ref[...] = jnp.full(m_ref.shape, NEG_INF, F32)
        l_ref[...] = jnp.zeros(l_ref.shape, F32)
        acc_ref[...] = jnp.zeros(acc_ref.shape, F32)

    def near_sub(c, kb, qb):
        return pminq_ref[i * nqb + qb] - pmaxk_ref[(j * nc + c) * nkb + kb] < LUT_SIZE - 1

    def refine_chunk(c):
        keys = slice(c * ck, (c + 1) * ck)
        for mp in range(nm):
            st_ref[mp] = _nt_dot(k_ref[keys, mp * dh:(mp + 1) * dh], q_ref[:, mp * dh:(mp + 1) * dh])
        for kb in range(nkb):
            for qb in range(nqb):
                @pl.when(near_sub(c, kb, qb))
                def _(kb=kb, qb=qb):
                    rows, cols = slice(kb * SB, (kb + 1) * SB), slice(qb * SB, (qb + 1) * SB)
                    idx = jnp.clip(pq_ref[:, cols] - pk_ref[c * ck + kb * SB:c * ck + (kb + 1) * SB, :],
                                   0, LUT_SIZE - 1)
                    for mp in range(nm):
                        far_bias = lut_ref[mp:mp + 1, LUT_SIZE - 1:LUT_SIZE]
                        st_ref[mp, rows, cols] = (st_ref[mp, rows, cols]
                                                  + (_lut_bias(lut_ref[mp:mp + 1, :], idx) - far_bias))
        for mp in range(nm):
            consume(c, mp, 0, "far", st_ref[mp])

    def scores(c, mp, q0, mode):
        keys = slice(c * ck, (c + 1) * ck)
        st = _nt_dot(k_ref[keys, mp * dh:(mp + 1) * dh], q_ref[q0:tq, mp * dh:(mp + 1) * dh])
        if mode != "far":
            idx = jnp.clip(pq_ref[:, q0:tq] - pk_ref[keys, :], 0, LUT_SIZE - 1)
            st = st + _lut_bias(lut_ref[mp:mp + 1, :], idx)
        if mode == "diag":
            st = jnp.where(_causal_t(c * ck, ck, q0, tq - q0), st, NEG_INF)
        return st

    def consume(c, mp, q0, mode, st):
        m_cur = offsets = None
        if mode == "far":
            bias = lut_ref[mp:mp + 1, LUT_SIZE - 1:LUT_SIZE]
            m_cur = jnp.max(st, axis=0, keepdims=True) + bias
            offsets = [(slice(None), lambda m_new: m_new - bias)]
        hd = mp // 2
        _online_softmax_step_t(st, vt_ref[hd * 2 * dh:(hd + 1) * 2 * dh, c * ck:(c + 1) * ck],
                               m_ref.at[mp], l_ref.at[mp], acc_ref.at[mp], slice(q0, tq), m_cur, offsets)

    def sweep(chunks, mode):
        units = [(c, mp, c * ck if mode == "diag" else 0, mode) for c in chunks for mp in range(nm)]
        _pipelined(units, scores, consume)

    @pl.when(j < i)
    def _():
        fars = [jnp.logical_not(functools.reduce(
            jnp.logical_or, [near_sub(c, kb, qb) for kb in range(nkb) for qb in range(nqb)])) for c in range(nc)]
        all_far = functools.reduce(jnp.logical_and, fars)
        pl.when(all_far)(functools.partial(sweep, range(nc), "far"))

        @pl.when(jnp.logical_not(all_far))
        def _():
            for c in range(nc):
                pl.when(fars[c])(functools.partial(sweep, [c], "far"))
                pl.when(jnp.logical_not(fars[c]))(functools.partial(refine_chunk, c))

    @pl.when(j == i)
    def _():
        sweep(range(nc), "diag")
        lam = lam_ref[...]
        lam_full = (jnp.exp(jnp.sum(lam[0:1] * lam[1:2], axis=-1, keepdims=True))
                    - jnp.exp(jnp.sum(lam[2:3] * lam[3:4], axis=-1, keepdims=True)) + lambda_init)
        for hd in range(hp):
            a = (acc_ref[2 * hd] / l_ref[2 * hd]
                 - lam_full * (acc_ref[2 * hd + 1] / l_ref[2 * hd + 1]))
            r = lax.rsqrt(jnp.mean(a * a, axis=0, keepdims=True) + NORM_EPS)
            y = a * r * (g_ref[...] * (1.0 - lambda_init))
            o_ref[:, hd * 2 * dh:(hd + 1) * 2 * dh] = y.T.astype(o_ref.dtype)


def diff_attention(qkv, vt, pos, lut, lam, subln_g, lambda_init, *, tile=1024, chunk=512, heads_per_step=1):
    S = qkv.shape[0]
    H, dh = DIFF_HEADS, DIFF_HEAD_DIM
    tq = _tile(S, tile)
    ck = _tile(tq, chunk)
    qi, kj = _pair_tables(S // tq)
    sb = _tile(ck, BIAS_BLOCK)
    pminq, pmaxk = _block_pos_bounds(pos, sb)
    hp = heads_per_step
    G = H // hp
    W = hp * 2 * dh
    kern = functools.partial(_diff_attn_kernel, tq=tq, ck=ck, sb=sb, hp=hp, lambda_init=lambda_init)
    grid_spec = pltpu.PrefetchScalarGridSpec(
        num_scalar_prefetch=4,
        grid=(G, qi.shape[0]),
        in_specs=[pl.BlockSpec((tq, W), lambda h, p, qi, kj, a, b: (qi[p], h)),
                  pl.BlockSpec((tq, W), lambda h, p, qi, kj, a, b: (kj[p], G + h)),
                  pl.BlockSpec((W, tq), lambda h, p, qi, kj, a, b: (h, kj[p])),
                  pl.BlockSpec((1, tq), lambda h, p, qi, kj, a, b: (0, qi[p])),
                  pl.BlockSpec((tq, 1), lambda h, p, qi, kj, a, b: (kj[p], 0)),
                  pl.BlockSpec((None, 2 * hp, LUT_SIZE), lambda h, p, qi, kj, a, b: (h, 0, 0)),
                  pl.BlockSpec((4, dh), lambda h, p, qi, kj, a, b: (0, 0)),
                  pl.BlockSpec((2 * dh, 1), lambda h, p, qi, kj, a, b: (0, 0))],
        out_specs=pl.BlockSpec((tq, W), lambda h, p, qi, kj, a, b: (qi[p], h)),
        scratch_shapes=[pltpu.VMEM((2 * hp, 1, tq), F32), pltpu.VMEM((2 * hp, 1, tq), F32),
                        pltpu.VMEM((2 * hp, 2 * dh, tq), F32), pltpu.VMEM((2 * hp, ck, tq), F32)],
    )
    return pl.pallas_call(
        kern,
        out_shape=jax.ShapeDtypeStruct((S, H * 2 * dh), BF16),
        grid_spec=grid_spec,
        compiler_params=_cparams("parallel", "arbitrary"),
        name="diff_attention",
    )(qi, kj, pminq, pmaxk, qkv, qkv, vt, pos.reshape(1, S), pos.reshape(S, 1),
      lut.reshape(G, 2 * hp, LUT_SIZE), lam, subln_g.reshape(2 * dh, 1))


def _kmean_kernel(k_ref, hi_ref, lo_ref):
    km = jnp.mean(k_ref[...].astype(F32), axis=0, keepdims=True)
    hi = km.astype(BF16)
    hi_ref[...] = hi
    lo_ref[...] = (km - hi.astype(F32)).astype(BF16)


def moba_block_means(qkv, D):
    S = qkv.shape[0]
    nblk = S // MOBA_BLOCK
    hi, lo = pl.pallas_call(
        _kmean_kernel,
        out_shape=[jax.ShapeDtypeStruct((nblk, 1, D), BF16)] * 2,
        grid=(nblk,),
        in_specs=[pl.BlockSpec((MOBA_BLOCK, D), lambda n: (n, 1))],
        out_specs=[pl.BlockSpec((None, 1, D), lambda n: (n, 0, 0))] * 2,
        compiler_params=_cparams("parallel"),
        name="moba_block_means",
    )(qkv)
    return hi.reshape(nblk, D), lo.reshape(nblk, D)


def _moba_attn_kernel(qi_ref, kj_ref, pminq_ref, pmaxk_ref,
                      q_ref, k_ref, vt_ref, kmh_ref, kml_ref, pq_ref, pk_ref, lut_ref,
                      o_ref, m_ref, l_ref, acc_ref, sel_ref, st_ref, *, tq, ck, sb, hp):
    p = pl.program_id(1)
    i, j = qi_ref[p], kj_ref[p]
    nc = tq // ck
    L, dh = MOBA_BLOCK, MOBA_HEAD_DIM
    bpc = ck // L
    log2_l = L.bit_length() - 1
    SB = sb
    nqb, nkb = tq // SB, ck // SB

    def near_sub(c, kb, qb):
        return pminq_ref[i * nqb + qb] - pmaxk_ref[(j * nc + c) * nkb + kb] < LUT_SIZE - 1

    def refine_chunk(c):
        keys = slice(c * ck, (c + 1) * ck)
        for hd in range(hp):
            st_ref[hd] = _nt_dot(k_ref[keys, hd * dh:(hd + 1) * dh], q_ref[:, hd * dh:(hd + 1) * dh])
        for kb in range(nkb):
            for qb in range(nqb):
                @pl.when(near_sub(c, kb, qb))
                def _(kb=kb, qb=qb):
                    rows, cols = slice(kb * SB, (kb + 1) * SB), slice(qb * SB, (qb + 1) * SB)
                    idx = jnp.clip(pq_ref[:, cols] - pk_ref[c * ck + kb * SB:c * ck + (kb + 1) * SB, :],
                                   0, LUT_SIZE - 1)
                    for hd in range(hp):
                        far_bias = lut_ref[hd:hd + 1, LUT_SIZE - 1:LUT_SIZE]
                        st_ref[hd, rows, cols] = (st_ref[hd, rows, cols]
                                                  + (_lut_bias(lut_ref[hd:hd + 1, :], idx) - far_bias))
        for hd in range(hp):
            consume(c, hd, 0, "far", st_ref[hd])

    @pl.when(j == 0)
    def _():
        m_ref[...] = jnp.full(m_ref.shape, NEG_INF, F32)
        l_ref[...] = jnp.zeros(l_ref.shape, F32)
        acc_ref[...] = jnp.zeros(acc_ref.shape, F32)
        blk = lax.broadcasted_iota(jnp.int32, (LANES, tq), 0)
        blk_f = blk.astype(F32)
        own = lax.shift_right_logical(i * tq + lax.broadcasted_iota(jnp.int32, (1, tq), 1), log2_l)
        for hd in range(hp):
            q = q_ref[:, hd * dh:(hd + 1) * dh]
            gs = (_nt_dot(kmh_ref[:, hd * dh:(hd + 1) * dh], q)
                  + _nt_dot(kml_ref[:, hd * dh:(hd + 1) * dh], q))
            g = jnp.where(blk < own, gs, NEG_INF)
            sel = jnp.zeros((LANES, tq), F32)
            for _ in range(MOBA_TOPK):
                best = jnp.max(g, axis=0, keepdims=True)
                first = jnp.min(jnp.where(g == best, blk_f, float(LANES)), axis=0, keepdims=True)
                pick = blk_f == first
                sel = jnp.where(pick, jnp.where(best > 0.5 * NEG_INF, 1.0, sel), sel)
                g = jnp.where(pick, -3e38, g)
            sel_ref[hd] = sel

    def scores(c, hd, q0, mode):
        keys = slice(c * ck, (c + 1) * ck)
        nq = tq - q0
        st = _nt_dot(k_ref[keys, hd * dh:(hd + 1) * dh], q_ref[q0:tq, hd * dh:(hd + 1) * dh])
        if mode != "far":
            idx = jnp.clip(pq_ref[:, q0:tq] - pk_ref[keys, :], 0, LUT_SIZE - 1)
            st = st + _lut_bias(lut_ref[hd:hd + 1, :], idx)
        if mode != "diag":
            return st
        allowed = []
        for b in range(bpc):
            n_local = c * bpc + b
            picked = jnp.broadcast_to(picked_row(hd, n_local, q0), (L, nq))
            own_local = lax.shift_right_logical(q0 + lax.broadcasted_iota(jnp.int32, (L, nq), 1), log2_l)
            visible = jnp.where(_causal_t(n_local * L, L, q0, nq), 1.0, 0.0)
            allowed.append(jnp.where(own_local == n_local, visible, picked))
        allowed = allowed[0] if bpc == 1 else jnp.concatenate(allowed, axis=0)
        return jnp.where(allowed > 0.0, st, NEG_INF)

    def picked_row(hd, n_local, q0):
        return sel_ref[hd, pl.ds(j * (tq // L) + n_local, 1), q0:tq]

    def consume(c, hd, q0, mode, st):
        m_cur = offsets = None
        if mode != "diag":
            bias = lut_ref[hd:hd + 1, LUT_SIZE - 1:LUT_SIZE] if mode == "far" else 0.0
            picked = [picked_row(hd, c * bpc + b, q0) > 0.0 for b in range(bpc)]
            m_cur = functools.reduce(jnp.maximum, [
                jnp.where(picked[b], jnp.max(st[b * L:(b + 1) * L, :], axis=0, keepdims=True) + bias, NEG_INF)
                for b in range(bpc)])
            offsets = [(slice(b * L, (b + 1) * L),
                        functools.partial(lambda m_new, pk: jnp.where(pk, m_new - bias, -NEG_INF), pk=picked[b]))
                       for b in range(bpc)]
        _online_softmax_step_t(st, vt_ref[hd * dh:(hd + 1) * dh, c * ck:(c + 1) * ck],
                               m_ref.at[hd], l_ref.at[hd], acc_ref.at[hd], slice(q0, tq), m_cur, offsets)

    def sweep(chunks, mode):
        _pipelined([(c, hd, c * ck if mode == "diag" else 0, mode) for c in chunks for hd in range(hp)],
                   scores, consume)

    @pl.when(j < i)
    def _():
        fars = [jnp.logical_not(functools.reduce(
            jnp.logical_or, [near_sub(c, kb, qb) for kb in range(nkb) for qb in range(nqb)])) for c in range(nc)]
        all_far = functools.reduce(jnp.logical_and, fars)
        pl.when(all_far)(functools.partial(sweep, range(nc), "far"))

        @pl.when(jnp.logical_not(all_far))
        def _():
            for c in range(nc):
                pl.when(fars[c])(functools.partial(sweep, [c], "far"))
                pl.when(jnp.logical_not(fars[c]))(functools.partial(refine_chunk, c))

    @pl.when(j == i)
    def _():
        sweep(range(nc), "diag")
        for hd in range(hp):
            o_ref[:, hd * dh:(hd + 1) * dh] = (acc_ref[hd] / l_ref[hd]).T.astype(o_ref.dtype)


def moba_attention(qkv, vt, pos, lut, *, tile=1024, chunk=512, heads_per_step=2):
    S = qkv.shape[0]
    H, dh, L = MOBA_HEADS, MOBA_HEAD_DIM, MOBA_BLOCK
    D = H * dh
    nblk = S // L
    assert S % L == 0 and nblk <= LANES
    tq = _tile(S, tile)
    ck = _tile(tq, chunk)
    assert tq % L == 0 and ck % L == 0
    km_hi, km_lo = moba_block_means(qkv, D)
    km_hi = jnp.pad(km_hi, ((0, LANES - nblk), (0, 0)))
    km_lo = jnp.pad(km_lo, ((0, LANES - nblk), (0, 0)))
    qi, kj = _pair_tables(S // tq)
    sb = _tile(ck, BIAS_BLOCK)
    pminq, pmaxk = _block_pos_bounds(pos, sb)
    hp = heads_per_step
    G = H // hp
    kern = functools.partial(_moba_attn_kernel, tq=tq, ck=ck, sb=sb, hp=hp)
    grid_spec = pltpu.PrefetchScalarGridSpec(
        num_scalar_prefetch=4,
        grid=(G, qi.shape[0]),
        in_specs=[pl.BlockSpec((tq, hp * dh), lambda h, p, qi, kj, a, b: (qi[p], h)),
                  pl.BlockSpec((tq, hp * dh), lambda h, p, qi, kj, a, b: (kj[p], G + h)),
                  pl.BlockSpec((hp * dh, tq), lambda h, p, qi, kj, a, b: (h, kj[p])),
                  pl.BlockSpec((LANES, hp * dh), lambda h, p, qi, kj, a, b: (0, h)),
                  pl.BlockSpec((LANES, hp * dh), lambda h, p, qi, kj, a, b: (0, h)),
                  pl.BlockSpec((1, tq), lambda h, p, qi, kj, a, b: (0, qi[p])),
                  pl.BlockSpec((tq, 1), lambda h, p, qi, kj, a, b: (kj[p], 0)),
                  pl.BlockSpec((None, hp, LUT_SIZE), lambda h, p, qi, kj, a, b: (h, 0, 0))],
        out_specs=pl.BlockSpec((tq, hp * dh), lambda h, p, qi, kj, a, b: (qi[p], h)),
        scratch_shapes=[pltpu.VMEM((hp, 1, tq), F32), pltpu.VMEM((hp, 1, tq), F32),
                        pltpu.VMEM((hp, dh, tq), F32), pltpu.VMEM((hp, LANES, tq), F32),
                        pltpu.VMEM((hp, ck, tq), F32)],
    )
    return pl.pallas_call(
        kern,
        out_shape=jax.ShapeDtypeStruct((S, D), BF16),
        grid_spec=grid_spec,
        compiler_params=_cparams("parallel", "arbitrary"),
        name="moba_attention",
    )(qi, kj, pminq, pmaxk, qkv, qkv, vt, km_hi, km_lo, pos.reshape(1, S), pos.reshape(S, 1),
      lut.reshape(G, hp, LUT_SIZE))


def _rope_fold(y):
    lane = lax.broadcasted_iota(jnp.int32, y.shape, 1)
    return jnp.where(lane < MLA_ROPE, y + pltpu.roll(y, MLA_ROPE, 1), 0.0)


def _mla_q_kernel(cq_ref, g_ref, w_ref, kr_ref, t_ref, q_ref, kro_ref, cqn_ref, *, scale, hp):
    t = t_ref[...]
    W = 2 * LANES

    @pl.when(pl.program_id(1) == 0)
    def _():
        cqn_ref[...] = _rms_rows(cq_ref[...], g_ref[...]).astype(BF16)
        kro_ref[...] = _rope_fold(kr_ref[...] * t).astype(kro_ref.dtype)

    qh = jnp.dot(cqn_ref[...], w_ref[...], preferred_element_type=F32)
    for hd in range(hp):
        q_ref[:, hd * W:hd * W + MLA_NOPE] = (qh[:, hd * W:hd * W + MLA_NOPE] * scale).astype(q_ref.dtype)
        q_ref[:, hd * W + MLA_NOPE:(hd + 1) * W] = (
            _rope_fold(qh[:, hd * W + MLA_NOPE:(hd + 1) * W] * t) * scale).astype(q_ref.dtype)


def mla_queries_and_rope_key(down, g_q, w_uq_ext, rope_tab, *, scale, tm=1024, heads_per_step=4):
    S = down.shape[0]
    H, hp = MLA_HEADS, heads_per_step
    tm = _tile(S, tm)
    W = 2 * LANES
    return pl.pallas_call(
        functools.partial(_mla_q_kernel, scale=scale, hp=hp),
        out_shape=[jax.ShapeDtypeStruct((S, H * W), BF16), jax.ShapeDtypeStruct((S, LANES), BF16)],
        grid=(S // tm, H // hp),
        in_specs=[pl.BlockSpec((tm, MLA_Q_RANK), lambda i, h: (i, 0)),
                  pl.BlockSpec((1, MLA_Q_RANK), lambda i, h: (0, 0)),
                  pl.BlockSpec((MLA_Q_RANK, hp * W), lambda i, h: (0, h)),
                  pl.BlockSpec((tm, LANES), lambda i, h: (i, (MLA_Q_RANK + MLA_KV_RANK) // LANES)),
                  pl.BlockSpec((tm, LANES), lambda i, h: (i, 0))],
        out_specs=[pl.BlockSpec((tm, hp * W), lambda i, h: (i, h)),
                   pl.BlockSpec((tm, LANES), lambda i, h: (i, 0))],
        scratch_shapes=[pltpu.VMEM((tm, MLA_Q_RANK), BF16)],
        compiler_params=_cparams("parallel", "arbitrary"),
        name="mla_queries",
    )(down, g_q.reshape(1, MLA_Q_RANK), w_uq_ext, down, rope_tab)


def _mla_attn_kernel(qi_ref, kj_ref, q_ref, kv_ref, kr_ref, vt_ref, o_ref, m_ref, l_ref, acc_ref, *, tq, ck, hp):
    p = pl.program_id(1)
    i, j = qi_ref[p], kj_ref[p]
    nc = tq // ck
    W = 2 * LANES

    @pl.when(j == 0)
    def _():
        m_ref[...] = jnp.full(m_ref.shape, NEG_INF, F32)
        l_ref[...] = jnp.zeros(l_ref.shape, F32)
        acc_ref[...] = jnp.zeros(acc_ref.shape, F32)

    def scores(c, g, q0, diag):
        keys = slice(c * ck, (c + 1) * ck)
        kc = jnp.concatenate([kv_ref[keys, g * MLA_NOPE:(g + 1) * MLA_NOPE], kr_ref[keys, :]], axis=1)
        st = _nt_dot(kc, q_ref[q0:tq, g * W:(g + 1) * W])
        if diag:
            st = jnp.where(_causal_t(c * ck, ck, q0, tq - q0), st, NEG_INF)
        return st

    def consume(c, g, q0, diag, st):
        _online_softmax_step_t(st, vt_ref[g * MLA_V:(g + 1) * MLA_V, c * ck:(c + 1) * ck],
                               m_ref.at[g], l_ref.at[g], acc_ref.at[g], slice(q0, tq))

    def sweep(diag):
        _pipelined([(c, g, c * ck if diag else 0, diag) for c in range(nc) for g in range(hp)], scores, consume)

    @pl.when(j < i)
    def _():
        sweep(False)

    @pl.when(j == i)
    def _():
        sweep(True)
        for g in range(hp):
            o_ref[:, g * MLA_V:(g + 1) * MLA_V] = (acc_ref[g] / l_ref[g]).T.astype(o_ref.dtype)


def mla_attention(q, kv, kr, vt, *, tile=2048, chunk=1024, heads_per_step=2):
    S = q.shape[0]
    H, hp = MLA_HEADS, heads_per_step
    tq = _tile(S, tile)
    ck = _tile(tq, chunk)
    qi, kj = _pair_tables(S // tq)
    W = 2 * LANES
    grid_spec = pltpu.PrefetchScalarGridSpec(
        num_scalar_prefetch=2,
        grid=(H // hp, qi.shape[0]),
        in_specs=[pl.BlockSpec((tq, hp * W), lambda h, p, qi, kj: (qi[p], h)),
                  pl.BlockSpec((tq, hp * MLA_NOPE), lambda h, p, qi, kj: (kj[p], h)),
                  pl.BlockSpec((tq, LANES), lambda h, p, qi, kj: (kj[p], 0)),
                  pl.BlockSpec((hp * MLA_V, tq), lambda h, p, qi, kj: (h, kj[p]))],
        out_specs=pl.BlockSpec((tq, hp * MLA_V), lambda h, p, qi, kj: (qi[p], h)),
        scratch_shapes=[pltpu.VMEM((hp, 1, tq), F32), pltpu.VMEM((hp, 1, tq), F32),
                        pltpu.VMEM((hp, MLA_V, tq), F32)],
    )
    return pl.pallas_call(
        functools.partial(_mla_attn_kernel, tq=tq, ck=ck, hp=hp),
        out_shape=jax.ShapeDtypeStruct((S, H * MLA_V), BF16),
        grid_spec=grid_spec,
        compiler_params=_cparams("parallel", "arbitrary"),
        name="mla_attention",
    )(qi, kj, q, kv, kr, vt)


_HI16 = 0xFFFF0000


def _pack_bf16_pairs(hi_f32):
    half = hi_f32.shape[1] // 2
    bits = pltpu.bitcast(hi_f32, jnp.uint32)
    return lax.shift_right_logical(bits[:, :half], jnp.uint32(16)) | (bits[:, half:] & jnp.uint32(_HI16))


def _unpack_pairs_by_tile(words, tile):
    half = tile // 2
    parts = []
    for c0 in range(0, words.shape[1], half):
        w = words[:, c0:c0 + half]
        parts.append(pltpu.bitcast(lax.shift_left(w, jnp.uint32(16)), F32))
        parts.append(pltpu.bitcast(w & jnp.uint32(_HI16), F32))
    return jnp.concatenate(parts, axis=1)


def _unpack_bf16_pairs(words):
    lo = pltpu.bitcast(lax.shift_left(words, jnp.uint32(16)), F32).astype(BF16)
    hi = pltpu.bitcast(words & jnp.uint32(_HI16), F32).astype(BF16)
    return jnp.concatenate([lo, hi], axis=1)


def _router_kernel(x_ref, g_ref, wh_ref, wl_ref, xn_ref, r_ref):
    xn = _rms_rows(x_ref[...], g_ref[...])
    hi = xn.astype(BF16)
    hi_f32 = hi.astype(F32)
    lo = (xn - hi_f32).astype(BF16)
    xn_ref[...] = _pack_bf16_pairs(hi_f32)
    logits = (jnp.dot(hi, wh_ref[...], preferred_element_type=F32)
              + jnp.dot(hi, wl_ref[...], preferred_element_type=F32)
              + jnp.dot(lo, wh_ref[...], preferred_element_type=F32))
    lane = lax.broadcasted_iota(jnp.int32, logits.shape, 1)
    lane_f = lane.astype(F32)
    g = jnp.where(lane < N_EXPERTS, logits, NEG_INF)
    v1 = jnp.max(g, axis=-1, keepdims=True)
    i1 = jnp.min(jnp.where(g == v1, lane_f, float(LANES)), axis=-1, keepdims=True)
    g = jnp.where(lane_f == i1, NEG_INF, g)
    v2 = jnp.max(g, axis=-1, keepdims=True)
    i2 = jnp.min(jnp.where(g == v2, lane_f, float(LANES)), axis=-1, keepdims=True)
    g2 = 1.0 / (1.0 + jnp.exp(v1 - v2))
    g1 = 1.0 - g2
    r_ref[...] = jnp.where(lane == 0, i1,
                           jnp.where(lane == 1, i2,
                                     jnp.where(lane == 2, g1, jnp.where(lane == 3, g2, 0.0))))


def moe_route(x, g, w_router, *, tm=512):
    M, D = x.shape
    tm = _tile(M, tm)
    wr = jnp.pad(w_router.astype(F32), ((0, 0), (0, LANES - N_EXPERTS)))
    wh = wr.astype(BF16)
    wl = (wr - wh.astype(F32)).astype(BF16)
    return pl.pallas_call(
        _router_kernel,
        out_shape=[jax.ShapeDtypeStruct((M, D // 2), jnp.uint32), jax.ShapeDtypeStruct((M, LANES), F32)],
        grid=(M // tm,),
        in_specs=[pl.BlockSpec((tm, D), lambda i: (i, 0)),
                  pl.BlockSpec((1, D), lambda i: (0, 0)),
                  pl.BlockSpec((D, LANES), lambda i: (0, 0)),
                  pl.BlockSpec((D, LANES), lambda i: (0, 0))],
        out_specs=[pl.BlockSpec((tm, D // 2), lambda i: (i, 0)),
                   pl.BlockSpec((tm, LANES), lambda i: (i, 0))],
        compiler_params=_cparams("parallel"),
        name="moe_route",
    )(x, g.reshape(1, D), wh, wl)


def _new_expert(te_ref, t):
    return jnp.logical_or(t == 0, te_ref[t] != te_ref[jnp.maximum(t - 1, 0)])


def _stream_expert_weights(te_ref, rid_ref, rexp_ref, nr_ref, used, n_col_tiles, copies_for, on_ready):
    j, t = pl.program_id(0), pl.program_id(1)

    @pl.when(jnp.logical_and(used, _new_expert(te_ref, t)))
    def _():
        n_runs = nr_ref[0]
        r = rid_ref[t]
        g = j * n_runs + r
        slot = lax.rem(g, 2)

        @pl.when(g == 0)
        def _():
            for c in copies_for(j, te_ref[t], slot):
                c.start()

        for c in copies_for(j, te_ref[t], slot):
            c.wait()
        on_ready(slot)
        last = r + 1 == n_runs
        nxt_r = jnp.where(last, 0, r + 1)
        nxt_j = jnp.where(last, j + 1, j)

        @pl.when(nxt_j < n_col_tiles)
        def _():
            for c in copies_for(nxt_j, rexp_ref[nxt_r], 1 - slot):
                c.start()


def _gmm_swiglu_kernel(te_ref, nu_ref, rid_ref, rexp_ref, nr_ref, *refs, n_parts, tiles_per_part, layer, nj, tn):
    x_refs = refs[:n_parts]
    w_hbm, o_ref, wbuf_ref, wgb_ref, wub_ref, sem = refs[n_parts:]
    t = pl.program_id(1)
    used = t < nu_ref[0]

    def copies_for(jj, e, slot):
        gate_cols = pl.ds(pl.multiple_of(jj * tn, tn), tn)
        up_cols = pl.ds(pl.multiple_of((nj + jj) * tn, tn), tn)
        return (pltpu.make_async_copy(w_hbm.at[layer, e, :, gate_cols], wbuf_ref.at[slot, 0], sem.at[slot, 0]),
                pltpu.make_async_copy(w_hbm.at[layer, e, :, up_cols], wbuf_ref.at[slot, 1], sem.at[slot, 1]))

    def on_ready(slot):
        wgb_ref[...] = wbuf_ref[slot, 0].astype(BF16)
        wub_ref[...] = wbuf_ref[slot, 1].astype(BF16)

    _stream_expert_weights(te_ref, rid_ref, rexp_ref, nr_ref, used, nj, copies_for, on_ready)

    for c in range(n_parts):
        @pl.when(jnp.logical_and(used, t // tiles_per_part == c))
        def _(c=c):
            x = _unpack_bf16_pairs(x_refs[c][...])
            a = jnp.dot(x, wgb_ref[...], preferred_element_type=F32)
            u = jnp.dot(x, wub_ref[...], preferred_element_type=F32)
            o_ref[...] = (a * jax.nn.sigmoid(a) * u).astype(o_ref.dtype)

    @pl.when(jnp.logical_not(used))
    def _():
        o_ref[...] = jnp.zeros(o_ref.shape, o_ref.dtype)


def _gmm_out_kernel(te_ref, nu_ref, rid_ref, rexp_ref, nr_ref, h_ref, w_hbm, o_ref, wbuf_ref, wb_ref, sem, *,
                    layer, nj, tn):
    t = pl.program_id(1)
    used = t < nu_ref[0]

    def copies_for(jj, e, slot):
        cols = pl.ds(pl.multiple_of(jj * tn, tn), tn)
        return (pltpu.make_async_copy(w_hbm.at[layer, e, :, cols], wbuf_ref.at[slot], sem.at[slot]),)

    def on_ready(slot):
        wb_ref[...] = wbuf_ref[slot].astype(BF16)

    _stream_expert_weights(te_ref, rid_ref, rexp_ref, nr_ref, used, nj, copies_for, on_ready)

    @pl.when(used)
    def _():
        y = jnp.dot(h_ref[...], wb_ref[...], preferred_element_type=F32)
        o_ref[...] = _pack_bf16_pairs(y.astype(BF16).astype(F32))

    @pl.when(jnp.logical_not(used))
    def _():
        o_ref[...] = jnp.zeros(o_ref.shape, o_ref.dtype)


def moe_experts(xs, tile_expert, n_used, runs, w_in, w_out, layer, *, tm, tn=MOE_COL_TILE):
    n_parts = len(xs)
    P = n_parts * xs[0].shape[0]
    D = w_in.shape[2]
    F = w_in.shape[3] // 2
    nt = P // tm
    ntp = nt // n_parts
    assert ntp * n_parts == nt
    tn1, tn2 = _tile(F, tn), _tile(D, tn)
    nj1, nj2 = F // tn1, D // tn2

    def part_spec(c):
        return pl.BlockSpec((tm, D // 2), lambda j, t, *_: (jnp.clip(t - c * ntp, 0, ntp - 1), 0))

    prefetch = (tile_expert, n_used) + tuple(runs)
    hbm = pl.BlockSpec(memory_space=pl.ANY)
    h = pl.pallas_call(
        functools.partial(_gmm_swiglu_kernel, n_parts=n_parts, tiles_per_part=ntp, layer=layer, nj=nj1, tn=tn1),
        out_shape=jax.ShapeDtypeStruct((P, F), BF16),
        grid_spec=pltpu.PrefetchScalarGridSpec(
            num_scalar_prefetch=len(prefetch),
            grid=(nj1, nt),
            in_specs=[part_spec(c) for c in range(n_parts)] + [hbm],
            out_specs=pl.BlockSpec((tm, tn1), lambda j, t, *_: (t, j)),
            scratch_shapes=[pltpu.VMEM((2, 2, D, tn1), F32), pltpu.VMEM((D, tn1), BF16),
                            pltpu.VMEM((D, tn1), BF16), pltpu.SemaphoreType.DMA((2, 2))],
        ),
        compiler_params=_cparams("arbitrary", "arbitrary"),
        name="moe_swiglu_in",
    )(*prefetch, *xs, w_in)
    return pl.pallas_call(
        functools.partial(_gmm_out_kernel, layer=layer, nj=nj2, tn=tn2),
        out_shape=jax.ShapeDtypeStruct((P, D // 2), jnp.uint32),
        grid_spec=pltpu.PrefetchScalarGridSpec(
            num_scalar_prefetch=len(prefetch),
            grid=(nj2, nt),
            in_specs=[pl.BlockSpec((tm, F), lambda j, t, *_: (t, 0)), hbm],
            out_specs=pl.BlockSpec((tm, tn2 // 2), lambda j, t, *_: (t, j)),
            scratch_shapes=[pltpu.VMEM((2, F, tn2), F32), pltpu.VMEM((F, tn2), BF16),
                            pltpu.SemaphoreType.DMA((2,))],
        ),
        compiler_params=_cparams("arbitrary", "arbitrary"),
        name="moe_out",
    )(*prefetch, h, w_out)


def moe_swiglu(x, g, w_router, w_in, w_out, layer, *, tm=512):
    S, D = x.shape
    E = N_EXPERTS
    tm = _tile(S, tm)
    xn, route = moe_route(x, g, w_router)
    experts = route[:, :2].astype(jnp.int32)
    gates = route[:, 2:4]
    flat_e = experts.reshape(-1)
    order = jnp.argsort(flat_e, stable=True).astype(jnp.int32)
    rank = jnp.argsort(order).astype(jnp.int32)
    counts = jnp.sum(flat_e[:, None] == jnp.arange(E, dtype=jnp.int32)[None, :], axis=0, dtype=jnp.int32)
    padded = ((counts + tm - 1) // tm) * tm
    start = jnp.cumsum(counts) - counts
    pstart = jnp.cumsum(padded) - padded
    pend = jnp.cumsum(padded)
    P = 2 * S + E * tm
    nt = P // tm
    tile_expert = jnp.minimum(
        jnp.sum(pend[None, :] <= (jnp.arange(nt, dtype=jnp.int32) * tm)[:, None], axis=1), E - 1).astype(jnp.int32)
    n_used = (pend[-1:] // tm).astype(jnp.int32)
    row_e = jnp.repeat(tile_expert, tm)
    row_rank = jnp.arange(P, dtype=jnp.int32) - pstart[row_e]
    row_valid = jnp.logical_and(row_rank < counts[row_e], jnp.arange(P) < pend[-1])
    row_token = jnp.where(row_valid, order[jnp.clip(start[row_e] + row_rank, 0, 2 * S - 1)] // 2, 0)
    slot = pstart[flat_e] + rank - start[flat_e]
    tile_expert = jnp.where(jnp.arange(nt) < n_used[0], tile_expert, tile_expert[jnp.maximum(n_used[0] - 1, 0)])
    slot = slot.reshape(S, 2)
    present = counts > 0
    run_of_expert = jnp.cumsum(present.astype(jnp.int32)) - 1
    runs = (run_of_expert[tile_expert].astype(jnp.int32),
            jnp.sort(jnp.where(present, jnp.arange(E, dtype=jnp.int32), E))[:E].clip(0, E - 1).astype(jnp.int32),
            jnp.sum(present).astype(jnp.int32).reshape(1))
    xs = [jnp.take(xn, part, axis=0, mode="clip") for part in jnp.split(row_token, MOE_GATHER_PARTS)]
    ys = moe_experts(xs, tile_expert, n_used, runs, w_in, w_out, layer, tm=tm)
    return jnp.take(ys, slot[:, 0], axis=0, mode="clip"), jnp.take(ys, slot[:, 1], axis=0, mode="clip"), gates


def _diff_lambda_init(layer):
    return 0.8 - 0.6 * math.exp(-0.3 * layer)


def _rope_table(pos):
    half = MLA_ROPE // 2
    inv_freq = ROPE_THETA ** (-jnp.arange(half, dtype=F32) / half)
    ang = pos.astype(F32)[:, None] * inv_freq
    cos, sin = jnp.cos(ang), jnp.sin(ang)
    return jnp.concatenate([cos, cos, -sin, sin], axis=1)


def _rot_half_cols(w):
    half = w.shape[-1] // 2
    return jnp.concatenate([w[..., half:], w[..., :half]], axis=-1)


def kernel(x, p, positions, rel_bias, norm_mix, norm_ffn, norm_ple, norm_final, diff_w_qkv, diff_lambda, diff_subln, diff_w_o, moba_w_qkv, moba_w_o, mla_w_down, mla_g_q, mla_w_uq, mla_g_kv, mla_w_ukv, mla_w_o, ffn_w_in, ffn_w_out, moe_w_router, moe_w_in, moe_w_out, ple_w_gate, ple_w_proj):
    B, S, D = x.shape
    depth = p.shape[0]
    lut = _bias_lut(rel_bias)
    outs = []
    for b in range(B):
        xb = x[b]
        pos = positions[b].astype(jnp.int32)
        for i in range(depth):
            jm = i // N_MIXERS
            if i % N_MIXERS == 0:
                qk, vt = norm_matmul_vt(xb, norm_mix[i], diff_w_qkv, layer=jm, n_direct=2 * D,
                                        scaled_cols=D, scale=DIFF_HEAD_DIM ** -0.5 * LOG2E)
                o = diff_attention(qk, vt, pos, lut, diff_lambda[jm].astype(F32), diff_subln[jm],
                                   _diff_lambda_init(i))
                xb = matmul_residual(o, diff_w_o, xb, layer=jm)
            elif i % N_MIXERS == 1:
                qk, vt = norm_matmul_vt(xb, norm_mix[i], moba_w_qkv, layer=jm, n_direct=2 * D,
                                        scaled_cols=D, scale=MOBA_HEAD_DIM ** -0.5 * LOG2E)
                o = moba_attention(qk, vt, pos, lut)
                xb = matmul_residual(o, moba_w_o, xb, layer=jm)
            else:
                wd = mla_w_down[jm]
                kr0 = MLA_Q_RANK + MLA_KV_RANK
                wd_ext = jnp.concatenate([wd, _rot_half_cols(wd[:, kr0:])], axis=1).astype(BF16)
                wq = mla_w_uq[jm].reshape(MLA_Q_RANK, MLA_HEADS, MLA_NOPE + MLA_ROPE)
                wq_ext = jnp.concatenate([wq, _rot_half_cols(wq[..., MLA_NOPE:])], axis=-1)
                wq_ext = wq_ext.reshape(MLA_Q_RANK, MLA_HEADS * 2 * LANES).astype(BF16)
                down = norm_matmul(xb, norm_mix[i], wd_ext, out_dtype=F32, tn=wd_ext.shape[1])
                q, kr = mla_queries_and_rope_key(down, mla_g_q[jm], wq_ext, _rope_table(pos),
                                                 scale=(MLA_NOPE + MLA_ROPE) ** -0.5 * LOG2E)
                wkv = mla_w_ukv[jm].reshape(MLA_KV_RANK, MLA_HEADS, 2, MLA_V).transpose(0, 2, 1, 3)
                wkv = wkv.reshape(MLA_KV_RANK, 2 * MLA_HEADS * MLA_V).astype(BF16)
                kn, vt = norm_matmul_vt(down, mla_g_kv[jm], wkv, n_direct=MLA_HEADS * MLA_NOPE, xcol=1)
                o = mla_attention(q, kn, kr, vt)
                xb = matmul_residual(o, mla_w_o, xb, layer=jm)
            if i % 2 == 0:
                h = norm_swiglu_in(xb, norm_ffn[i], ffn_w_in, layer=i // 2)
                xb = matmul_residual(h, ffn_w_out[i // 2].astype(BF16), xb)
                xb = ple_update(xb, norm_ple[i], p[i, b], ple_w_gate, ple_w_proj, layer=i)
            else:
                y1, y2, gates = moe_swiglu(xb, norm_ffn[i], moe_w_router[i // 2], moe_w_in, moe_w_out, i // 2)
                xb = ple_update_moe(xb, y1, y2, gates, norm_ple[i], p[i, b], ple_w_gate, ple_w_proj, layer=i)
        outs.append(final_norm(xb, norm_final))
    return jnp.stack(outs, axis=0)
```

```python
import functools
import math

import numpy as np
import jax
import jax.numpy as jnp
from jax import lax
from jax.experimental import pallas as pl
from jax.experimental.pallas import tpu as pltpu

F32 = jnp.float32
BF16 = jnp.bfloat16

NORM_EPS = 1e-6
NEG_INF = -1e30
LOG2E = math.log2(math.e)
LANES = 128
VMEM_LIMIT_BYTES = 56 * 1024 * 1024

REL_BUCKETS = 32
REL_MAX_DIST = 128
LUT_SIZE = LANES
BIAS_BLOCK = 256
DIFF_HEADS = 8
DIFF_HEAD_DIM = 128
MOBA_HEADS = 16
MOBA_HEAD_DIM = 128
MOBA_BLOCK = 256
MOBA_TOPK = 3
MLA_HEADS = 16
MLA_Q_RANK = 512
MLA_KV_RANK = 512
MLA_NOPE = 128
MLA_ROPE = 64
MLA_V = 128
ROPE_THETA = 10000.0
N_EXPERTS = 8
N_MIXERS = 3
MOE_GATHER_PARTS = 4
MOE_COL_TILE = 512


def _cparams(*sem):
    return pltpu.CompilerParams(dimension_semantics=sem, vmem_limit_bytes=VMEM_LIMIT_BYTES)


def _tile(n, pref):
    if n <= pref:
        return n
    t = pref
    while n % t:
        t //= 2
    return t


def _weight_spec(w, layer, tn, col=lambda j: j):
    K = w.shape[-2]
    if w.ndim == 2:
        return pl.BlockSpec((K, tn), lambda i, j: (0, col(j)))
    return pl.BlockSpec((None, K, tn), lambda i, j: (layer, 0, col(j)))


def _rms_rows(x, g):
    r = lax.rsqrt(jnp.mean(x * x, axis=-1, keepdims=True) + NORM_EPS)
    return x * r * g


def _norm_mm_kernel(x_ref, g_ref, w_ref, o_ref, xn_ref, *, scaled_tiles, scale):
    j = pl.program_id(1)

    @pl.when(j == 0)
    def _():
        xn_ref[...] = _rms_rows(x_ref[...], g_ref[...]).astype(BF16)

    acc = jnp.dot(xn_ref[...], w_ref[...].astype(BF16), preferred_element_type=F32)
    if scaled_tiles:
        acc = acc * jnp.where(j < scaled_tiles, scale, 1.0)
    o_ref[...] = acc.astype(o_ref.dtype)


def norm_matmul(x, g, w, *, out_dtype, xcol=0, tm=1024, tn=512, scaled_cols=0, scale=1.0):
    M = x.shape[0]
    K, N = w.shape
    tm, tn = _tile(M, tm), _tile(N, tn)
    assert scaled_cols % tn == 0
    kern = functools.partial(_norm_mm_kernel, scaled_tiles=scaled_cols // tn, scale=scale)
    return pl.pallas_call(
        kern,
        out_shape=jax.ShapeDtypeStruct((M, N), out_dtype),
        grid=(M // tm, N // tn),
        in_specs=[pl.BlockSpec((tm, K), lambda i, j: (i, xcol)),
                  pl.BlockSpec((1, K), lambda i, j: (0, 0)),
                  pl.BlockSpec((K, tn), lambda i, j: (0, j))],
        out_specs=pl.BlockSpec((tm, tn), lambda i, j: (i, j)),
        scratch_shapes=[pltpu.VMEM((tm, K), BF16)],
        compiler_params=_cparams("parallel", "arbitrary"),
        name="norm_matmul",
    )(x, g.reshape(1, K), w)


def _norm_mm_vt_kernel(x_ref, g_ref, w_ref, o_ref, ot_ref, xn_ref, *, direct_tiles, scaled_tiles, scale):
    j = pl.program_id(1)

    @pl.when(j == 0)
    def _():
        xn_ref[...] = _rms_rows(x_ref[...], g_ref[...]).astype(BF16)

    acc = jnp.dot(xn_ref[...], w_ref[...].astype(BF16), preferred_element_type=F32)

    @pl.when(j < direct_tiles)
    def _():
        out = acc * jnp.where(j < scaled_tiles, scale, 1.0) if scaled_tiles else acc
        o_ref[...] = out.astype(o_ref.dtype)

    @pl.when(j >= direct_tiles)
    def _():
        ot_ref[...] = acc.T.astype(ot_ref.dtype)


def norm_matmul_vt(x, g, w, *, n_direct, layer=None, xcol=0, tm=1024, tn=512, scaled_cols=0, scale=1.0):
    M = x.shape[0]
    K, N = w.shape[-2:]
    tm, tn = _tile(M, tm), _tile(N, tn)
    assert scaled_cols % tn == 0 and n_direct % tn == 0 and 0 < n_direct < N
    nd = n_direct // tn
    kern = functools.partial(_norm_mm_vt_kernel, direct_tiles=nd, scaled_tiles=scaled_cols // tn, scale=scale)
    return pl.pallas_call(
        kern,
        out_shape=[jax.ShapeDtypeStruct((M, n_direct), BF16), jax.ShapeDtypeStruct((N - n_direct, M), BF16)],
        grid=(M // tm, N // tn),
        in_specs=[pl.BlockSpec((tm, K), lambda i, j: (i, xcol)),
                  pl.BlockSpec((1, K), lambda i, j: (0, 0)),
                  _weight_spec(w, layer, tn)],
        out_specs=[pl.BlockSpec((tm, tn), lambda i, j: (i, jnp.minimum(j, nd - 1))),
                   pl.BlockSpec((tn, tm), lambda i, j: (jnp.maximum(j - nd, 0), i))],
        scratch_shapes=[pltpu.VMEM((tm, K), BF16)],
        compiler_params=_cparams("parallel", "arbitrary"),
        name="norm_matmul_vt",
    )(x, g.reshape(1, K), w)


def _norm_swiglu_kernel(x_ref, g_ref, wg_ref, wu_ref, o_ref, xn_ref):
    @pl.when(pl.program_id(1) == 0)
    def _():
        xn_ref[...] = _rms_rows(x_ref[...], g_ref[...]).astype(BF16)

    xn = xn_ref[...]
    a = jnp.dot(xn, wg_ref[...].astype(BF16), preferred_element_type=F32)
    u = jnp.dot(xn, wu_ref[...].astype(BF16), preferred_element_type=F32)
    o_ref[...] = (a * jax.nn.sigmoid(a) * u).astype(o_ref.dtype)


def norm_swiglu_in(x, g, w_in, *, layer=None, tm=1024, tn=512):
    M, K = x.shape
    F = w_in.shape[-1] // 2
    tm, tn = _tile(M, tm), _tile(F, tn)
    nj = F // tn
    return pl.pallas_call(
        _norm_swiglu_kernel,
        out_shape=jax.ShapeDtypeStruct((M, F), BF16),
        grid=(M // tm, nj),
        in_specs=[pl.BlockSpec((tm, K), lambda i, j: (i, 0)),
                  pl.BlockSpec((1, K), lambda i, j: (0, 0)),
                  _weight_spec(w_in, layer, tn),
                  _weight_spec(w_in, layer, tn, lambda j: j + nj)],
        out_specs=pl.BlockSpec((tm, tn), lambda i, j: (i, j)),
        scratch_shapes=[pltpu.VMEM((tm, K), BF16)],
        compiler_params=_cparams("parallel", "arbitrary"),
        name="norm_swiglu_in",
    )(x, g.reshape(1, K), w_in, w_in)


def _mm_res_kernel(a_ref, w_ref, r_ref, o_ref):
    o_ref[...] = r_ref[...] + jnp.dot(a_ref[...], w_ref[...].astype(BF16), preferred_element_type=F32)


def matmul_residual(a, w, res, *, layer=None, tm=1024, tn=512):
    M, K = a.shape
    N = w.shape[-1]
    tm, tn = _tile(M, tm), _tile(N, tn)
    return pl.pallas_call(
        _mm_res_kernel,
        out_shape=jax.ShapeDtypeStruct((M, N), F32),
        grid=(M // tm, N // tn),
        in_specs=[pl.BlockSpec((tm, K), lambda i, j: (i, 0)),
                  _weight_spec(w, layer, tn),
                  pl.BlockSpec((tm, tn), lambda i, j: (i, j))],
        out_specs=pl.BlockSpec((tm, tn), lambda i, j: (i, j)),
        compiler_params=_cparams("parallel", "parallel"),
        name="matmul_residual",
    )(a, w, res)


def _ple_kernel(x_ref, g_ref, xr_ref, p_ref, wg_ref, wp_ref, o_ref, xn_ref):
    @pl.when(pl.program_id(1) == 0)
    def _():
        xn_ref[...] = _rms_rows(x_ref[...], g_ref[...]).astype(BF16)

    gate = jax.nn.sigmoid(jnp.dot(xn_ref[...], wg_ref[...].astype(BF16), preferred_element_type=F32))
    proj = jnp.dot(p_ref[...].astype(BF16), wp_ref[...].astype(BF16), preferred_element_type=F32)
    o_ref[...] = xr_ref[...] + gate * proj


def ple_update(x, g, p, w_gate, w_proj, *, layer=None, tm=1024, tn=512):
    M, D = x.shape
    P = p.shape[1]
    tm, tn = _tile(M, tm), _tile(D, tn)
    return pl.pallas_call(
        _ple_kernel,
        out_shape=jax.ShapeDtypeStruct((M, D), F32),
        grid=(M // tm, D // tn),
        in_specs=[pl.BlockSpec((tm, D), lambda i, j: (i, 0)),
                  pl.BlockSpec((1, D), lambda i, j: (0, 0)),
                  pl.BlockSpec((tm, tn), lambda i, j: (i, j)),
                  pl.BlockSpec((tm, P), lambda i, j: (i, 0)),
                  _weight_spec(w_gate, layer, tn),
                  _weight_spec(w_proj, layer, tn)],
        out_specs=pl.BlockSpec((tm, tn), lambda i, j: (i, j)),
        scratch_shapes=[pltpu.VMEM((tm, D), BF16)],
        compiler_params=_cparams("parallel", "arbitrary"),
        name="ple_update",
    )(x, g.reshape(1, D), x, p, w_gate, w_proj)


def _ple_moe_kernel(x_ref, y1_ref, y2_ref, gt_ref, g_ref, p_ref, wg_ref, wp_ref, o_ref, xs_ref, xn_ref, *, tn,
                    pair_tile):
    j = pl.program_id(1)

    @pl.when(j == 0)
    def _():
        gt = gt_ref[...]
        xnew = (x_ref[...] + gt[:, 0:1] * _unpack_pairs_by_tile(y1_ref[...], pair_tile)
                + gt[:, 1:2] * _unpack_pairs_by_tile(y2_ref[...], pair_tile))
        xn_ref[...] = _rms_rows(xnew, g_ref[...]).astype(BF16)
        for jj in range(xs_ref.shape[0]):
            xs_ref[jj] = xnew[:, jj * tn:(jj + 1) * tn]

    gate = jax.nn.sigmoid(jnp.dot(xn_ref[...], wg_ref[...].astype(BF16), preferred_element_type=F32))
    proj = jnp.dot(p_ref[...].astype(BF16), wp_ref[...].astype(BF16), preferred_element_type=F32)
    o_ref[...] = xs_ref[j] + gate * proj


def ple_update_moe(x, y1, y2, gates, g, p, w_gate, w_proj, *, layer=None, tm=512, tn=512):
    M, D = x.shape
    P = p.shape[1]
    tm, tn = _tile(M, tm), _tile(D, tn)
    row = pl.BlockSpec((tm, D), lambda i, j: (i, 0))
    packed_row = pl.BlockSpec((tm, D // 2), lambda i, j: (i, 0))
    return pl.pallas_call(
        functools.partial(_ple_moe_kernel, tn=tn, pair_tile=_tile(D, MOE_COL_TILE)),
        out_shape=jax.ShapeDtypeStruct((M, D), F32),
        grid=(M // tm, D // tn),
        in_specs=[row, packed_row, packed_row,
                  pl.BlockSpec((tm, 2), lambda i, j: (i, 0)),
                  pl.BlockSpec((1, D), lambda i, j: (0, 0)),
                  pl.BlockSpec((tm, P), lambda i, j: (i, 0)),
                  _weight_spec(w_gate, layer, tn),
                  _weight_spec(w_proj, layer, tn)],
        out_specs=pl.BlockSpec((tm, tn), lambda i, j: (i, j)),
        scratch_shapes=[pltpu.VMEM((D // tn, tm, tn), F32), pltpu.VMEM((tm, D), BF16)],
        compiler_params=_cparams("parallel", "arbitrary"),
        name="ple_update_moe",
    )(x, y1, y2, gates, g.reshape(1, D), p, w_gate, w_proj)


def _final_norm_kernel(x_ref, g_ref, o_ref):
    o_ref[...] = _rms_rows(x_ref[...], g_ref[...])


def final_norm(x, g, *, tm=512):
    M, D = x.shape
    tm = _tile(M, tm)
    return pl.pallas_call(
        _final_norm_kernel,
        out_shape=jax.ShapeDtypeStruct((M, D), F32),
        grid=(M // tm,),
        in_specs=[pl.BlockSpec((tm, D), lambda i: (i, 0)),
                  pl.BlockSpec((1, D), lambda i: (0, 0))],
        out_specs=pl.BlockSpec((tm, D), lambda i: (i, 0)),
        compiler_params=_cparams("parallel"),
        name="final_norm",
    )(x, g.reshape(1, D))


def _pair_tables(n_tiles):
    qi, kj = [], []
    for i in range(n_tiles):
        for j in range(i + 1):
            qi.append(i)
            kj.append(j)
    return jnp.asarray(np.array(qi, np.int32)), jnp.asarray(np.array(kj, np.int32))


def _rel_bucket(dist):
    n = jnp.maximum(dist, 0)
    max_exact = REL_BUCKETS // 2
    nf = jnp.maximum(n, 1).astype(F32)
    large = max_exact + (jnp.log(nf / max_exact) / math.log(REL_MAX_DIST / max_exact)
                         * (REL_BUCKETS - max_exact)).astype(jnp.int32)
    large = jnp.minimum(large, REL_BUCKETS - 1)
    return jnp.where(n < max_exact, n, large)


def _bias_lut(rel_bias):
    buckets = _rel_bucket(jnp.arange(LUT_SIZE, dtype=jnp.int32))
    return rel_bias.astype(F32)[buckets].T * LOG2E


def _block_pos_bounds(pos, sb):
    blocks = pos.reshape(pos.shape[0] // sb, sb)
    wide = jnp.any(jnp.logical_or(pos >= 2 ** 30, pos <= -(2 ** 30)))
    return jnp.where(wide, 0, blocks.min(axis=1)), jnp.where(wide, 0, blocks.max(axis=1))


def _lut_bias(lut_row, idx):
    R, C = idx.shape
    lut = jnp.broadcast_to(lut_row, (R, LANES))
    parts = [jnp.take_along_axis(lut, idx[:, c:c + LANES], axis=1, mode="promise_in_bounds")
             for c in range(0, C, LANES)]
    return parts[0] if len(parts) == 1 else jnp.concatenate(parts, axis=1)


def _pipelined(units, scores, consume):
    st = scores(*units[0])
    for k, u in enumerate(units):
        nxt = scores(*units[k + 1]) if k + 1 < len(units) else None
        consume(*u, st)
        st = nxt


def _online_softmax_step_t(st, vt, m_ref, l_ref, acc_ref, qcols, m_cur=None, offsets=None):
    m_prev = m_ref[:, qcols]
    if m_cur is None:
        m_cur = jnp.max(st, axis=0, keepdims=True)
    m_new = jnp.maximum(m_prev, m_cur)
    alpha = jnp.exp2(m_prev - m_new)
    if offsets is None:
        p = jnp.exp2(st - m_new)
    else:
        parts = [jnp.exp2(st[rows, :] - fn(m_new)) for rows, fn in offsets]
        p = parts[0] if len(parts) == 1 else jnp.concatenate(parts, axis=0)
    l_ref[:, qcols] = alpha * l_ref[:, qcols] + jnp.sum(p, axis=0, keepdims=True)
    acc_ref[:, qcols] = alpha * acc_ref[:, qcols] + jnp.dot(vt, p.astype(BF16), preferred_element_type=F32)
    m_ref[:, qcols] = m_new


def _causal_t(k0, nk, q0, nq):
    r = k0 + lax.broadcasted_iota(jnp.int32, (nk, nq), 0)
    c = q0 + lax.broadcasted_iota(jnp.int32, (nk, nq), 1)
    return r <= c


def _nt_dot(a, b):
    return lax.dot_general(a, b, (((1,), (1,)), ((), ())), preferred_element_type=F32)


def _diff_attn_kernel(qi_ref, kj_ref, pminq_ref, pmaxk_ref,
                      q_ref, k_ref, vt_ref, pq_ref, pk_ref, lut_ref, lam_ref, g_ref,
                      o_ref, m_ref, l_ref, acc_ref, st_ref, *, tq, ck, sb, hp, lambda_init):
    p = pl.program_id(1)
    i, j = qi_ref[p], kj_ref[p]
    nc = tq // ck
    dh = DIFF_HEAD_DIM
    nm = 2 * hp
    SB = sb
    nqb, nkb = tq // SB, ck // SB

    @pl.when(j == 0)
    def _():
        m_ref[...] = jnp.full(m_ref.shape, NEG_INF, F32)
        l_ref[...] = jnp.zeros(l_ref.shape, F32)
        acc_ref[...] = jnp.zeros(acc_ref.shape, F32)

    def near_sub(c, kb, qb):
        return pminq_ref[i * nqb + qb] - pmaxk_ref[(j * nc + c) * nkb + kb] < LUT_SIZE - 1

    def refine_chunk(c):
        keys = slice(c * ck, (c + 1) * ck)
        for mp in range(nm):
            st_ref[mp] = _nt_dot(k_ref[keys, mp * dh:(mp + 1) * dh], q_ref[:, mp * dh:(mp + 1) * dh])
        for kb in range(nkb):
            for qb in range(nqb):
                @pl.when(near_sub(c, kb, qb))
                def _(kb=kb, qb=qb):
                    rows, cols = slice(kb * SB, (kb + 1) * SB), slice(qb * SB, (qb + 1) * SB)
                    idx = jnp.clip(pq_ref[:, cols] - pk_ref[c * ck + kb * SB:c * ck + (kb + 1) * SB, :],
                                   0, LUT_SIZE - 1)
                    for mp in range(nm):
                        far_bias = lut_ref[mp:mp + 1, LUT_SIZE - 1:LUT_SIZE]
                        st_ref[mp, rows, cols] = (st_ref[mp, rows, cols]
                                                  + (_lut_bias(lut_ref[mp:mp + 1, :], idx) - far_bias))
        for mp in range(nm):
            consume(c, mp, 0, "far", st_ref[mp])

    def scores(c, mp, q0, mode):
        keys = slice(c * ck, (c + 1) * ck)
        st = _nt_dot(k_ref[keys, mp * dh:(mp + 1) * dh], q_ref[q0:tq, mp * dh:(mp + 1) * dh])
        if mode != "far":
            idx = jnp.clip(pq_ref[:, q0:tq] - pk_ref[keys, :], 0, LUT_SIZE - 1)
            st = st + _lut_bias(lut_ref[mp:mp + 1, :], idx)
        if mode == "diag":
            st = jnp.where(_causal_t(c * ck, ck, q0, tq - q0), st, NEG_INF)
        return st

    def consume(c, mp, q0, mode, st):
        m_cur = offsets = None
        if mode == "far":
            bias = lut_ref[mp:mp + 1, LUT_SIZE - 1:LUT_SIZE]
            m_cur = jnp.max(st, axis=0, keepdims=True) + bias
            offsets = [(slice(None), lambda m_new: m_new - bias)]
        hd = mp // 2
        _online_softmax_step_t(st, vt_ref[hd * 2 * dh:(hd + 1) * 2 * dh, c * ck:(c + 1) * ck],
                               m_ref.at[mp], l_ref.at[mp], acc_ref.at[mp], slice(q0, tq), m_cur, offsets)

    def sweep(chunks, mode):
        units = [(c, mp, c * ck if mode == "diag" else 0, mode) for c in chunks for mp in range(nm)]
        _pipelined(units, scores, consume)

    @pl.when(j < i)
    def _():
        fars = [jnp.logical_not(functools.reduce(
            jnp.logical_or, [near_sub(c, kb, qb) for kb in range(nkb) for qb in range(nqb)])) for c in range(nc)]
        all_far = functools.reduce(jnp.logical_and, fars)
        pl.when(all_far)(functools.partial(sweep, range(nc), "far"))

        @pl.when(jnp.logical_not(all_far))
        def _():
            for c in range(nc):
                pl.when(fars[c])(functools.partial(sweep, [c], "far"))
                pl.when(jnp.logical_not(fars[c]))(functools.partial(refine_chunk, c))

    @pl.when(j == i)
    def _():
        sweep(range(nc), "diag")
        lam = lam_ref[...]
        lam_full = (jnp.exp(jnp.sum(lam[0:1] * lam[1:2], axis=-1, keepdims=True))
                    - jnp.exp(jnp.sum(lam[2:3] * lam[3:4], axis=-1, keepdims=True)) + lambda_init)
        for hd in range(hp):
            a = (acc_ref[2 * hd] / l_ref[2 * hd]
                 - lam_full * (acc_ref[2 * hd + 1] / l_ref[2 * hd + 1]))
            r = lax.rsqrt(jnp.mean(a * a, axis=0, keepdims=True) + NORM_EPS)
            y = a * r * (g_ref[...] * (1.0 - lambda_init))
            o_ref[:, hd * 2 * dh:(hd + 1) * 2 * dh] = y.T.astype(o_ref.dtype)


def diff_attention(qkv, vt, pos, lut, lam, subln_g, lambda_init, *, tile=1024, chunk=512, heads_per_step=1):
    S = qkv.shape[0]
    H, dh = DIFF_HEADS, DIFF_HEAD_DIM
    tq = _tile(S, tile)
    ck = _tile(tq, chunk)
    qi, kj = _pair_tables(S // tq)
    sb = _tile(ck, BIAS_BLOCK)
    pminq, pmaxk = _block_pos_bounds(pos, sb)
    hp = heads_per_step
    G = H // hp
    W = hp * 2 * dh
    kern = functools.partial(_diff_attn_kernel, tq=tq, ck=ck, sb=sb, hp=hp, lambda_init=lambda_init)
    grid_spec = pltpu.PrefetchScalarGridSpec(
        num_scalar_prefetch=4,
        grid=(G, qi.shape[0]),
        in_specs=[pl.BlockSpec((tq, W), lambda h, p, qi, kj, a, b: (qi[p], h)),
                  pl.BlockSpec((tq, W), lambda h, p, qi, kj, a, b: (kj[p], G + h)),
                  pl.BlockSpec((W, tq), lambda h, p, qi, kj, a, b: (h, kj[p])),
                  pl.BlockSpec((1, tq), lambda h, p, qi, kj, a, b: (0, qi[p])),
                  pl.BlockSpec((tq, 1), lambda h, p, qi, kj, a, b: (kj[p], 0)),
                  pl.BlockSpec((None, 2 * hp, LUT_SIZE), lambda h, p, qi, kj, a, b: (h, 0, 0)),
                  pl.BlockSpec((4, dh), lambda h, p, qi, kj, a, b: (0, 0)),
                  pl.BlockSpec((2 * dh, 1), lambda h, p, qi, kj, a, b: (0, 0))],
        out_specs=pl.BlockSpec((tq, W), lambda h, p, qi, kj, a, b: (qi[p], h)),
        scratch_shapes=[pltpu.VMEM((2 * hp, 1, tq), F32), pltpu.VMEM((2 * hp, 1, tq), F32),
                        pltpu.VMEM((2 * hp, 2 * dh, tq), F32), pltpu.VMEM((2 * hp, ck, tq), F32)],
    )
    return pl.pallas_call(
        kern,
        out_shape=jax.ShapeDtypeStruct((S, H * 2 * dh), BF16),
        grid_spec=grid_spec,
        compiler_params=_cparams("parallel", "arbitrary"),
        name="diff_attention",
    )(qi, kj, pminq, pmaxk, qkv, qkv, vt, pos.reshape(1, S), pos.reshape(S, 1),
      lut.reshape(G, 2 * hp, LUT_SIZE), lam, subln_g.reshape(2 * dh, 1))


def _kmean_kernel(k_ref, hi_ref, lo_ref):
    km = jnp.mean(k_ref[...].astype(F32), axis=0, keepdims=True)
    hi = km.astype(BF16)
    hi_ref[...] = hi
    lo_ref[...] = (km - hi.astype(F32)).astype(BF16)


def moba_block_means(qkv, D):
    S = qkv.shape[0]
    nblk = S // MOBA_BLOCK
    hi, lo = pl.pallas_call(
        _kmean_kernel,
        out_shape=[jax.ShapeDtypeStruct((nblk, 1, D), BF16)] * 2,
        grid=(nblk,),
        in_specs=[pl.BlockSpec((MOBA_BLOCK, D), lambda n: (n, 1))],
        out_specs=[pl.BlockSpec((None, 1, D), lambda n: (n, 0, 0))] * 2,
        compiler_params=_cparams("parallel"),
        name="moba_block_means",
    )(qkv)
    return hi.reshape(nblk, D), lo.reshape(nblk, D)


def _moba_attn_kernel(qi_ref, kj_ref, pminq_ref, pmaxk_ref,
                      q_ref, k_ref, vt_ref, kmh_ref, kml_ref, pq_ref, pk_ref, lut_ref,
                      o_ref, m_ref, l_ref, acc_ref, sel_ref, st_ref, *, tq, ck, sb, hp):
    p = pl.program_id(1)
    i, j = qi_ref[p], kj_ref[p]
    nc = tq // ck
    L, dh = MOBA_BLOCK, MOBA_HEAD_DIM
    bpc = ck // L
    log2_l = L.bit_length() - 1
    SB = sb
    nqb, nkb = tq // SB, ck // SB

    def near_sub(c, kb, qb):
        return pminq_ref[i * nqb + qb] - pmaxk_ref[(j * nc + c) * nkb + kb] < LUT_SIZE - 1

    def refine_chunk(c):
        keys = slice(c * ck, (c + 1) * ck)
        for hd in range(hp):
            st_ref[hd] = _nt_dot(k_ref[keys, hd * dh:(hd + 1) * dh], q_ref[:, hd * dh:(hd + 1) * dh])
        for kb in range(nkb):
            for qb in range(nqb):
                @pl.when(near_sub(c, kb, qb))
                def _(kb=kb, qb=qb):
                    rows, cols = slice(kb * SB, (kb + 1) * SB), slice(qb * SB, (qb + 1) * SB)
                    idx = jnp.clip(pq_ref[:, cols] - pk_ref[c * ck + kb * SB:c * ck + (kb + 1) * SB, :],
                                   0, LUT_SIZE - 1)
                    for hd in range(hp):
                        far_bias = lut_ref[hd:hd + 1, LUT_SIZE - 1:LUT_SIZE]
                        st_ref[hd, rows, cols] = (st_ref[hd, rows, cols]
                                                  + (_lut_bias(lut_ref[hd:hd + 1, :], idx) - far_bias))
        for hd in range(hp):
            consume(c, hd, 0, "far", st_ref[hd])

    @pl.when(j == 0)
    def _():
        m_ref[...] = jnp.full(m_ref.shape, NEG_INF, F32)
        l_ref[...] = jnp.zeros(l_ref.shape, F32)
        acc_ref[...] = jnp.zeros(acc_ref.shape, F32)
        blk = lax.broadcasted_iota(jnp.int32, (LANES, tq), 0)
        blk_f = blk.astype(F32)
        own = lax.shift_right_logical(i * tq + lax.broadcasted_iota(jnp.int32, (1, tq), 1), log2_l)
        for hd in range(hp):
            q = q_ref[:, hd * dh:(hd + 1) * dh]
            gs = (_nt_dot(kmh_ref[:, hd * dh:(hd + 1) * dh], q)
                  + _nt_dot(kml_ref[:, hd * dh:(hd + 1) * dh], q))
            g = jnp.where(blk < own, gs, NEG_INF)
            sel = jnp.zeros((LANES, tq), F32)
            for _ in range(MOBA_TOPK):
                best = jnp.max(g, axis=0, keepdims=True)
                first = jnp.min(jnp.where(g == best, blk_f, float(LANES)), axis=0, keepdims=True)
                pick = blk_f == first
                sel = jnp.where(pick, jnp.where(best > 0.5 * NEG_INF, 1.0, sel), sel)
                g = jnp.where(pick, -3e38, g)
            sel_ref[hd] = sel

    def scores(c, hd, q0, mode):
        keys = slice(c * ck, (c + 1) * ck)
        nq = tq - q0
        st = _nt_dot(k_ref[keys, hd * dh:(hd + 1) * dh], q_ref[q0:tq, hd * dh:(hd + 1) * dh])
        if mode != "far":
            idx = jnp.clip(pq_ref[:, q0:tq] - pk_ref[keys, :], 0, LUT_SIZE - 1)
            st = st + _lut_bias(lut_ref[hd:hd + 1, :], idx)
        if mode != "diag":
            return st
        allowed = []
        for b in range(bpc):
            n_local = c * bpc + b
            picked = jnp.broadcast_to(picked_row(hd, n_local, q0), (L, nq))
            own_local = lax.shift_right_logical(q0 + lax.broadcasted_iota(jnp.int32, (L, nq), 1), log2_l)
            visible = jnp.where(_causal_t(n_local * L, L, q0, nq), 1.0, 0.0)
            allowed.append(jnp.where(own_local == n_local, visible, picked))
        allowed = allowed[0] if bpc == 1 else jnp.concatenate(allowed, axis=0)
        return jnp.where(allowed > 0.0, st, NEG_INF)

    def picked_row(hd, n_local, q0):
        return sel_ref[hd, pl.ds(j * (tq // L) + n_local, 1), q0:tq]

    def consume(c, hd, q0, mode, st):
        m_cur = offsets = None
        if mode != "diag":
            bias = lut_ref[hd:hd + 1, LUT_SIZE - 1:LUT_SIZE] if mode == "far" else 0.0
            picked = [picked_row(hd, c * bpc + b, q0) > 0.0 for b in range(bpc)]
            m_cur = functools.reduce(jnp.maximum, [
                jnp.where(picked[b], jnp.max(st[b * L:(b + 1) * L, :], axis=0, keepdims=True) + bias, NEG_INF)
                for b in range(bpc)])
            offsets = [(slice(b * L, (b + 1) * L),
                        functools.partial(lambda m_new, pk: jnp.where(pk, m_new - bias, -NEG_INF), pk=picked[b]))
                       for b in range(bpc)]
        _online_softmax_step_t(st, vt_ref[hd * dh:(hd + 1) * dh, c * ck:(c + 1) * ck],
                               m_ref.at[hd], l_ref.at[hd], acc_ref.at[hd], slice(q0, tq), m_cur, offsets)

    def sweep(chunks, mode):
        _pipelined([(c, hd, c * ck if mode == "diag" else 0, mode) for c in chunks for hd in range(hp)],
                   scores, consume)

    @pl.when(j < i)
    def _():
        fars = [jnp.logical_not(functools.reduce(
            jnp.logical_or, [near_sub(c, kb, qb) for kb in range(nkb) for qb in range(nqb)])) for c in range(nc)]
        all_far = functools.reduce(jnp.logical_and, fars)
        pl.when(all_far)(functools.partial(sweep, range(nc), "far"))

        @pl.when(jnp.logical_not(all_far))
        def _():
            for c in range(nc):
                pl.when(fars[c])(functools.partial(sweep, [c], "far"))
                pl.when(jnp.logical_not(fars[c]))(functools.partial(refine_chunk, c))

    @pl.when(j == i)
    def _():
        sweep(range(nc), "diag")
        for hd in range(hp):
            o_ref[:, hd * dh:(hd + 1) * dh] = (acc_ref[hd] / l_ref[hd]).T.astype(o_ref.dtype)


def moba_attention(qkv, vt, pos, lut, *, tile=1024, chunk=512, heads_per_step=2):
    S = qkv.shape[0]
    H, dh, L = MOBA_HEADS, MOBA_HEAD_DIM, MOBA_BLOCK
    D = H * dh
    nblk = S // L
    assert S % L == 0 and nblk <= LANES
    tq = _tile(S, tile)
    ck = _tile(tq, chunk)
    assert tq % L == 0 and ck % L == 0
    km_hi, km_lo = moba_block_means(qkv, D)
    km_hi = jnp.pad(km_hi, ((0, LANES - nblk), (0, 0)))
    km_lo = jnp.pad(km_lo, ((0, LANES - nblk), (0, 0)))
    qi, kj = _pair_tables(S // tq)
    sb = _tile(ck, BIAS_BLOCK)
    pminq, pmaxk = _block_pos_bounds(pos, sb)
    hp = heads_per_step
    G = H // hp
    kern = functools.partial(_moba_attn_kernel, tq=tq, ck=ck, sb=sb, hp=hp)
    grid_spec = pltpu.PrefetchScalarGridSpec(
        num_scalar_prefetch=4,
        grid=(G, qi.shape[0]),
        in_specs=[pl.BlockSpec((tq, hp * dh), lambda h, p, qi, kj, a, b: (qi[p], h)),
                  pl.BlockSpec((tq, hp * dh), lambda h, p, qi, kj, a, b: (kj[p], G + h)),
                  pl.BlockSpec((hp * dh, tq), lambda h, p, qi, kj, a, b: (h, kj[p])),
                  pl.BlockSpec((LANES, hp * dh), lambda h, p, qi, kj, a, b: (0, h)),
                  pl.BlockSpec((LANES, hp * dh), lambda h, p, qi, kj, a, b: (0, h)),
                  pl.BlockSpec((1, tq), lambda h, p, qi, kj, a, b: (0, qi[p])),
                  pl.BlockSpec((tq, 1), lambda h, p, qi, kj, a, b: (kj[p], 0)),
                  pl.BlockSpec((None, hp, LUT_SIZE), lambda h, p, qi, kj, a, b: (h, 0, 0))],
        out_specs=pl.BlockSpec((tq, hp * dh), lambda h, p, qi, kj, a, b: (qi[p], h)),
        scratch_shapes=[pltpu.VMEM((hp, 1, tq), F32), pltpu.VMEM((hp, 1, tq), F32),
                        pltpu.VMEM((hp, dh, tq), F32), pltpu.VMEM((hp, LANES, tq), F32),
                        pltpu.VMEM((hp, ck, tq), F32)],
    )
    return pl.pallas_call(
        kern,
        out_shape=jax.ShapeDtypeStruct((S, D), BF16),
        grid_spec=grid_spec,
        compiler_params=_cparams("parallel", "arbitrary"),
        name="moba_attention",
    )(qi, kj, pminq, pmaxk, qkv, qkv, vt, km_hi, km_lo, pos.reshape(1, S), pos.reshape(S, 1),
      lut.reshape(G, hp, LUT_SIZE))


def _rope_fold(y):
    lane = lax.broadcasted_iota(jnp.int32, y.shape, 1)
    return jnp.where(lane < MLA_ROPE, y + pltpu.roll(y, MLA_ROPE, 1), 0.0)


def _mla_q_kernel(cq_ref, g_ref, w_ref, kr_ref, t_ref, q_ref, kro_ref, cqn_ref, *, scale, hp):
    t = t_ref[...]
    W = 2 * LANES

    @pl.when(pl.program_id(1) == 0)
    def _():
        cqn_ref[...] = _rms_rows(cq_ref[...], g_ref[...]).astype(BF16)
        kro_ref[...] = _rope_fold(kr_ref[...] * t).astype(kro_ref.dtype)

    qh = jnp.dot(cqn_ref[...], w_ref[...], preferred_element_type=F32)
    for hd in range(hp):
        q_ref[:, hd * W:hd * W + MLA_NOPE] = (qh[:, hd * W:hd * W + MLA_NOPE] * scale).astype(q_ref.dtype)
        q_ref[:, hd * W + MLA_NOPE:(hd + 1) * W] = (
            _rope_fold(qh[:, hd * W + MLA_NOPE:(hd + 1) * W] * t) * scale).astype(q_ref.dtype)


def mla_queries_and_rope_key(down, g_q, w_uq_ext, rope_tab, *, scale, tm=1024, heads_per_step=4):
    S = down.shape[0]
    H, hp = MLA_HEADS, heads_per_step
    tm = _tile(S, tm)
    W = 2 * LANES
    return pl.pallas_call(
        functools.partial(_mla_q_kernel, scale=scale, hp=hp),
        out_shape=[jax.ShapeDtypeStruct((S, H * W), BF16), jax.ShapeDtypeStruct((S, LANES), BF16)],
        grid=(S // tm, H // hp),
        in_specs=[pl.BlockSpec((tm, MLA_Q_RANK), lambda i, h: (i, 0)),
                  pl.BlockSpec((1, MLA_Q_RANK), lambda i, h: (0, 0)),
                  pl.BlockSpec((MLA_Q_RANK, hp * W), lambda i, h: (0, h)),
                  pl.BlockSpec((tm, LANES), lambda i, h: (i, (MLA_Q_RANK + MLA_KV_RANK) // LANES)),
                  pl.BlockSpec((tm, LANES), lambda i, h: (i, 0))],
        out_specs=[pl.BlockSpec((tm, hp * W), lambda i, h: (i, h)),
                   pl.BlockSpec((tm, LANES), lambda i, h: (i, 0))],
        scratch_shapes=[pltpu.VMEM((tm, MLA_Q_RANK), BF16)],
        compiler_params=_cparams("parallel", "arbitrary"),
        name="mla_queries",
    )(down, g_q.reshape(1, MLA_Q_RANK), w_uq_ext, down, rope_tab)


def _mla_attn_kernel(qi_ref, kj_ref, q_ref, kv_ref, kr_ref, vt_ref, o_ref, m_ref, l_ref, acc_ref, *, tq, ck, hp):
    p = pl.program_id(1)
    i, j = qi_ref[p], kj_ref[p]
    nc = tq // ck
    W = 2 * LANES

    @pl.when(j == 0)
    def _():
        m_ref[...] = jnp.full(m_ref.shape, NEG_INF, F32)
        l_ref[...] = jnp.zeros(l_ref.shape, F32)
        acc_ref[...] = jnp.zeros(acc_ref.shape, F32)

    def scores(c, g, q0, diag):
        keys = slice(c * ck, (c + 1) * ck)
        kc = jnp.concatenate([kv_ref[keys, g * MLA_NOPE:(g + 1) * MLA_NOPE], kr_ref[keys, :]], axis=1)
        st = _nt_dot(kc, q_ref[q0:tq, g * W:(g + 1) * W])
        if diag:
            st = jnp.where(_causal_t(c * ck, ck, q0, tq - q0), st, NEG_INF)
        return st

    def consume(c, g, q0, diag, st):
        _online_softmax_step_t(st, vt_ref[g * MLA_V:(g + 1) * MLA_V, c * ck:(c + 1) * ck],
                               m_ref.at[g], l_ref.at[g], acc_ref.at[g], slice(q0, tq))

    def sweep(diag):
        _pipelined([(c, g, c * ck if diag else 0, diag) for c in range(nc) for g in range(hp)], scores, consume)

    @pl.when(j < i)
    def _():
        sweep(False)

    @pl.when(j == i)
    def _():
        sweep(True)
        for g in range(hp):
            o_ref[:, g * MLA_V:(g + 1) * MLA_V] = (acc_ref[g] / l_ref[g]).T.astype(o_ref.dtype)


def mla_attention(q, kv, kr, vt, *, tile=2048, chunk=1024, heads_per_step=2):
    S = q.shape[0]
    H, hp = MLA_HEADS, heads_per_step
    tq = _tile(S, tile)
    ck = _tile(tq, chunk)
    qi, kj = _pair_tables(S // tq)
    W = 2 * LANES
    grid_spec = pltpu.PrefetchScalarGridSpec(
        num_scalar_prefetch=2,
        grid=(H // hp, qi.shape[0]),
        in_specs=[pl.BlockSpec((tq, hp * W), lambda h, p, qi, kj: (qi[p], h)),
                  pl.BlockSpec((tq, hp * MLA_NOPE), lambda h, p, qi, kj: (kj[p], h)),
                  pl.BlockSpec((tq, LANES), lambda h, p, qi, kj: (kj[p], 0)),
                  pl.BlockSpec((hp * MLA_V, tq), lambda h, p, qi, kj: (h, kj[p]))],
        out_specs=pl.BlockSpec((tq, hp * MLA_V), lambda h, p, qi, kj: (qi[p], h)),
        scratch_shapes=[pltpu.VMEM((hp, 1, tq), F32), pltpu.VMEM((hp, 1, tq), F32),
                        pltpu.VMEM((hp, MLA_V, tq), F32)],
    )
    return pl.pallas_call(
        functools.partial(_mla_attn_kernel, tq=tq, ck=ck, hp=hp),
        out_shape=jax.ShapeDtypeStruct((S, H * MLA_V), BF16),
        grid_spec=grid_spec,
        compiler_params=_cparams("parallel", "arbitrary"),
        name="mla_attention",
    )(qi, kj, q, kv, kr, vt)


_HI16 = 0xFFFF0000


def _pack_bf16_pairs(hi_f32):
    half = hi_f32.shape[1] // 2
    bits = pltpu.bitcast(hi_f32, jnp.uint32)
    return lax.shift_right_logical(bits[:, :half], jnp.uint32(16)) | (bits[:, half:] & jnp.uint32(_HI16))


def _unpack_pairs_by_tile(words, tile):
    half = tile // 2
    parts = []
    for c0 in range(0, words.shape[1], half):
        w = words[:, c0:c0 + half]
        parts.append(pltpu.bitcast(lax.shift_left(w, jnp.uint32(16)), F32))
        parts.append(pltpu.bitcast(w & jnp.uint32(_HI16), F32))
    return jnp.concatenate(parts, axis=1)


def _unpack_bf16_pairs(words):
    lo = pltpu.bitcast(lax.shift_left(words, jnp.uint32(16)), F32).astype(BF16)
    hi = pltpu.bitcast(words & jnp.uint32(_HI16), F32).astype(BF16)
    return jnp.concatenate([lo, hi], axis=1)


def _router_kernel(x_ref, g_ref, wh_ref, wl_ref, xn_ref, r_ref):
    xn = _rms_rows(x_ref[...], g_ref[...])
    hi = xn.astype(BF16)
    hi_f32 = hi.astype(F32)
    lo = (xn - hi_f32).astype(BF16)
    xn_ref[...] = _pack_bf16_pairs(hi_f32)
    logits = (jnp.dot(hi, wh_ref[...], preferred_element_type=F32)
              + jnp.dot(hi, wl_ref[...], preferred_element_type=F32)
              + jnp.dot(lo, wh_ref[...], preferred_element_type=F32))
    lane = lax.broadcasted_iota(jnp.int32, logits.shape, 1)
    lane_f = lane.astype(F32)
    g = jnp.where(lane < N_EXPERTS, logits, NEG_INF)
    v1 = jnp.max(g, axis=-1, keepdims=True)
    i1 = jnp.min(jnp.where(g == v1, lane_f, float(LANES)), axis=-1, keepdims=True)
    g = jnp.where(lane_f == i1, NEG_INF, g)
    v2 = jnp.max(g, axis=-1, keepdims=True)
    i2 = jnp.min(jnp.where(g == v2, lane_f, float(LANES)), axis=-1, keepdims=True)
    g2 = 1.0 / (1.0 + jnp.exp(v1 - v2))
    g1 = 1.0 - g2
    r_ref[...] = jnp.where(lane == 0, i1,
                           jnp.where(lane == 1, i2,
                                     jnp.where(lane == 2, g1, jnp.where(lane == 3, g2, 0.0))))


def moe_route(x, g, w_router, *, tm=512):
    M, D = x.shape
    tm = _tile(M, tm)
    wr = jnp.pad(w_router.astype(F32), ((0, 0), (0, LANES - N_EXPERTS)))
    wh = wr.astype(BF16)
    wl = (wr - wh.astype(F32)).astype(BF16)
    return pl.pallas_call(
        _router_kernel,
        out_shape=[jax.ShapeDtypeStruct((M, D // 2), jnp.uint32), jax.ShapeDtypeStruct((M, LANES), F32)],
        grid=(M // tm,),
        in_specs=[pl.BlockSpec((tm, D), lambda i: (i, 0)),
                  pl.BlockSpec((1, D), lambda i: (0, 0)),
                  pl.BlockSpec((D, LANES), lambda i: (0, 0)),
                  pl.BlockSpec((D, LANES), lambda i: (0, 0))],
        out_specs=[pl.BlockSpec((tm, D // 2), lambda i: (i, 0)),
                   pl.BlockSpec((tm, LANES), lambda i: (i, 0))],
        compiler_params=_cparams("parallel"),
        name="moe_route",
    )(x, g.reshape(1, D), wh, wl)


def _new_expert(te_ref, t):
    return jnp.logical_or(t == 0, te_ref[t] != te_ref[jnp.maximum(t - 1, 0)])


def _stream_expert_weights(te_ref, rid_ref, rexp_ref, nr_ref, used, n_col_tiles, copies_for, on_ready):
    j, t = pl.program_id(0), pl.program_id(1)

    @pl.when(jnp.logical_and(used, _new_expert(te_ref, t)))
    def _():
        n_runs = nr_ref[0]
        r = rid_ref[t]
        g = j * n_runs + r
        slot = lax.rem(g, 2)

        @pl.when(g == 0)
        def _():
            for c in copies_for(j, te_ref[t], slot):
                c.start()

        for c in copies_for(j, te_ref[t], slot):
            c.wait()
        on_ready(slot)
        last = r + 1 == n_runs
        nxt_r = jnp.where(last, 0, r + 1)
        nxt_j = jnp.where(last, j + 1, j)

        @pl.when(nxt_j < n_col_tiles)
        def _():
            for c in copies_for(nxt_j, rexp_ref[nxt_r], 1 - slot):
                c.start()


def _gmm_swiglu_kernel(te_ref, nu_ref, rid_ref, rexp_ref, nr_ref, *refs, n_parts, tiles_per_part, layer, nj, tn):
    x_refs = refs[:n_parts]
    w_hbm, o_ref, wbuf_ref, wgb_ref, wub_ref, sem = refs[n_parts:]
    t = pl.program_id(1)
    used = t < nu_ref[0]

    def copies_for(jj, e, slot):
        gate_cols = pl.ds(pl.multiple_of(jj * tn, tn), tn)
        up_cols = pl.ds(pl.multiple_of((nj + jj) * tn, tn), tn)
        return (pltpu.make_async_copy(w_hbm.at[layer, e, :, gate_cols], wbuf_ref.at[slot, 0], sem.at[slot, 0]),
                pltpu.make_async_copy(w_hbm.at[layer, e, :, up_cols], wbuf_ref.at[slot, 1], sem.at[slot, 1]))

    def on_ready(slot):
        wgb_ref[...] = wbuf_ref[slot, 0].astype(BF16)
        wub_ref[...] = wbuf_ref[slot, 1].astype(BF16)

    _stream_expert_weights(te_ref, rid_ref, rexp_ref, nr_ref, used, nj, copies_for, on_ready)

    for c in range(n_parts):
        @pl.when(jnp.logical_and(used, t // tiles_per_part == c))
        def _(c=c):
            x = _unpack_bf16_pairs(x_refs[c][...])
            a = jnp.dot(x, wgb_ref[...], preferred_element_type=F32)
            u = jnp.dot(x, wub_ref[...], preferred_element_type=F32)
            o_ref[...] = (a * jax.nn.sigmoid(a) * u).astype(o_ref.dtype)

    @pl.when(jnp.logical_not(used))
    def _():
        o_ref[...] = jnp.zeros(o_ref.shape, o_ref.dtype)


def _gmm_out_kernel(te_ref, nu_ref, rid_ref, rexp_ref, nr_ref, h_ref, w_hbm, o_ref, wbuf_ref, wb_ref, sem, *,
                    layer, nj, tn):
    t = pl.program_id(1)
    used = t < nu_ref[0]

    def copies_for(jj, e, slot):
        cols = pl.ds(pl.multiple_of(jj * tn, tn), tn)
        return (pltpu.make_async_copy(w_hbm.at[layer, e, :, cols], wbuf_ref.at[slot], sem.at[slot]),)

    def on_ready(slot):
        wb_ref[...] = wbuf_ref[slot].astype(BF16)

    _stream_expert_weights(te_ref, rid_ref, rexp_ref, nr_ref, used, nj, copies_for, on_ready)

    @pl.when(used)
    def _():
        y = jnp.dot(h_ref[...], wb_ref[...], preferred_element_type=F32)
        o_ref[...] = _pack_bf16_pairs(y.astype(BF16).astype(F32))

    @pl.when(jnp.logical_not(used))
    def _():
        o_ref[...] = jnp.zeros(o_ref.shape, o_ref.dtype)


def moe_experts(xs, tile_expert, n_used, runs, w_in, w_out, layer, *, tm, tn=MOE_COL_TILE):
    n_parts = len(xs)
    P = n_parts * xs[0].shape[0]
    D = w_in.shape[2]
    F = w_in.shape[3] // 2
    nt = P // tm
    ntp = nt // n_parts
    assert ntp * n_parts == nt
    tn1, tn2 = _tile(F, tn), _tile(D, tn)
    nj1, nj2 = F // tn1, D // tn2

    def part_spec(c):
        return pl.BlockSpec((tm, D // 2), lambda j, t, *_: (jnp.clip(t - c * ntp, 0, ntp - 1), 0))

    prefetch = (tile_expert, n_used) + tuple(runs)
    hbm = pl.BlockSpec(memory_space=pl.ANY)
    h = pl.pallas_call(
        functools.partial(_gmm_swiglu_kernel, n_parts=n_parts, tiles_per_part=ntp, layer=layer, nj=nj1, tn=tn1),
        out_shape=jax.ShapeDtypeStruct((P, F), BF16),
        grid_spec=pltpu.PrefetchScalarGridSpec(
            num_scalar_prefetch=len(prefetch),
            grid=(nj1, nt),
            in_specs=[part_spec(c) for c in range(n_parts)] + [hbm],
            out_specs=pl.BlockSpec((tm, tn1), lambda j, t, *_: (t, j)),
            scratch_shapes=[pltpu.VMEM((2, 2, D, tn1), F32), pltpu.VMEM((D, tn1), BF16),
                            pltpu.VMEM((D, tn1), BF16), pltpu.SemaphoreType.DMA((2, 2))],
        ),
        compiler_params=_cparams("arbitrary", "arbitrary"),
        name="moe_swiglu_in",
    )(*prefetch, *xs, w_in)
    return pl.pallas_call(
        functools.partial(_gmm_out_kernel, layer=layer, nj=nj2, tn=tn2),
        out_shape=jax.ShapeDtypeStruct((P, D // 2), jnp.uint32),
        grid_spec=pltpu.PrefetchScalarGridSpec(
            num_scalar_prefetch=len(prefetch),
            grid=(nj2, nt),
            in_specs=[pl.BlockSpec((tm, F), lambda j, t, *_: (t, 0)), hbm],
            out_specs=pl.BlockSpec((tm, tn2 // 2), lambda j, t, *_: (t, j)),
            scratch_shapes=[pltpu.VMEM((2, F, tn2), F32), pltpu.VMEM((F, tn2), BF16),
                            pltpu.SemaphoreType.DMA((2,))],
        ),
        compiler_params=_cparams("arbitrary", "arbitrary"),
        name="moe_out",
    )(*prefetch, h, w_out)


def moe_swiglu(x, g, w_router, w_in, w_out, layer, *, tm=512):
    S, D = x.shape
    E = N_EXPERTS
    tm = _tile(S, tm)
    xn, route = moe_route(x, g, w_router)
    experts = route[:, :2].astype(jnp.int32)
    gates = route[:, 2:4]
    flat_e = experts.reshape(-1)
    order = jnp.argsort(flat_e, stable=True).astype(jnp.int32)
    rank = jnp.argsort(order).astype(jnp.int32)
    counts = jnp.sum(flat_e[:, None] == jnp.arange(E, dtype=jnp.int32)[None, :], axis=0, dtype=jnp.int32)
    padded = ((counts + tm - 1) // tm) * tm
    start = jnp.cumsum(counts) - counts
    pstart = jnp.cumsum(padded) - padded
    pend = jnp.cumsum(padded)
    P = 2 * S + E * tm
    nt = P // tm
    tile_expert = jnp.minimum(
        jnp.sum(pend[None, :] <= (jnp.arange(nt, dtype=jnp.int32) * tm)[:, None], axis=1), E - 1).astype(jnp.int32)
    n_used = (pend[-1:] // tm).astype(jnp.int32)
    row_e = jnp.repeat(tile_expert, tm)
    row_rank = jnp.arange(P, dtype=jnp.int32) - pstart[row_e]
    row_valid = jnp.logical_and(row_rank < counts[row_e], jnp.arange(P) < pend[-1])
    row_token = jnp.where(row_valid, order[jnp.clip(start[row_e] + row_rank, 0, 2 * S - 1)] // 2, 0)
    slot = pstart[flat_e] + rank - start[flat_e]
    tile_expert = jnp.where(jnp.arange(nt) < n_used[0], tile_expert, tile_expert[jnp.maximum(n_used[0] - 1, 0)])
    slot = slot.reshape(S, 2)
    present = counts > 0
    run_of_expert = jnp.cumsum(present.astype(jnp.int32)) - 1
    runs = (run_of_expert[tile_expert].astype(jnp.int32),
            jnp.sort(jnp.where(present, jnp.arange(E, dtype=jnp.int32), E))[:E].clip(0, E - 1).astype(jnp.int32),
            jnp.sum(present).astype(jnp.int32).reshape(1))
    xs = [jnp.take(xn, part, axis=0, mode="clip") for part in jnp.split(row_token, MOE_GATHER_PARTS)]
    ys = moe_experts(xs, tile_expert, n_used, runs, w_in, w_out, layer, tm=tm)
    return jnp.take(ys, slot[:, 0], axis=0, mode="clip"), jnp.take(ys, slot[:, 1], axis=0, mode="clip"), gates


def _diff_lambda_init(layer):
    return 0.8 - 0.6 * math.exp(-0.3 * layer)


def _rope_table(pos):
    half = MLA_ROPE // 2
    inv_freq = ROPE_THETA ** (-jnp.arange(half, dtype=F32) / half)
    ang = pos.astype(F32)[:, None] * inv_freq
    cos, sin = jnp.cos(ang), jnp.sin(ang)
    return jnp.concatenate([cos, cos, -sin, sin], axis=1)


def _rot_half_cols(w):
    half = w.shape[-1] // 2
    return jnp.concatenate([w[..., half:], w[..., :half]], axis=-1)


def kernel(x, p, positions, rel_bias, norm_mix, norm_ffn, norm_ple, norm_final, diff_w_qkv, diff_lambda, diff_subln, diff_w_o, moba_w_qkv, moba_w_o, mla_w_down, mla_g_q, mla_w_uq, mla_g_kv, mla_w_ukv, mla_w_o, ffn_w_in, ffn_w_out, moe_w_router, moe_w_in, moe_w_out, ple_w_gate, ple_w_proj):
    B, S, D = x.shape
    depth = p.shape[0]
    lut = _bias_lut(rel_bias)
    outs = []
    for b in range(B):
        xb = x[b]
        pos = positions[b].astype(jnp.int32)
        for i in range(depth):
            jm = i // N_MIXERS
            if i % N_MIXERS == 0:
                qk, vt = norm_matmul_vt(xb, norm_mix[i], diff_w_qkv, layer=jm, n_direct=2 * D,
                                        scaled_cols=D, scale=DIFF_HEAD_DIM ** -0.5 * LOG2E)
                o = diff_attention(qk, vt, pos, lut, diff_lambda[jm].astype(F32), diff_subln[jm],
                                   _diff_lambda_init(i))
                xb = matmul_residual(o, diff_w_o, xb, layer=jm)
            elif i % N_MIXERS == 1:
                qk, vt = norm_matmul_vt(xb, norm_mix[i], moba_w_qkv, layer=jm, n_direct=2 * D,
                                        scaled_cols=D, scale=MOBA_HEAD_DIM ** -0.5 * LOG2E)
                o = moba_attention(qk, vt, pos, lut)
                xb = matmul_residual(o, moba_w_o, xb, layer=jm)
            else:
                wd = mla_w_down[jm]
                kr0 = MLA_Q_RANK + MLA_KV_RANK
                wd_ext = jnp.concatenate([wd, _rot_half_cols(wd[:, kr0:])], axis=1).astype(BF16)
                wq = mla_w_uq[jm].reshape(MLA_Q_RANK, MLA_HEADS, MLA_NOPE + MLA_ROPE)
                wq_ext = jnp.concatenate([wq, _rot_half_cols(wq[..., MLA_NOPE:])], axis=-1)
                wq_ext = wq_ext.reshape(MLA_Q_RANK, MLA_HEADS * 2 * LANES).astype(BF16)
                down = norm_matmul(xb, norm_mix[i], wd_ext, out_dtype=F32, tn=wd_ext.shape[1])
                q, kr = mla_queries_and_rope_key(down, mla_g_q[jm], wq_ext, _rope_table(pos),
                                                 scale=(MLA_NOPE + MLA_ROPE) ** -0.5 * LOG2E)
                wkv = mla_w_ukv[jm].reshape(MLA_KV_RANK, MLA_HEADS, 2, MLA_V).transpose(0, 2, 1, 3)
                wkv = wkv.reshape(MLA_KV_RANK, 2 * MLA_HEADS * MLA_V).astype(BF16)
                kn, vt = norm_matmul_vt(down, mla_g_kv[jm], wkv, n_direct=MLA_HEADS * MLA_NOPE, xcol=1)
                o = mla_attention(q, kn, kr, vt)
                xb = matmul_residual(o, mla_w_o, xb, layer=jm)
            if i % 2 == 0:
                h = norm_swiglu_in(xb, norm_ffn[i], ffn_w_in, layer=i // 2)
                xb = matmul_residual(h, ffn_w_out[i // 2].astype(BF16), xb)
                xb = ple_update(xb, norm_ple[i], p[i, b], ple_w_gate, ple_w_proj, layer=i)
            else:
                y1, y2, gates = moe_swiglu(xb, norm_ffn[i], moe_w_router[i // 2], moe_w_in, moe_w_out, i // 2)
                xb = ple_update_moe(xb, y1, y2, gates, norm_ple[i], p[i, b], ple_w_gate, ple_w_proj, layer=i)
        outs.append(final_norm(xb, norm_final))
    return jnp.stack(outs, axis=0)
```

```python
import functools
import math

import numpy as np
import jax
import jax.numpy as jnp
from jax import lax
from jax.experimental import pallas as pl
from jax.experimental.pallas import tpu as pltpu

F32 = jnp.float32
BF16 = jnp.bfloat16

NORM_EPS = 1e-6
NEG_INF = -1e30
LOG2E = math.log2(math.e)
LANES = 128
VMEM_LIMIT_BYTES = 56 * 1024 * 1024

REL_BUCKETS = 32
REL_MAX_DIST = 128
LUT_SIZE = LANES
BIAS_BLOCK = 256
DIFF_HEADS = 8
DIFF_HEAD_DIM = 128
MOBA_HEADS = 16
MOBA_HEAD_DIM = 128
MOBA_BLOCK = 256
MOBA_TOPK = 3
MLA_HEADS = 16
MLA_Q_RANK = 512
MLA_KV_RANK = 512
MLA_NOPE = 128
MLA_ROPE = 64
MLA_V = 128
ROPE_THETA = 10000.0
N_EXPERTS = 8
N_MIXERS = 3
MOE_GATHER_PARTS = 4
MOE_COL_TILE = 512


def _cparams(*sem):
    return pltpu.CompilerParams(dimension_semantics=sem, vmem_limit_bytes=VMEM_LIMIT_BYTES)


def _tile(n, pref):
    if n <= pref:
        return n
    t = pref
    while n % t:
        t //= 2
    return t


def _weight_spec(w, layer, tn, col=lambda j: j):
    K = w.shape[-2]
    if w.ndim == 2:
        return pl.BlockSpec((K, tn), lambda i, j: (0, col(j)))
    return pl.BlockSpec((None, K, tn), lambda i, j: (layer, 0, col(j)))


def _rms_rows(x, g):
    r = lax.rsqrt(jnp.mean(x * x, axis=-1, keepdims=True) + NORM_EPS)
    return x * r * g


def _norm_mm_kernel(x_ref, g_ref, w_ref, o_ref, xn_ref, *, scaled_tiles, scale):
    j = pl.program_id(1)

    @pl.when(j == 0)
    def _():
        xn_ref[...] = _rms_rows(x_ref[...], g_ref[...]).astype(BF16)

    acc = jnp.dot(xn_ref[...], w_ref[...].astype(BF16), preferred_element_type=F32)
    if scaled_tiles:
        acc = acc * jnp.where(j < scaled_tiles, scale, 1.0)
    o_ref[...] = acc.astype(o_ref.dtype)


def norm_matmul(x, g, w, *, out_dtype, xcol=0, tm=1024, tn=512, scaled_cols=0, scale=1.0):
    M = x.shape[0]
    K, N = w.shape
    tm, tn = _tile(M, tm), _tile(N, tn)
    assert scaled_cols % tn == 0
    kern = functools.partial(_norm_mm_kernel, scaled_tiles=scaled_cols // tn, scale=scale)
    return pl.pallas_call(
        kern,
        out_shape=jax.ShapeDtypeStruct((M, N), out_dtype),
        grid=(M // tm, N // tn),
        in_specs=[pl.BlockSpec((tm, K), lambda i, j: (i, xcol)),
                  pl.BlockSpec((1, K), lambda i, j: (0, 0)),
                  pl.BlockSpec((K, tn), lambda i, j: (0, j))],
        out_specs=pl.BlockSpec((tm, tn), lambda i, j: (i, j)),
        scratch_shapes=[pltpu.VMEM((tm, K), BF16)],
        compiler_params=_cparams("parallel", "arbitrary"),
        name="norm_matmul",
    )(x, g.reshape(1, K), w)


def _norm_mm_vt_kernel(x_ref, g_ref, w_ref, o_ref, ot_ref, xn_ref, *, direct_tiles, scaled_tiles, scale):
    j = pl.program_id(1)

    @pl.when(j == 0)
    def _():
        xn_ref[...] = _rms_rows(x_ref[...], g_ref[...]).astype(BF16)

    acc = jnp.dot(xn_ref[...], w_ref[...].astype(BF16), preferred_element_type=F32)

    @pl.when(j < direct_tiles)
    def _():
        out = acc * jnp.where(j < scaled_tiles, scale, 1.0) if scaled_tiles else acc
        o_ref[...] = out.astype(o_ref.dtype)

    @pl.when(j >= direct_tiles)
    def _():
        ot_ref[...] = acc.T.astype(ot_ref.dtype)


def norm_matmul_vt(x, g, w, *, n_direct, layer=None, xcol=0, tm=1024, tn=512, scaled_cols=0, scale=1.0):
    M = x.shape[0]
    K, N = w.shape[-2:]
    tm, tn = _tile(M, tm), _tile(N, tn)
    assert scaled_cols % tn == 0 and n_direct % tn == 0 and 0 < n_direct < N
    nd = n_direct // tn
    kern = functools.partial(_norm_mm_vt_kernel, direct_tiles=nd, scaled_tiles=scaled_cols // tn, scale=scale)
    return pl.pallas_call(
        kern,
        out_shape=[jax.ShapeDtypeStruct((M, n_direct), BF16), jax.ShapeDtypeStruct((N - n_direct, M), BF16)],
        grid=(M // tm, N // tn),
        in_specs=[pl.BlockSpec((tm, K), lambda i, j: (i, xcol)),
                  pl.BlockSpec((1, K), lambda i, j: (0, 0)),
                  _weight_spec(w, layer, tn)],
        out_specs=[pl.BlockSpec((tm, tn), lambda i, j: (i, jnp.minimum(j, nd - 1))),
                   pl.BlockSpec((tn, tm), lambda i, j: (jnp.maximum(j - nd, 0), i))],
        scratch_shapes=[pltpu.VMEM((tm, K), BF16)],
        compiler_params=_cparams("parallel", "arbitrary"),
        name="norm_matmul_vt",
    )(x, g.reshape(1, K), w)


def _norm_swiglu_kernel(x_ref, g_ref, wg_ref, wu_ref, o_ref, xn_ref):
    @pl.when(pl.program_id(1) == 0)
    def _():
        xn_ref[...] = _rms_rows(x_ref[...], g_ref[...]).astype(BF16)

    xn = xn_ref[...]
    a = jnp.dot(xn, wg_ref[...].astype(BF16), preferred_element_type=F32)
    u = jnp.dot(xn, wu_ref[...].astype(BF16), preferred_element_type=F32)
    o_ref[...] = (a * jax.nn.sigmoid(a) * u).astype(o_ref.dtype)


def norm_swiglu_in(x, g, w_in, *, layer=None, tm=1024, tn=512):
    M, K = x.shape
    F = w_in.shape[-1] // 2
    tm, tn = _tile(M, tm), _tile(F, tn)
    nj = F // tn
    return pl.pallas_call(
        _norm_swiglu_kernel,
        out_shape=jax.ShapeDtypeStruct((M, F), BF16),
        grid=(M // tm, nj),
        in_specs=[pl.BlockSpec((tm, K), lambda i, j: (i, 0)),
                  pl.BlockSpec((1, K), lambda i, j: (0, 0)),
                  _weight_spec(w_in, layer, tn),
                  _weight_spec(w_in, layer, tn, lambda j: j + nj)],
        out_specs=pl.BlockSpec((tm, tn), lambda i, j: (i, j)),
        scratch_shapes=[pltpu.VMEM((tm, K), BF16)],
        compiler_params=_cparams("parallel", "arbitrary"),
        name="norm_swiglu_in",
    )(x, g.reshape(1, K), w_in, w_in)


def _mm_res_kernel(a_ref, w_ref, r_ref, o_ref):
    o_ref[...] = r_ref[...] + jnp.dot(a_ref[...], w_ref[...].astype(BF16), preferred_element_type=F32)


def matmul_residual(a, w, res, *, layer=None, tm=1024, tn=512):
    M, K = a.shape
    N = w.shape[-1]
    tm, tn = _tile(M, tm), _tile(N, tn)
    return pl.pallas_call(
        _mm_res_kernel,
        out_shape=jax.ShapeDtypeStruct((M, N), F32),
        grid=(M // tm, N // tn),
        in_specs=[pl.BlockSpec((tm, K), lambda i, j: (i, 0)),
                  _weight_spec(w, layer, tn),
                  pl.BlockSpec((tm, tn), lambda i, j: (i, j))],
        out_specs=pl.BlockSpec((tm, tn), lambda i, j: (i, j)),
        compiler_params=_cparams("parallel", "parallel"),
        name="matmul_residual",
    )(a, w, res)


def _ple_kernel(x_ref, g_ref, xr_ref, p_ref, wg_ref, wp_ref, o_ref, xn_ref):
    @pl.when(pl.program_id(1) == 0)
    def _():
        xn_ref[...] = _rms_rows(x_ref[...], g_ref[...]).astype(BF16)

    gate = jax.nn.sigmoid(jnp.dot(xn_ref[...], wg_ref[...].astype(BF16), preferred_element_type=F32))
    proj = jnp.dot(p_ref[...].astype(BF16), wp_ref[...].astype(BF16), preferred_element_type=F32)
    o_ref[...] = xr_ref[...] + gate * proj


def ple_update(x, g, p, w_gate, w_proj, *, layer=None, tm=1024, tn=512):
    M, D = x.shape
    P = p.shape[1]
    tm, tn = _tile(M, tm), _tile(D, tn)
    return pl.pallas_call(
        _ple_kernel,
        out_shape=jax.ShapeDtypeStruct((M, D), F32),
        grid=(M // tm, D // tn),
        in_specs=[pl.BlockSpec((tm, D), lambda i, j: (i, 0)),
                  pl.BlockSpec((1, D), lambda i, j: (0, 0)),
                  pl.BlockSpec((tm, tn), lambda i, j: (i, j)),
                  pl.BlockSpec((tm, P), lambda i, j: (i, 0)),
                  _weight_spec(w_gate, layer, tn),
                  _weight_spec(w_proj, layer, tn)],
        out_specs=pl.BlockSpec((tm, tn), lambda i, j: (i, j)),
        scratch_shapes=[pltpu.VMEM((tm, D), BF16)],
        compiler_params=_cparams("parallel", "arbitrary"),
        name="ple_update",
    )(x, g.reshape(1, D), x, p, w_gate, w_proj)


def _ple_moe_kernel(x_ref, y1_ref, y2_ref, gt_ref, g_ref, p_ref, wg_ref, wp_ref, o_ref, xs_ref, xn_ref, *, tn,
                    pair_tile):
    j = pl.program_id(1)

    @pl.when(j == 0)
    def _():
        gt = gt_ref[...]
        xnew = (x_ref[...] + gt[:, 0:1] * _unpack_pairs_by_tile(y1_ref[...], pair_tile)
                + gt[:, 1:2] * _unpack_pairs_by_tile(y2_ref[...], pair_tile))
        xn_ref[...] = _rms_rows(xnew, g_ref[...]).astype(BF16)
        for jj in range(xs_ref.shape[0]):
            xs_ref[jj] = xnew[:, jj * tn:(jj + 1) * tn]

    gate = jax.nn.sigmoid(jnp.dot(xn_ref[...], wg_ref[...].astype(BF16), preferred_element_type=F32))
    proj = jnp.dot(p_ref[...].astype(BF16), wp_ref[...].astype(BF16), preferred_element_type=F32)
    o_ref[...] = xs_ref[j] + gate * proj


def ple_update_moe(x, y1, y2, gates, g, p, w_gate, w_proj, *, layer=None, tm=512, tn=512):
    M, D = x.shape
    P = p.shape[1]
    tm, tn = _tile(M, tm), _tile(D, tn)
    row = pl.BlockSpec((tm, D), lambda i, j: (i, 0))
    packed_row = pl.BlockSpec((tm, D // 2), lambda i, j: (i, 0))
    return pl.pallas_call(
        functools.partial(_ple_moe_kernel, tn=tn, pair_tile=_tile(D, MOE_COL_TILE)),
        out_shape=jax.ShapeDtypeStruct((M, D), F32),
        grid=(M // tm, D // tn),
        in_specs=[row, packed_row, packed_row,
                  pl.BlockSpec((tm, 2), lambda i, j: (i, 0)),
                  pl.BlockSpec((1, D), lambda i, j: (0, 0)),
                  pl.BlockSpec((tm, P), lambda i, j: (i, 0)),
                  _weight_spec(w_gate, layer, tn),
                  _weight_spec(w_proj, layer, tn)],
        out_specs=pl.BlockSpec((tm, tn), lambda i, j: (i, j)),
        scratch_shapes=[pltpu.VMEM((D // tn, tm, tn), F32), pltpu.VMEM((tm, D), BF16)],
        compiler_params=_cparams("parallel", "arbitrary"),
        name="ple_update_moe",
    )(x, y1, y2, gates, g.reshape(1, D), p, w_gate, w_proj)


def _final_norm_kernel(x_ref, g_ref, o_ref):
    o_ref[...] = _rms_rows(x_ref[...], g_ref[...])


def final_norm(x, g, *, tm=512):
    M, D = x.shape
    tm = _tile(M, tm)
    return pl.pallas_call(
        _final_norm_kernel,
        out_shape=jax.ShapeDtypeStruct((M, D), F32),
        grid=(M // tm,),
        in_specs=[pl.BlockSpec((tm, D), lambda i: (i, 0)),
                  pl.BlockSpec((1, D), lambda i: (0, 0))],
        out_specs=pl.BlockSpec((tm, D), lambda i: (i, 0)),
        compiler_params=_cparams("parallel"),
        name="final_norm",
    )(x, g.reshape(1, D))


def _pair_tables(n_tiles):
    qi, kj = [], []
    for i in range(n_tiles):
        for j in range(i + 1):
            qi.append(i)
            kj.append(j)
    return jnp.asarray(np.array(qi, np.int32)), jnp.asarray(np.array(kj, np.int32))


def _rel_bucket(dist):
    n = jnp.maximum(dist, 0)
    max_exact = REL_BUCKETS // 2
    nf = jnp.maximum(n, 1).astype(F32)
    large = max_exact + (jnp.log(nf / max_exact) / math.log(REL_MAX_DIST / max_exact)
                         * (REL_BUCKETS - max_exact)).astype(jnp.int32)
    large = jnp.minimum(large, REL_BUCKETS - 1)
    return jnp.where(n < max_exact, n, large)


def _bias_lut(rel_bias):
    buckets = _rel_bucket(jnp.arange(LUT_SIZE, dtype=jnp.int32))
    return rel_bias.astype(F32)[buckets].T * LOG2E


def _block_pos_bounds(pos, sb):
    blocks = pos.reshape(pos.shape[0] // sb, sb)
    wide = jnp.any(jnp.logical_or(pos >= 2 ** 30, pos <= -(2 ** 30)))
    return jnp.where(wide, 0, blocks.min(axis=1)), jnp.where(wide, 0, blocks.max(axis=1))


def _lut_bias(lut_row, idx):
    R, C = idx.shape
    lut = jnp.broadcast_to(lut_row, (R, LANES))
    parts = [jnp.take_along_axis(lut, idx[:, c:c + LANES], axis=1, mode="promise_in_bounds")
             for c in range(0, C, LANES)]
    return parts[0] if len(parts) == 1 else jnp.concatenate(parts, axis=1)


def _pipelined(units, scores, consume):
    st = scores(*units[0])
    for k, u in enumerate(units):
        nxt = scores(*units[k + 1]) if k + 1 < len(units) else None
        consume(*u, st)
        st = nxt


def _online_softmax_step_t(st, vt, m_ref, l_ref, acc_ref, qcols, m_cur=None, offsets=None):
    m_prev = m_ref[:, qcols]
    if m_cur is None:
        m_cur = jnp.max(st, axis=0, keepdims=True)
    m_new = jnp.maximum(m_prev, m_cur)
    alpha = jnp.exp2(m_prev - m_new)
    if offsets is None:
        p = jnp.exp2(st - m_new)
    else:
        parts = [jnp.exp2(st[rows, :] - fn(m_new)) for rows, fn in offsets]
        p = parts[0] if len(parts) == 1 else jnp.concatenate(parts, axis=0)
    l_ref[:, qcols] = alpha * l_ref[:, qcols] + jnp.sum(p, axis=0, keepdims=True)
    acc_ref[:, qcols] = alpha * acc_ref[:, qcols] + jnp.dot(vt, p.astype(BF16), preferred_element_type=F32)
    m_ref[:, qcols] = m_new


def _causal_t(k0, nk, q0, nq):
    r = k0 + lax.broadcasted_iota(jnp.int32, (nk, nq), 0)
    c = q0 + lax.broadcasted_iota(jnp.int32, (nk, nq), 1)
    return r <= c


def _nt_dot(a, b):
    return lax.dot_general(a, b, (((1,), (1,)), ((), ())), preferred_element_type=F32)


def _diff_attn_kernel(qi_ref, kj_ref, pminq_ref, pmaxk_ref,
                      q_ref, k_ref, vt_ref, pq_ref, pk_ref, lut_ref, lam_ref, g_ref,
                      o_ref, m_ref, l_ref, acc_ref, st_ref, *, tq, ck, sb, hp, lambda_init):
    p = pl.program_id(1)
    i, j = qi_ref[p], kj_ref[p]
    nc = tq // ck
    dh = DIFF_HEAD_DIM
    nm = 2 * hp
    SB = sb
    nqb, nkb = tq // SB, ck // SB

    @pl.when(j == 0)
    def _():
        m_ref[...] = jnp.full(m_ref.shape, NEG_INF, F32)
        l_ref[...] = jnp.zeros(l_ref.shape, F32)
        acc_ref[...] = jnp.zeros(acc_ref.shape, F32)

    def near_sub(c, kb, qb):
        return pminq_ref[i * nqb + qb] - pmaxk_ref[(j * nc + c) * nkb + kb] < LUT_SIZE - 1

    def refine_chunk(c):
        keys = slice(c * ck, (c + 1) * ck)
        for mp in range(nm):
            st_ref[mp] = _nt_dot(k_ref[keys, mp * dh:(mp + 1) * dh], q_ref[:, mp * dh:(mp + 1) * dh])
        for kb in range(nkb):
            for qb in range(nqb):
                @pl.when(near_sub(c, kb, qb))
                def _(kb=kb, qb=qb):
                    rows, cols = slice(kb * SB, (kb + 1) * SB), slice(qb * SB, (qb + 1) * SB)
                    idx = jnp.clip(pq_ref[:, cols] - pk_ref[c * ck + kb * SB:c * ck + (kb + 1) * SB, :],
                                   0, LUT_SIZE - 1)
                    for mp in range(nm):
                        far_bias = lut_ref[mp:mp + 1, LUT_SIZE - 1:LUT_SIZE]
                        st_ref[mp, rows, cols] = (st_ref[mp, rows, cols]
                                                  + (_lut_bias(lut_ref[mp:mp + 1, :], idx) - far_bias))
        for mp in range(nm):
            consume(c, mp, 0, "far", st_ref[mp])

    def scores(c, mp, q0, mode):
        keys = slice(c * ck, (c + 1) * ck)
        st = _nt_dot(k_ref[keys, mp * dh:(mp + 1) * dh], q_ref[q0:tq, mp * dh:(mp + 1) * dh])
        if mode != "far":
            idx = jnp.clip(pq_ref[:, q0:tq] - pk_ref[keys, :], 0, LUT_SIZE - 1)
            st = st + _lut_bias(lut_ref[mp:mp + 1, :], idx)
        if mode == "diag":
            st = jnp.where(_causal_t(c * ck, ck, q0, tq - q0), st, NEG_INF)
        return st

    def consume(c, mp, q0, mode, st):
        m_cur = offsets = None
        if mode == "far":
            bias = lut_ref[mp:mp + 1, LUT_SIZE - 1:LUT_SIZE]
            m_cur = jnp.max(st, axis=0, keepdims=True) + bias
            offsets = [(slice(None), lambda m_new: m_new - bias)]
        hd = mp // 2
        _online_softmax_step_t(st, vt_ref[hd * 2 * dh:(hd + 1) * 2 * dh, c * ck:(c + 1) * ck],
                               m_ref.at[mp], l_ref.at[mp], acc_ref.at[mp], slice(q0, tq), m_cur, offsets)

    def sweep(chunks, mode):
        units = [(c, mp, c * ck if mode == "diag" else 0, mode) for c in chunks for mp in range(nm)]
        _pipelined(units, scores, consume)

    @pl.when(j < i)
    def _():
        fars = [jnp.logical_not(functools.reduce(
            jnp.logical_or, [near_sub(c, kb, qb) for kb in range(nkb) for qb in range(nqb)])) for c in range(nc)]
        all_far = functools.reduce(jnp.logical_and, fars)
        pl.when(all_far)(functools.partial(sweep, range(nc), "far"))

        @pl.when(jnp.logical_not(all_far))
        def _():
            for c in range(nc):
                refine_chunk(c)

    @pl.when(j == i)
    def _():
        sweep(range(nc), "diag")
        lam = lam_ref[...]
        lam_full = (jnp.exp(jnp.sum(lam[0:1] * lam[1:2], axis=-1, keepdims=True))
                    - jnp.exp(jnp.sum(lam[2:3] * lam[3:4], axis=-1, keepdims=True)) + lambda_init)
        for hd in range(hp):
            a = (acc_ref[2 * hd] / l_ref[2 * hd]
                 - lam_full * (acc_ref[2 * hd + 1] / l_ref[2 * hd + 1]))
            r = lax.rsqrt(jnp.mean(a * a, axis=0, keepdims=True) + NORM_EPS)
            y = a * r * (g_ref[...] * (1.0 - lambda_init))
            o_ref[:, hd * 2 * dh:(hd + 1) * 2 * dh] = y.T.astype(o_ref.dtype)


def diff_attention(qkv, vt, pos, lut, lam, subln_g, lambda_init, *, tile=1024, chunk=512, heads_per_step=1):
    S = qkv.shape[0]
    H, dh = DIFF_HEADS, DIFF_HEAD_DIM
    tq = _tile(S, tile)
    ck = _tile(tq, chunk)
    qi, kj = _pair_tables(S // tq)
    sb = _tile(ck, BIAS_BLOCK)
    pminq, pmaxk = _block_pos_bounds(pos, sb)
    hp = heads_per_step
    G = H // hp
    W = hp * 2 * dh
    kern = functools.partial(_diff_attn_kernel, tq=tq, ck=ck, sb=sb, hp=hp, lambda_init=lambda_init)
    grid_spec = pltpu.PrefetchScalarGridSpec(
        num_scalar_prefetch=4,
        grid=(G, qi.shape[0]),
        in_specs=[pl.BlockSpec((tq, W), lambda h, p, qi, kj, a, b: (qi[p], h)),
                  pl.BlockSpec((tq, W), lambda h, p, qi, kj, a, b: (kj[p], G + h)),
                  pl.BlockSpec((W, tq), lambda h, p, qi, kj, a, b: (h, kj[p])),
                  pl.BlockSpec((1, tq), lambda h, p, qi, kj, a, b: (0, qi[p])),
                  pl.BlockSpec((tq, 1), lambda h, p, qi, kj, a, b: (kj[p], 0)),
                  pl.BlockSpec((None, 2 * hp, LUT_SIZE), lambda h, p, qi, kj, a, b: (h, 0, 0)),
                  pl.BlockSpec((4, dh), lambda h, p, qi, kj, a, b: (0, 0)),
                  pl.BlockSpec((2 * dh, 1), lambda h, p, qi, kj, a, b: (0, 0))],
        out_specs=pl.BlockSpec((tq, W), lambda h, p, qi, kj, a, b: (qi[p], h)),
        scratch_shapes=[pltpu.VMEM((2 * hp, 1, tq), F32), pltpu.VMEM((2 * hp, 1, tq), F32),
                        pltpu.VMEM((2 * hp, 2 * dh, tq), F32), pltpu.VMEM((2 * hp, ck, tq), F32)],
    )
    return pl.pallas_call(
        kern,
        out_shape=jax.ShapeDtypeStruct((S, H * 2 * dh), BF16),
        grid_spec=grid_spec,
        compiler_params=_cparams("parallel", "arbitrary"),
        name="diff_attention",
    )(qi, kj, pminq, pmaxk, qkv, qkv, vt, pos.reshape(1, S), pos.reshape(S, 1),
      lut.reshape(G, 2 * hp, LUT_SIZE), lam, subln_g.reshape(2 * dh, 1))


def _kmean_kernel(k_ref, hi_ref, lo_ref):
    km = jnp.mean(k_ref[...].astype(F32), axis=0, keepdims=True)
    hi = km.astype(BF16)
    hi_ref[...] = hi
    lo_ref[...] = (km - hi.astype(F32)).astype(BF16)


def moba_block_means(qkv, D):
    S = qkv.shape[0]
    nblk = S // MOBA_BLOCK
    hi, lo = pl.pallas_call(
        _kmean_kernel,
        out_shape=[jax.ShapeDtypeStruct((nblk, 1, D), BF16)] * 2,
        grid=(nblk,),
        in_specs=[pl.BlockSpec((MOBA_BLOCK, D), lambda n: (n, 1))],
        out_specs=[pl.BlockSpec((None, 1, D), lambda n: (n, 0, 0))] * 2,
        compiler_params=_cparams("parallel"),
        name="moba_block_means",
    )(qkv)
    return hi.reshape(nblk, D), lo.reshape(nblk, D)


def _moba_attn_kernel(qi_ref, kj_ref, pminq_ref, pmaxk_ref,
                      q_ref, k_ref, vt_ref, kmh_ref, kml_ref, pq_ref, pk_ref, lut_ref,
                      o_ref, m_ref, l_ref, acc_ref, sel_ref, st_ref, *, tq, ck, sb, hp):
    p = pl.program_id(1)
    i, j = qi_ref[p], kj_ref[p]
    nc = tq // ck
    L, dh = MOBA_BLOCK, MOBA_HEAD_DIM
    bpc = ck // L
    log2_l = L.bit_length() - 1
    SB = sb
    nqb, nkb = tq // SB, ck // SB

    def near_sub(c, kb, qb):
        return pminq_ref[i * nqb + qb] - pmaxk_ref[(j * nc + c) * nkb + kb] < LUT_SIZE - 1

    def refine_chunk(c):
        keys = slice(c * ck, (c + 1) * ck)
        for hd in range(hp):
            st_ref[hd] = _nt_dot(k_ref[keys, hd * dh:(hd + 1) * dh], q_ref[:, hd * dh:(hd + 1) * dh])
        for kb in range(nkb):
            for qb in range(nqb):
                @pl.when(near_sub(c, kb, qb))
                def _(kb=kb, qb=qb):
                    rows, cols = slice(kb * SB, (kb + 1) * SB), slice(qb * SB, (qb + 1) * SB)
                    idx = jnp.clip(pq_ref[:, cols] - pk_ref[c * ck + kb * SB:c * ck + (kb + 1) * SB, :],
                                   0, LUT_SIZE - 1)
                    for hd in range(hp):
                        far_bias = lut_ref[hd:hd + 1, LUT_SIZE - 1:LUT_SIZE]
                        st_ref[hd, rows, cols] = (st_ref[hd, rows, cols]
                                                  + (_lut_bias(lut_ref[hd:hd + 1, :], idx) - far_bias))
        for hd in range(hp):
            consume(c, hd, 0, "far", st_ref[hd])

    @pl.when(j == 0)
    def _():
        m_ref[...] = jnp.full(m_ref.shape, NEG_INF, F32)
        l_ref[...] = jnp.zeros(l_ref.shape, F32)
        acc_ref[...] = jnp.zeros(acc_ref.shape, F32)
        blk = lax.broadcasted_iota(jnp.int32, (LANES, tq), 0)
        blk_f = blk.astype(F32)
        own = lax.shift_right_logical(i * tq + lax.broadcasted_iota(jnp.int32, (1, tq), 1), log2_l)
        for hd in range(hp):
            q = q_ref[:, hd * dh:(hd + 1) * dh]
            gs = (_nt_dot(kmh_ref[:, hd * dh:(hd + 1) * dh], q)
                  + _nt_dot(kml_ref[:, hd * dh:(hd + 1) * dh], q))
            g = jnp.where(blk < own, gs, NEG_INF)
            sel = jnp.zeros((LANES, tq), F32)
            for _ in range(MOBA_TOPK):
                best = jnp.max(g, axis=0, keepdims=True)
                first = jnp.min(jnp.where(g == best, blk_f, float(LANES)), axis=0, keepdims=True)
                pick = blk_f == first
                sel = jnp.where(pick, jnp.where(best > 0.5 * NEG_INF, 1.0, sel), sel)
                g = jnp.where(pick, -3e38, g)
            sel_ref[hd] = sel

    def scores(c, hd, q0, mode):
        keys = slice(c * ck, (c + 1) * ck)
        nq = tq - q0
        st = _nt_dot(k_ref[keys, hd * dh:(hd + 1) * dh], q_ref[q0:tq, hd * dh:(hd + 1) * dh])
        if mode != "far":
            idx = jnp.clip(pq_ref[:, q0:tq] - pk_ref[keys, :], 0, LUT_SIZE - 1)
            st = st + _lut_bias(lut_ref[hd:hd + 1, :], idx)
        if mode != "diag":
            return st
        allowed = []
        for b in range(bpc):
            n_local = c * bpc + b
            picked = jnp.broadcast_to(picked_row(hd, n_local, q0), (L, nq))
            own_local = lax.shift_right_logical(q0 + lax.broadcasted_iota(jnp.int32, (L, nq), 1), log2_l)
            visible = jnp.where(_causal_t(n_local * L, L, q0, nq), 1.0, 0.0)
            allowed.append(jnp.where(own_local == n_local, visible, picked))
        allowed = allowed[0] if bpc == 1 else jnp.concatenate(allowed, axis=0)
        return jnp.where(allowed > 0.0, st, NEG_INF)

    def picked_row(hd, n_local, q0):
        return sel_ref[hd, pl.ds(j * (tq // L) + n_local, 1), q0:tq]

    def consume(c, hd, q0, mode, st):
        m_cur = offsets = None
        if mode != "diag":
            bias = lut_ref[hd:hd + 1, LUT_SIZE - 1:LUT_SIZE] if mode == "far" else 0.0
            picked = [picked_row(hd, c * bpc + b, q0) > 0.0 for b in range(bpc)]
            m_cur = functools.reduce(jnp.maximum, [
                jnp.where(picked[b], jnp.max(st[b * L:(b + 1) * L, :], axis=0, keepdims=True) + bias, NEG_INF)
                for b in range(bpc)])
            offsets = [(slice(b * L, (b + 1) * L),
                        functools.partial(lambda m_new, pk: jnp.where(pk, m_new - bias, -NEG_INF), pk=picked[b]))
                       for b in range(bpc)]
        _online_softmax_step_t(st, vt_ref[hd * dh:(hd + 1) * dh, c * ck:(c + 1) * ck],
                               m_ref.at[hd], l_ref.at[hd], acc_ref.at[hd], slice(q0, tq), m_cur, offsets)

    def sweep(chunks, mode):
        _pipelined([(c, hd, c * ck if mode == "diag" else 0, mode) for c in chunks for hd in range(hp)],
                   scores, consume)

    @pl.when(j < i)
    def _():
        fars = [jnp.logical_not(functools.reduce(
            jnp.logical_or, [near_sub(c, kb, qb) for kb in range(nkb) for qb in range(nqb)])) for c in range(nc)]
        all_far = functools.reduce(jnp.logical_and, fars)
        pl.when(all_far)(functools.partial(sweep, range(nc), "far"))

        @pl.when(jnp.logical_not(all_far))
        def _():
            for c in range(nc):
                refine_chunk(c)

    @pl.when(j == i)
    def _():
        sweep(range(nc), "diag")
        for hd in range(hp):
            o_ref[:, hd * dh:(hd + 1) * dh] = (acc_ref[hd] / l_ref[hd]).T.astype(o_ref.dtype)


def moba_attention(qkv, vt, pos, lut, *, tile=1024, chunk=512, heads_per_step=2):
    S = qkv.shape[0]
    H, dh, L = MOBA_HEADS, MOBA_HEAD_DIM, MOBA_BLOCK
    D = H * dh
    nblk = S // L
    assert S % L == 0 and nblk <= LANES
    tq = _tile(S, tile)
    ck = _tile(tq, chunk)
    assert tq % L == 0 and ck % L == 0
    km_hi, km_lo = moba_block_means(qkv, D)
    km_hi = jnp.pad(km_hi, ((0, LANES - nblk), (0, 0)))
    km_lo = jnp.pad(km_lo, ((0, LANES - nblk), (0, 0)))
    qi, kj = _pair_tables(S // tq)
    sb = _tile(ck, BIAS_BLOCK)
    pminq, pmaxk = _block_pos_bounds(pos, sb)
    hp = heads_per_step
    G = H // hp
    kern = functools.partial(_moba_attn_kernel, tq=tq, ck=ck, sb=sb, hp=hp)
    grid_spec = pltpu.PrefetchScalarGridSpec(
        num_scalar_prefetch=4,
        grid=(G, qi.shape[0]),
        in_specs=[pl.BlockSpec((tq, hp * dh), lambda h, p, qi, kj, a, b: (qi[p], h)),
                  pl.BlockSpec((tq, hp * dh), lambda h, p, qi, kj, a, b: (kj[p], G + h)),
                  pl.BlockSpec((hp * dh, tq), lambda h, p, qi, kj, a, b: (h, kj[p])),
                  pl.BlockSpec((LANES, hp * dh), lambda h, p, qi, kj, a, b: (0, h)),
                  pl.BlockSpec((LANES, hp * dh), lambda h, p, qi, kj, a, b: (0, h)),
                  pl.BlockSpec((1, tq), lambda h, p, qi, kj, a, b: (0, qi[p])),
                  pl.BlockSpec((tq, 1), lambda h, p, qi, kj, a, b: (kj[p], 0)),
                  pl.BlockSpec((None, hp, LUT_SIZE), lambda h, p, qi, kj, a, b: (h, 0, 0))],
        out_specs=pl.BlockSpec((tq, hp * dh), lambda h, p, qi, kj, a, b: (qi[p], h)),
        scratch_shapes=[pltpu.VMEM((hp, 1, tq), F32), pltpu.VMEM((hp, 1, tq), F32),
                        pltpu.VMEM((hp, dh, tq), F32), pltpu.VMEM((hp, LANES, tq), F32),
                        pltpu.VMEM((hp, ck, tq), F32)],
    )
    return pl.pallas_call(
        kern,
        out_shape=jax.ShapeDtypeStruct((S, D), BF16),
        grid_spec=grid_spec,
        compiler_params=_cparams("parallel", "arbitrary"),
        name="moba_attention",
    )(qi, kj, pminq, pmaxk, qkv, qkv, vt, km_hi, km_lo, pos.reshape(1, S), pos.reshape(S, 1),
      lut.reshape(G, hp, LUT_SIZE))


def _rope_fold(y):
    lane = lax.broadcasted_iota(jnp.int32, y.shape, 1)
    return jnp.where(lane < MLA_ROPE, y + pltpu.roll(y, MLA_ROPE, 1), 0.0)


def _mla_q_kernel(cq_ref, g_ref, w_ref, kr_ref, t_ref, q_ref, kro_ref, cqn_ref, *, scale, hp):
    t = t_ref[...]
    W = 2 * LANES

    @pl.when(pl.program_id(1) == 0)
    def _():
        cqn_ref[...] = _rms_rows(cq_ref[...], g_ref[...]).astype(BF16)
        kro_ref[...] = _rope_fold(kr_ref[...] * t).astype(kro_ref.dtype)

    qh = jnp.dot(cqn_ref[...], w_ref[...], preferred_element_type=F32)
    for hd in range(hp):
        q_ref[:, hd * W:hd * W + MLA_NOPE] = (qh[:, hd * W:hd * W + MLA_NOPE] * scale).astype(q_ref.dtype)
        q_ref[:, hd * W + MLA_NOPE:(hd + 1) * W] = (
            _rope_fold(qh[:, hd * W + MLA_NOPE:(hd + 1) * W] * t) * scale).astype(q_ref.dtype)


def mla_queries_and_rope_key(down, g_q, w_uq_ext, rope_tab, *, scale, tm=1024, heads_per_step=4):
    S = down.shape[0]
    H, hp = MLA_HEADS, heads_per_step
    tm = _tile(S, tm)
    W = 2 * LANES
    return pl.pallas_call(
        functools.partial(_mla_q_kernel, scale=scale, hp=hp),
        out_shape=[jax.ShapeDtypeStruct((S, H * W), BF16), jax.ShapeDtypeStruct((S, LANES), BF16)],
        grid=(S // tm, H // hp),
        in_specs=[pl.BlockSpec((tm, MLA_Q_RANK), lambda i, h: (i, 0)),
                  pl.BlockSpec((1, MLA_Q_RANK), lambda i, h: (0, 0)),
                  pl.BlockSpec((MLA_Q_RANK, hp * W), lambda i, h: (0, h)),
                  pl.BlockSpec((tm, LANES), lambda i, h: (i, (MLA_Q_RANK + MLA_KV_RANK) // LANES)),
                  pl.BlockSpec((tm, LANES), lambda i, h: (i, 0))],
        out_specs=[pl.BlockSpec((tm, hp * W), lambda i, h: (i, h)),
                   pl.BlockSpec((tm, LANES), lambda i, h: (i, 0))],
        scratch_shapes=[pltpu.VMEM((tm, MLA_Q_RANK), BF16)],
        compiler_params=_cparams("parallel", "arbitrary"),
        name="mla_queries",
    )(down, g_q.reshape(1, MLA_Q_RANK), w_uq_ext, down, rope_tab)


def _mla_attn_kernel(qi_ref, kj_ref, q_ref, kv_ref, kr_ref, vt_ref, o_ref, m_ref, l_ref, acc_ref, *, tq, ck, hp):
    p = pl.program_id(1)
    i, j = qi_ref[p], kj_ref[p]
    nc = tq // ck
    W = 2 * LANES

    @pl.when(j == 0)
    def _():
        m_ref[...] = jnp.full(m_ref.shape, NEG_INF, F32)
        l_ref[...] = jnp.zeros(l_ref.shape, F32)
        acc_ref[...] = jnp.zeros(acc_ref.shape, F32)

    def scores(c, g, q0, diag):
        keys = slice(c * ck, (c + 1) * ck)
        kc = jnp.concatenate([kv_ref[keys, g * MLA_NOPE:(g + 1) * MLA_NOPE], kr_ref[keys, :]], axis=1)
        st = _nt_dot(kc, q_ref[q0:tq, g * W:(g + 1) * W])
        if diag:
            st = jnp.where(_causal_t(c * ck, ck, q0, tq - q0), st, NEG_INF)
        return st

    def consume(c, g, q0, diag, st):
        _online_softmax_step_t(st, vt_ref[g * MLA_V:(g + 1) * MLA_V, c * ck:(c + 1) * ck],
                               m_ref.at[g], l_ref.at[g], acc_ref.at[g], slice(q0, tq))

    def sweep(diag):
        _pipelined([(c, g, c * ck if diag else 0, diag) for c in range(nc) for g in range(hp)], scores, consume)

    @pl.when(j < i)
    def _():
        sweep(False)

    @pl.when(j == i)
    def _():
        sweep(True)
        for g in range(hp):
            o_ref[:, g * MLA_V:(g + 1) * MLA_V] = (acc_ref[g] / l_ref[g]).T.astype(o_ref.dtype)


def mla_attention(q, kv, kr, vt, *, tile=2048, chunk=1024, heads_per_step=2):
    S = q.shape[0]
    H, hp = MLA_HEADS, heads_per_step
    tq = _tile(S, tile)
    ck = _tile(tq, chunk)
    qi, kj = _pair_tables(S // tq)
    W = 2 * LANES
    grid_spec = pltpu.PrefetchScalarGridSpec(
        num_scalar_prefetch=2,
        grid=(H // hp, qi.shape[0]),
        in_specs=[pl.BlockSpec((tq, hp * W), lambda h, p, qi, kj: (qi[p], h)),
                  pl.BlockSpec((tq, hp * MLA_NOPE), lambda h, p, qi, kj: (kj[p], h)),
                  pl.BlockSpec((tq, LANES), lambda h, p, qi, kj: (kj[p], 0)),
                  pl.BlockSpec((hp * MLA_V, tq), lambda h, p, qi, kj: (h, kj[p]))],
        out_specs=pl.BlockSpec((tq, hp * MLA_V), lambda h, p, qi, kj: (qi[p], h)),
        scratch_shapes=[pltpu.VMEM((hp, 1, tq), F32), pltpu.VMEM((hp, 1, tq), F32),
                        pltpu.VMEM((hp, MLA_V, tq), F32)],
    )
    return pl.pallas_call(
        functools.partial(_mla_attn_kernel, tq=tq, ck=ck, hp=hp),
        out_shape=jax.ShapeDtypeStruct((S, H * MLA_V), BF16),
        grid_spec=grid_spec,
        compiler_params=_cparams("parallel", "arbitrary"),
        name="mla_attention",
    )(qi, kj, q, kv, kr, vt)


_HI16 = 0xFFFF0000


def _pack_bf16_pairs(hi_f32):
    half = hi_f32.shape[1] // 2
    bits = pltpu.bitcast(hi_f32, jnp.uint32)
    return lax.shift_right_logical(bits[:, :half], jnp.uint32(16)) | (bits[:, half:] & jnp.uint32(_HI16))


def _unpack_pairs_by_tile(words, tile):
    half = tile // 2
    parts = []
    for c0 in range(0, words.shape[1], half):
        w = words[:, c0:c0 + half]
        parts.append(pltpu.bitcast(lax.shift_left(w, jnp.uint32(16)), F32))
        parts.append(pltpu.bitcast(w & jnp.uint32(_HI16), F32))
    return jnp.concatenate(parts, axis=1)


def _unpack_bf16_pairs(words):
    lo = pltpu.bitcast(lax.shift_left(words, jnp.uint32(16)), F32).astype(BF16)
    hi = pltpu.bitcast(words & jnp.uint32(_HI16), F32).astype(BF16)
    return jnp.concatenate([lo, hi], axis=1)


def _router_kernel(x_ref, g_ref, wh_ref, wl_ref, xn_ref, r_ref):
    xn = _rms_rows(x_ref[...], g_ref[...])
    hi = xn.astype(BF16)
    hi_f32 = hi.astype(F32)
    lo = (xn - hi_f32).astype(BF16)
    xn_ref[...] = _pack_bf16_pairs(hi_f32)
    logits = (jnp.dot(hi, wh_ref[...], preferred_element_type=F32)
              + jnp.dot(hi, wl_ref[...], preferred_element_type=F32)
              + jnp.dot(lo, wh_ref[...], preferred_element_type=F32))
    lane = lax.broadcasted_iota(jnp.int32, logits.shape, 1)
    lane_f = lane.astype(F32)
    g = jnp.where(lane < N_EXPERTS, logits, NEG_INF)
    v1 = jnp.max(g, axis=-1, keepdims=True)
    i1 = jnp.min(jnp.where(g == v1, lane_f, float(LANES)), axis=-1, keepdims=True)
    g = jnp.where(lane_f == i1, NEG_INF, g)
    v2 = jnp.max(g, axis=-1, keepdims=True)
    i2 = jnp.min(jnp.where(g == v2, lane_f, float(LANES)), axis=-1, keepdims=True)
    g2 = 1.0 / (1.0 + jnp.exp(v1 - v2))
    g1 = 1.0 - g2
    r_ref[...] = jnp.where(lane == 0, i1,
                           jnp.where(lane == 1, i2,
                                     jnp.where(lane == 2, g1, jnp.where(lane == 3, g2, 0.0))))


def moe_route(x, g, w_router, *, tm=512):
    M, D = x.shape
    tm = _tile(M, tm)
    wr = jnp.pad(w_router.astype(F32), ((0, 0), (0, LANES - N_EXPERTS)))
    wh = wr.astype(BF16)
    wl = (wr - wh.astype(F32)).astype(BF16)
    return pl.pallas_call(
        _router_kernel,
        out_shape=[jax.ShapeDtypeStruct((M, D // 2), jnp.uint32), jax.ShapeDtypeStruct((M, LANES), F32)],
        grid=(M // tm,),
        in_specs=[pl.BlockSpec((tm, D), lambda i: (i, 0)),
                  pl.BlockSpec((1, D), lambda i: (0, 0)),
                  pl.BlockSpec((D, LANES), lambda i: (0, 0)),
                  pl.BlockSpec((D, LANES), lambda i: (0, 0))],
        out_specs=[pl.BlockSpec((tm, D // 2), lambda i: (i, 0)),
                   pl.BlockSpec((tm, LANES), lambda i: (i, 0))],
        compiler_params=_cparams("parallel"),
        name="moe_route",
    )(x, g.reshape(1, D), wh, wl)


def _new_expert(te_ref, t):
    return jnp.logical_or(t == 0, te_ref[t] != te_ref[jnp.maximum(t - 1, 0)])


def _stream_expert_weights(te_ref, rid_ref, rexp_ref, nr_ref, used, n_col_tiles, copies_for, on_ready):
    j, t = pl.program_id(0), pl.program_id(1)

    @pl.when(jnp.logical_and(used, _new_expert(te_ref, t)))
    def _():
        n_runs = nr_ref[0]
        r = rid_ref[t]
        g = j * n_runs + r
        slot = lax.rem(g, 2)

        @pl.when(g == 0)
        def _():
            for c in copies_for(j, te_ref[t], slot):
                c.start()

        for c in copies_for(j, te_ref[t], slot):
            c.wait()
        on_ready(slot)
        last = r + 1 == n_runs
        nxt_r = jnp.where(last, 0, r + 1)
        nxt_j = jnp.where(last, j + 1, j)

        @pl.when(nxt_j < n_col_tiles)
        def _():
            for c in copies_for(nxt_j, rexp_ref[nxt_r], 1 - slot):
                c.start()


def _gmm_swiglu_kernel(te_ref, nu_ref, rid_ref, rexp_ref, nr_ref, *refs, n_parts, tiles_per_part, layer, nj, tn):
    x_refs = refs[:n_parts]
    w_hbm, o_ref, wbuf_ref, wgb_ref, wub_ref, sem = refs[n_parts:]
    t = pl.program_id(1)
    used = t < nu_ref[0]

    def copies_for(jj, e, slot):
        gate_cols = pl.ds(pl.multiple_of(jj * tn, tn), tn)
        up_cols = pl.ds(pl.multiple_of((nj + jj) * tn, tn), tn)
        return (pltpu.make_async_copy(w_hbm.at[layer, e, :, gate_cols], wbuf_ref.at[slot, 0], sem.at[slot, 0]),
                pltpu.make_async_copy(w_hbm.at[layer, e, :, up_cols], wbuf_ref.at[slot, 1], sem.at[slot, 1]))

    def on_ready(slot):
        wgb_ref[...] = wbuf_ref[slot, 0].astype(BF16)
        wub_ref[...] = wbuf_ref[slot, 1].astype(BF16)

    _stream_expert_weights(te_ref, rid_ref, rexp_ref, nr_ref, used, nj, copies_for, on_ready)

    for c in range(n_parts):
        @pl.when(jnp.logical_and(used, t // tiles_per_part == c))
        def _(c=c):
            x = _unpack_bf16_pairs(x_refs[c][...])
            a = jnp.dot(x, wgb_ref[...], preferred_element_type=F32)
            u = jnp.dot(x, wub_ref[...], preferred_element_type=F32)
            o_ref[...] = (a * jax.nn.sigmoid(a) * u).astype(o_ref.dtype)

    @pl.when(jnp.logical_not(used))
    def _():
        o_ref[...] = jnp.zeros(o_ref.shape, o_ref.dtype)


def _gmm_out_kernel(te_ref, nu_ref, rid_ref, rexp_ref, nr_ref, h_ref, w_hbm, o_ref, wbuf_ref, wb_ref, sem, *,
                    layer, nj, tn):
    t = pl.program_id(1)
    used = t < nu_ref[0]

    def copies_for(jj, e, slot):
        cols = pl.ds(pl.multiple_of(jj * tn, tn), tn)
        return (pltpu.make_async_copy(w_hbm.at[layer, e, :, cols], wbuf_ref.at[slot], sem.at[slot]),)

    def on_ready(slot):
        wb_ref[...] = wbuf_ref[slot].astype(BF16)

    _stream_expert_weights(te_ref, rid_ref, rexp_ref, nr_ref, used, nj, copies_for, on_ready)

    @pl.when(used)
    def _():
        y = jnp.dot(h_ref[...], wb_ref[...], preferred_element_type=F32)
        o_ref[...] = _pack_bf16_pairs(y.astype(BF16).astype(F32))

    @pl.when(jnp.logical_not(used))
    def _():
        o_ref[...] = jnp.zeros(o_ref.shape, o_ref.dtype)


def moe_experts(xs, tile_expert, n_used, runs, w_in, w_out, layer, *, tm, tn=MOE_COL_TILE):
    n_parts = len(xs)
    P = n_parts * xs[0].shape[0]
    D = w_in.shape[2]
    F = w_in.shape[3] // 2
    nt = P // tm
    ntp = nt // n_parts
    assert ntp * n_parts == nt
    tn1, tn2 = _tile(F, tn), _tile(D, tn)
    nj1, nj2 = F // tn1, D // tn2

    def part_spec(c):
        return pl.BlockSpec((tm, D // 2), lambda j, t, *_: (jnp.clip(t - c * ntp, 0, ntp - 1), 0))

    prefetch = (tile_expert, n_used) + tuple(runs)
    hbm = pl.BlockSpec(memory_space=pl.ANY)
    h = pl.pallas_call(
        functools.partial(_gmm_swiglu_kernel, n_parts=n_parts, tiles_per_part=ntp, layer=layer, nj=nj1, tn=tn1),
        out_shape=jax.ShapeDtypeStruct((P, F), BF16),
        grid_spec=pltpu.PrefetchScalarGridSpec(
            num_scalar_prefetch=len(prefetch),
            grid=(nj1, nt),
            in_specs=[part_spec(c) for c in range(n_parts)] + [hbm],
            out_specs=pl.BlockSpec((tm, tn1), lambda j, t, *_: (t, j)),
            scratch_shapes=[pltpu.VMEM((2, 2, D, tn1), F32), pltpu.VMEM((D, tn1), BF16),
                            pltpu.VMEM((D, tn1), BF16), pltpu.SemaphoreType.DMA((2, 2))],
        ),
        compiler_params=_cparams("arbitrary", "arbitrary"),
        name="moe_swiglu_in",
    )(*prefetch, *xs, w_in)
    return pl.pallas_call(
        functools.partial(_gmm_out_kernel, layer=layer, nj=nj2, tn=tn2),
        out_shape=jax.ShapeDtypeStruct((P, D // 2), jnp.uint32),
        grid_spec=pltpu.PrefetchScalarGridSpec(
            num_scalar_prefetch=len(prefetch),
            grid=(nj2, nt),
            in_specs=[pl.BlockSpec((tm, F), lambda j, t, *_: (t, 0)), hbm],
            out_specs=pl.BlockSpec((tm, tn2 // 2), lambda j, t, *_: (t, j)),
            scratch_shapes=[pltpu.VMEM((2, F, tn2), F32), pltpu.VMEM((F, tn2), BF16),
                            pltpu.SemaphoreType.DMA((2,))],
        ),
        compiler_params=_cparams("arbitrary", "arbitrary"),
        name="moe_out",
    )(*prefetch, h, w_out)


def moe_swiglu(x, g, w_router, w_in, w_out, layer, *, tm=512):
    S, D = x.shape
    E = N_EXPERTS
    tm = _tile(S, tm)
    xn, route = moe_route(x, g, w_router)
    experts = route[:, :2].astype(jnp.int32)
    gates = route[:, 2:4]
    flat_e = experts.reshape(-1)
    order = jnp.argsort(flat_e, stable=True).astype(jnp.int32)
    rank = jnp.argsort(order).astype(jnp.int32)
    counts = jnp.sum(flat_e[:, None] == jnp.arange(E, dtype=jnp.int32)[None, :], axis=0, dtype=jnp.int32)
    padded = ((counts + tm - 1) // tm) * tm
    start = jnp.cumsum(counts) - counts
    pstart = jnp.cumsum(padded) - padded
    pend = jnp.cumsum(padded)
    P = 2 * S + E * tm
    nt = P // tm
    tile_expert = jnp.minimum(
        jnp.sum(pend[None, :] <= (jnp.arange(nt, dtype=jnp.int32) * tm)[:, None], axis=1), E - 1).astype(jnp.int32)
    n_used = (pend[-1:] // tm).astype(jnp.int32)
    row_e = jnp.repeat(tile_expert, tm)
    row_rank = jnp.arange(P, dtype=jnp.int32) - pstart[row_e]
    row_valid = jnp.logical_and(row_rank < counts[row_e], jnp.arange(P) < pend[-1])
    row_token = jnp.where(row_valid, order[jnp.clip(start[row_e] + row_rank, 0, 2 * S - 1)] // 2, 0)
    slot = pstart[flat_e] + rank - start[flat_e]
    tile_expert = jnp.where(jnp.arange(nt) < n_used[0], tile_expert, tile_expert[jnp.maximum(n_used[0] - 1, 0)])
    slot = slot.reshape(S, 2)
    present = counts > 0
    run_of_expert = jnp.cumsum(present.astype(jnp.int32)) - 1
    runs = (run_of_expert[tile_expert].astype(jnp.int32),
            jnp.sort(jnp.where(present, jnp.arange(E, dtype=jnp.int32), E))[:E].clip(0, E - 1).astype(jnp.int32),
            jnp.sum(present).astype(jnp.int32).reshape(1))
    xs = [jnp.take(xn, part, axis=0, mode="clip") for part in jnp.split(row_token, MOE_GATHER_PARTS)]
    ys = moe_experts(xs, tile_expert, n_used, runs, w_in, w_out, layer, tm=tm)
    return jnp.take(ys, slot[:, 0], axis=0, mode="clip"), jnp.take(ys, slot[:, 1], axis=0, mode="clip"), gates


def _diff_lambda_init(layer):
    return 0.8 - 0.6 * math.exp(-0.3 * layer)


def _rope_table(pos):
    half = MLA_ROPE // 2
    inv_freq = ROPE_THETA ** (-jnp.arange(half, dtype=F32) / half)
    ang = pos.astype(F32)[:, None] * inv_freq
    cos, sin = jnp.cos(ang), jnp.sin(ang)
    return jnp.concatenate([cos, cos, -sin, sin], axis=1)


def _rot_half_cols(w):
    half = w.shape[-1] // 2
    return jnp.concatenate([w[..., half:], w[..., :half]], axis=-1)


def kernel(x, p, positions, rel_bias, norm_mix, norm_ffn, norm_ple, norm_final, diff_w_qkv, diff_lambda, diff_subln, diff_w_o, moba_w_qkv, moba_w_o, mla_w_down, mla_g_q, mla_w_uq, mla_g_kv, mla_w_ukv, mla_w_o, ffn_w_in, ffn_w_out, moe_w_router, moe_w_in, moe_w_out, ple_w_gate, ple_w_proj):
    B, S, D = x.shape
    depth = p.shape[0]
    lut = _bias_lut(rel_bias)
    outs = []
    for b in range(B):
        xb = x[b]
        pos = positions[b].astype(jnp.int32)
        for i in range(depth):
            jm = i // N_MIXERS
            if i % N_MIXERS == 0:
                qk, vt = norm_matmul_vt(xb, norm_mix[i], diff_w_qkv, layer=jm, n_direct=2 * D,
                                        scaled_cols=D, scale=DIFF_HEAD_DIM ** -0.5 * LOG2E)
                o = diff_attention(qk, vt, pos, lut, diff_lambda[jm].astype(F32), diff_subln[jm],
                                   _diff_lambda_init(i))
                xb = matmul_residual(o, diff_w_o, xb, layer=jm)
            elif i % N_MIXERS == 1:
                qk, vt = norm_matmul_vt(xb, norm_mix[i], moba_w_qkv, layer=jm, n_direct=2 * D,
                                        scaled_cols=D, scale=MOBA_HEAD_DIM ** -0.5 * LOG2E)
                o = moba_attention(qk, vt, pos, lut)
                xb = matmul_residual(o, moba_w_o, xb, layer=jm)
            else:
                wd = mla_w_down[jm]
                kr0 = MLA_Q_RANK + MLA_KV_RANK
                wd_ext = jnp.concatenate([wd, _rot_half_cols(wd[:, kr0:])], axis=1).astype(BF16)
                wq = mla_w_uq[jm].reshape(MLA_Q_RANK, MLA_HEADS, MLA_NOPE + MLA_ROPE)
                wq_ext = jnp.concatenate([wq, _rot_half_cols(wq[..., MLA_NOPE:])], axis=-1)
                wq_ext = wq_ext.reshape(MLA_Q_RANK, MLA_HEADS * 2 * LANES).astype(BF16)
                down = norm_matmul(xb, norm_mix[i], wd_ext, out_dtype=F32, tn=wd_ext.shape[1])
                q, kr = mla_queries_and_rope_key(down, mla_g_q[jm], wq_ext, _rope_table(pos),
                                                 scale=(MLA_NOPE + MLA_ROPE) ** -0.5 * LOG2E)
                wkv = mla_w_ukv[jm].reshape(MLA_KV_RANK, MLA_HEADS, 2, MLA_V).transpose(0, 2, 1, 3)
                wkv = wkv.reshape(MLA_KV_RANK, 2 * MLA_HEADS * MLA_V).astype(BF16)
                kn, vt = norm_matmul_vt(down, mla_g_kv[jm], wkv, n_direct=MLA_HEADS * MLA_NOPE, xcol=1)
                o = mla_attention(q, kn, kr, vt)
                xb = matmul_residual(o, mla_w_o, xb, layer=jm)
            if i % 2 == 0:
                h = norm_swiglu_in(xb, norm_ffn[i], ffn_w_in, layer=i // 2)
                xb = matmul_residual(h, ffn_w_out[i // 2].astype(BF16), xb)
                xb = ple_update(xb, norm_ple[i], p[i, b], ple_w_gate, ple_w_proj, layer=i)
            else:
                y1, y2, gates = moe_swiglu(xb, norm_ffn[i], moe_w_router[i // 2], moe_w_in, moe_w_out, i // 2)
                xb = ple_update_moe(xb, y1, y2, gates, norm_ple[i], p[i, b], ple_w_gate, ple_w_proj, layer=i)
        outs.append(final_norm(xb, norm_final))
    return jnp.stack(outs, axis=0)
```
